```python
import jax
import jax.numpy as jnp
from jax import lax
import numpy as np

D_MODEL = 1024
BATCH = 2
SEQ = 16384
DEPTH = 2

N_MIXERS = 4
GROUP_W = D_MODEL // N_MIXERS
D_MIX = N_MIXERS * GROUP_W
HEAD_DIM = 64
CONV_WIDTH = 3
MLA_HEADS = GROUP_W // HEAD_DIM
MLA_NOPE = HEAD_DIM
MLA_ROPE = HEAD_DIM // 2
MLA_V = HEAD_DIM
MLA_QK = MLA_NOPE + MLA_ROPE
Q_LORA = 3 * D_MODEL // 16
KV_LORA = D_MODEL // 8
ATTN_BLOCK = 128
RET_HEADS = GROUP_W // HEAD_DIM
RET_DK = HEAD_DIM
RET_DV = HEAD_DIM
RET_CHUNK = 128
LRU_WIDTH = GROUP_W
LRU_BLOCKS = 4
LRU_BLOCK = LRU_WIDTH // LRU_BLOCKS
LRU_CONV = 4
LRU_C = 8.0
MOE_GROUPS = 4
EXPERTS_PER_GROUP = 8
N_EXPERTS = MOE_GROUPS * EXPERTS_PER_GROUP
MOE_TOPK = 2
D_EXPERT = D_MODEL // 4

ROPE_BASE = 10000.0
EPS = 1e-6

CONV_COLS = 3 * GROUP_W
MLA_COLS = Q_LORA + KV_LORA + MLA_ROPE
RET_COLS = 4 * GROUP_W
LRU_COLS = 2 * LRU_WIDTH
IN_COLS = CONV_COLS + MLA_COLS + RET_COLS + LRU_COLS

kernel_name = 'hymba_style_hybrid_hmoe_block'


def rms_norm(x, g):
    xf = x.astype(jnp.float32)
    y = xf * lax.rsqrt(jnp.mean(xf * xf, axis=-1, keepdims=True) + EPS)
    return (y * g.astype(jnp.float32)).astype(x.dtype)


def rope_tables(positions, dim):
    inv = 1.0 / (ROPE_BASE ** (jnp.arange(0, dim, 2, dtype=jnp.float32) / dim))
    ang = positions.astype(jnp.float32)[..., None] * inv
    return jnp.cos(ang), jnp.sin(ang)


def apply_rope(x, cos, sin):
    x1, x2 = jnp.split(x.astype(jnp.float32), 2, axis=-1)
    out = jnp.concatenate([x1 * cos - x2 * sin, x2 * cos + x1 * sin], axis=-1)
    return out.astype(x.dtype)


def causal_depthwise_conv(x, w):
    width, ch = w.shape
    return lax.conv_general_dilated(
        x, w[:, None, :].astype(x.dtype), window_strides=(1,),
        padding=[(width - 1, 0)], dimension_numbers=('NWC', 'WIO', 'NWC'),
        feature_group_count=ch)


def short_conv_mixer(u, conv_w):
    b_gate, c_gate, xin = jnp.split(u, 3, axis=-1)
    return b_gate * causal_depthwise_conv(c_gate * xin, conv_w)


def causal_block_attention(q, k, v):
    bsz, nh, s, dq = q.shape
    nb = s // ATTN_BLOCK
    scale = dq ** -0.5
    q_blocks = q.reshape(bsz, nh, nb, ATTN_BLOCK, dq).transpose(2, 0, 1, 3, 4)
    key_pos = jnp.arange(s)

    def one_block(args):
        qb, bi = args
        sc = jnp.einsum('bhqd,bhkd->bhqk', qb, k).astype(jnp.float32) * scale
        q_pos = bi * ATTN_BLOCK + jnp.arange(ATTN_BLOCK)
        sc = jnp.where(key_pos[None, :] <= q_pos[:, None], sc, -jnp.inf)
        p = jax.nn.softmax(sc, axis=-1).astype(v.dtype)
        return jnp.einsum('bhqk,bhkd->bhqd', p, v)

    out = lax.map(one_block, (q_blocks, jnp.arange(nb)))
    return out.transpose(1, 2, 0, 3, 4).reshape(bsz, nh, s, v.shape[-1])


def mla_mixer(u, cos, sin, q_norm_g, w_uq, kv_norm_g, w_ukv, q_qk_g, k_qk_g):
    bsz, s, _ = u.shape
    cq, ckv, k_rope = jnp.split(u, [Q_LORA, Q_LORA + KV_LORA], axis=-1)
    q = (rms_norm(cq, q_norm_g) @ w_uq).reshape(bsz, s, MLA_HEADS, MLA_QK)
    kv = (rms_norm(ckv, kv_norm_g) @ w_ukv).reshape(bsz, s, MLA_HEADS, MLA_NOPE + MLA_V)
    q_nope, q_rope = jnp.split(q, [MLA_NOPE], axis=-1)
    k_nope, v = jnp.split(kv, [MLA_NOPE], axis=-1)
    q_nope = rms_norm(q_nope, q_qk_g[:MLA_NOPE])
    q_rope = rms_norm(q_rope, q_qk_g[MLA_NOPE:])
    k_nope = rms_norm(k_nope, k_qk_g[:MLA_NOPE])
    k_rope = rms_norm(k_rope, k_qk_g[MLA_NOPE:])
    q_rope = apply_rope(q_rope, cos[:, :, None, :], sin[:, :, None, :])
    k_rope = apply_rope(k_rope, cos, sin)
    k_rope = jnp.broadcast_to(k_rope[:, :, None, :], (bsz, s, MLA_HEADS, MLA_ROPE))
    q = jnp.concatenate([q_nope, q_rope], axis=-1).transpose(0, 2, 1, 3)
    k = jnp.concatenate([k_nope, k_rope], axis=-1).transpose(0, 2, 1, 3)
    out = causal_block_attention(q, k, v.transpose(0, 2, 1, 3))
    return out.transpose(0, 2, 1, 3).reshape(bsz, s, MLA_HEADS * MLA_V)


def chunkwise_retention(q, k, v):
    bsz, s, nh, dk = q.shape
    dv = v.shape[-1]
    n = s // RET_CHUNK
    log_g = jnp.log(1.0 - 2.0 ** (-5.0 - jnp.arange(nh, dtype=jnp.float32)))
    idx = jnp.arange(RET_CHUNK, dtype=jnp.float32)
    rel = idx[:, None] - idx[None, :]
    inner = jnp.where(rel >= 0, jnp.exp(log_g[:, None, None] * jnp.maximum(rel, 0.0)), 0.0)
    q_dec = jnp.exp(log_g[:, None] * (idx + 1.0))[None, :, :, None]
    k_dec = jnp.exp(log_g[:, None] * (RET_CHUNK - 1.0 - idx))[None, :, :, None]
    chunk_dec = jnp.exp(log_g * RET_CHUNK)[None, :, None, None]

    def to_chunks(t):
        return t.astype(jnp.float32).reshape(bsz, n, RET_CHUNK, nh, t.shape[-1]).transpose(1, 0, 3, 2, 4)

    def step(state, xs):
        qi, ki, vi = xs
        scores = jnp.einsum('bhnd,bhmd->bhnm', qi, ki) * inner
        o = (jnp.einsum('bhnm,bhme->bhne', scores, vi)
             + jnp.einsum('bhnd,bhde->bhne', qi, state) * q_dec)
        state = chunk_dec * state + jnp.einsum('bhmd,bhme->bhde', ki * k_dec, vi)
        return state, o

    init = jnp.zeros((bsz, nh, dk, dv), jnp.float32)
    _, out = lax.scan(step, init, (to_chunks(q), to_chunks(k), to_chunks(v)))
    return out.transpose(1, 0, 3, 2, 4).reshape(bsz, s, nh, dv)


def retention_mixer(u, cos, sin):
    bsz, s, _ = u.shape
    q, k, v, g = jnp.split(u, 4, axis=-1)
    heads = lambda t: t.reshape(bsz, s, RET_HEADS, HEAD_DIM)
    q = apply_rope(heads(q), cos[:, :, None, :], sin[:, :, None, :])
    k = apply_rope(heads(k), cos[:, :, None, :], sin[:, :, None, :]) * (RET_DK ** -0.5)
    r = chunkwise_retention(q, k, heads(v))
    mu = jnp.mean(r, axis=-1, keepdims=True)
    var = jnp.mean(jnp.square(r - mu), axis=-1, keepdims=True)
    y = ((r - mu) * lax.rsqrt(var + EPS)).reshape(bsz, s, GROUP_W).astype(u.dtype)
    return jax.nn.silu(g) * y


def rglru_mixer(u, conv_w, conv_b, w_a, b_a, w_x, b_x, lam):
    bsz, s, _ = u.shape
    xb, gate = jnp.split(u, 2, axis=-1)
    xb = causal_depthwise_conv(xb, conv_w) + conv_b
    xblk = xb.reshape(bsz, s, LRU_BLOCKS, LRU_BLOCK)
    r = jax.nn.sigmoid(jnp.einsum('bsnc,ncd->bsnd', xblk, w_a).reshape(bsz, s, LRU_WIDTH) + b_a)
    i = jax.nn.sigmoid(jnp.einsum('bsnc,ncd->bsnd', xblk, w_x).reshape(bsz, s, LRU_WIDTH) + b_x)
    log_a = -LRU_C * r.astype(jnp.float32) * jax.nn.softplus(-lam.astype(jnp.float32))
    a = jnp.exp(log_a)
    b_in = jnp.sqrt(-jnp.expm1(2.0 * log_a)) * (i * xb).astype(jnp.float32)

    def combine(left, right):
        a1, b1 = left
        a2, b2 = right
        return a1 * a2, a2 * b1 + b2

    _, h = lax.associative_scan(combine, (a, b_in), axis=1)
    return h.astype(u.dtype) * jax.nn.gelu(gate)


def hierarchical_moe(h, wg_r, bg_r, we_r, be_r, w_gate, w_up, w_down):
    bsz, s, d = h.shape
    t = h.reshape(-1, d)
    group_prob = jax.nn.softmax((t @ wg_r).astype(jnp.float32), axis=-1)
    g_onehot = jax.nn.one_hot(jnp.argmax(group_prob + bg_r.astype(jnp.float32), axis=-1),
                              MOE_GROUPS, dtype=jnp.float32)
    g_weight = jnp.sum(group_prob * g_onehot, axis=-1)
    exp_logits = (t @ we_r).astype(jnp.float32).reshape(-1, MOE_GROUPS, EXPERTS_PER_GROUP)
    exp_prob = jax.nn.softmax(jnp.einsum('tge,tg->te', exp_logits, g_onehot), axis=-1)
    sel_bias = jnp.einsum('ge,tg->te', be_r.astype(jnp.float32).reshape(MOE_GROUPS, EXPERTS_PER_GROUP), g_onehot)
    _, top_idx = lax.top_k(exp_prob + sel_bias, MOE_TOPK)
    top_p = jnp.take_along_axis(exp_prob, top_idx, axis=-1)
    top_w = top_p / jnp.sum(top_p, axis=-1, keepdims=True) * g_weight[:, None]
    e_weight = jnp.sum(jax.nn.one_hot(top_idx, EXPERTS_PER_GROUP, dtype=jnp.float32) * top_w[..., None], axis=1)
    combine = (g_onehot[:, :, None] * e_weight[:, None, :]).reshape(-1, N_EXPERTS).astype(h.dtype)
    y = jnp.zeros_like(t)
    for gi in range(MOE_GROUPS):
        sl = slice(gi * EXPERTS_PER_GROUP, (gi + 1) * EXPERTS_PER_GROUP)
        hid = jax.nn.silu(jnp.einsum('td,edf->tef', t, w_gate[sl])) * jnp.einsum('td,edf->tef', t, w_up[sl])
        y = y + jnp.einsum('tef,efd->td', hid * combine[:, sl, None], w_down[sl])
    return y.reshape(bsz, s, d)


def setup_inputs(seed: int = 0) -> dict:
    key = jax.random.key(seed)
    k = jax.random.split(key, 32)
    L = DEPTH

    def nrm(kk, shape, scale):
        return jax.random.normal(kk, shape, jnp.float32) * scale

    def gain(kk, shape):
        return 1.0 + 0.02 * jax.random.normal(kk, shape, jnp.float32)

    x = nrm(k[0], (BATCH, SEQ, D_MODEL), 1.0)
    c = nrm(k[1], (BATCH, D_MODEL), 1.0)
    offset = jax.random.randint(k[2], (BATCH, 1), 0, 4096, dtype=jnp.int32)
    positions = offset + jnp.arange(SEQ, dtype=jnp.int32)[None, :]
    u = jax.random.uniform(k[3], (L, LRU_WIDTH), jnp.float32, 0.9, 0.999)
    sa = u ** (1.0 / LRU_C)
    lru_lambda = jnp.log(sa) - jnp.log1p(-sa)
    return {
        'x': x,
        'c': c,
        'positions': positions,
        'ada_w': nrm(k[4], (L, D_MODEL, 6 * D_MODEL), 0.5 * D_MODEL ** -0.5),
        'ada_b': nrm(k[5], (L, 6 * D_MODEL), 0.01),
        'norm_mix_g': gain(k[6], (L, D_MODEL)),
        'w_in': nrm(k[7], (L, D_MODEL, IN_COLS), D_MODEL ** -0.5),
        'conv_w': nrm(k[8], (L, CONV_WIDTH, GROUP_W), CONV_WIDTH ** -0.5),
        'mla_q_norm_g': gain(k[9], (L, Q_LORA)),
        'mla_w_uq': nrm(k[10], (L, Q_LORA, MLA_HEADS * MLA_QK), Q_LORA ** -0.5),
        'mla_kv_norm_g': gain(k[11], (L, KV_LORA)),
        'mla_w_ukv': nrm(k[12], (L, KV_LORA, MLA_HEADS * (MLA_NOPE + MLA_V)), KV_LORA ** -0.5),
        'mla_q_qk_g': gain(k[13], (L, MLA_QK)),
        'mla_k_qk_g': gain(k[14], (L, MLA_QK)),
        'lru_conv_w': nrm(k[15], (L, LRU_CONV, LRU_WIDTH), LRU_CONV ** -0.5),
        'lru_conv_b': nrm(k[16], (L, LRU_WIDTH), 0.01),
        'lru_w_a': nrm(k[17], (L, LRU_BLOCKS, LRU_BLOCK, LRU_BLOCK), LRU_BLOCK ** -0.5),
        'lru_b_a': nrm(k[18], (L, LRU_WIDTH), 0.01),
        'lru_w_x': nrm(k[19], (L, LRU_BLOCKS, LRU_BLOCK, LRU_BLOCK), LRU_BLOCK ** -0.5),
        'lru_b_x': nrm(k[20], (L, LRU_WIDTH), 0.01),
        'lru_lambda': lru_lambda,
        'mix_norm_g': gain(k[21], (L, D_MIX)),
        'w_out': nrm(k[22], (L, D_MIX, D_MODEL), D_MIX ** -0.5),
        'norm_ffn_g': gain(k[23], (L, D_MODEL)),
        'router_group_w': nrm(k[24], (L, D_MODEL, MOE_GROUPS), D_MODEL ** -0.5),
        'router_group_b': nrm(k[25], (L, MOE_GROUPS), 0.01),
        'router_expert_w': nrm(k[26], (L, D_MODEL, N_EXPERTS), D_MODEL ** -0.5),
        'router_expert_b': nrm(k[27], (L, N_EXPERTS), 0.01),
        'exp_w_gate': nrm(k[28], (L, N_EXPERTS, D_MODEL, D_EXPERT), D_MODEL ** -0.5),
        'exp_w_up': nrm(k[29], (L, N_EXPERTS, D_MODEL, D_EXPERT), D_MODEL ** -0.5),
        'exp_w_down': nrm(k[30], (L, N_EXPERTS, D_EXPERT, D_MODEL), D_EXPERT ** -0.5),
    }


def reference(x, c, positions, ada_w, ada_b, norm_mix_g, w_in, conv_w, mla_q_norm_g, mla_w_uq,
              mla_kv_norm_g, mla_w_ukv, mla_q_qk_g, mla_k_qk_g, lru_conv_w, lru_conv_b, lru_w_a,
              lru_b_a, lru_w_x, lru_b_x, lru_lambda, mix_norm_g, w_out, norm_ffn_g, router_group_w,
              router_group_b, router_expert_w, router_expert_b, exp_w_gate, exp_w_up, exp_w_down):
    bsz, s, _ = x.shape
    cos_mla, sin_mla = rope_tables(positions, MLA_ROPE)
    cos_ret, sin_ret = rope_tables(positions, RET_DK)
    c_act = jax.nn.silu(c)
    splits = [CONV_COLS, CONV_COLS + MLA_COLS, CONV_COLS + MLA_COLS + RET_COLS]
    for l in range(DEPTH):
        mod = c_act @ ada_w[l] + ada_b[l]
        sh_m, sc_m, gt_m, sh_f, sc_f, gt_f = [m[:, None, :] for m in jnp.split(mod, 6, axis=-1)]
        h = rms_norm(x, norm_mix_g[l]) * (1.0 + sc_m) + sh_m
        u = h @ w_in[l]
        u_conv, u_mla, u_ret, u_lru = jnp.split(u, splits, axis=-1)
        y_conv = short_conv_mixer(u_conv, conv_w[l])
        y_mla = mla_mixer(u_mla, cos_mla, sin_mla, mla_q_norm_g[l], mla_w_uq[l], mla_kv_norm_g[l],
                          mla_w_ukv[l], mla_q_qk_g[l], mla_k_qk_g[l])
        y_ret = retention_mixer(u_ret, cos_ret, sin_ret)
        y_lru = rglru_mixer(u_lru, lru_conv_w[l], lru_conv_b[l], lru_w_a[l], lru_b_a[l],
                            lru_w_x[l], lru_b_x[l], lru_lambda[l])
        y = jnp.stack([y_conv, y_mla, y_ret, y_lru], axis=2)
        y = rms_norm(y, mix_norm_g[l].reshape(N_MIXERS, GROUP_W)).reshape(bsz, s, D_MIX)
        x = x + gt_m * (y @ w_out[l])
        h = rms_norm(x, norm_ffn_g[l]) * (1.0 + sc_f) + sh_f
        x = x + gt_f * hierarchical_moe(h, router_group_w[l], router_group_b[l], router_expert_w[l],
                                        router_expert_b[l], exp_w_gate[l], exp_w_up[l], exp_w_down[l])
    return x
```

```python
import functools
import math

import jax
import jax.numpy as jnp
import numpy as np
from jax import lax
from jax.experimental import pallas as pl
from jax.experimental.pallas import tpu as pltpu

F32 = jnp.float32
BF16 = jnp.bfloat16
HIGHEST = lax.Precision.HIGHEST

D_MODEL = 1024
GROUP_W = 256
HEAD_DIM = 64
N_HEADS = 4
MLA_NOPE = 64
MLA_ROPE = 32
MLA_QK = 96
Q_LORA = 192
KV_LORA = 128
RET_DK = 64
LRU_C = 8.0
MOE_GROUPS = 4
EXPERTS_PER_GROUP = 8
N_EXPERTS = 32
D_EXPERT = 256
ROPE_BASE = 10000.0
EPS = 1e-6

LANES = 128
U_RET, U_LRU, U_CONV, U_MLA = 1024, 512, 768, 384
U_COLS = U_RET + U_LRU + U_CONV + U_MLA
N_FREQ = MLA_ROPE // 2 + RET_DK // 2
NEG_BIG = -1e30
VMEM_LIMIT = 56 * 1024 * 1024


def _cparams(sem):
    return pltpu.CompilerParams(dimension_semantics=sem, vmem_limit_bytes=VMEM_LIMIT)


def _dot(a, b):
    return jnp.dot(a, b, preferred_element_type=F32)


def _dot_nt(a, b):
    return lax.dot_general(a, b, (((1,), (1,)), ((), ())), preferred_element_type=F32)


def _dot_tn(a, b):
    return lax.dot_general(a, b, (((0,), (0,)), ((), ())), preferred_element_type=F32)


def _rms_rows(y, g):
    return y * lax.rsqrt(jnp.mean(y * y, axis=-1, keepdims=True) + EPS) * g


def _sigmoid(x):
    return 1.0 / (1.0 + jnp.exp(-x))


def _rope_kernel(pos_ref, inv_ref, cos_ref, sin_ref):
    ang = pos_ref[0].astype(F32) * inv_ref[...]
    cos_ref[0] = jnp.cos(ang)
    sin_ref[0] = jnp.sin(ang)


def _rope_tables(positions, inv):
    bsz, s = positions.shape
    ts = min(s, 2048)
    out = jax.ShapeDtypeStruct((bsz, N_FREQ, s), F32)
    return pl.pallas_call(
        _rope_kernel,
        grid=(bsz, s // ts),
        in_specs=[pl.BlockSpec((1, 1, ts), lambda b, i: (b, 0, i)),
                  pl.BlockSpec((N_FREQ, 1), lambda b, i: (0, 0))],
        out_specs=[pl.BlockSpec((1, N_FREQ, ts), lambda b, i: (b, 0, i))] * 2,
        out_shape=[out, out],
        compiler_params=_cparams(("parallel", "parallel")),
        name="rope_tables",
    )(positions.reshape(bsz, 1, s), inv)


def _mod_kernel(c_ref, w_ref, b_ref, o_ref):
    c = c_ref[...]
    ca = c * _sigmoid(c)
    o_ref[0] = jnp.dot(ca, w_ref[0], precision=HIGHEST, preferred_element_type=F32) + b_ref[0]


def _modulation(c_pad, ada_w, ada_b):
    nl, d, n = ada_w.shape
    tn = 1536
    return pl.pallas_call(
        _mod_kernel,
        grid=(nl, n // tn),
        in_specs=[pl.BlockSpec((8, d), lambda l, j: (0, 0)),
                  pl.BlockSpec((1, d, tn), lambda l, j: (l, 0, j)),
                  pl.BlockSpec((1, 1, tn), lambda l, j: (l, 0, j))],
        out_specs=pl.BlockSpec((1, 8, tn), lambda l, j: (l, 0, j)),
        out_shape=jax.ShapeDtypeStruct((nl, 8, n), F32),
        compiler_params=_cparams(("parallel", "parallel")),
        name="adaln_mod",
    )(c_pad, ada_w, ada_b.reshape(nl, 1, n))


def _inproj_kernel(x_ref, g_ref, sc_ref, sh_ref, w_ref, u_ref):
    x = x_ref[0]
    h = _rms_rows(x, g_ref[...]) * (1.0 + sc_ref[0]) + sh_ref[0]
    u_ref[0] = _dot(h.astype(BF16), w_ref[...]).astype(BF16)


def _inproj(x, g, sc, sh, w, tm):
    bsz, s, d = x.shape
    vec = pl.BlockSpec((1, 1, d), lambda b, i: (b, 0, 0))
    return pl.pallas_call(
        _inproj_kernel,
        grid=(bsz, s // tm),
        in_specs=[pl.BlockSpec((1, tm, d), lambda b, i: (b, i, 0)),
                  pl.BlockSpec((1, d), lambda b, i: (0, 0)),
                  vec, vec,
                  pl.BlockSpec((d, U_COLS), lambda b, i: (0, 0))],
        out_specs=pl.BlockSpec((1, tm, U_COLS), lambda b, i: (b, i, 0)),
        out_shape=jax.ShapeDtypeStruct((bsz, s, U_COLS), BF16),
        compiler_params=_cparams(("parallel", "parallel")),
        name="inproj",
    )(x, g, sc, sh, w)


def _conv_kernel(u_ref, w_ref, g_ref, y_ref, buf_ref, *, tm):
    @pl.when(pl.program_id(1) == 0)
    def _():
        buf_ref[0:8, :] = jnp.zeros((8, GROUP_W), F32)

    u = u_ref[0].astype(F32)
    b_gate, c_gate, xin = u[:, :GROUP_W], u[:, GROUP_W:2 * GROUP_W], u[:, 2 * GROUP_W:]
    cx = c_gate * xin
    buf_ref[8:8 + tm, :] = cx
    conv = (w_ref[2:3, :] * cx + w_ref[1:2, :] * buf_ref[7:7 + tm, :]
            + w_ref[0:1, :] * buf_ref[6:6 + tm, :])
    buf_ref[0:8, :] = cx[tm - 8:, :]
    y_ref[0] = _rms_rows(b_gate * conv, g_ref[...]).astype(BF16)


def _conv_mixer(u, w, g, tm):
    bsz, s, _ = u.shape
    return pl.pallas_call(
        functools.partial(_conv_kernel, tm=tm),
        grid=(bsz, s // tm),
        in_specs=[pl.BlockSpec((1, tm, U_CONV), lambda b, i: (b, i, (U_RET + U_LRU) // U_CONV)),
                  pl.BlockSpec((3, GROUP_W), lambda b, i: (0, 0)),
                  pl.BlockSpec((1, GROUP_W), lambda b, i: (0, 0))],
        out_specs=pl.BlockSpec((1, tm, GROUP_W), lambda b, i: (b, i, 0)),
        out_shape=jax.ShapeDtypeStruct((bsz, s, GROUP_W), BF16),
        scratch_shapes=[pltpu.VMEM((tm + 8, GROUP_W), F32)],
        compiler_params=_cparams(("parallel", "arbitrary")),
        name="conv_mixer",
    )(u, w, g)


def _lru_kernel(u_ref, cw_ref, cb_ref, wa_ref, ba_ref, wx_ref, bx_ref, lam_ref, g_ref,
                y_ref, buf_ref, h_ref, *, tm):
    @pl.when(pl.program_id(1) == 0)
    def _():
        buf_ref[0:8, :] = jnp.zeros((8, GROUP_W), F32)
        h_ref[...] = jnp.zeros((1, GROUP_W), F32)

    u = u_ref[0].astype(F32)
    xraw, gate = u[:, :GROUP_W], u[:, GROUP_W:]
    buf_ref[8:8 + tm, :] = xraw
    xb = (cw_ref[3:4, :] * xraw + cw_ref[2:3, :] * buf_ref[7:7 + tm, :]
          + cw_ref[1:2, :] * buf_ref[6:6 + tm, :] + cw_ref[0:1, :] * buf_ref[5:5 + tm, :]
          + cb_ref[...])
    buf_ref[0:8, :] = xraw[tm - 8:, :]

    xbb = xb.astype(BF16)
    r = _sigmoid(_dot(xbb, wa_ref[...]) + ba_ref[...])
    i = _sigmoid(_dot(xbb, wx_ref[...]) + bx_ref[...])
    nlam = -lam_ref[...]
    softplus = jnp.maximum(nlam, 0.0) + jnp.log(1.0 + jnp.exp(-jnp.abs(nlam)))
    log_a = (-LRU_C) * r * softplus
    a = jnp.exp(log_a)
    b = jnp.sqrt(1.0 - a * a) * (i * xb)

    row = lax.broadcasted_iota(jnp.int32, (tm, GROUP_W), 0)
    d = 1
    while d < tm:
        keep = row >= d
        a_sh = jnp.where(keep, pltpu.roll(a, d, 0), 1.0)
        b_sh = jnp.where(keep, pltpu.roll(b, d, 0), 0.0)
        b = a * b_sh + b
        a = a * a_sh
        d *= 2
    h = a * h_ref[...] + b
    h_ref[...] = h[tm - 1:tm, :]

    gelu = 0.5 * gate * (1.0 + jnp.tanh(math.sqrt(2.0 / math.pi) * (gate + 0.044715 * gate * gate * gate)))
    y_ref[0] = _rms_rows(h * gelu, g_ref[...]).astype(BF16)


def _lru_mixer(u, cw, cb, wa, ba, wx, bx, lam, g, tm):
    bsz, s, _ = u.shape
    row = pl.BlockSpec((1, GROUP_W), lambda b, i: (0, 0))
    mat = pl.BlockSpec((GROUP_W, GROUP_W), lambda b, i: (0, 0))
    return pl.pallas_call(
        functools.partial(_lru_kernel, tm=tm),
        grid=(bsz, s // tm),
        in_specs=[pl.BlockSpec((1, tm, U_LRU), lambda b, i: (b, i, U_RET // U_LRU)),
                  pl.BlockSpec((4, GROUP_W), lambda b, i: (0, 0)),
                  row, mat, row, mat, row, row, row],
        out_specs=pl.BlockSpec((1, tm, GROUP_W), lambda b, i: (b, i, 0)),
        out_shape=jax.ShapeDtypeStruct((bsz, s, GROUP_W), BF16),
        scratch_shapes=[pltpu.VMEM((tm + 8, GROUP_W), F32), pltpu.VMEM((1, GROUP_W), F32)],
        compiler_params=_cparams(("parallel", "arbitrary")),
        name="lru_mixer",
    )(u, cw, cb, wa, ba, wx, bx, lam, g)


def _ret_kernel(u_ref, cos_ref, sin_ref, inner_ref, qd_ref, kd_ref, cd_ref, bm_ref, gm_ref,
                mq_ref, mv_ref, g_ref, y_ref, st_ref):
    @pl.when(pl.program_id(1) == 0)
    def _():
        st_ref[...] = jnp.zeros((GROUP_W, GROUP_W), F32)

    u = u_ref[0].astype(F32)
    q, k = u[:, :GROUP_W], u[:, GROUP_W:2 * GROUP_W]
    v, gate = u[:, 2 * GROUP_W:3 * GROUP_W], u[:, 3 * GROUP_W:]
    cos, sin = cos_ref[0], sin_ref[0]

    def rope(t):
        t1, t2 = t[:, :LANES], t[:, LANES:]
        return jnp.concatenate([t1 * cos - t2 * sin, t2 * cos + t1 * sin], axis=-1)

    qr = rope(q)
    kr = rope(k) * (RET_DK ** -0.5)
    krb = kr.astype(BF16)
    vb = v.astype(BF16)
    state = st_ref[...]
    o = _dot(qr.astype(BF16), state.astype(BF16)) * qd_ref[...]
    for h in range(N_HEADS):
        qh = (qr * mq_ref[h]).astype(BF16)
        sc = _dot_nt(qh, krb) * inner_ref[h]
        o = o + _dot(sc.astype(BF16), vb) * mv_ref[h]
    st_ref[...] = state * cd_ref[...] + bm_ref[...] * _dot_tn((kr * kd_ref[...]).astype(BF16), vb)

    gm = gm_ref[...]
    o_hi = o.astype(BF16)
    o_lo = (o - o_hi.astype(F32)).astype(BF16)
    mu = _dot(o_hi, gm) + _dot(o_lo, gm)
    dlt = o - mu
    d2 = dlt * dlt
    d2_hi = d2.astype(BF16)
    d2_lo = (d2 - d2_hi.astype(F32)).astype(BF16)
    var = _dot(d2_hi, gm) + _dot(d2_lo, gm)
    y = dlt * lax.rsqrt(var + EPS)
    y = gate * _sigmoid(gate) * y
    y_ref[0] = _rms_rows(y, g_ref[...]).astype(BF16)


def _ret_mixer(u, cos_r, sin_r, consts, g, chunk):
    bsz, s, _ = u.shape
    inner, qd, kd, cd, bm, gm, mq, mv = consts
    full = lambda shape: pl.BlockSpec(shape, lambda b, i: (0,) * len(shape))
    tab = pl.BlockSpec((1, chunk, LANES), lambda b, i: (b, i, 0))
    return pl.pallas_call(
        _ret_kernel,
        grid=(bsz, s // chunk),
        in_specs=[pl.BlockSpec((1, chunk, U_RET), lambda b, i: (b, i, 0)), tab, tab,
                  full(inner.shape), full(qd.shape), full(kd.shape), full(cd.shape),
                  full(bm.shape), full(gm.shape), full(mq.shape), full(mv.shape),
                  full((1, GROUP_W))],
        out_specs=pl.BlockSpec((1, chunk, GROUP_W), lambda b, i: (b, i, 0)),
        out_shape=jax.ShapeDtypeStruct((bsz, s, GROUP_W), BF16),
        scratch_shapes=[pltpu.VMEM((GROUP_W, GROUP_W), F32)],
        compiler_params=_cparams(("parallel", "arbitrary")),
        name="ret_mixer",
    )(u, cos_r, sin_r, inner, qd, kd, cd, bm, gm, mq, mv, g)


def _ret_consts(chunk):
    nh = N_HEADS
    log_g = jnp.log(1.0 - 2.0 ** (-5.0 - jnp.arange(nh, dtype=F32)))
    idx = jnp.arange(chunk, dtype=F32)
    rel = idx[:, None] - idx[None, :]
    inner = jnp.where(rel >= 0, jnp.exp(log_g[:, None, None] * jnp.maximum(rel, 0.0)), 0.0)
    v_head = jnp.arange(GROUP_W) // HEAD_DIM
    q_head = (jnp.arange(GROUP_W) % LANES) // (RET_DK // 2)
    qd = jnp.exp(log_g[v_head][None, :] * (idx[:, None] + 1.0))
    kd = jnp.exp(log_g[q_head][None, :] * (chunk - 1.0 - idx[:, None]))
    cd = jnp.exp(log_g[v_head] * chunk)[None, :]
    bm = (q_head[:, None] == v_head[None, :]).astype(F32)
    gm = ((v_head[:, None] == v_head[None, :]).astype(F32) / HEAD_DIM).astype(BF16)
    mq = (q_head[None, :] == jnp.arange(nh)[:, None]).astype(F32)[:, None, :]
    mv = (v_head[None, :] == jnp.arange(nh)[:, None]).astype(F32)[:, None, :]
    return inner, qd, kd, cd, bm, gm, mq, mv


def _mla_prep_kernel(u_ref, cos_ref, sin_ref, mu_ref, invu_ref, gu_ref, wbig_ref, sq_ref,
                     invq_ref, gq_ref, gqs_ref, gk_ref, onev_ref, q_ref, k_ref, v_ref):
    x = u_ref[0].astype(F32)
    ss = _dot((x * x).astype(BF16), mu_ref[...]) * invu_ref[...]
    xn = (x * lax.rsqrt(ss + EPS) * gu_ref[...]).astype(BF16)
    big = _dot(xn, wbig_ref[...])
    hw = N_HEADS * LANES
    q, qs, kn, v = big[:, :hw], big[:, hw:2 * hw], big[:, 2 * hw:3 * hw], big[:, 3 * hw:4 * hw]
    kr, krs = big[:, 4 * hw:4 * hw + LANES], big[:, 4 * hw + LANES:]
    cos, sin = cos_ref[0], sin_ref[0]
    rq = lax.rsqrt(_dot((q * q).astype(BF16), sq_ref[...]) * invq_ref[...] + EPS)
    rk = lax.rsqrt(_dot((kn * kn).astype(BF16), sq_ref[...]) * invq_ref[...] + EPS)
    krot = kr * cos + krs * sin
    for h in range(N_HEADS):
        sl = slice(h * LANES, (h + 1) * LANES)
        qh = (q[:, sl] * gq_ref[...] * cos + qs[:, sl] * gqs_ref[...] * sin) * rq[:, sl]
        q_ref[0, h] = qh.astype(BF16)
        k_ref[0, h] = (kn[:, sl] * gk_ref[...] * rk[:, sl] + krot).astype(BF16)
        v_ref[0, h] = (v[:, sl] + onev_ref[...]).astype(BF16)


def _mla_prep(u, cos_m, sin_m, mu, invu, gu, wbig, sq, invq, gq, gqs, gk, onev, tm):
    bsz, s, _ = u.shape
    full = lambda a: pl.BlockSpec(a.shape, lambda b, i: (0,) * a.ndim)
    tab = pl.BlockSpec((1, tm, LANES), lambda b, i: (b, i, 0))
    out = jax.ShapeDtypeStruct((bsz, N_HEADS, s, LANES), BF16)
    ospec = pl.BlockSpec((1, N_HEADS, tm, LANES), lambda b, i: (b, 0, i, 0))
    return pl.pallas_call(
        _mla_prep_kernel,
        grid=(bsz, s // tm),
        in_specs=[pl.BlockSpec((1, tm, U_MLA), lambda b, i: (b, i, (U_COLS - U_MLA) // U_MLA)),
                  tab, tab, full(mu), full(invu), full(gu), full(wbig), full(sq), full(invq),
                  full(gq), full(gqs), full(gk), full(onev)],
        out_specs=[ospec, ospec, ospec],
        out_shape=[out, out, out],
        compiler_params=_cparams(("parallel", "parallel")),
        name="mla_prep",
    )(u, cos_m, sin_m, mu, invu, gu, wbig, sq, invq, gq, gqs, gk, onev)


def _flash_kernel(q_ref, k_ref, v_ref, o_ref, m_ref, acc_ref, *, tq, tk):
    qi = pl.program_id(2)
    q = q_ref[0, 0]
    m_ref[...] = jnp.full((tq, 1), NEG_BIG, F32)
    acc_ref[...] = jnp.zeros((tq, LANES), F32)

    def step(kv_start, masked):
        kb = k_ref[0, 0, pl.ds(kv_start, tk), :]
        vb = v_ref[0, 0, pl.ds(kv_start, tk), :]
        s = _dot_nt(q, kb)
        if masked:
            row = qi * tq + lax.broadcasted_iota(jnp.int32, (tq, tk), 0)
            col = kv_start + lax.broadcasted_iota(jnp.int32, (tq, tk), 1)
            s = jnp.where(col <= row, s, NEG_BIG)
        m_prev = m_ref[...]
        m_new = jnp.maximum(m_prev, jnp.max(s, axis=-1, keepdims=True))
        alpha = jnp.exp2(m_prev - m_new)
        p = jnp.exp2(s - m_new)
        acc_ref[...] = alpha * acc_ref[...] + _dot(p.astype(BF16), vb)
        m_ref[...] = m_new

    n_diag = tq // tk
    n_full = qi * n_diag

    def body(j, carry):
        step(pl.multiple_of(j * tk, tk), False)
        return carry

    lax.fori_loop(0, n_full, body, 0)
    for j in range(n_diag):
        step(pl.multiple_of((n_full + j) * tk, tk), True)

    acc = acc_ref[...]
    lane = lax.broadcasted_iota(jnp.int32, (tq, LANES), 1)
    denom = jnp.sum(jnp.where(lane == HEAD_DIM, acc, 0.0), axis=-1, keepdims=True)
    o_ref[0, 0] = jnp.where(lane < HEAD_DIM, acc / denom, 0.0).astype(BF16)


def _flash_attention(q, k, v, tq, tk):
    bsz, nh, s, _ = q.shape
    kv_spec = pl.BlockSpec((1, 1, s, LANES), lambda b, h, i: (b, h, 0, 0))
    blk = pl.BlockSpec((1, 1, tq, LANES), lambda b, h, i: (b, h, i, 0))
    return pl.pallas_call(
        functools.partial(_flash_kernel, tq=tq, tk=tk),
        grid=(bsz, nh, s // tq),
        in_specs=[blk, kv_spec, kv_spec],
        out_specs=blk,
        out_shape=jax.ShapeDtypeStruct((bsz, nh, s, LANES), BF16),
        scratch_shapes=[pltpu.VMEM((tq, 1), F32), pltpu.VMEM((tq, LANES), F32)],
        compiler_params=_cparams(("parallel", "parallel", "arbitrary")),
        name="flash_attention",
    )(q, k, v)


def _outproj_kernel(x_ref, yc_ref, om_ref, yr_ref, yl_ref, wc_ref, wm_ref, wr_ref, wl_ref,
                    gmla_ref, gt_ref, gf_ref, scf_ref, shf_ref, wrt_ref, brt_ref,
                    xo_ref, h_ref, chi_ref, clo_ref):
    om = [om_ref[0, h].astype(F32) for h in range(N_HEADS)]
    ssq = om[0] * om[0]
    for h in range(1, N_HEADS):
        ssq = ssq + om[h] * om[h]
    r_mla = lax.rsqrt(jnp.sum(ssq, axis=-1, keepdims=True) / GROUP_W + EPS)
    y = _dot(yc_ref[0], wc_ref[...]) + _dot(yr_ref[0], wr_ref[...]) + _dot(yl_ref[0], wl_ref[...])
    for h in range(N_HEADS):
        y = y + _dot((om[h] * r_mla * gmla_ref[h]).astype(BF16), wm_ref[h])
    x = x_ref[0] + gt_ref[0] * y
    xo_ref[0] = x
    hf = _rms_rows(x, gf_ref[...]) * (1.0 + scf_ref[0]) + shf_ref[0]
    h_ref[0] = hf.astype(BF16)

    lg = jnp.dot(hf, wrt_ref[...], precision=HIGHEST, preferred_element_type=F32)
    tm = lg.shape[0]
    lane = lax.broadcasted_iota(jnp.int32, (tm, LANES), 1)
    bias = brt_ref[...]
    is_g = (lane >= N_EXPERTS) & (lane < N_EXPERTS + MOE_GROUPS)
    is_e = lane < N_EXPERTS

    def first_argmax(val):
        mx = jnp.max(val, axis=-1, keepdims=True)
        return jnp.min(jnp.where(val == mx, lane, LANES), axis=-1, keepdims=True)

    gl = jnp.where(is_g, lg, NEG_BIG)
    ge = jnp.exp(gl - jnp.max(gl, axis=-1, keepdims=True))
    gp = ge / jnp.sum(ge, axis=-1, keepdims=True)
    g_idx = first_argmax(jnp.where(is_g, gp + bias, NEG_BIG))
    g_weight = jnp.sum(jnp.where(lane == g_idx, gp, 0.0), axis=-1, keepdims=True)
    in_group = is_e & ((lane // EXPERTS_PER_GROUP) == (g_idx - N_EXPERTS))
    el = jnp.where(in_group, lg, NEG_BIG)
    ee = jnp.exp(el - jnp.max(el, axis=-1, keepdims=True))
    ep = ee / jnp.sum(ee, axis=-1, keepdims=True)
    score = jnp.where(in_group, ep + bias, NEG_BIG)
    i1 = first_argmax(score)
    sel1 = lane == i1
    i2 = first_argmax(jnp.where(sel1, NEG_BIG, score))
    sel2 = lane == i2
    p1 = jnp.sum(jnp.where(sel1, ep, 0.0), axis=-1, keepdims=True)
    p2 = jnp.sum(jnp.where(sel2, ep, 0.0), axis=-1, keepdims=True)
    psum = p1 + p2
    comb = jnp.where(sel1, p1 / psum * g_weight, 0.0) + jnp.where(sel2, p2 / psum * g_weight, 0.0)
    c_hi = comb.astype(BF16)
    chi_ref[0] = c_hi
    clo_ref[0] = (comb - c_hi.astype(F32)).astype(BF16)


def _outproj(x, yc, om, yr, yl, wc, wm, wr, wl, gmla, gt, gf, scf, shf, wrt, brt, tm):
    bsz, s, d = x.shape
    full = lambda a: pl.BlockSpec(a.shape, lambda b, i: (0,) * a.ndim)
    tok = lambda w: pl.BlockSpec((1, tm, w), lambda b, i: (b, i, 0))
    vec = pl.BlockSpec((1, 1, d), lambda b, i: (b, 0, 0))
    return pl.pallas_call(
        _outproj_kernel,
        grid=(bsz, s // tm),
        in_specs=[tok(d), tok(GROUP_W),
                  pl.BlockSpec((1, N_HEADS, tm, LANES), lambda b, i: (b, 0, i, 0)),
                  tok(GROUP_W), tok(GROUP_W),
                  full(wc), full(wm), full(wr), full(wl), full(gmla), vec, full(gf), vec, vec,
                  full(wrt), full(brt)],
        out_specs=[tok(d), tok(d), tok(LANES), tok(LANES)],
        out_shape=[jax.ShapeDtypeStruct((bsz, s, d), F32), jax.ShapeDtypeStruct((bsz, s, d), BF16),
                   jax.ShapeDtypeStruct((bsz, s, LANES), BF16), jax.ShapeDtypeStruct((bsz, s, LANES), BF16)],
        compiler_params=_cparams(("parallel", "parallel")),
        name="outproj_router",
    )(x, yc, om, yr, yl, wc, wm, wr, wl, gmla, gt, gf, scf, shf, wrt, brt)


def _moe_kernel(x_ref, h_ref, chi_ref, clo_ref, wg_ref, wu_ref, wd_ref, gt_ref, o_ref, acc_ref):
    e = pl.program_id(2)

    @pl.when(e == 0)
    def _():
        acc_ref[...] = jnp.zeros(acc_ref.shape, F32)

    h = h_ref[0]
    onehot = (lax.broadcasted_iota(jnp.int32, (LANES, D_EXPERT), 0) == e).astype(BF16)
    cw = _dot(chi_ref[0], onehot) + _dot(clo_ref[0], onehot)
    gate = _dot(h, wg_ref[0])
    hid = gate * _sigmoid(gate) * _dot(h, wu_ref[0]) * cw
    acc_ref[...] += _dot(hid.astype(BF16), wd_ref[0])

    @pl.when(e == N_EXPERTS - 1)
    def _():
        o_ref[0] = x_ref[0] + gt_ref[0] * acc_ref[...]


def _moe(x, h, chi, clo, wg, wu, wd, gt, tm):
    bsz, s, d = x.shape
    tok = lambda w: pl.BlockSpec((1, tm, w), lambda b, i, e: (b, i, 0))
    return pl.pallas_call(
        _moe_kernel,
        grid=(bsz, s // tm, N_EXPERTS),
        in_specs=[tok(d), tok(d), tok(LANES), tok(LANES),
                  pl.BlockSpec((1, d, D_EXPERT), lambda b, i, e: (e, 0, 0)),
                  pl.BlockSpec((1, d, D_EXPERT), lambda b, i, e: (e, 0, 0)),
                  pl.BlockSpec((1, D_EXPERT, d), lambda b, i, e: (e, 0, 0)),
                  pl.BlockSpec((1, 1, d), lambda b, i, e: (b, 0, 0))],
        out_specs=tok(d),
        out_shape=jax.ShapeDtypeStruct((bsz, s, d), F32),
        scratch_shapes=[pltpu.VMEM((tm, d), F32)],
        compiler_params=_cparams(("parallel", "parallel", "arbitrary")),
        name="moe_dense",
    )(x, h, chi, clo, wg, wu, wd, gt)


def _layer_weights(l, w_in, mla_q_norm_g, mla_w_uq, mla_kv_norm_g, mla_w_ukv, mla_q_qk_g,
                   mla_k_qk_g, lru_w_a, lru_w_x, mix_norm_g, w_out, router_group_w,
                   router_group_b, router_expert_w, router_expert_b):
    half = RET_DK // 2
    perm = np.concatenate([np.arange(half) + HEAD_DIM * h for h in range(N_HEADS)]
                          + [np.arange(half) + half + HEAD_DIM * h for h in range(N_HEADS)])
    w = w_in[l]
    o_mla, o_ret, o_lru = U_CONV, U_CONV + 352, U_CONV + 352 + U_RET
    w_ret = w[:, o_ret:o_ret + U_RET]
    w_ret = jnp.concatenate([w_ret[:, perm], w_ret[:, GROUP_W + perm], w_ret[:, 2 * GROUP_W:]], axis=1)
    w_all = jnp.concatenate([w_ret, w[:, o_lru:o_lru + U_LRU], w[:, :U_CONV], w[:, o_mla:o_mla + 352],
                             jnp.zeros((D_MODEL, U_MLA - 352), F32)], axis=1).astype(BF16)

    hw = N_HEADS * LANES
    r16 = MLA_ROPE // 2
    wq = mla_w_uq[l].reshape(Q_LORA, N_HEADS, MLA_QK)
    zq = jnp.zeros((Q_LORA, N_HEADS, LANES - MLA_QK), F32)
    q_cols = jnp.concatenate([wq, zq], axis=2).reshape(Q_LORA, hw)
    wq_sw = jnp.concatenate([jnp.zeros((Q_LORA, N_HEADS, MLA_NOPE), F32), wq[:, :, MLA_NOPE + r16:],
                             wq[:, :, MLA_NOPE:MLA_NOPE + r16], zq], axis=2).reshape(Q_LORA, hw)
    wkv = mla_w_ukv[l].reshape(KV_LORA, N_HEADS, MLA_NOPE + HEAD_DIM)
    zk = jnp.zeros((KV_LORA, N_HEADS, LANES - MLA_NOPE), F32)
    k_cols = jnp.concatenate([wkv[:, :, :MLA_NOPE], zk], axis=2).reshape(KV_LORA, hw)
    v_cols = jnp.concatenate([wkv[:, :, MLA_NOPE:], zk], axis=2).reshape(KV_LORA, hw)
    eye = jnp.eye(MLA_ROPE, dtype=F32)
    place = jnp.concatenate([jnp.zeros((MLA_ROPE, MLA_NOPE), F32), eye,
                             jnp.zeros((MLA_ROPE, LANES - MLA_QK), F32)], axis=1)
    eye_sw = jnp.concatenate([eye[:, r16:], eye[:, :r16]], axis=1)
    place_sw = jnp.concatenate([jnp.zeros((MLA_ROPE, MLA_NOPE), F32), eye_sw,
                                jnp.zeros((MLA_ROPE, LANES - MLA_QK), F32)], axis=1)
    n_big = 4 * hw + 2 * LANES
    wbig = jnp.zeros((U_MLA, n_big), F32)
    wbig = wbig.at[:Q_LORA, :hw].set(q_cols).at[:Q_LORA, hw:2 * hw].set(wq_sw)
    wbig = wbig.at[Q_LORA:Q_LORA + KV_LORA, 2 * hw:3 * hw].set(k_cols)
    wbig = wbig.at[Q_LORA:Q_LORA + KV_LORA, 3 * hw:4 * hw].set(v_cols)
    wbig = wbig.at[Q_LORA + KV_LORA:352, 4 * hw:4 * hw + LANES].set(place)
    wbig = wbig.at[Q_LORA + KV_LORA:352, 4 * hw + LANES:].set(place_sw)
    wbig = wbig.astype(BF16)

    gu = jnp.concatenate([mla_q_norm_g[l], mla_kv_norm_g[l], mla_k_qk_g[l][MLA_NOPE:],
                          jnp.zeros((U_MLA - 352,), F32)])[None, :]
    qscale = (MLA_QK ** -0.5) * math.log2(math.e)
    gq_full = mla_q_qk_g[l]
    pad = jnp.zeros((LANES - MLA_QK,), F32)
    gq = (jnp.concatenate([gq_full, pad]) * qscale)[None, :]
    gqs = (jnp.concatenate([jnp.zeros((MLA_NOPE,), F32), gq_full[MLA_NOPE + r16:],
                            gq_full[MLA_NOPE:MLA_NOPE + r16], pad]) * qscale)[None, :]
    gk = jnp.concatenate([mla_k_qk_g[l][:MLA_NOPE], jnp.zeros((LANES - MLA_NOPE,), F32)])[None, :]

    def blockdiag(wb):
        out = jnp.zeros((GROUP_W, GROUP_W), F32)
        for n in range(wb.shape[0]):
            out = out.at[n * HEAD_DIM:(n + 1) * HEAD_DIM, n * HEAD_DIM:(n + 1) * HEAD_DIM].set(wb[n])
        return out.astype(BF16)

    gmix = mix_norm_g[l]
    wo = w_out[l].astype(BF16)
    wm = wo[GROUP_W:2 * GROUP_W].reshape(N_HEADS, HEAD_DIM, D_MODEL)
    wm = jnp.concatenate([wm, jnp.zeros((N_HEADS, LANES - HEAD_DIM, D_MODEL), BF16)], axis=1)
    gmla = jnp.concatenate([gmix[GROUP_W:2 * GROUP_W].reshape(N_HEADS, 1, HEAD_DIM),
                            jnp.zeros((N_HEADS, 1, LANES - HEAD_DIM), F32)], axis=2)
    wrt = jnp.concatenate([router_expert_w[l], router_group_w[l],
                           jnp.zeros((D_MODEL, LANES - N_EXPERTS - MOE_GROUPS), F32)], axis=1)
    brt = jnp.concatenate([router_expert_b[l], router_group_b[l],
                           jnp.zeros((LANES - N_EXPERTS - MOE_GROUPS,), F32)])[None, :]
    return dict(w_all=w_all, wbig=wbig, gu=gu, gq=gq, gqs=gqs, gk=gk,
                wa=blockdiag(lru_w_a[l]), wx=blockdiag(lru_w_x[l]),
                g_conv=gmix[None, :GROUP_W], g_ret=gmix[None, 2 * GROUP_W:3 * GROUP_W],
                g_lru=gmix[None, 3 * GROUP_W:], gmla=gmla,
                wc=wo[:GROUP_W], wm=wm, wr=wo[2 * GROUP_W:3 * GROUP_W], wl=wo[3 * GROUP_W:],
                wrt=wrt, brt=brt)


def _mla_consts():
    seg_u = np.concatenate([np.zeros(Q_LORA), np.ones(KV_LORA), 2 * np.ones(MLA_ROPE),
                            3 * np.ones(U_MLA - 352)])
    mu = jnp.asarray(seg_u[:, None] == seg_u[None, :], BF16)
    invu = jnp.asarray(np.concatenate([np.full(Q_LORA, 1.0 / Q_LORA), np.full(KV_LORA, 1.0 / KV_LORA),
                                       np.full(MLA_ROPE, 1.0 / MLA_ROPE), np.ones(U_MLA - 352)]), F32)[None, :]
    lane = np.arange(N_HEADS * LANES)
    seg_q = (lane // LANES) * 3 + np.where(lane % LANES < MLA_NOPE, 0, np.where(lane % LANES < MLA_QK, 1, 2))
    sq = jnp.asarray(seg_q[:, None] == seg_q[None, :], BF16)
    inv_head = np.concatenate([np.full(MLA_NOPE, 1.0 / MLA_NOPE), np.full(MLA_ROPE, 1.0 / MLA_ROPE),
                               np.ones(LANES - MLA_QK)])
    invq = jnp.asarray(np.tile(inv_head, N_HEADS), F32)[None, :]
    onev = jnp.asarray((np.arange(LANES) == HEAD_DIM).astype(np.float32))[None, :]
    return mu, invu, sq, invq, onev


def kernel(x, c, positions, ada_w, ada_b, norm_mix_g, w_in, conv_w, mla_q_norm_g, mla_w_uq, mla_kv_norm_g, mla_w_ukv, mla_q_qk_g, mla_k_qk_g, lru_conv_w, lru_conv_b, lru_w_a, lru_b_a, lru_w_x, lru_b_x, lru_lambda, mix_norm_g, w_out, norm_ffn_g, router_group_w, router_group_b, router_expert_w, router_expert_b, exp_w_gate, exp_w_up, exp_w_down):
    bsz, s, d = x.shape
    depth = ada_w.shape[0]
    tm = min(512, s)
    chunk = min(256, s)
    tq = min(1024, s)
    tk = min(512, s)
    tmoe = min(1024, s)

    r16, r32 = MLA_ROPE // 2, RET_DK // 2
    inv = jnp.concatenate([1.0 / (ROPE_BASE ** (jnp.arange(0, MLA_ROPE, 2, dtype=F32) / MLA_ROPE)),
                           1.0 / (ROPE_BASE ** (jnp.arange(0, RET_DK, 2, dtype=F32) / RET_DK))])[:, None]
    cos_t, sin_t = _rope_tables(positions, inv)
    cos_t, sin_t = cos_t.transpose(0, 2, 1), sin_t.transpose(0, 2, 1)
    ones = jnp.ones((bsz, s, MLA_NOPE), F32)
    zpad = jnp.zeros((bsz, s, LANES - MLA_QK), F32)
    cos_m = jnp.concatenate([ones, cos_t[..., :r16], cos_t[..., :r16], zpad], axis=-1)
    sin_m = jnp.concatenate([0.0 * ones, -sin_t[..., :r16], sin_t[..., :r16], zpad], axis=-1)
    cos_r = jnp.tile(cos_t[..., r16:], (1, 1, N_HEADS))
    sin_r = jnp.tile(sin_t[..., r16:], (1, 1, N_HEADS))

    c_pad = jnp.concatenate([c, jnp.zeros((8 - bsz, d), F32)], axis=0)
    mod = _modulation(c_pad, ada_w, ada_b)[:, :bsz]
    ret_consts = _ret_consts(chunk)
    mu, invu, sq, invq, onev = _mla_consts()

    for l in range(depth):
        sh_m, sc_m, gt_m, sh_f, sc_f, gt_f = [m[:, None, :] for m in jnp.split(mod[l], 6, axis=-1)]
        lw = _layer_weights(l, w_in, mla_q_norm_g, mla_w_uq, mla_kv_norm_g, mla_w_ukv, mla_q_qk_g,
                            mla_k_qk_g, lru_w_a, lru_w_x, mix_norm_g, w_out, router_group_w,
                            router_group_b, router_expert_w, router_expert_b)
        u = _inproj(x, norm_mix_g[l][None, :], sc_m, sh_m, lw["w_all"], tm)
        y_conv = _conv_mixer(u, conv_w[l], lw["g_conv"], tm)
        y_lru = _lru_mixer(u, lru_conv_w[l], lru_conv_b[l][None, :], lw["wa"], lru_b_a[l][None, :],
                           lw["wx"], lru_b_x[l][None, :], lru_lambda[l][None, :], lw["g_lru"], tm)
        y_ret = _ret_mixer(u, cos_r, sin_r, ret_consts, lw["g_ret"], chunk)
        q, k, v = _mla_prep(u, cos_m, sin_m, mu, invu, lw["gu"], lw["wbig"], sq, invq,
                            lw["gq"], lw["gqs"], lw["gk"], onev, tm)
        o_mla = _flash_attention(q, k, v, tq, tk)
        x, h, chi, clo = _outproj(x, y_conv, o_mla, y_ret, y_lru, lw["wc"], lw["wm"], lw["wr"],
                                  lw["wl"], lw["gmla"], gt_m, norm_ffn_g[l][None, :], sc_f, sh_f,
                                  lw["wrt"], lw["brt"], tm)
        x = _moe(x, h, chi, clo, exp_w_gate[l].astype(BF16), exp_w_up[l].astype(BF16),
                 exp_w_down[l].astype(BF16), gt_f, tmoe)
    return x
```

```python
import functools
import math

import jax
import jax.numpy as jnp
import numpy as np
from jax import lax
from jax.experimental import pallas as pl
from jax.experimental.pallas import tpu as pltpu

F32 = jnp.float32
BF16 = jnp.bfloat16
HIGHEST = lax.Precision.HIGHEST

D_MODEL = 1024
GROUP_W = 256
HEAD_DIM = 64
N_HEADS = 4
MLA_NOPE = 64
MLA_ROPE = 32
MLA_QK = 96
Q_LORA = 192
KV_LORA = 128
RET_DK = 64
LRU_C = 8.0
MOE_GROUPS = 4
EXPERTS_PER_GROUP = 8
N_EXPERTS = 32
D_EXPERT = 256
ROPE_BASE = 10000.0
EPS = 1e-6

LANES = 128
MXU_DIM = 256
U_RET, U_LRU, U_CONV, U_MLA = 1024, 512, 768, 384
U_COLS = U_RET + U_LRU + U_CONV + U_MLA
N_FREQ = MLA_ROPE // 2 + RET_DK // 2
NEG_BIG = -1e30
VMEM_LIMIT = 56 * 1024 * 1024


def _cparams(sem):
    return pltpu.CompilerParams(dimension_semantics=sem, vmem_limit_bytes=VMEM_LIMIT)


def _dot(a, b):
    return jnp.dot(a, b, preferred_element_type=F32)


def _dot_nt(a, b):
    return lax.dot_general(a, b, (((1,), (1,)), ((), ())), preferred_element_type=F32)


def _dot_tn(a, b):
    return lax.dot_general(a, b, (((0,), (0,)), ((), ())), preferred_element_type=F32)


def _rms_rows(y, g):
    return y * lax.rsqrt(jnp.mean(y * y, axis=-1, keepdims=True) + EPS) * g


def _sigmoid(x):
    return 1.0 / (1.0 + jnp.exp(-x))


def _rope_kernel(pos_ref, inv_ref, cos_ref, sin_ref):
    ang = pos_ref[0].astype(F32) * inv_ref[...]
    cos_ref[0] = jnp.cos(ang)
    sin_ref[0] = jnp.sin(ang)


def _rope_tables(positions, inv):
    bsz, s = positions.shape
    ts = min(s, 2048)
    out = jax.ShapeDtypeStruct((bsz, N_FREQ, s), F32)
    return pl.pallas_call(
        _rope_kernel,
        grid=(bsz, s // ts),
        in_specs=[pl.BlockSpec((1, 1, ts), lambda b, i: (b, 0, i)),
                  pl.BlockSpec((N_FREQ, 1), lambda b, i: (0, 0))],
        out_specs=[pl.BlockSpec((1, N_FREQ, ts), lambda b, i: (b, 0, i))] * 2,
        out_shape=[out, out],
        compiler_params=_cparams(("parallel", "parallel")),
        name="rope_tables",
    )(positions.reshape(bsz, 1, s), inv)


def _mod_kernel(c_ref, w_ref, b_ref, o_ref):
    c = c_ref[...]
    ca = c * _sigmoid(c)
    o_ref[0] = jnp.dot(ca, w_ref[0], precision=HIGHEST, preferred_element_type=F32) + b_ref[0]


def _modulation(c_pad, ada_w, ada_b):
    nl, d, n = ada_w.shape
    tn = 1536
    return pl.pallas_call(
        _mod_kernel,
        grid=(nl, n // tn),
        in_specs=[pl.BlockSpec((8, d), lambda l, j: (0, 0)),
                  pl.BlockSpec((1, d, tn), lambda l, j: (l, 0, j)),
                  pl.BlockSpec((1, 1, tn), lambda l, j: (l, 0, j))],
        out_specs=pl.BlockSpec((1, 8, tn), lambda l, j: (l, 0, j)),
        out_shape=jax.ShapeDtypeStruct((nl, 8, n), F32),
        compiler_params=_cparams(("parallel", "parallel")),
        name="adaln_mod",
    )(c_pad, ada_w, ada_b.reshape(nl, 1, n))


def _inproj_kernel(x_ref, g_ref, sc_ref, sh_ref, w_ref, u_ref):
    x = x_ref[0]
    h = _rms_rows(x, g_ref[...]) * (1.0 + sc_ref[0]) + sh_ref[0]
    u_ref[0] = _dot(h.astype(BF16), w_ref[...]).astype(BF16)


def _inproj(x, g, sc, sh, w, tm):
    bsz, s, d = x.shape
    vec = pl.BlockSpec((1, 1, d), lambda b, i: (b, 0, 0))
    return pl.pallas_call(
        _inproj_kernel,
        grid=(bsz, s // tm),
        in_specs=[pl.BlockSpec((1, tm, d), lambda b, i: (b, i, 0)),
                  pl.BlockSpec((1, d), lambda b, i: (0, 0)),
                  vec, vec,
                  pl.BlockSpec((d, U_COLS), lambda b, i: (0, 0))],
        out_specs=pl.BlockSpec((1, tm, U_COLS), lambda b, i: (b, i, 0)),
        out_shape=jax.ShapeDtypeStruct((bsz, s, U_COLS), BF16),
        compiler_params=_cparams(("parallel", "parallel")),
        name="inproj",
    )(x, g, sc, sh, w)


def _conv_kernel(u_ref, w_ref, g_ref, y_ref, buf_ref, *, tm):
    @pl.when(pl.program_id(1) == 0)
    def _():
        buf_ref[0:8, :] = jnp.zeros((8, GROUP_W), F32)

    u = u_ref[0].astype(F32)
    b_gate, c_gate, xin = u[:, :GROUP_W], u[:, GROUP_W:2 * GROUP_W], u[:, 2 * GROUP_W:]
    cx = c_gate * xin
    buf_ref[8:8 + tm, :] = cx
    conv = (w_ref[2:3, :] * cx + w_ref[1:2, :] * buf_ref[7:7 + tm, :]
            + w_ref[0:1, :] * buf_ref[6:6 + tm, :])
    buf_ref[0:8, :] = cx[tm - 8:, :]
    y_ref[0] = _rms_rows(b_gate * conv, g_ref[...]).astype(BF16)


def _conv_mixer(u, w, g, tm):
    bsz, s, _ = u.shape
    return pl.pallas_call(
        functools.partial(_conv_kernel, tm=tm),
        grid=(bsz, s // tm),
        in_specs=[pl.BlockSpec((1, tm, U_CONV), lambda b, i: (b, i, (U_RET + U_LRU) // U_CONV)),
                  pl.BlockSpec((3, GROUP_W), lambda b, i: (0, 0)),
                  pl.BlockSpec((1, GROUP_W), lambda b, i: (0, 0))],
        out_specs=pl.BlockSpec((1, tm, GROUP_W), lambda b, i: (b, i, 0)),
        out_shape=jax.ShapeDtypeStruct((bsz, s, GROUP_W), BF16),
        scratch_shapes=[pltpu.VMEM((tm + 8, GROUP_W), F32)],
        compiler_params=_cparams(("parallel", "arbitrary")),
        name="conv_mixer",
    )(u, w, g)


def _lru_kernel(u_ref, cw_ref, cb_ref, wa_ref, ba_ref, wx_ref, bx_ref, lam_ref, g_ref,
                y_ref, buf_ref, h_ref, *, tm):
    @pl.when(pl.program_id(1) == 0)
    def _():
        buf_ref[0:8, :] = jnp.zeros((8, GROUP_W), F32)
        h_ref[...] = jnp.zeros((1, GROUP_W), F32)

    u = u_ref[0].astype(F32)
    xraw, gate = u[:, :GROUP_W], u[:, GROUP_W:]
    buf_ref[8:8 + tm, :] = xraw
    xb = (cw_ref[3:4, :] * xraw + cw_ref[2:3, :] * buf_ref[7:7 + tm, :]
          + cw_ref[1:2, :] * buf_ref[6:6 + tm, :] + cw_ref[0:1, :] * buf_ref[5:5 + tm, :]
          + cb_ref[...])
    buf_ref[0:8, :] = xraw[tm - 8:, :]

    xbb = xb.astype(BF16)
    r = _sigmoid(_dot(xbb, wa_ref[...]) + ba_ref[...])
    i = _sigmoid(_dot(xbb, wx_ref[...]) + bx_ref[...])
    nlam = -lam_ref[...]
    softplus = jnp.maximum(nlam, 0.0) + jnp.log(1.0 + jnp.exp(-jnp.abs(nlam)))
    log_a = (-LRU_C) * r * softplus
    a = jnp.exp(log_a)
    b = jnp.sqrt(1.0 - a * a) * (i * xb)

    row = lax.broadcasted_iota(jnp.int32, (tm, GROUP_W), 0)
    d = 1
    while d < tm:
        keep = row >= d
        a_sh = jnp.where(keep, pltpu.roll(a, d, 0), 1.0)
        b_sh = jnp.where(keep, pltpu.roll(b, d, 0), 0.0)
        b = a * b_sh + b
        a = a * a_sh
        d *= 2
    h = a * h_ref[...] + b
    h_ref[...] = h[tm - 1:tm, :]

    gelu = 0.5 * gate * (1.0 + jnp.tanh(math.sqrt(2.0 / math.pi) * (gate + 0.044715 * gate * gate * gate)))
    y_ref[0] = _rms_rows(h * gelu, g_ref[...]).astype(BF16)


def _lru_mixer(u, cw, cb, wa, ba, wx, bx, lam, g, tm):
    bsz, s, _ = u.shape
    row = pl.BlockSpec((1, GROUP_W), lambda b, i: (0, 0))
    mat = pl.BlockSpec((GROUP_W, GROUP_W), lambda b, i: (0, 0))
    return pl.pallas_call(
        functools.partial(_lru_kernel, tm=tm),
        grid=(bsz, s // tm),
        in_specs=[pl.BlockSpec((1, tm, U_LRU), lambda b, i: (b, i, U_RET // U_LRU)),
                  pl.BlockSpec((4, GROUP_W), lambda b, i: (0, 0)),
                  row, mat, row, mat, row, row, row],
        out_specs=pl.BlockSpec((1, tm, GROUP_W), lambda b, i: (b, i, 0)),
        out_shape=jax.ShapeDtypeStruct((bsz, s, GROUP_W), BF16),
        scratch_shapes=[pltpu.VMEM((tm + 8, GROUP_W), F32), pltpu.VMEM((1, GROUP_W), F32)],
        compiler_params=_cparams(("parallel", "arbitrary")),
        name="lru_mixer",
    )(u, cw, cb, wa, ba, wx, bx, lam, g)


def _ret_kernel(u_ref, cos_ref, sin_ref, inner_ref, qd_ref, kd_ref, cd_ref, bm_ref, gm_ref,
                mq_ref, mv_ref, g_ref, y_ref, st_ref):
    @pl.when(pl.program_id(1) == 0)
    def _():
        st_ref[...] = jnp.zeros((GROUP_W, GROUP_W), F32)

    u = u_ref[0].astype(F32)
    q, k = u[:, :GROUP_W], u[:, GROUP_W:2 * GROUP_W]
    v, gate = u[:, 2 * GROUP_W:3 * GROUP_W], u[:, 3 * GROUP_W:]
    cos, sin = cos_ref[0], sin_ref[0]

    def rope(t):
        t1, t2 = t[:, :LANES], t[:, LANES:]
        return jnp.concatenate([t1 * cos - t2 * sin, t2 * cos + t1 * sin], axis=-1)

    qr = rope(q)
    kr = rope(k) * (RET_DK ** -0.5)
    krb = kr.astype(BF16)
    vb = v.astype(BF16)
    state = st_ref[...]
    o = _dot(qr.astype(BF16), state.astype(BF16)) * qd_ref[...]
    for h in range(N_HEADS):
        qh = (qr * mq_ref[h]).astype(BF16)
        sc = _dot_nt(qh, krb) * inner_ref[h]
        o = o + _dot(sc.astype(BF16), vb) * mv_ref[h]
    st_ref[...] = state * cd_ref[...] + bm_ref[...] * _dot_tn((kr * kd_ref[...]).astype(BF16), vb)

    gm = gm_ref[...]
    o_hi = o.astype(BF16)
    o_lo = (o - o_hi.astype(F32)).astype(BF16)
    mu = _dot(o_hi, gm) + _dot(o_lo, gm)
    dlt = o - mu
    d2 = dlt * dlt
    d2_hi = d2.astype(BF16)
    d2_lo = (d2 - d2_hi.astype(F32)).astype(BF16)
    var = _dot(d2_hi, gm) + _dot(d2_lo, gm)
    y = dlt * lax.rsqrt(var + EPS)
    y = gate * _sigmoid(gate) * y
    y_ref[0] = _rms_rows(y, g_ref[...]).astype(BF16)


def _ret_mixer(u, cos_r, sin_r, consts, g, chunk):
    bsz, s, _ = u.shape
    inner, qd, kd, cd, bm, gm, mq, mv = consts
    full = lambda shape: pl.BlockSpec(shape, lambda b, i: (0,) * len(shape))
    tab = pl.BlockSpec((1, chunk, LANES), lambda b, i: (b, i, 0))
    return pl.pallas_call(
        _ret_kernel,
        grid=(bsz, s // chunk),
        in_specs=[pl.BlockSpec((1, chunk, U_RET), lambda b, i: (b, i, 0)), tab, tab,
                  full(inner.shape), full(qd.shape), full(kd.shape), full(cd.shape),
                  full(bm.shape), full(gm.shape), full(mq.shape), full(mv.shape),
                  full((1, GROUP_W))],
        out_specs=pl.BlockSpec((1, chunk, GROUP_W), lambda b, i: (b, i, 0)),
        out_shape=jax.ShapeDtypeStruct((bsz, s, GROUP_W), BF16),
        scratch_shapes=[pltpu.VMEM((GROUP_W, GROUP_W), F32)],
        compiler_params=_cparams(("parallel", "arbitrary")),
        name="ret_mixer",
    )(u, cos_r, sin_r, inner, qd, kd, cd, bm, gm, mq, mv, g)


def _ret_consts(chunk):
    nh = N_HEADS
    log_g = jnp.log(1.0 - 2.0 ** (-5.0 - jnp.arange(nh, dtype=F32)))
    idx = jnp.arange(chunk, dtype=F32)
    rel = idx[:, None] - idx[None, :]
    inner = jnp.where(rel >= 0, jnp.exp(log_g[:, None, None] * jnp.maximum(rel, 0.0)), 0.0)
    v_head = jnp.arange(GROUP_W) // HEAD_DIM
    q_head = (jnp.arange(GROUP_W) % LANES) // (RET_DK // 2)
    qd = jnp.exp(log_g[v_head][None, :] * (idx[:, None] + 1.0))
    kd = jnp.exp(log_g[q_head][None, :] * (chunk - 1.0 - idx[:, None]))
    cd = jnp.exp(log_g[v_head] * chunk)[None, :]
    bm = (q_head[:, None] == v_head[None, :]).astype(F32)
    gm = ((v_head[:, None] == v_head[None, :]).astype(F32) / HEAD_DIM).astype(BF16)
    mq = (q_head[None, :] == jnp.arange(nh)[:, None]).astype(F32)[:, None, :]
    mv = (v_head[None, :] == jnp.arange(nh)[:, None]).astype(F32)[:, None, :]
    return inner, qd, kd, cd, bm, gm, mq, mv


def _mla_prep_kernel(u_ref, cos_ref, sin_ref, mu_ref, invu_ref, gu_ref, wbig_ref, sq_ref,
                     invq_ref, gq_ref, gqs_ref, gk_ref, onev_ref, q_ref, k_ref, v_ref):
    x = u_ref[0].astype(F32)
    ss = _dot((x * x).astype(BF16), mu_ref[...]) * invu_ref[...]
    xn = (x * lax.rsqrt(ss + EPS) * gu_ref[...]).astype(BF16)
    big = _dot(xn, wbig_ref[...])
    hw = N_HEADS * LANES
    q, qs, kn, v = big[:, :hw], big[:, hw:2 * hw], big[:, 2 * hw:3 * hw], big[:, 3 * hw:4 * hw]
    kr, krs = big[:, 4 * hw:4 * hw + LANES], big[:, 4 * hw + LANES:]
    cos, sin = cos_ref[0], sin_ref[0]
    rq = lax.rsqrt(_dot((q * q).astype(BF16), sq_ref[...]) * invq_ref[...] + EPS)
    rk = lax.rsqrt(_dot((kn * kn).astype(BF16), sq_ref[...]) * invq_ref[...] + EPS)
    krot = kr * cos + krs * sin
    for h in range(N_HEADS):
        sl = slice(h * LANES, (h + 1) * LANES)
        qh = (q[:, sl] * gq_ref[...] * cos + qs[:, sl] * gqs_ref[...] * sin) * rq[:, sl]
        q_ref[0, h] = qh.astype(BF16)
        k_ref[0, h] = (kn[:, sl] * gk_ref[...] * rk[:, sl] + krot).astype(BF16)
        v_ref[0, h] = (v[:, sl] + onev_ref[...]).astype(BF16)


def _mla_prep(u, cos_m, sin_m, mu, invu, gu, wbig, sq, invq, gq, gqs, gk, onev, tm):
    bsz, s, _ = u.shape
    full = lambda a: pl.BlockSpec(a.shape, lambda b, i: (0,) * a.ndim)
    tab = pl.BlockSpec((1, tm, LANES), lambda b, i: (b, i, 0))
    out = jax.ShapeDtypeStruct((bsz, N_HEADS, s, LANES), BF16)
    ospec = pl.BlockSpec((1, N_HEADS, tm, LANES), lambda b, i: (b, 0, i, 0))
    return pl.pallas_call(
        _mla_prep_kernel,
        grid=(bsz, s // tm),
        in_specs=[pl.BlockSpec((1, tm, U_MLA), lambda b, i: (b, i, (U_COLS - U_MLA) // U_MLA)),
                  tab, tab, full(mu), full(invu), full(gu), full(wbig), full(sq), full(invq),
                  full(gq), full(gqs), full(gk), full(onev)],
        out_specs=[ospec, ospec, ospec],
        out_shape=[out, out, out],
        compiler_params=_cparams(("parallel", "parallel")),
        name="mla_prep",
    )(u, cos_m, sin_m, mu, invu, gu, wbig, sq, invq, gq, gqs, gk, onev)


def _flash_kernel(q_ref, k_ref, v_ref, o_ref, sa_ref, sb_ref, mca_ref, mcb_ref, m_ref, acc_ref, *, tq):
    qi = pl.program_id(2)
    q = q_ref[0, 0]
    bufs = ((sa_ref, mca_ref), (sb_ref, mcb_ref))
    m_ref[...] = jnp.full((tq, LANES), NEG_BIG, F32)
    acc_ref[...] = jnp.zeros((tq, LANES), F32)

    def scores(c, masked, dst):
        s_ref, mc_ref = dst
        start = pl.multiple_of(c * tq, tq)
        s = _dot_nt(q, k_ref[0, 0, pl.ds(start, tq), :])
        if masked:
            row = qi * tq + lax.broadcasted_iota(jnp.int32, (tq, tq), 0)
            col = start + lax.broadcasted_iota(jnp.int32, (tq, tq), 1)
            s = jnp.where(col <= row, s, NEG_BIG)
        s_ref[...] = s
        mc_ref[...] = jnp.broadcast_to(jnp.max(s, axis=-1, keepdims=True), (tq, LANES))

    def accumulate(c, src):
        s_ref, mc_ref = src
        start = pl.multiple_of(c * tq, tq)
        m_prev = m_ref[...]
        m_new = jnp.maximum(m_prev, mc_ref[...])
        alpha = jnp.exp2(m_prev - m_new)
        p = jnp.exp2(s_ref[...] - pltpu.repeat(m_new, tq // LANES, axis=1))
        pv = _dot(p.astype(BF16), v_ref[0, 0, pl.ds(start, tq), :])
        acc_ref[...] = alpha * acc_ref[...] + pv
        m_ref[...] = m_new

    def by_parity(c, fn):
        for par in range(2):
            pl.when(c % 2 == par)(functools.partial(fn, par))

    def pipelined(c, masked, par):
        scores(c + 1, masked, bufs[1 - par])
        accumulate(c, bufs[par])

    scores(0, True, bufs[0])

    def body(j, carry):
        by_parity(j, functools.partial(pipelined, j, False))
        return carry

    lax.fori_loop(0, jnp.maximum(qi - 1, 0), body, 0)

    @pl.when(qi >= 1)
    def _():
        by_parity(qi - 1, functools.partial(pipelined, qi - 1, True))

    by_parity(qi, lambda par: accumulate(qi, bufs[par]))

    acc = acc_ref[...]
    lane = lax.broadcasted_iota(jnp.int32, (tq, LANES), 1)
    denom = jnp.sum(jnp.where(lane == HEAD_DIM, acc, 0.0), axis=-1, keepdims=True)
    o_ref[0, 0] = jnp.where(lane < HEAD_DIM, acc / denom, 0.0).astype(BF16)


def _flash_attention(q, k, v, tq):
    bsz, nh, s, _ = q.shape
    kv_spec = pl.BlockSpec((1, 1, s, LANES), lambda b, h, i: (b, h, 0, 0))
    blk = pl.BlockSpec((1, 1, tq, LANES), lambda b, h, i: (b, h, i, 0))
    stat = pltpu.VMEM((tq, LANES), F32)
    return pl.pallas_call(
        functools.partial(_flash_kernel, tq=tq),
        grid=(bsz, nh, s // tq),
        in_specs=[blk, kv_spec, kv_spec],
        out_specs=blk,
        out_shape=jax.ShapeDtypeStruct((bsz, nh, s, LANES), BF16),
        scratch_shapes=[pltpu.VMEM((tq, tq), F32), pltpu.VMEM((tq, tq), F32), stat, stat, stat, stat],
        compiler_params=_cparams(("parallel", "parallel", "arbitrary")),
        name="flash_attention",
    )(q, k, v)


def _outproj_kernel(x_ref, yc_ref, om_ref, yr_ref, yl_ref, wc_ref, wm_ref, wr_ref, wl_ref,
                    gmla_ref, gt_ref, gf_ref, scf_ref, shf_ref, wrt_ref, brt_ref,
                    xo_ref, h_ref, chi_ref, clo_ref):
    om = [om_ref[0, h].astype(F32) for h in range(N_HEADS)]
    ssq = om[0] * om[0]
    for h in range(1, N_HEADS):
        ssq = ssq + om[h] * om[h]
    r_mla = lax.rsqrt(jnp.sum(ssq, axis=-1, keepdims=True) / GROUP_W + EPS)
    y = _dot(yc_ref[0], wc_ref[...]) + _dot(yr_ref[0], wr_ref[...]) + _dot(yl_ref[0], wl_ref[...])
    for h in range(N_HEADS):
        y = y + _dot((om[h] * r_mla * gmla_ref[h]).astype(BF16), wm_ref[h])
    x = x_ref[0] + gt_ref[0] * y
    xo_ref[0] = x
    hf = _rms_rows(x, gf_ref[...]) * (1.0 + scf_ref[0]) + shf_ref[0]
    h_ref[0] = hf.astype(BF16)

    lg = jnp.dot(hf, wrt_ref[...], precision=HIGHEST, preferred_element_type=F32)
    tm = lg.shape[0]
    lane = lax.broadcasted_iota(jnp.int32, (tm, LANES), 1)
    bias = brt_ref[...]
    is_g = (lane >= N_EXPERTS) & (lane < N_EXPERTS + MOE_GROUPS)
    is_e = lane < N_EXPERTS

    def first_argmax(val):
        mx = jnp.max(val, axis=-1, keepdims=True)
        return jnp.min(jnp.where(val == mx, lane, LANES), axis=-1, keepdims=True)

    gl = jnp.where(is_g, lg, NEG_BIG)
    ge = jnp.exp(gl - jnp.max(gl, axis=-1, keepdims=True))
    gp = ge / jnp.sum(ge, axis=-1, keepdims=True)
    g_idx = first_argmax(jnp.where(is_g, gp + bias, NEG_BIG))
    g_weight = jnp.sum(jnp.where(lane == g_idx, gp, 0.0), axis=-1, keepdims=True)
    in_group = is_e & ((lane // EXPERTS_PER_GROUP) == (g_idx - N_EXPERTS))
    el = jnp.where(in_group, lg, NEG_BIG)
    ee = jnp.exp(el - jnp.max(el, axis=-1, keepdims=True))
    ep = ee / jnp.sum(ee, axis=-1, keepdims=True)
    score = jnp.where(in_group, ep + bias, NEG_BIG)
    i1 = first_argmax(score)
    sel1 = lane == i1
    i2 = first_argmax(jnp.where(sel1, NEG_BIG, score))
    sel2 = lane == i2
    p1 = jnp.sum(jnp.where(sel1, ep, 0.0), axis=-1, keepdims=True)
    p2 = jnp.sum(jnp.where(sel2, ep, 0.0), axis=-1, keepdims=True)
    psum = p1 + p2
    comb = jnp.where(sel1, p1 / psum * g_weight, 0.0) + jnp.where(sel2, p2 / psum * g_weight, 0.0)
    c_hi = comb.astype(BF16)
    chi_ref[0] = c_hi
    clo_ref[0] = (comb - c_hi.astype(F32)).astype(BF16)


def _outproj(x, yc, om, yr, yl, wc, wm, wr, wl, gmla, gt, gf, scf, shf, wrt, brt, tm):
    bsz, s, d = x.shape
    full = lambda a: pl.BlockSpec(a.shape, lambda b, i: (0,) * a.ndim)
    tok = lambda w: pl.BlockSpec((1, tm, w), lambda b, i: (b, i, 0))
    vec = pl.BlockSpec((1, 1, d), lambda b, i: (b, 0, 0))
    return pl.pallas_call(
        _outproj_kernel,
        grid=(bsz, s // tm),
        in_specs=[tok(d), tok(GROUP_W),
                  pl.BlockSpec((1, N_HEADS, tm, LANES), lambda b, i: (b, 0, i, 0)),
                  tok(GROUP_W), tok(GROUP_W),
                  full(wc), full(wm), full(wr), full(wl), full(gmla), vec, full(gf), vec, vec,
                  full(wrt), full(brt)],
        out_specs=[tok(d), tok(d), tok(LANES), tok(LANES)],
        out_shape=[jax.ShapeDtypeStruct((bsz, s, d), F32), jax.ShapeDtypeStruct((bsz, s, d), BF16),
                   jax.ShapeDtypeStruct((bsz, s, LANES), BF16), jax.ShapeDtypeStruct((bsz, s, LANES), BF16)],
        compiler_params=_cparams(("parallel", "parallel")),
        name="outproj_router",
    )(x, yc, om, yr, yl, wc, wm, wr, wl, gmla, gt, gf, scf, shf, wrt, brt)


def _moe_kernel(x_ref, h_ref, chi_ref, clo_ref, wg_ref, wu_ref, wd_ref, gt_ref, o_ref, acc_ref):
    e = pl.program_id(2)

    @pl.when(e == 0)
    def _():
        acc_ref[...] = jnp.zeros(acc_ref.shape, F32)

    h = h_ref[0]
    onehot = (lax.broadcasted_iota(jnp.int32, (LANES, D_EXPERT), 0) == e).astype(BF16)
    cw = _dot(chi_ref[0], onehot) + _dot(clo_ref[0], onehot)
    gate = _dot(h, wg_ref[0])
    hid = gate * _sigmoid(gate) * _dot(h, wu_ref[0]) * cw
    acc_ref[...] += _dot(hid.astype(BF16), wd_ref[0])

    @pl.when(e == N_EXPERTS - 1)
    def _():
        o_ref[0] = x_ref[0] + gt_ref[0] * acc_ref[...]


def _moe(x, h, chi, clo, wg, wu, wd, gt, tm):
    bsz, s, d = x.shape
    tok = lambda w: pl.BlockSpec((1, tm, w), lambda b, i, e: (b, i, 0))
    return pl.pallas_call(
        _moe_kernel,
        grid=(bsz, s // tm, N_EXPERTS),
        in_specs=[tok(d), tok(d), tok(LANES), tok(LANES),
                  pl.BlockSpec((1, d, D_EXPERT), lambda b, i, e: (e, 0, 0)),
                  pl.BlockSpec((1, d, D_EXPERT), lambda b, i, e: (e, 0, 0)),
                  pl.BlockSpec((1, D_EXPERT, d), lambda b, i, e: (e, 0, 0)),
                  pl.BlockSpec((1, 1, d), lambda b, i, e: (b, 0, 0))],
        out_specs=tok(d),
        out_shape=jax.ShapeDtypeStruct((bsz, s, d), F32),
        scratch_shapes=[pltpu.VMEM((tm, d), F32)],
        compiler_params=_cparams(("parallel", "parallel", "arbitrary")),
        name="moe_dense",
    )(x, h, chi, clo, wg, wu, wd, gt)


def _layer_weights(l, w_in, mla_q_norm_g, mla_w_uq, mla_kv_norm_g, mla_w_ukv, mla_q_qk_g,
                   mla_k_qk_g, lru_w_a, lru_w_x, mix_norm_g, w_out, router_group_w,
                   router_group_b, router_expert_w, router_expert_b):
    half = RET_DK // 2
    perm = np.concatenate([np.arange(half) + HEAD_DIM * h for h in range(N_HEADS)]
                          + [np.arange(half) + half + HEAD_DIM * h for h in range(N_HEADS)])
    w = w_in[l]
    o_mla, o_ret, o_lru = U_CONV, U_CONV + 352, U_CONV + 352 + U_RET
    w_ret = w[:, o_ret:o_ret + U_RET]
    w_ret = jnp.concatenate([w_ret[:, perm], w_ret[:, GROUP_W + perm], w_ret[:, 2 * GROUP_W:]], axis=1)
    w_all = jnp.concatenate([w_ret, w[:, o_lru:o_lru + U_LRU], w[:, :U_CONV], w[:, o_mla:o_mla + 352],
                             jnp.zeros((D_MODEL, U_MLA - 352), F32)], axis=1).astype(BF16)

    hw = N_HEADS * LANES
    r16 = MLA_ROPE // 2
    wq = mla_w_uq[l].reshape(Q_LORA, N_HEADS, MLA_QK)
    zq = jnp.zeros((Q_LORA, N_HEADS, LANES - MLA_QK), F32)
    q_cols = jnp.concatenate([wq, zq], axis=2).reshape(Q_LORA, hw)
    wq_sw = jnp.concatenate([jnp.zeros((Q_LORA, N_HEADS, MLA_NOPE), F32), wq[:, :, MLA_NOPE + r16:],
                             wq[:, :, MLA_NOPE:MLA_NOPE + r16], zq], axis=2).reshape(Q_LORA, hw)
    wkv = mla_w_ukv[l].reshape(KV_LORA, N_HEADS, MLA_NOPE + HEAD_DIM)
    zk = jnp.zeros((KV_LORA, N_HEADS, LANES - MLA_NOPE), F32)
    k_cols = jnp.concatenate([wkv[:, :, :MLA_NOPE], zk], axis=2).reshape(KV_LORA, hw)
    v_cols = jnp.concatenate([wkv[:, :, MLA_NOPE:], zk], axis=2).reshape(KV_LORA, hw)
    eye = jnp.eye(MLA_ROPE, dtype=F32)
    place = jnp.concatenate([jnp.zeros((MLA_ROPE, MLA_NOPE), F32), eye,
                             jnp.zeros((MLA_ROPE, LANES - MLA_QK), F32)], axis=1)
    eye_sw = jnp.concatenate([eye[:, r16:], eye[:, :r16]], axis=1)
    place_sw = jnp.concatenate([jnp.zeros((MLA_ROPE, MLA_NOPE), F32), eye_sw,
                                jnp.zeros((MLA_ROPE, LANES - MLA_QK), F32)], axis=1)
    n_big = 4 * hw + 2 * LANES
    wbig = jnp.zeros((U_MLA, n_big), F32)
    wbig = wbig.at[:Q_LORA, :hw].set(q_cols).at[:Q_LORA, hw:2 * hw].set(wq_sw)
    wbig = wbig.at[Q_LORA:Q_LORA + KV_LORA, 2 * hw:3 * hw].set(k_cols)
    wbig = wbig.at[Q_LORA:Q_LORA + KV_LORA, 3 * hw:4 * hw].set(v_cols)
    wbig = wbig.at[Q_LORA + KV_LORA:352, 4 * hw:4 * hw + LANES].set(place)
    wbig = wbig.at[Q_LORA + KV_LORA:352, 4 * hw + LANES:].set(place_sw)
    wbig = wbig.astype(BF16)

    gu = jnp.concatenate([mla_q_norm_g[l], mla_kv_norm_g[l], mla_k_qk_g[l][MLA_NOPE:],
                          jnp.zeros((U_MLA - 352,), F32)])[None, :]
    qscale = (MLA_QK ** -0.5) * math.log2(math.e)
    gq_full = mla_q_qk_g[l]
    pad = jnp.zeros((LANES - MLA_QK,), F32)
    gq = (jnp.concatenate([gq_full, pad]) * qscale)[None, :]
    gqs = (jnp.concatenate([jnp.zeros((MLA_NOPE,), F32), gq_full[MLA_NOPE + r16:],
                            gq_full[MLA_NOPE:MLA_NOPE + r16], pad]) * qscale)[None, :]
    gk = jnp.concatenate([mla_k_qk_g[l][:MLA_NOPE], jnp.zeros((LANES - MLA_NOPE,), F32)])[None, :]

    def blockdiag(wb):
        out = jnp.zeros((GROUP_W, GROUP_W), F32)
        for n in range(wb.shape[0]):
            out = out.at[n * HEAD_DIM:(n + 1) * HEAD_DIM, n * HEAD_DIM:(n + 1) * HEAD_DIM].set(wb[n])
        return out.astype(BF16)

    gmix = mix_norm_g[l]
    wo = w_out[l].astype(BF16)
    wm = wo[GROUP_W:2 * GROUP_W].reshape(N_HEADS, HEAD_DIM, D_MODEL)
    wm = jnp.concatenate([wm, jnp.zeros((N_HEADS, LANES - HEAD_DIM, D_MODEL), BF16)], axis=1)
    gmla = jnp.concatenate([gmix[GROUP_W:2 * GROUP_W].reshape(N_HEADS, 1, HEAD_DIM),
                            jnp.zeros((N_HEADS, 1, LANES - HEAD_DIM), F32)], axis=2)
    wrt = jnp.concatenate([router_expert_w[l], router_group_w[l],
                           jnp.zeros((D_MODEL, LANES - N_EXPERTS - MOE_GROUPS), F32)], axis=1)
    brt = jnp.concatenate([router_expert_b[l], router_group_b[l],
                           jnp.zeros((LANES - N_EXPERTS - MOE_GROUPS,), F32)])[None, :]
    return dict(w_all=w_all, wbig=wbig, gu=gu, gq=gq, gqs=gqs, gk=gk,
                wa=blockdiag(lru_w_a[l]), wx=blockdiag(lru_w_x[l]),
                g_conv=gmix[None, :GROUP_W], g_ret=gmix[None, 2 * GROUP_W:3 * GROUP_W],
                g_lru=gmix[None, 3 * GROUP_W:], gmla=gmla,
                wc=wo[:GROUP_W], wm=wm, wr=wo[2 * GROUP_W:3 * GROUP_W], wl=wo[3 * GROUP_W:],
                wrt=wrt, brt=brt)


def _mla_consts():
    seg_u = np.concatenate([np.zeros(Q_LORA), np.ones(KV_LORA), 2 * np.ones(MLA_ROPE),
                            3 * np.ones(U_MLA - 352)])
    mu = jnp.asarray(seg_u[:, None] == seg_u[None, :], BF16)
    invu = jnp.asarray(np.concatenate([np.full(Q_LORA, 1.0 / Q_LORA), np.full(KV_LORA, 1.0 / KV_LORA),
                                       np.full(MLA_ROPE, 1.0 / MLA_ROPE), np.ones(U_MLA - 352)]), F32)[None, :]
    lane = np.arange(N_HEADS * LANES)
    seg_q = (lane // LANES) * 3 + np.where(lane % LANES < MLA_NOPE, 0, np.where(lane % LANES < MLA_QK, 1, 2))
    sq = jnp.asarray(seg_q[:, None] == seg_q[None, :], BF16)
    inv_head = np.concatenate([np.full(MLA_NOPE, 1.0 / MLA_NOPE), np.full(MLA_ROPE, 1.0 / MLA_ROPE),
                               np.ones(LANES - MLA_QK)])
    invq = jnp.asarray(np.tile(inv_head, N_HEADS), F32)[None, :]
    onev = jnp.asarray((np.arange(LANES) == HEAD_DIM).astype(np.float32))[None, :]
    return mu, invu, sq, invq, onev


def kernel(x, c, positions, ada_w, ada_b, norm_mix_g, w_in, conv_w, mla_q_norm_g, mla_w_uq, mla_kv_norm_g, mla_w_ukv, mla_q_qk_g, mla_k_qk_g, lru_conv_w, lru_conv_b, lru_w_a, lru_b_a, lru_w_x, lru_b_x, lru_lambda, mix_norm_g, w_out, norm_ffn_g, router_group_w, router_group_b, router_expert_w, router_expert_b, exp_w_gate, exp_w_up, exp_w_down):
    bsz, s, d = x.shape
    depth = ada_w.shape[0]
    tm = min(512, s)
    chunk = min(256, s)
    tq = min(1024, s)
    tmoe = min(1024, s)

    r16, r32 = MLA_ROPE // 2, RET_DK // 2
    inv = jnp.concatenate([1.0 / (ROPE_BASE ** (jnp.arange(0, MLA_ROPE, 2, dtype=F32) / MLA_ROPE)),
                           1.0 / (ROPE_BASE ** (jnp.arange(0, RET_DK, 2, dtype=F32) / RET_DK))])[:, None]
    cos_t, sin_t = _rope_tables(positions, inv)
    cos_t, sin_t = cos_t.transpose(0, 2, 1), sin_t.transpose(0, 2, 1)
    ones = jnp.ones((bsz, s, MLA_NOPE), F32)
    zpad = jnp.zeros((bsz, s, LANES - MLA_QK), F32)
    cos_m = jnp.concatenate([ones, cos_t[..., :r16], cos_t[..., :r16], zpad], axis=-1)
    sin_m = jnp.concatenate([0.0 * ones, -sin_t[..., :r16], sin_t[..., :r16], zpad], axis=-1)
    cos_r = jnp.tile(cos_t[..., r16:], (1, 1, N_HEADS))
    sin_r = jnp.tile(sin_t[..., r16:], (1, 1, N_HEADS))

    c_pad = jnp.concatenate([c, jnp.zeros((8 - bsz, d), F32)], axis=0)
    mod = _modulation(c_pad, ada_w, ada_b)[:, :bsz]
    ret_consts = _ret_consts(chunk)
    mu, invu, sq, invq, onev = _mla_consts()

    for l in range(depth):
        sh_m, sc_m, gt_m, sh_f, sc_f, gt_f = [m[:, None, :] for m in jnp.split(mod[l], 6, axis=-1)]
        lw = _layer_weights(l, w_in, mla_q_norm_g, mla_w_uq, mla_kv_norm_g, mla_w_ukv, mla_q_qk_g,
                            mla_k_qk_g, lru_w_a, lru_w_x, mix_norm_g, w_out, router_group_w,
                            router_group_b, router_expert_w, router_expert_b)
        u = _inproj(x, norm_mix_g[l][None, :], sc_m, sh_m, lw["w_all"], tm)
        y_conv = _conv_mixer(u, conv_w[l], lw["g_conv"], tm)
        y_lru = _lru_mixer(u, lru_conv_w[l], lru_conv_b[l][None, :], lw["wa"], lru_b_a[l][None, :],
                           lw["wx"], lru_b_x[l][None, :], lru_lambda[l][None, :], lw["g_lru"], tm)
        y_ret = _ret_mixer(u, cos_r, sin_r, ret_consts, lw["g_ret"], chunk)
        q, k, v = _mla_prep(u, cos_m, sin_m, mu, invu, lw["gu"], lw["wbig"], sq, invq,
                            lw["gq"], lw["gqs"], lw["gk"], onev, tm)
        o_mla = _flash_attention(q, k, v, tq)
        x, h, chi, clo = _outproj(x, y_conv, o_mla, y_ret, y_lru, lw["wc"], lw["wm"], lw["wr"],
                                  lw["wl"], lw["gmla"], gt_m, norm_ffn_g[l][None, :], sc_f, sh_f,
                                  lw["wrt"], lw["brt"], tm)
        x = _moe(x, h, chi, clo, exp_w_gate[l].astype(BF16), exp_w_up[l].astype(BF16),
                 exp_w_down[l].astype(BF16), gt_f, tmoe)
    return x
```

```python
import functools
import math

import jax
import jax.numpy as jnp
import numpy as np
from jax import lax
from jax.experimental import pallas as pl
from jax.experimental.pallas import tpu as pltpu

F32 = jnp.float32
BF16 = jnp.bfloat16
HIGHEST = lax.Precision.HIGHEST

D_MODEL = 1024
GROUP_W = 256
HEAD_DIM = 64
N_HEADS = 4
MLA_NOPE = 64
MLA_ROPE = 32
MLA_QK = 96
Q_LORA = 192
KV_LORA = 128
RET_DK = 64
LRU_C = 8.0
MOE_GROUPS = 4
EXPERTS_PER_GROUP = 8
N_EXPERTS = 32
D_EXPERT = 256
ROPE_BASE = 10000.0
EPS = 1e-6

LANES = 128
MXU_DIM = 256
U_RET, U_LRU, U_CONV, U_MLA = 1024, 512, 768, 384
U_COLS = U_RET + U_LRU + U_CONV + U_MLA
N_FREQ = MLA_ROPE // 2 + RET_DK // 2
NEG_BIG = -1e30
VMEM_LIMIT = 56 * 1024 * 1024


def _cparams(sem):
    return pltpu.CompilerParams(dimension_semantics=sem, vmem_limit_bytes=VMEM_LIMIT)


def _dot(a, b):
    return jnp.dot(a, b, preferred_element_type=F32)


def _dot_nt(a, b):
    return lax.dot_general(a, b, (((1,), (1,)), ((), ())), preferred_element_type=F32)


def _dot_tn(a, b):
    return lax.dot_general(a, b, (((0,), (0,)), ((), ())), preferred_element_type=F32)


def _rms_rows(y, g):
    return y * lax.rsqrt(jnp.mean(y * y, axis=-1, keepdims=True) + EPS) * g


def _sigmoid(x):
    return 1.0 / (1.0 + jnp.exp(-x))


def _pack_bf16_pairs(a):
    k = a.shape[1] // 2
    rounded = a.astype(BF16).astype(F32)
    lo = lax.bitcast_convert_type(rounded[:, :k], jnp.uint32) >> 16
    hi = lax.bitcast_convert_type(rounded[:, k:], jnp.uint32) & jnp.uint32(0xFFFF0000)
    return lo | hi


def _unpack_bf16_pairs(w):
    lo = lax.bitcast_convert_type(w << 16, F32)
    hi = lax.bitcast_convert_type(w & jnp.uint32(0xFFFF0000), F32)
    return jnp.concatenate([lo, hi], axis=1)


def _rope_kernel(pos_ref, inv_ref, cos_ref, sin_ref):
    ang = pos_ref[0].astype(F32) * inv_ref[...]
    cos_ref[0] = jnp.cos(ang)
    sin_ref[0] = jnp.sin(ang)


def _rope_tables(positions, inv):
    bsz, s = positions.shape
    ts = min(s, 2048)
    out = jax.ShapeDtypeStruct((bsz, N_FREQ, s), F32)
    return pl.pallas_call(
        _rope_kernel,
        grid=(bsz, s // ts),
        in_specs=[pl.BlockSpec((1, 1, ts), lambda b, i: (b, 0, i)),
                  pl.BlockSpec((N_FREQ, 1), lambda b, i: (0, 0))],
        out_specs=[pl.BlockSpec((1, N_FREQ, ts), lambda b, i: (b, 0, i))] * 2,
        out_shape=[out, out],
        compiler_params=_cparams(("parallel", "parallel")),
        name="rope_tables",
    )(positions.reshape(bsz, 1, s), inv)


def _mod_kernel(c_ref, w_ref, b_ref, o_ref):
    c = c_ref[...]
    ca = c * _sigmoid(c)
    o_ref[0] = jnp.dot(ca, w_ref[0], precision=HIGHEST, preferred_element_type=F32) + b_ref[0]


def _modulation(c_pad, ada_w, ada_b):
    nl, d, n = ada_w.shape
    tn = 1536
    return pl.pallas_call(
        _mod_kernel,
        grid=(nl, n // tn),
        in_specs=[pl.BlockSpec((8, d), lambda l, j: (0, 0)),
                  pl.BlockSpec((1, d, tn), lambda l, j: (l, 0, j)),
                  pl.BlockSpec((1, 1, tn), lambda l, j: (l, 0, j))],
        out_specs=pl.BlockSpec((1, 8, tn), lambda l, j: (l, 0, j)),
        out_shape=jax.ShapeDtypeStruct((nl, 8, n), F32),
        compiler_params=_cparams(("parallel", "parallel")),
        name="adaln_mod",
    )(c_pad, ada_w, ada_b.reshape(nl, 1, n))


def _inproj_kernel(x_ref, g_ref, sc_ref, sh_ref, w_ref, u_ref):
    x = x_ref[0]
    h = _rms_rows(x, g_ref[...]) * (1.0 + sc_ref[0]) + sh_ref[0]
    u_ref[0] = _dot(h.astype(BF16), w_ref[...]).astype(BF16)


def _inproj(x, g, sc, sh, w, tm):
    bsz, s, d = x.shape
    vec = pl.BlockSpec((1, 1, d), lambda b, i: (b, 0, 0))
    return pl.pallas_call(
        _inproj_kernel,
        grid=(bsz, s // tm),
        in_specs=[pl.BlockSpec((1, tm, d), lambda b, i: (b, i, 0)),
                  pl.BlockSpec((1, d), lambda b, i: (0, 0)),
                  vec, vec,
                  pl.BlockSpec((d, U_COLS), lambda b, i: (0, 0))],
        out_specs=pl.BlockSpec((1, tm, U_COLS), lambda b, i: (b, i, 0)),
        out_shape=jax.ShapeDtypeStruct((bsz, s, U_COLS), BF16),
        compiler_params=_cparams(("parallel", "parallel")),
        name="inproj",
    )(x, g, sc, sh, w)


def _conv_kernel(u_ref, w_ref, g_ref, y_ref, buf_ref, *, tm):
    @pl.when(pl.program_id(1) == 0)
    def _():
        buf_ref[0:8, :] = jnp.zeros((8, GROUP_W), F32)

    u = u_ref[0].astype(F32)
    b_gate, c_gate, xin = u[:, :GROUP_W], u[:, GROUP_W:2 * GROUP_W], u[:, 2 * GROUP_W:]
    cx = c_gate * xin
    buf_ref[8:8 + tm, :] = cx
    conv = (w_ref[2:3, :] * cx + w_ref[1:2, :] * buf_ref[7:7 + tm, :]
            + w_ref[0:1, :] * buf_ref[6:6 + tm, :])
    buf_ref[0:8, :] = cx[tm - 8:, :]
    y_ref[0] = _rms_rows(b_gate * conv, g_ref[...]).astype(BF16)


def _conv_mixer(u, w, g, tm):
    bsz, s, _ = u.shape
    return pl.pallas_call(
        functools.partial(_conv_kernel, tm=tm),
        grid=(bsz, s // tm),
        in_specs=[pl.BlockSpec((1, tm, U_CONV), lambda b, i: (b, i, (U_RET + U_LRU) // U_CONV)),
                  pl.BlockSpec((3, GROUP_W), lambda b, i: (0, 0)),
                  pl.BlockSpec((1, GROUP_W), lambda b, i: (0, 0))],
        out_specs=pl.BlockSpec((1, tm, GROUP_W), lambda b, i: (b, i, 0)),
        out_shape=jax.ShapeDtypeStruct((bsz, s, GROUP_W), BF16),
        scratch_shapes=[pltpu.VMEM((tm + 8, GROUP_W), F32)],
        compiler_params=_cparams(("parallel", "arbitrary")),
        name="conv_mixer",
    )(u, w, g)


def _lru_kernel(u_ref, cw_ref, cb_ref, wa_ref, ba_ref, wx_ref, bx_ref, lam_ref, g_ref,
                y_ref, buf_ref, h_ref, *, tm):
    @pl.when(pl.program_id(1) == 0)
    def _():
        buf_ref[0:8, :] = jnp.zeros((8, GROUP_W), F32)
        h_ref[...] = jnp.zeros((1, GROUP_W), F32)

    u = u_ref[0].astype(F32)
    xraw, gate = u[:, :GROUP_W], u[:, GROUP_W:]
    buf_ref[8:8 + tm, :] = xraw
    xb = (cw_ref[3:4, :] * xraw + cw_ref[2:3, :] * buf_ref[7:7 + tm, :]
          + cw_ref[1:2, :] * buf_ref[6:6 + tm, :] + cw_ref[0:1, :] * buf_ref[5:5 + tm, :]
          + cb_ref[...])
    buf_ref[0:8, :] = xraw[tm - 8:, :]

    xbb = xb.astype(BF16)
    r = _sigmoid(_dot(xbb, wa_ref[...]) + ba_ref[...])
    i = _sigmoid(_dot(xbb, wx_ref[...]) + bx_ref[...])
    nlam = -lam_ref[...]
    softplus = jnp.maximum(nlam, 0.0) + jnp.log(1.0 + jnp.exp(-jnp.abs(nlam)))
    log_a = (-LRU_C) * r * softplus
    a = jnp.exp(log_a)
    b = jnp.sqrt(1.0 - a * a) * (i * xb)

    row = lax.broadcasted_iota(jnp.int32, (tm, GROUP_W), 0)
    d = 1
    while d < tm:
        keep = row >= d
        a_sh = jnp.where(keep, pltpu.roll(a, d, 0), 1.0)
        b_sh = jnp.where(keep, pltpu.roll(b, d, 0), 0.0)
        b = a * b_sh + b
        a = a * a_sh
        d *= 2
    h = a * h_ref[...] + b
    h_ref[...] = h[tm - 1:tm, :]

    gelu = 0.5 * gate * (1.0 + jnp.tanh(math.sqrt(2.0 / math.pi) * (gate + 0.044715 * gate * gate * gate)))
    y_ref[0] = _rms_rows(h * gelu, g_ref[...]).astype(BF16)


def _lru_mixer(u, cw, cb, wa, ba, wx, bx, lam, g, tm):
    bsz, s, _ = u.shape
    row = pl.BlockSpec((1, GROUP_W), lambda b, i: (0, 0))
    mat = pl.BlockSpec((GROUP_W, GROUP_W), lambda b, i: (0, 0))
    return pl.pallas_call(
        functools.partial(_lru_kernel, tm=tm),
        grid=(bsz, s // tm),
        in_specs=[pl.BlockSpec((1, tm, U_LRU), lambda b, i: (b, i, U_RET // U_LRU)),
                  pl.BlockSpec((4, GROUP_W), lambda b, i: (0, 0)),
                  row, mat, row, mat, row, row, row],
        out_specs=pl.BlockSpec((1, tm, GROUP_W), lambda b, i: (b, i, 0)),
        out_shape=jax.ShapeDtypeStruct((bsz, s, GROUP_W), BF16),
        scratch_shapes=[pltpu.VMEM((tm + 8, GROUP_W), F32), pltpu.VMEM((1, GROUP_W), F32)],
        compiler_params=_cparams(("parallel", "arbitrary")),
        name="lru_mixer",
    )(u, cw, cb, wa, ba, wx, bx, lam, g)


def _ret_kernel(u_ref, cos_ref, sin_ref, inner_ref, qd_ref, kd_ref, cd_ref, bm_ref, gm_ref,
                mq_ref, mv_ref, g_ref, y_ref, st_ref):
    @pl.when(pl.program_id(1) == 0)
    def _():
        st_ref[...] = jnp.zeros((GROUP_W, GROUP_W), F32)

    u = u_ref[0].astype(F32)
    q, k = u[:, :GROUP_W], u[:, GROUP_W:2 * GROUP_W]
    v, gate = u[:, 2 * GROUP_W:3 * GROUP_W], u[:, 3 * GROUP_W:]
    cos, sin = cos_ref[0], sin_ref[0]

    def rope(t):
        t1, t2 = t[:, :LANES], t[:, LANES:]
        return jnp.concatenate([t1 * cos - t2 * sin, t2 * cos + t1 * sin], axis=-1)

    qr = rope(q)
    kr = rope(k) * (RET_DK ** -0.5)
    krb = kr.astype(BF16)
    vb = v.astype(BF16)
    state = st_ref[...]
    o = _dot(qr.astype(BF16), state.astype(BF16)) * qd_ref[...]
    for h in range(N_HEADS):
        qh = (qr * mq_ref[h]).astype(BF16)
        sc = _dot_nt(qh, krb) * inner_ref[h]
        o = o + _dot(sc.astype(BF16), vb) * mv_ref[h]
    st_ref[...] = state * cd_ref[...] + bm_ref[...] * _dot_tn((kr * kd_ref[...]).astype(BF16), vb)

    gm = gm_ref[...]
    o_hi = o.astype(BF16)
    o_lo = (o - o_hi.astype(F32)).astype(BF16)
    mu = _dot(o_hi, gm) + _dot(o_lo, gm)
    dlt = o - mu
    d2 = dlt * dlt
    d2_hi = d2.astype(BF16)
    d2_lo = (d2 - d2_hi.astype(F32)).astype(BF16)
    var = _dot(d2_hi, gm) + _dot(d2_lo, gm)
    y = dlt * lax.rsqrt(var + EPS)
    y = gate * _sigmoid(gate) * y
    y_ref[0] = _rms_rows(y, g_ref[...]).astype(BF16)


def _ret_mixer(u, cos_r, sin_r, consts, g, chunk):
    bsz, s, _ = u.shape
    inner, qd, kd, cd, bm, gm, mq, mv = consts
    full = lambda shape: pl.BlockSpec(shape, lambda b, i: (0,) * len(shape))
    tab = pl.BlockSpec((1, chunk, LANES), lambda b, i: (b, i, 0))
    return pl.pallas_call(
        _ret_kernel,
        grid=(bsz, s // chunk),
        in_specs=[pl.BlockSpec((1, chunk, U_RET), lambda b, i: (b, i, 0)), tab, tab,
                  full(inner.shape), full(qd.shape), full(kd.shape), full(cd.shape),
                  full(bm.shape), full(gm.shape), full(mq.shape), full(mv.shape),
                  full((1, GROUP_W))],
        out_specs=pl.BlockSpec((1, chunk, GROUP_W), lambda b, i: (b, i, 0)),
        out_shape=jax.ShapeDtypeStruct((bsz, s, GROUP_W), BF16),
        scratch_shapes=[pltpu.VMEM((GROUP_W, GROUP_W), F32)],
        compiler_params=_cparams(("parallel", "arbitrary")),
        name="ret_mixer",
    )(u, cos_r, sin_r, inner, qd, kd, cd, bm, gm, mq, mv, g)


def _ret_consts(chunk):
    nh = N_HEADS
    log_g = jnp.log(1.0 - 2.0 ** (-5.0 - jnp.arange(nh, dtype=F32)))
    idx = jnp.arange(chunk, dtype=F32)
    rel = idx[:, None] - idx[None, :]
    inner = jnp.where(rel >= 0, jnp.exp(log_g[:, None, None] * jnp.maximum(rel, 0.0)), 0.0)
    v_head = jnp.arange(GROUP_W) // HEAD_DIM
    q_head = (jnp.arange(GROUP_W) % LANES) // (RET_DK // 2)
    qd = jnp.exp(log_g[v_head][None, :] * (idx[:, None] + 1.0))
    kd = jnp.exp(log_g[q_head][None, :] * (chunk - 1.0 - idx[:, None]))
    cd = jnp.exp(log_g[v_head] * chunk)[None, :]
    bm = (q_head[:, None] == v_head[None, :]).astype(F32)
    gm = ((v_head[:, None] == v_head[None, :]).astype(F32) / HEAD_DIM).astype(BF16)
    mq = (q_head[None, :] == jnp.arange(nh)[:, None]).astype(F32)[:, None, :]
    mv = (v_head[None, :] == jnp.arange(nh)[:, None]).astype(F32)[:, None, :]
    return inner, qd, kd, cd, bm, gm, mq, mv


def _mla_prep_kernel(u_ref, cos_ref, sin_ref, mu_ref, invu_ref, gu_ref, wbig_ref, sq_ref,
                     invq_ref, gq_ref, gqs_ref, gk_ref, onev_ref, q_ref, k_ref, v_ref):
    x = u_ref[0].astype(F32)
    ss = _dot((x * x).astype(BF16), mu_ref[...]) * invu_ref[...]
    xn = (x * lax.rsqrt(ss + EPS) * gu_ref[...]).astype(BF16)
    big = _dot(xn, wbig_ref[...])
    hw = N_HEADS * LANES
    q, qs, kn, v = big[:, :hw], big[:, hw:2 * hw], big[:, 2 * hw:3 * hw], big[:, 3 * hw:4 * hw]
    kr, krs = big[:, 4 * hw:4 * hw + LANES], big[:, 4 * hw + LANES:]
    cos, sin = cos_ref[0], sin_ref[0]
    rq = lax.rsqrt(_dot((q * q).astype(BF16), sq_ref[...]) * invq_ref[...] + EPS)
    rk = lax.rsqrt(_dot((kn * kn).astype(BF16), sq_ref[...]) * invq_ref[...] + EPS)
    krot = kr * cos + krs * sin
    for h in range(N_HEADS):
        sl = slice(h * LANES, (h + 1) * LANES)
        qh = (q[:, sl] * gq_ref[...] * cos + qs[:, sl] * gqs_ref[...] * sin) * rq[:, sl]
        q_ref[0, h] = qh.astype(BF16)
        k_ref[0, h] = (kn[:, sl] * gk_ref[...] * rk[:, sl] + krot).astype(BF16)
        v_ref[0, h] = (v[:, sl] + onev_ref[...]).astype(BF16)


def _mla_prep(u, cos_m, sin_m, mu, invu, gu, wbig, sq, invq, gq, gqs, gk, onev, tm):
    bsz, s, _ = u.shape
    full = lambda a: pl.BlockSpec(a.shape, lambda b, i: (0,) * a.ndim)
    tab = pl.BlockSpec((1, tm, LANES), lambda b, i: (b, i, 0))
    out = jax.ShapeDtypeStruct((bsz, N_HEADS, s, LANES), BF16)
    ospec = pl.BlockSpec((1, N_HEADS, tm, LANES), lambda b, i: (b, 0, i, 0))
    return pl.pallas_call(
        _mla_prep_kernel,
        grid=(bsz, s // tm),
        in_specs=[pl.BlockSpec((1, tm, U_MLA), lambda b, i: (b, i, (U_COLS - U_MLA) // U_MLA)),
                  tab, tab, full(mu), full(invu), full(gu), full(wbig), full(sq), full(invq),
                  full(gq), full(gqs), full(gk), full(onev)],
        out_specs=[ospec, ospec, ospec],
        out_shape=[out, out, out],
        compiler_params=_cparams(("parallel", "parallel")),
        name="mla_prep",
    )(u, cos_m, sin_m, mu, invu, gu, wbig, sq, invq, gq, gqs, gk, onev)


def _flash_kernel(q_ref, k_ref, v_ref, o_ref, sa_ref, sb_ref, mca_ref, mcb_ref, m_ref, acc_ref, *, tq):
    qi = pl.program_id(2)
    q = q_ref[0, 0]
    bufs = ((sa_ref, mca_ref), (sb_ref, mcb_ref))
    m_ref[...] = jnp.full((tq, LANES), NEG_BIG, F32)
    acc_ref[...] = jnp.zeros((tq, LANES), F32)

    def scores(c, masked, dst):
        s_ref, mc_ref = dst
        start = pl.multiple_of(c * tq, tq)
        s = _dot_nt(q, k_ref[0, 0, pl.ds(start, tq), :])
        if masked:
            row = qi * tq + lax.broadcasted_iota(jnp.int32, (tq, tq), 0)
            col = start + lax.broadcasted_iota(jnp.int32, (tq, tq), 1)
            s = jnp.where(col <= row, s, NEG_BIG)
        s_ref[...] = s
        mc_ref[...] = jnp.broadcast_to(jnp.max(s, axis=-1, keepdims=True), (tq, LANES))

    def accumulate(c, src):
        s_ref, mc_ref = src
        start = pl.multiple_of(c * tq, tq)
        m_prev = m_ref[...]
        m_new = jnp.maximum(m_prev, mc_ref[...])
        alpha = jnp.exp2(m_prev - m_new)
        p = jnp.exp2(s_ref[...] - jnp.tile(m_new, (1, tq // LANES)))
        pv = _dot(p.astype(BF16), v_ref[0, 0, pl.ds(start, tq), :])
        acc_ref[...] = alpha * acc_ref[...] + pv
        m_ref[...] = m_new

    def by_parity(c, fn):
        for par in range(2):
            pl.when(c % 2 == par)(functools.partial(fn, par))

    def pipelined(c, masked, par):
        scores(c + 1, masked, bufs[1 - par])
        accumulate(c, bufs[par])

    scores(0, True, bufs[0])

    def body(j, carry):
        by_parity(j, functools.partial(pipelined, j, False))
        return carry

    lax.fori_loop(0, jnp.maximum(qi - 1, 0), body, 0)

    @pl.when(qi >= 1)
    def _():
        by_parity(qi - 1, functools.partial(pipelined, qi - 1, True))

    by_parity(qi, lambda par: accumulate(qi, bufs[par]))

    acc = acc_ref[...]
    lane = lax.broadcasted_iota(jnp.int32, (tq, LANES), 1)
    denom = jnp.sum(jnp.where(lane == HEAD_DIM, acc, 0.0), axis=-1, keepdims=True)
    o_ref[0, 0] = jnp.where(lane < HEAD_DIM, acc / denom, 0.0).astype(BF16)


def _flash_attention(q, k, v, tq):
    bsz, nh, s, _ = q.shape
    kv_spec = pl.BlockSpec((1, 1, s, LANES), lambda b, h, i: (b, h, 0, 0))
    blk = pl.BlockSpec((1, 1, tq, LANES), lambda b, h, i: (b, h, i, 0))
    stat = pltpu.VMEM((tq, LANES), F32)
    return pl.pallas_call(
        functools.partial(_flash_kernel, tq=tq),
        grid=(bsz, nh, s // tq),
        in_specs=[blk, kv_spec, kv_spec],
        out_specs=blk,
        out_shape=jax.ShapeDtypeStruct((bsz, nh, s, LANES), BF16),
        scratch_shapes=[pltpu.VMEM((tq, tq), F32), pltpu.VMEM((tq, tq), F32), stat, stat, stat, stat],
        compiler_params=_cparams(("parallel", "parallel", "arbitrary")),
        name="flash_attention",
    )(q, k, v)


def _outproj_kernel(x_ref, yc_ref, om_ref, yr_ref, yl_ref, wc_ref, wm_ref, wr_ref, wl_ref,
                    gmla_ref, gt_ref, gf_ref, scf_ref, shf_ref, wrt_ref, brt_ref, tri_ref,
                    xo_ref, h_ref, meta_ref, cnt_ref, run_ref):
    @pl.when((pl.program_id(0) == 0) & (pl.program_id(1) == 0))
    def _():
        run_ref[...] = jnp.zeros((1, LANES), F32)

    om = [om_ref[0, h].astype(F32) for h in range(N_HEADS)]
    ssq = om[0] * om[0]
    for h in range(1, N_HEADS):
        ssq = ssq + om[h] * om[h]
    r_mla = lax.rsqrt(jnp.sum(ssq, axis=-1, keepdims=True) / GROUP_W + EPS)
    y = _dot(yc_ref[0], wc_ref[...]) + _dot(yr_ref[0], wr_ref[...]) + _dot(yl_ref[0], wl_ref[...])
    for h in range(N_HEADS):
        y = y + _dot((om[h] * r_mla * gmla_ref[h]).astype(BF16), wm_ref[h])
    x = x_ref[0] + gt_ref[0] * y
    xo_ref[0] = x
    hf = _rms_rows(x, gf_ref[...]) * (1.0 + scf_ref[0]) + shf_ref[0]
    h_ref[0] = _pack_bf16_pairs(hf)

    lg = jnp.dot(hf, wrt_ref[...], precision=HIGHEST, preferred_element_type=F32)
    tm = lg.shape[0]
    lane = lax.broadcasted_iota(jnp.int32, (tm, LANES), 1)
    bias = brt_ref[...]
    is_g = (lane >= N_EXPERTS) & (lane < N_EXPERTS + MOE_GROUPS)
    is_e = lane < N_EXPERTS

    def first_argmax(val):
        mx = jnp.max(val, axis=-1, keepdims=True)
        return jnp.min(jnp.where(val == mx, lane, LANES), axis=-1, keepdims=True)

    gl = jnp.where(is_g, lg, NEG_BIG)
    ge = jnp.exp(gl - jnp.max(gl, axis=-1, keepdims=True))
    gp = ge / jnp.sum(ge, axis=-1, keepdims=True)
    g_idx = first_argmax(jnp.where(is_g, gp + bias, NEG_BIG))
    g_weight = jnp.sum(jnp.where(lane == g_idx, gp, 0.0), axis=-1, keepdims=True)
    in_group = is_e & ((lane // EXPERTS_PER_GROUP) == (g_idx - N_EXPERTS))
    el = jnp.where(in_group, lg, NEG_BIG)
    ee = jnp.exp(el - jnp.max(el, axis=-1, keepdims=True))
    ep = ee / jnp.sum(ee, axis=-1, keepdims=True)
    score = jnp.where(in_group, ep + bias, NEG_BIG)
    i1 = first_argmax(score)
    sel1 = lane == i1
    i2 = first_argmax(jnp.where(sel1, NEG_BIG, score))
    sel2 = lane == i2
    p1 = jnp.sum(jnp.where(sel1, ep, 0.0), axis=-1, keepdims=True)
    p2 = jnp.sum(jnp.where(sel2, ep, 0.0), axis=-1, keepdims=True)
    psum = p1 + p2
    w1 = p1 / psum * g_weight
    w2 = p2 / psum * g_weight

    onehot = jnp.where(sel1, 1.0, jnp.where(sel2, 1.0, 0.0)).astype(BF16)
    incl = _dot(tri_ref[...], onehot)
    base = run_ref[...] + incl - 1.0
    r1 = jnp.sum(jnp.where(sel1, base, 0.0), axis=-1, keepdims=True)
    r2 = jnp.sum(jnp.where(sel2, base, 0.0), axis=-1, keepdims=True)
    run_ref[...] = run_ref[...] + incl[tm - 1:tm, :]
    cnt_ref[0] = run_ref[...]
    fields = (i1.astype(F32), i2.astype(F32), r1, r2, w1, w2)
    meta = jnp.zeros((tm, LANES), F32)
    for pos, val in enumerate(fields):
        meta = jnp.where(lane == pos, val, meta)
    meta_ref[0] = meta


def _outproj(x, yc, om, yr, yl, wc, wm, wr, wl, gmla, gt, gf, scf, shf, wrt, brt, tm):
    bsz, s, d = x.shape
    nt = s // tm
    full = lambda a: pl.BlockSpec(a.shape, lambda b, i: (0,) * a.ndim)
    tok = lambda w: pl.BlockSpec((1, tm, w), lambda b, i: (b, i, 0))
    vec = pl.BlockSpec((1, 1, d), lambda b, i: (b, 0, 0))
    tri = jnp.asarray(np.tril(np.ones((tm, tm), np.float32)), BF16)
    return pl.pallas_call(
        _outproj_kernel,
        grid=(bsz, nt),
        in_specs=[tok(d), tok(GROUP_W),
                  pl.BlockSpec((1, N_HEADS, tm, LANES), lambda b, i: (b, 0, i, 0)),
                  tok(GROUP_W), tok(GROUP_W),
                  full(wc), full(wm), full(wr), full(wl), full(gmla), vec, full(gf), vec, vec,
                  full(wrt), full(brt), full(tri)],
        out_specs=[tok(d), tok(d // 2), tok(LANES),
                   pl.BlockSpec((1, 1, LANES), lambda b, i: (b * nt + i, 0, 0))],
        out_shape=[jax.ShapeDtypeStruct((bsz, s, d), F32), jax.ShapeDtypeStruct((bsz, s, d // 2), jnp.uint32),
                   jax.ShapeDtypeStruct((bsz, s, LANES), F32),
                   jax.ShapeDtypeStruct((bsz * nt, 1, LANES), F32)],
        scratch_shapes=[pltpu.VMEM((1, LANES), F32)],
        compiler_params=_cparams(("arbitrary", "arbitrary")),
        name="outproj_router",
    )(x, yc, om, yr, yl, wc, wm, wr, wl, gmla, gt, gf, scf, shf, wrt, brt, tri)


def _route_plan(meta, cnt, tmg):
    t = meta.shape[0] * meta.shape[1]
    m = meta.reshape(t, LANES)
    e = m[:, 0:2].astype(jnp.int32)
    r = m[:, 2:4].astype(jnp.int32)
    counts = cnt[-1, 0, :N_EXPERTS].astype(jnp.int32)
    padded = (counts + tmg - 1) // tmg * tmg
    ends = jnp.cumsum(padded)
    starts = ends - padded
    dest = jnp.take(starts, e) + r
    n_tiles = (2 * t) // tmg + N_EXPERTS
    tile_exp = jnp.searchsorted(ends, jnp.arange(n_tiles, dtype=jnp.int32) * tmg, side="right")
    tile_exp = jnp.minimum(tile_exp, N_EXPERTS - 1).astype(jnp.int32)
    n_used = (ends[-1:] // tmg).astype(jnp.int32)
    last_tile = jnp.maximum(ends - tmg, 0).astype(jnp.int32)
    return dest[:, 0], dest[:, 1], tile_exp, n_used, last_tile, padded.astype(jnp.int32)


def _dispatch_kernel(zs_ref, zv_ref, nu_ref, d1_ref, d2_ref, h_ref, xs_ref, zero_ref, sem, *, tmc, tmg):
    @pl.when(pl.program_id(0) == 0)
    def _():
        zero_ref[...] = jnp.zeros(zero_ref.shape, jnp.uint32)
        n_tiles = xs_ref.shape[0] // tmg

        def fill(start):
            return pltpu.make_async_copy(zero_ref, xs_ref.at[pl.ds(pl.multiple_of(start, tmg), tmg), :], sem)

        def fill_tail(j, carry, wait):
            cp = fill(j * tmg)
            cp.wait() if wait else cp.start()
            return carry

        for wait in (False, True):
            for e in range(N_EXPERTS):
                cp = fill(zs_ref[e])
                pl.when(zv_ref[e] > 0)(cp.wait if wait else cp.start)
            lax.fori_loop(nu_ref[0], n_tiles, functools.partial(fill_tail, wait=wait), 0)

    def push(r, carry):
        src = h_ref.at[pl.ds(r, 1), :]
        pltpu.make_async_copy(src, xs_ref.at[pl.ds(d1_ref[0, 0, r], 1), :], sem).start()
        pltpu.make_async_copy(src, xs_ref.at[pl.ds(d2_ref[0, 0, r], 1), :], sem).start()
        return carry

    lax.fori_loop(0, tmc, push, 0, unroll=8)
    one_row = pltpu.make_async_copy(h_ref.at[pl.ds(0, 1), :], xs_ref.at[pl.ds(0, 1), :], sem)
    for _ in range(2 * tmc):
        one_row.wait()


def _dispatch(hp, d1, d2, last_tile, padded, n_used, tmc, tmg):
    t, dw = hp.shape
    n_rows = 2 * t + N_EXPERTS * tmg
    smem_rows = pl.BlockSpec((1, 1, tmc), lambda i, zs, zv, nu: (i, 0, 0), memory_space=pltpu.SMEM)
    return pl.pallas_call(
        functools.partial(_dispatch_kernel, tmc=tmc, tmg=tmg),
        grid_spec=pltpu.PrefetchScalarGridSpec(
            num_scalar_prefetch=3, grid=(t // tmc,),
            in_specs=[smem_rows, smem_rows, pl.BlockSpec((tmc, dw), lambda i, zs, zv, nu: (i, 0))],
            out_specs=pl.BlockSpec(memory_space=pl.ANY),
            scratch_shapes=[pltpu.VMEM((tmg, dw), jnp.uint32), pltpu.SemaphoreType.DMA(())]),
        out_shape=jax.ShapeDtypeStruct((n_rows, dw), jnp.uint32),
        compiler_params=_cparams(("arbitrary",)),
        name="moe_dispatch",
    )(last_tile, padded, n_used, d1.reshape(t // tmc, 1, tmc), d2.reshape(t // tmc, 1, tmc), hp)


def _experts_kernel(te_ref, nu_ref, xs_ref, wg_ref, wu_ref, wd_ref, y_ref):
    used = pl.program_id(0) < nu_ref[0]

    @pl.when(used)
    def _():
        x = _unpack_bf16_pairs(xs_ref[...]).astype(BF16)
        gate = _dot(x, wg_ref[0])
        hid = gate * _sigmoid(gate) * _dot(x, wu_ref[0])
        y_ref[...] = _pack_bf16_pairs(_dot(hid.astype(BF16), wd_ref[0]))

    @pl.when(jnp.logical_not(used))
    def _():
        y_ref[...] = jnp.zeros(y_ref.shape, jnp.uint32)


def _experts(xs, tile_exp, n_used, wg, wu, wd, tmg):
    n_rows, dw = xs.shape
    d = 2 * dw
    tile = lambda i, te, nu: jnp.minimum(i, nu[0] - 1)
    rows = pl.BlockSpec((tmg, dw), lambda i, te, nu: (tile(i, te, nu), 0))
    wspec = lambda shape: pl.BlockSpec((1,) + shape, lambda i, te, nu: (te[tile(i, te, nu)], 0, 0))
    return pl.pallas_call(
        _experts_kernel,
        grid_spec=pltpu.PrefetchScalarGridSpec(
            num_scalar_prefetch=2, grid=(n_rows // tmg,),
            in_specs=[rows, wspec((d, D_EXPERT)), wspec((d, D_EXPERT)), wspec((D_EXPERT, d))],
            out_specs=pl.BlockSpec((tmg, dw), lambda i, te, nu: (i, 0))),
        out_shape=jax.ShapeDtypeStruct((n_rows, dw), jnp.uint32),
        compiler_params=_cparams(("arbitrary",)),
        name="moe_experts",
    )(tile_exp, n_used, xs, wg, wu, wd)


def _combine_kernel(d1_ref, d2_ref, x_ref, meta_ref, gt_ref, y_ref, o_ref, b1_ref, b2_ref, sem, *, tmc):
    def pull(r, carry):
        pltpu.make_async_copy(y_ref.at[pl.ds(d1_ref[0, 0, r], 1), :], b1_ref.at[pl.ds(r, 1), :], sem).start()
        pltpu.make_async_copy(y_ref.at[pl.ds(d2_ref[0, 0, r], 1), :], b2_ref.at[pl.ds(r, 1), :], sem).start()
        return carry

    lax.fori_loop(0, tmc, pull, 0, unroll=8)
    one_row = pltpu.make_async_copy(y_ref.at[pl.ds(0, 1), :], b1_ref.at[pl.ds(0, 1), :], sem)
    for _ in range(2 * tmc):
        one_row.wait()
    meta = meta_ref[0]
    lane = lax.broadcasted_iota(jnp.int32, meta.shape, 1)
    w1 = jnp.sum(jnp.where(lane == 4, meta, 0.0), axis=-1, keepdims=True)
    w2 = jnp.sum(jnp.where(lane == 5, meta, 0.0), axis=-1, keepdims=True)
    y = w1 * _unpack_bf16_pairs(b1_ref[...]) + w2 * _unpack_bf16_pairs(b2_ref[...])
    o_ref[0] = x_ref[0] + gt_ref[0] * y


def _combine(x, meta, gt, y, d1, d2, tmc):
    bsz, s, d = x.shape
    nt = s // tmc
    t = bsz * s
    smem_rows = pl.BlockSpec((1, 1, tmc), lambda b, i: (b * nt + i, 0, 0), memory_space=pltpu.SMEM)
    tok = lambda w: pl.BlockSpec((1, tmc, w), lambda b, i: (b, i, 0))
    buf = pltpu.VMEM((tmc, d // 2), jnp.uint32)
    return pl.pallas_call(
        functools.partial(_combine_kernel, tmc=tmc),
        grid=(bsz, nt),
        in_specs=[smem_rows, smem_rows, tok(d), tok(LANES),
                  pl.BlockSpec((1, 1, d), lambda b, i: (b, 0, 0)),
                  pl.BlockSpec(memory_space=pl.ANY)],
        out_specs=tok(d),
        out_shape=jax.ShapeDtypeStruct((bsz, s, d), F32),
        scratch_shapes=[buf, buf, pltpu.SemaphoreType.DMA(())],
        compiler_params=_cparams(("arbitrary", "arbitrary")),
        name="moe_combine",
    )(d1.reshape(t // tmc, 1, tmc), d2.reshape(t // tmc, 1, tmc), x, meta, gt, y)


def _layer_weights(l, w_in, mla_q_norm_g, mla_w_uq, mla_kv_norm_g, mla_w_ukv, mla_q_qk_g,
                   mla_k_qk_g, lru_w_a, lru_w_x, mix_norm_g, w_out, router_group_w,
                   router_group_b, router_expert_w, router_expert_b):
    half = RET_DK // 2
    perm = np.concatenate([np.arange(half) + HEAD_DIM * h for h in range(N_HEADS)]
                          + [np.arange(half) + half + HEAD_DIM * h for h in range(N_HEADS)])
    w = w_in[l]
    o_mla, o_ret, o_lru = U_CONV, U_CONV + 352, U_CONV + 352 + U_RET
    w_ret = w[:, o_ret:o_ret + U_RET]
    w_ret = jnp.concatenate([w_ret[:, perm], w_ret[:, GROUP_W + perm], w_ret[:, 2 * GROUP_W:]], axis=1)
    w_all = jnp.concatenate([w_ret, w[:, o_lru:o_lru + U_LRU], w[:, :U_CONV], w[:, o_mla:o_mla + 352],
                             jnp.zeros((D_MODEL, U_MLA - 352), F32)], axis=1).astype(BF16)

    hw = N_HEADS * LANES
    r16 = MLA_ROPE // 2
    wq = mla_w_uq[l].reshape(Q_LORA, N_HEADS, MLA_QK)
    zq = jnp.zeros((Q_LORA, N_HEADS, LANES - MLA_QK), F32)
    q_cols = jnp.concatenate([wq, zq], axis=2).reshape(Q_LORA, hw)
    wq_sw = jnp.concatenate([jnp.zeros((Q_LORA, N_HEADS, MLA_NOPE), F32), wq[:, :, MLA_NOPE + r16:],
                             wq[:, :, MLA_NOPE:MLA_NOPE + r16], zq], axis=2).reshape(Q_LORA, hw)
    wkv = mla_w_ukv[l].reshape(KV_LORA, N_HEADS, MLA_NOPE + HEAD_DIM)
    zk = jnp.zeros((KV_LORA, N_HEADS, LANES - MLA_NOPE), F32)
    k_cols = jnp.concatenate([wkv[:, :, :MLA_NOPE], zk], axis=2).reshape(KV_LORA, hw)
    v_cols = jnp.concatenate([wkv[:, :, MLA_NOPE:], zk], axis=2).reshape(KV_LORA, hw)
    eye = jnp.eye(MLA_ROPE, dtype=F32)
    place = jnp.concatenate([jnp.zeros((MLA_ROPE, MLA_NOPE), F32), eye,
                             jnp.zeros((MLA_ROPE, LANES - MLA_QK), F32)], axis=1)
    eye_sw = jnp.concatenate([eye[:, r16:], eye[:, :r16]], axis=1)
    place_sw = jnp.concatenate([jnp.zeros((MLA_ROPE, MLA_NOPE), F32), eye_sw,
                                jnp.zeros((MLA_ROPE, LANES - MLA_QK), F32)], axis=1)
    n_big = 4 * hw + 2 * LANES
    wbig = jnp.zeros((U_MLA, n_big), F32)
    wbig = wbig.at[:Q_LORA, :hw].set(q_cols).at[:Q_LORA, hw:2 * hw].set(wq_sw)
    wbig = wbig.at[Q_LORA:Q_LORA + KV_LORA, 2 * hw:3 * hw].set(k_cols)
    wbig = wbig.at[Q_LORA:Q_LORA + KV_LORA, 3 * hw:4 * hw].set(v_cols)
    wbig = wbig.at[Q_LORA + KV_LORA:352, 4 * hw:4 * hw + LANES].set(place)
    wbig = wbig.at[Q_LORA + KV_LORA:352, 4 * hw + LANES:].set(place_sw)
    wbig = wbig.astype(BF16)

    gu = jnp.concatenate([mla_q_norm_g[l], mla_kv_norm_g[l], mla_k_qk_g[l][MLA_NOPE:],
                          jnp.zeros((U_MLA - 352,), F32)])[None, :]
    qscale = (MLA_QK ** -0.5) * math.log2(math.e)
    gq_full = mla_q_qk_g[l]
    pad = jnp.zeros((LANES - MLA_QK,), F32)
    gq = (jnp.concatenate([gq_full, pad]) * qscale)[None, :]
    gqs = (jnp.concatenate([jnp.zeros((MLA_NOPE,), F32), gq_full[MLA_NOPE + r16:],
                            gq_full[MLA_NOPE:MLA_NOPE + r16], pad]) * qscale)[None, :]
    gk = jnp.concatenate([mla_k_qk_g[l][:MLA_NOPE], jnp.zeros((LANES - MLA_NOPE,), F32)])[None, :]

    def blockdiag(wb):
        out = jnp.zeros((GROUP_W, GROUP_W), F32)
        for n in range(wb.shape[0]):
            out = out.at[n * HEAD_DIM:(n + 1) * HEAD_DIM, n * HEAD_DIM:(n + 1) * HEAD_DIM].set(wb[n])
        return out.astype(BF16)

    gmix = mix_norm_g[l]
    wo = w_out[l].astype(BF16)
    wm = wo[GROUP_W:2 * GROUP_W].reshape(N_HEADS, HEAD_DIM, D_MODEL)
    wm = jnp.concatenate([wm, jnp.zeros((N_HEADS, LANES - HEAD_DIM, D_MODEL), BF16)], axis=1)
    gmla = jnp.concatenate([gmix[GROUP_W:2 * GROUP_W].reshape(N_HEADS, 1, HEAD_DIM),
                            jnp.zeros((N_HEADS, 1, LANES - HEAD_DIM), F32)], axis=2)
    wrt = jnp.concatenate([router_expert_w[l], router_group_w[l],
                           jnp.zeros((D_MODEL, LANES - N_EXPERTS - MOE_GROUPS), F32)], axis=1)
    brt = jnp.concatenate([router_expert_b[l], router_group_b[l],
                           jnp.zeros((LANES - N_EXPERTS - MOE_GROUPS,), F32)])[None, :]
    return dict(w_all=w_all, wbig=wbig, gu=gu, gq=gq, gqs=gqs, gk=gk,
                wa=blockdiag(lru_w_a[l]), wx=blockdiag(lru_w_x[l]),
                g_conv=gmix[None, :GROUP_W], g_ret=gmix[None, 2 * GROUP_W:3 * GROUP_W],
                g_lru=gmix[None, 3 * GROUP_W:], gmla=gmla,
                wc=wo[:GROUP_W], wm=wm, wr=wo[2 * GROUP_W:3 * GROUP_W], wl=wo[3 * GROUP_W:],
                wrt=wrt, brt=brt)


def _mla_consts():
    seg_u = np.concatenate([np.zeros(Q_LORA), np.ones(KV_LORA), 2 * np.ones(MLA_ROPE),
                            3 * np.ones(U_MLA - 352)])
    mu = jnp.asarray(seg_u[:, None] == seg_u[None, :], BF16)
    invu = jnp.asarray(np.concatenate([np.full(Q_LORA, 1.0 / Q_LORA), np.full(KV_LORA, 1.0 / KV_LORA),
                                       np.full(MLA_ROPE, 1.0 / MLA_ROPE), np.ones(U_MLA - 352)]), F32)[None, :]
    lane = np.arange(N_HEADS * LANES)
    seg_q = (lane // LANES) * 3 + np.where(lane % LANES < MLA_NOPE, 0, np.where(lane % LANES < MLA_QK, 1, 2))
    sq = jnp.asarray(seg_q[:, None] == seg_q[None, :], BF16)
    inv_head = np.concatenate([np.full(MLA_NOPE, 1.0 / MLA_NOPE), np.full(MLA_ROPE, 1.0 / MLA_ROPE),
                               np.ones(LANES - MLA_QK)])
    invq = jnp.asarray(np.tile(inv_head, N_HEADS), F32)[None, :]
    onev = jnp.asarray((np.arange(LANES) == HEAD_DIM).astype(np.float32))[None, :]
    return mu, invu, sq, invq, onev


def kernel(x, c, positions, ada_w, ada_b, norm_mix_g, w_in, conv_w, mla_q_norm_g, mla_w_uq, mla_kv_norm_g, mla_w_ukv, mla_q_qk_g, mla_k_qk_g, lru_conv_w, lru_conv_b, lru_w_a, lru_b_a, lru_w_x, lru_b_x, lru_lambda, mix_norm_g, w_out, norm_ffn_g, router_group_w, router_group_b, router_expert_w, router_expert_b, exp_w_gate, exp_w_up, exp_w_down):
    bsz, s, d = x.shape
    depth = ada_w.shape[0]
    tm = min(512, s)
    chunk = min(256, s)
    tq = min(1024, s)
    tmc = min(512, s)
    tmg = 512

    r16, r32 = MLA_ROPE // 2, RET_DK // 2
    inv = jnp.concatenate([1.0 / (ROPE_BASE ** (jnp.arange(0, MLA_ROPE, 2, dtype=F32) / MLA_ROPE)),
                           1.0 / (ROPE_BASE ** (jnp.arange(0, RET_DK, 2, dtype=F32) / RET_DK))])[:, None]
    cos_t, sin_t = _rope_tables(positions, inv)
    cos_t, sin_t = cos_t.transpose(0, 2, 1), sin_t.transpose(0, 2, 1)
    ones = jnp.ones((bsz, s, MLA_NOPE), F32)
    zpad = jnp.zeros((bsz, s, LANES - MLA_QK), F32)
    cos_m = jnp.concatenate([ones, cos_t[..., :r16], cos_t[..., :r16], zpad], axis=-1)
    sin_m = jnp.concatenate([0.0 * ones, -sin_t[..., :r16], sin_t[..., :r16], zpad], axis=-1)
    cos_r = jnp.tile(cos_t[..., r16:], (1, 1, N_HEADS))
    sin_r = jnp.tile(sin_t[..., r16:], (1, 1, N_HEADS))

    c_pad = jnp.concatenate([c, jnp.zeros((8 - bsz, d), F32)], axis=0)
    mod = _modulation(c_pad, ada_w, ada_b)[:, :bsz]
    ret_consts = _ret_consts(chunk)
    mu, invu, sq, invq, onev = _mla_consts()

    for l in range(depth):
        sh_m, sc_m, gt_m, sh_f, sc_f, gt_f = [m[:, None, :] for m in jnp.split(mod[l], 6, axis=-1)]
        lw = _layer_weights(l, w_in, mla_q_norm_g, mla_w_uq, mla_kv_norm_g, mla_w_ukv, mla_q_qk_g,
                            mla_k_qk_g, lru_w_a, lru_w_x, mix_norm_g, w_out, router_group_w,
                            router_group_b, router_expert_w, router_expert_b)
        u = _inproj(x, norm_mix_g[l][None, :], sc_m, sh_m, lw["w_all"], tm)
        y_conv = _conv_mixer(u, conv_w[l], lw["g_conv"], tm)
        y_lru = _lru_mixer(u, lru_conv_w[l], lru_conv_b[l][None, :], lw["wa"], lru_b_a[l][None, :],
                           lw["wx"], lru_b_x[l][None, :], lru_lambda[l][None, :], lw["g_lru"], tm)
        y_ret = _ret_mixer(u, cos_r, sin_r, ret_consts, lw["g_ret"], chunk)
        q, k, v = _mla_prep(u, cos_m, sin_m, mu, invu, lw["gu"], lw["wbig"], sq, invq,
                            lw["gq"], lw["gqs"], lw["gk"], onev, tm)
        o_mla = _flash_attention(q, k, v, tq)
        x, hp, meta, cnt = _outproj(x, y_conv, o_mla, y_ret, y_lru, lw["wc"], lw["wm"], lw["wr"],
                                    lw["wl"], lw["gmla"], gt_m, norm_ffn_g[l][None, :], sc_f, sh_f,
                                    lw["wrt"], lw["brt"], tm)
        d1, d2, tile_exp, n_used, last_tile, padded = _route_plan(meta, cnt, tmg)
        xs = _dispatch(hp.reshape(bsz * s, d // 2), d1, d2, last_tile, padded, n_used, tmc, tmg)
        ys = _experts(xs, tile_exp, n_used, exp_w_gate[l].astype(BF16), exp_w_up[l].astype(BF16),
                      exp_w_down[l].astype(BF16), tmg)
        x = _combine(x, meta, gt_f, ys, d1, d2, tmc)
    return x
```

```python
import functools
import math

import jax
import jax.numpy as jnp
import numpy as np
from jax import lax
from jax.experimental import pallas as pl
from jax.experimental.pallas import tpu as pltpu

F32 = jnp.float32
BF16 = jnp.bfloat16
HIGHEST = lax.Precision.HIGHEST

D_MODEL = 1024
GROUP_W = 256
HEAD_DIM = 64
N_HEADS = 4
MLA_NOPE = 64
MLA_ROPE = 32
MLA_QK = 96
Q_LORA = 192
KV_LORA = 128
RET_DK = 64
LRU_C = 8.0
MOE_GROUPS = 4
EXPERTS_PER_GROUP = 8
N_EXPERTS = 32
D_EXPERT = 256
ROPE_BASE = 10000.0
EPS = 1e-6

LANES = 128
MXU_DIM = 256
U_RET, U_LRU, U_CONV, U_MLA = 1024, 512, 768, 384
U_COLS = U_RET + U_LRU + U_CONV + U_MLA
N_FREQ = MLA_ROPE // 2 + RET_DK // 2
NEG_BIG = -1e30
VMEM_LIMIT = 56 * 1024 * 1024


def _cparams(sem):
    return pltpu.CompilerParams(dimension_semantics=sem, vmem_limit_bytes=VMEM_LIMIT)


def _dot(a, b):
    return jnp.dot(a, b, preferred_element_type=F32)


def _dot_nt(a, b):
    return lax.dot_general(a, b, (((1,), (1,)), ((), ())), preferred_element_type=F32)


def _dot_tn(a, b):
    return lax.dot_general(a, b, (((0,), (0,)), ((), ())), preferred_element_type=F32)


def _rms_rows(y, g):
    return y * lax.rsqrt(jnp.mean(y * y, axis=-1, keepdims=True) + EPS) * g


def _sigmoid(x):
    return 1.0 / (1.0 + jnp.exp(-x))


def _pack_bf16_pairs(a):
    k = a.shape[1] // 2
    rounded = a.astype(BF16).astype(F32)
    lo = lax.bitcast_convert_type(rounded[:, :k], jnp.uint32) >> 16
    hi = lax.bitcast_convert_type(rounded[:, k:], jnp.uint32) & jnp.uint32(0xFFFF0000)
    return lo | hi


def _unpack_bf16_pairs(w):
    lo = lax.bitcast_convert_type(w << 16, F32)
    hi = lax.bitcast_convert_type(w & jnp.uint32(0xFFFF0000), F32)
    return jnp.concatenate([lo, hi], axis=1)


def _rope_kernel(pos_ref, inv_ref, tab_ref):
    ang = pos_ref[0].astype(F32) * inv_ref[...]
    row = lax.broadcasted_iota(jnp.int32, (LANES - 2 * N_FREQ, ang.shape[1]), 0)
    pad = jnp.where(row == 0, 1.0, 0.0)
    tab_ref[0] = jnp.concatenate([jnp.cos(ang), jnp.sin(ang), pad], axis=0).T


def _rope_tables(positions, inv):
    bsz, s = positions.shape
    ts = min(s, 2048)
    return pl.pallas_call(
        _rope_kernel,
        grid=(bsz, s // ts),
        in_specs=[pl.BlockSpec((1, 1, ts), lambda b, i: (b, 0, i)),
                  pl.BlockSpec((N_FREQ, 1), lambda b, i: (0, 0))],
        out_specs=pl.BlockSpec((1, ts, LANES), lambda b, i: (b, i, 0)),
        out_shape=jax.ShapeDtypeStruct((bsz, s, LANES), F32),
        compiler_params=_cparams(("parallel", "parallel")),
        name="rope_tables",
    )(positions.reshape(bsz, 1, s), inv)


def _expand_trig(tab, expand):
    hi = tab.astype(BF16)
    lo = (tab - hi.astype(F32)).astype(BF16)
    trig = _dot(hi, expand) + _dot(lo, expand)
    return trig[:, :LANES], trig[:, LANES:]


def _mod_kernel(c_ref, w_ref, b_ref, o_ref):
    c = c_ref[...]
    ca = c * _sigmoid(c)
    o_ref[0] = jnp.dot(ca, w_ref[0], precision=HIGHEST, preferred_element_type=F32) + b_ref[0]


def _modulation(c_pad, ada_w, ada_b):
    nl, d, n = ada_w.shape
    tn = 1536
    return pl.pallas_call(
        _mod_kernel,
        grid=(nl, n // tn),
        in_specs=[pl.BlockSpec((8, d), lambda l, j: (0, 0)),
                  pl.BlockSpec((1, d, tn), lambda l, j: (l, 0, j)),
                  pl.BlockSpec((1, 1, tn), lambda l, j: (l, 0, j))],
        out_specs=pl.BlockSpec((1, 8, tn), lambda l, j: (l, 0, j)),
        out_shape=jax.ShapeDtypeStruct((nl, 8, n), F32),
        compiler_params=_cparams(("parallel", "parallel")),
        name="adaln_mod",
    )(c_pad, ada_w, ada_b.reshape(nl, 1, n))


def _inproj_kernel(x_ref, g_ref, sc_ref, sh_ref, w_ref, u_ref):
    x = x_ref[0]
    h = _rms_rows(x, g_ref[...]) * (1.0 + sc_ref[0]) + sh_ref[0]
    u_ref[0] = _dot(h.astype(BF16), w_ref[...]).astype(BF16)


def _inproj(x, g, sc, sh, w, tm):
    bsz, s, d = x.shape
    vec = pl.BlockSpec((1, 1, d), lambda b, i: (b, 0, 0))
    return pl.pallas_call(
        _inproj_kernel,
        grid=(bsz, s // tm),
        in_specs=[pl.BlockSpec((1, tm, d), lambda b, i: (b, i, 0)),
                  pl.BlockSpec((1, d), lambda b, i: (0, 0)),
                  vec, vec,
                  pl.BlockSpec((d, U_COLS), lambda b, i: (0, 0))],
        out_specs=pl.BlockSpec((1, tm, U_COLS), lambda b, i: (b, i, 0)),
        out_shape=jax.ShapeDtypeStruct((bsz, s, U_COLS), BF16),
        compiler_params=_cparams(("parallel", "parallel")),
        name="inproj",
    )(x, g, sc, sh, w)


def _conv_kernel(u_ref, w_ref, g_ref, y_ref, buf_ref, *, tm):
    @pl.when(pl.program_id(1) == 0)
    def _():
        buf_ref[0:8, :] = jnp.zeros((8, GROUP_W), F32)

    u = u_ref[0].astype(F32)
    b_gate, c_gate, xin = u[:, :GROUP_W], u[:, GROUP_W:2 * GROUP_W], u[:, 2 * GROUP_W:]
    cx = c_gate * xin
    buf_ref[8:8 + tm, :] = cx
    conv = (w_ref[2:3, :] * cx + w_ref[1:2, :] * buf_ref[7:7 + tm, :]
            + w_ref[0:1, :] * buf_ref[6:6 + tm, :])
    buf_ref[0:8, :] = cx[tm - 8:, :]
    y_ref[0] = _rms_rows(b_gate * conv, g_ref[...]).astype(BF16)


def _conv_mixer(u, w, g, tm):
    bsz, s, _ = u.shape
    return pl.pallas_call(
        functools.partial(_conv_kernel, tm=tm),
        grid=(bsz, s // tm),
        in_specs=[pl.BlockSpec((1, tm, U_CONV), lambda b, i: (b, i, (U_RET + U_LRU) // U_CONV)),
                  pl.BlockSpec((3, GROUP_W), lambda b, i: (0, 0)),
                  pl.BlockSpec((1, GROUP_W), lambda b, i: (0, 0))],
        out_specs=pl.BlockSpec((1, tm, GROUP_W), lambda b, i: (b, i, 0)),
        out_shape=jax.ShapeDtypeStruct((bsz, s, GROUP_W), BF16),
        scratch_shapes=[pltpu.VMEM((tm + 8, GROUP_W), F32)],
        compiler_params=_cparams(("parallel", "arbitrary")),
        name="conv_mixer",
    )(u, w, g)


def _lru_kernel(u_ref, cw_ref, cb_ref, wa_ref, ba_ref, wx_ref, bx_ref, lam_ref, g_ref,
                y_ref, buf_ref, h_ref, *, tm):
    @pl.when(pl.program_id(1) == 0)
    def _():
        buf_ref[0:8, :] = jnp.zeros((8, GROUP_W), F32)
        h_ref[...] = jnp.zeros((1, GROUP_W), F32)

    u = u_ref[0].astype(F32)
    xraw, gate = u[:, :GROUP_W], u[:, GROUP_W:]
    buf_ref[8:8 + tm, :] = xraw
    xb = (cw_ref[3:4, :] * xraw + cw_ref[2:3, :] * buf_ref[7:7 + tm, :]
          + cw_ref[1:2, :] * buf_ref[6:6 + tm, :] + cw_ref[0:1, :] * buf_ref[5:5 + tm, :]
          + cb_ref[...])
    buf_ref[0:8, :] = xraw[tm - 8:, :]

    xbb = xb.astype(BF16)
    r = _sigmoid(_dot(xbb, wa_ref[...]) + ba_ref[...])
    i = _sigmoid(_dot(xbb, wx_ref[...]) + bx_ref[...])
    nlam = -lam_ref[...]
    softplus = jnp.maximum(nlam, 0.0) + jnp.log(1.0 + jnp.exp(-jnp.abs(nlam)))
    log_a = (-LRU_C) * r * softplus
    a = jnp.exp(log_a)
    b = jnp.sqrt(1.0 - a * a) * (i * xb)

    row = lax.broadcasted_iota(jnp.int32, (tm, GROUP_W), 0)
    d = 1
    while d < tm:
        keep = row >= d
        a_sh = jnp.where(keep, pltpu.roll(a, d, 0), 1.0)
        b_sh = jnp.where(keep, pltpu.roll(b, d, 0), 0.0)
        b = a * b_sh + b
        a = a * a_sh
        d *= 2
    h = a * h_ref[...] + b
    h_ref[...] = h[tm - 1:tm, :]

    gelu = 0.5 * gate * (1.0 + jnp.tanh(math.sqrt(2.0 / math.pi) * (gate + 0.044715 * gate * gate * gate)))
    y_ref[0] = _rms_rows(h * gelu, g_ref[...]).astype(BF16)


def _lru_mixer(u, cw, cb, wa, ba, wx, bx, lam, g, tm):
    bsz, s, _ = u.shape
    row = pl.BlockSpec((1, GROUP_W), lambda b, i: (0, 0))
    mat = pl.BlockSpec((GROUP_W, GROUP_W), lambda b, i: (0, 0))
    return pl.pallas_call(
        functools.partial(_lru_kernel, tm=tm),
        grid=(bsz, s // tm),
        in_specs=[pl.BlockSpec((1, tm, U_LRU), lambda b, i: (b, i, U_RET // U_LRU)),
                  pl.BlockSpec((4, GROUP_W), lambda b, i: (0, 0)),
                  row, mat, row, mat, row, row, row],
        out_specs=pl.BlockSpec((1, tm, GROUP_W), lambda b, i: (b, i, 0)),
        out_shape=jax.ShapeDtypeStruct((bsz, s, GROUP_W), BF16),
        scratch_shapes=[pltpu.VMEM((tm + 8, GROUP_W), F32), pltpu.VMEM((1, GROUP_W), F32)],
        compiler_params=_cparams(("parallel", "arbitrary")),
        name="lru_mixer",
    )(u, cw, cb, wa, ba, wx, bx, lam, g)


def _ret_kernel(u_ref, tab_ref, ex_ref, inner_ref, qd_ref, kd_ref, cd_ref, bm_ref, gm_ref,
                mq_ref, mv_ref, g_ref, y_ref, st_ref):
    @pl.when(pl.program_id(1) == 0)
    def _():
        st_ref[...] = jnp.zeros((GROUP_W, GROUP_W), F32)

    u = u_ref[0].astype(F32)
    q, k = u[:, :GROUP_W], u[:, GROUP_W:2 * GROUP_W]
    v, gate = u[:, 2 * GROUP_W:3 * GROUP_W], u[:, 3 * GROUP_W:]
    cos, sin = _expand_trig(tab_ref[0], ex_ref[...])

    def rope(t):
        t1, t2 = t[:, :LANES], t[:, LANES:]
        return jnp.concatenate([t1 * cos - t2 * sin, t2 * cos + t1 * sin], axis=-1)

    qr = rope(q)
    kr = rope(k) * (RET_DK ** -0.5)
    krb = kr.astype(BF16)
    vb = v.astype(BF16)
    state = st_ref[...]
    o = _dot(qr.astype(BF16), state.astype(BF16)) * qd_ref[...]
    for h in range(N_HEADS):
        qh = (qr * mq_ref[h]).astype(BF16)
        sc = _dot_nt(qh, krb) * inner_ref[h]
        o = o + _dot(sc.astype(BF16), vb) * mv_ref[h]
    st_ref[...] = state * cd_ref[...] + bm_ref[...] * _dot_tn((kr * kd_ref[...]).astype(BF16), vb)

    gm = gm_ref[...]
    o_hi = o.astype(BF16)
    o_lo = (o - o_hi.astype(F32)).astype(BF16)
    mu = _dot(o_hi, gm) + _dot(o_lo, gm)
    dlt = o - mu
    d2 = dlt * dlt
    d2_hi = d2.astype(BF16)
    d2_lo = (d2 - d2_hi.astype(F32)).astype(BF16)
    var = _dot(d2_hi, gm) + _dot(d2_lo, gm)
    y = dlt * lax.rsqrt(var + EPS)
    y = gate * _sigmoid(gate) * y
    y_ref[0] = _rms_rows(y, g_ref[...]).astype(BF16)


def _ret_mixer(u, trig, expand, consts, g, chunk):
    bsz, s, _ = u.shape
    inner, qd, kd, cd, bm, gm, mq, mv = consts
    full = lambda shape: pl.BlockSpec(shape, lambda b, i: (0,) * len(shape))
    tab = pl.BlockSpec((1, chunk, LANES), lambda b, i: (b, i, 0))
    return pl.pallas_call(
        _ret_kernel,
        grid=(bsz, s // chunk),
        in_specs=[pl.BlockSpec((1, chunk, U_RET), lambda b, i: (b, i, 0)), tab, full(expand.shape),
                  full(inner.shape), full(qd.shape), full(kd.shape), full(cd.shape),
                  full(bm.shape), full(gm.shape), full(mq.shape), full(mv.shape),
                  full((1, GROUP_W))],
        out_specs=pl.BlockSpec((1, chunk, GROUP_W), lambda b, i: (b, i, 0)),
        out_shape=jax.ShapeDtypeStruct((bsz, s, GROUP_W), BF16),
        scratch_shapes=[pltpu.VMEM((GROUP_W, GROUP_W), F32)],
        compiler_params=_cparams(("parallel", "arbitrary")),
        name="ret_mixer",
    )(u, trig, expand, inner, qd, kd, cd, bm, gm, mq, mv, g)


def _ret_consts(chunk):
    nh = N_HEADS
    f32 = np.float32
    log_g = np.log(f32(1.0) - f32(2.0) ** (f32(-5.0) - np.arange(nh, dtype=f32)))
    idx = np.arange(chunk, dtype=f32)
    rel = idx[:, None] - idx[None, :]
    inner = np.where(rel >= 0, np.exp(log_g[:, None, None] * np.maximum(rel, 0.0)), 0.0).astype(f32)
    v_head = np.arange(GROUP_W) // HEAD_DIM
    q_head = (np.arange(GROUP_W) % LANES) // (RET_DK // 2)
    qd = np.exp(log_g[v_head][None, :] * (idx[:, None] + 1.0)).astype(f32)
    kd = np.exp(log_g[q_head][None, :] * (chunk - 1.0 - idx[:, None])).astype(f32)
    cd = np.exp(log_g[v_head] * chunk)[None, :].astype(f32)
    bm = (q_head[:, None] == v_head[None, :]).astype(f32)
    gm = jnp.asarray((v_head[:, None] == v_head[None, :]).astype(f32) / HEAD_DIM, BF16)
    mq = (q_head[None, :] == np.arange(nh)[:, None]).astype(f32)[:, None, :]
    mv = (v_head[None, :] == np.arange(nh)[:, None]).astype(f32)[:, None, :]
    return tuple(jnp.asarray(a) for a in (inner, qd, kd, cd, bm)) + (gm, jnp.asarray(mq), jnp.asarray(mv))


def _trig_expanders():
    r16, r32 = MLA_ROPE // 2, RET_DK // 2
    ret = np.zeros((LANES, 2 * LANES), np.float32)
    mla = np.zeros((LANES, 2 * LANES), np.float32)
    for j in range(r32):
        for h in range(N_HEADS):
            ret[r16 + j, h * r32 + j] = 1.0
            ret[N_FREQ + r16 + j, LANES + h * r32 + j] = 1.0
    for j in range(r16):
        for half, sign in ((0, -1.0), (1, 1.0)):
            lane = MLA_NOPE + half * r16 + j
            mla[j, lane] = 1.0
            mla[N_FREQ + j, LANES + lane] = sign
    mla[2 * N_FREQ, :MLA_NOPE] = 1.0
    return jnp.asarray(ret, BF16), jnp.asarray(mla, BF16)


def _mla_prep_kernel(u_ref, tab_ref, ex_ref, mu_ref, invu_ref, gu_ref, wbig_ref, sq_ref,
                     invq_ref, gq_ref, gqs_ref, gk_ref, onev_ref, q_ref, k_ref, v_ref):
    x = u_ref[0].astype(F32)
    ss = _dot((x * x).astype(BF16), mu_ref[...]) * invu_ref[...]
    xn = (x * lax.rsqrt(ss + EPS) * gu_ref[...]).astype(BF16)
    big = _dot(xn, wbig_ref[...])
    hw = N_HEADS * LANES
    q, qs, kn, v = big[:, :hw], big[:, hw:2 * hw], big[:, 2 * hw:3 * hw], big[:, 3 * hw:4 * hw]
    kr, krs = big[:, 4 * hw:4 * hw + LANES], big[:, 4 * hw + LANES:]
    cos, sin = _expand_trig(tab_ref[0], ex_ref[...])
    rq = lax.rsqrt(_dot((q * q).astype(BF16), sq_ref[...]) * invq_ref[...] + EPS)
    rk = lax.rsqrt(_dot((kn * kn).astype(BF16), sq_ref[...]) * invq_ref[...] + EPS)
    krot = kr * cos + krs * sin
    for h in range(N_HEADS):
        sl = slice(h * LANES, (h + 1) * LANES)
        qh = (q[:, sl] * gq_ref[...] * cos + qs[:, sl] * gqs_ref[...] * sin) * rq[:, sl]
        q_ref[0, h] = qh.astype(BF16)
        k_ref[0, h] = (kn[:, sl] * gk_ref[...] * rk[:, sl] + krot).astype(BF16)
        v_ref[0, h] = (v[:, sl] + onev_ref[...]).astype(BF16)


def _mla_prep(u, trig, expand, mu, invu, gu, wbig, sq, invq, gq, gqs, gk, onev, tm):
    bsz, s, _ = u.shape
    full = lambda a: pl.BlockSpec(a.shape, lambda b, i: (0,) * a.ndim)
    tab = pl.BlockSpec((1, tm, LANES), lambda b, i: (b, i, 0))
    out = jax.ShapeDtypeStruct((bsz, N_HEADS, s, LANES), BF16)
    ospec = pl.BlockSpec((1, N_HEADS, tm, LANES), lambda b, i: (b, 0, i, 0))
    return pl.pallas_call(
        _mla_prep_kernel,
        grid=(bsz, s // tm),
        in_specs=[pl.BlockSpec((1, tm, U_MLA), lambda b, i: (b, i, (U_COLS - U_MLA) // U_MLA)),
                  tab, full(expand), full(mu), full(invu), full(gu), full(wbig), full(sq), full(invq),
                  full(gq), full(gqs), full(gk), full(onev)],
        out_specs=[ospec, ospec, ospec],
        out_shape=[out, out, out],
        compiler_params=_cparams(("parallel", "parallel")),
        name="mla_prep",
    )(u, trig, expand, mu, invu, gu, wbig, sq, invq, gq, gqs, gk, onev)


def _flash_kernel(q_ref, k_ref, v_ref, o_ref, sa_ref, sb_ref, mca_ref, mcb_ref, m_ref, acc_ref, *, tq):
    qi = pl.program_id(2)
    q = q_ref[0, 0]
    bufs = ((sa_ref, mca_ref), (sb_ref, mcb_ref))
    m_ref[...] = jnp.full((tq, LANES), NEG_BIG, F32)
    acc_ref[...] = jnp.zeros((tq, LANES), F32)

    def scores(c, masked, dst):
        s_ref, mc_ref = dst
        start = pl.multiple_of(c * tq, tq)
        s = _dot_nt(q, k_ref[0, 0, pl.ds(start, tq), :])
        if masked:
            row = qi * tq + lax.broadcasted_iota(jnp.int32, (tq, tq), 0)
            col = start + lax.broadcasted_iota(jnp.int32, (tq, tq), 1)
            s = jnp.where(col <= row, s, NEG_BIG)
        s_ref[...] = s
        mc_ref[...] = jnp.broadcast_to(jnp.max(s, axis=-1, keepdims=True), (tq, LANES))

    def accumulate(c, src):
        s_ref, mc_ref = src
        start = pl.multiple_of(c * tq, tq)
        m_prev = m_ref[...]
        m_new = jnp.maximum(m_prev, mc_ref[...])
        alpha = jnp.exp2(m_prev - m_new)
        p = jnp.exp2(s_ref[...] - jnp.tile(m_new, (1, tq // LANES)))
        pv = _dot(p.astype(BF16), v_ref[0, 0, pl.ds(start, tq), :])
        acc_ref[...] = alpha * acc_ref[...] + pv
        m_ref[...] = m_new

    def by_parity(c, fn):
        for par in range(2):
            pl.when(c % 2 == par)(functools.partial(fn, par))

    def pipelined(c, masked, par):
        scores(c + 1, masked, bufs[1 - par])
        accumulate(c, bufs[par])

    scores(0, True, bufs[0])

    def body(j, carry):
        by_parity(j, functools.partial(pipelined, j, False))
        return carry

    lax.fori_loop(0, jnp.maximum(qi - 1, 0), body, 0)

    @pl.when(qi >= 1)
    def _():
        by_parity(qi - 1, functools.partial(pipelined, qi - 1, True))

    by_parity(qi, lambda par: accumulate(qi, bufs[par]))

    acc = acc_ref[...]
    lane = lax.broadcasted_iota(jnp.int32, (tq, LANES), 1)
    denom = jnp.sum(jnp.where(lane == HEAD_DIM, acc, 0.0), axis=-1, keepdims=True)
    o_ref[0, 0] = jnp.where(lane < HEAD_DIM, acc / denom, 0.0).astype(BF16)


def _flash_attention(q, k, v, tq):
    bsz, nh, s, _ = q.shape
    kv_spec = pl.BlockSpec((1, 1, s, LANES), lambda b, h, i: (b, h, 0, 0))
    blk = pl.BlockSpec((1, 1, tq, LANES), lambda b, h, i: (b, h, i, 0))
    stat = pltpu.VMEM((tq, LANES), F32)
    return pl.pallas_call(
        functools.partial(_flash_kernel, tq=tq),
        grid=(bsz, nh, s // tq),
        in_specs=[blk, kv_spec, kv_spec],
        out_specs=blk,
        out_shape=jax.ShapeDtypeStruct((bsz, nh, s, LANES), BF16),
        scratch_shapes=[pltpu.VMEM((tq, tq), F32), pltpu.VMEM((tq, tq), F32), stat, stat, stat, stat],
        compiler_params=_cparams(("parallel", "parallel", "arbitrary")),
        name="flash_attention",
    )(q, k, v)


def _outproj_kernel(x_ref, yc_ref, om_ref, yr_ref, yl_ref, wc_ref, wm_ref, wr_ref, wl_ref,
                    gmla_ref, gt_ref, gf_ref, scf_ref, shf_ref, wrt_ref, brt_ref, tri_ref,
                    xo_ref, h_ref, meta_ref, cnt_ref, run_ref):
    @pl.when((pl.program_id(0) == 0) & (pl.program_id(1) == 0))
    def _():
        run_ref[...] = jnp.zeros((1, LANES), F32)

    om = [om_ref[0, h].astype(F32) for h in range(N_HEADS)]
    ssq = om[0] * om[0]
    for h in range(1, N_HEADS):
        ssq = ssq + om[h] * om[h]
    r_mla = lax.rsqrt(jnp.sum(ssq, axis=-1, keepdims=True) / GROUP_W + EPS)
    y = _dot(yc_ref[0], wc_ref[...]) + _dot(yr_ref[0], wr_ref[...]) + _dot(yl_ref[0], wl_ref[...])
    for h in range(N_HEADS):
        y = y + _dot((om[h] * r_mla * gmla_ref[h]).astype(BF16), wm_ref[h])
    x = x_ref[0] + gt_ref[0] * y
    xo_ref[0] = x
    hf = _rms_rows(x, gf_ref[...]) * (1.0 + scf_ref[0]) + shf_ref[0]
    h_ref[0] = _pack_bf16_pairs(hf)

    h_hi = hf.astype(BF16)
    h_lo = (hf - h_hi.astype(F32)).astype(BF16)
    lg = _dot(h_hi, wrt_ref[0]) + _dot(h_lo, wrt_ref[0]) + _dot(h_hi, wrt_ref[1])
    tm = lg.shape[0]
    lane = lax.broadcasted_iota(jnp.int32, (tm, LANES), 1)
    bias = brt_ref[...]
    is_g = (lane >= N_EXPERTS) & (lane < N_EXPERTS + MOE_GROUPS)
    is_e = lane < N_EXPERTS

    def first_argmax(val):
        mx = jnp.max(val, axis=-1, keepdims=True)
        return jnp.min(jnp.where(val == mx, lane, LANES), axis=-1, keepdims=True)

    gl = jnp.where(is_g, lg, NEG_BIG)
    ge = jnp.exp(gl - jnp.max(gl, axis=-1, keepdims=True))
    gp = ge / jnp.sum(ge, axis=-1, keepdims=True)
    g_idx = first_argmax(jnp.where(is_g, gp + bias, NEG_BIG))
    g_weight = jnp.sum(jnp.where(lane == g_idx, gp, 0.0), axis=-1, keepdims=True)
    in_group = is_e & ((lane // EXPERTS_PER_GROUP) == (g_idx - N_EXPERTS))
    el = jnp.where(in_group, lg, NEG_BIG)
    ee = jnp.exp(el - jnp.max(el, axis=-1, keepdims=True))
    ep = ee / jnp.sum(ee, axis=-1, keepdims=True)
    score = jnp.where(in_group, ep + bias, NEG_BIG)
    i1 = first_argmax(score)
    sel1 = lane == i1
    i2 = first_argmax(jnp.where(sel1, NEG_BIG, score))
    sel2 = lane == i2
    p1 = jnp.sum(jnp.where(sel1, ep, 0.0), axis=-1, keepdims=True)
    p2 = jnp.sum(jnp.where(sel2, ep, 0.0), axis=-1, keepdims=True)
    psum = p1 + p2
    w1 = p1 / psum * g_weight
    w2 = p2 / psum * g_weight

    onehot = jnp.where(sel1, 1.0, jnp.where(sel2, 1.0, 0.0)).astype(BF16)
    incl = _dot(tri_ref[...], onehot)
    base = run_ref[...] + incl - 1.0
    r1 = jnp.sum(jnp.where(sel1, base, 0.0), axis=-1, keepdims=True)
    r2 = jnp.sum(jnp.where(sel2, base, 0.0), axis=-1, keepdims=True)
    run_ref[...] = run_ref[...] + incl[tm - 1:tm, :]
    cnt_ref[0] = run_ref[...]
    fields = (i1.astype(F32), i2.astype(F32), r1, r2, w1, w2)
    meta = jnp.zeros((tm, LANES), F32)
    for pos, val in enumerate(fields):
        meta = jnp.where(lane == pos, val, meta)
    meta_ref[0] = meta


def _outproj(x, yc, om, yr, yl, wc, wm, wr, wl, gmla, gt, gf, scf, shf, wrt, brt, tm):
    bsz, s, d = x.shape
    nt = s // tm
    full = lambda a: pl.BlockSpec(a.shape, lambda b, i: (0,) * a.ndim)
    tok = lambda w: pl.BlockSpec((1, tm, w), lambda b, i: (b, i, 0))
    vec = pl.BlockSpec((1, 1, d), lambda b, i: (b, 0, 0))
    tri = jnp.asarray(np.tril(np.ones((tm, tm), np.float32)), BF16)
    return pl.pallas_call(
        _outproj_kernel,
        grid=(bsz, nt),
        in_specs=[tok(d), tok(GROUP_W),
                  pl.BlockSpec((1, N_HEADS, tm, LANES), lambda b, i: (b, 0, i, 0)),
                  tok(GROUP_W), tok(GROUP_W),
                  full(wc), full(wm), full(wr), full(wl), full(gmla), vec, full(gf), vec, vec,
                  full(wrt), full(brt), full(tri)],
        out_specs=[tok(d), tok(d // 2), tok(LANES),
                   pl.BlockSpec((1, 1, LANES), lambda b, i: (b * nt + i, 0, 0))],
        out_shape=[jax.ShapeDtypeStruct((bsz, s, d), F32), jax.ShapeDtypeStruct((bsz, s, d // 2), jnp.uint32),
                   jax.ShapeDtypeStruct((bsz, s, LANES), F32),
                   jax.ShapeDtypeStruct((bsz * nt, 1, LANES), F32)],
        scratch_shapes=[pltpu.VMEM((1, LANES), F32)],
        compiler_params=_cparams(("arbitrary", "arbitrary")),
        name="outproj_router",
    )(x, yc, om, yr, yl, wc, wm, wr, wl, gmla, gt, gf, scf, shf, wrt, brt, tri)


def _route_plan(meta, cnt, tmg):
    t = meta.shape[0] * meta.shape[1]
    m = meta.reshape(t, LANES)
    e = m[:, 0:2].astype(jnp.int32)
    r = m[:, 2:4].astype(jnp.int32)
    counts = cnt[-1, 0, :N_EXPERTS].astype(jnp.int32)
    padded = (counts + tmg - 1) // tmg * tmg
    ends = jnp.cumsum(padded)
    starts = ends - padded
    dest = jnp.take(starts, e) + r
    n_tiles = (2 * t) // tmg + N_EXPERTS
    tile_start = jnp.arange(n_tiles, dtype=jnp.int32) * tmg
    tile_exp = jnp.sum((ends[None, :] <= tile_start[:, None]).astype(jnp.int32), axis=1)
    tile_exp = jnp.minimum(tile_exp, N_EXPERTS - 1)
    n_used = (ends[-1:] // tmg).astype(jnp.int32)
    last_tile = jnp.maximum(ends - tmg, 0).astype(jnp.int32)
    return dest[:, 0], dest[:, 1], tile_exp, n_used, last_tile, padded.astype(jnp.int32)


def _dispatch_kernel(zs_ref, zv_ref, nu_ref, d1_ref, d2_ref, h_ref, xs_ref, zero_ref, sem, *, tmc, tmg):
    @pl.when(pl.program_id(0) == 0)
    def _():
        zero_ref[...] = jnp.zeros(zero_ref.shape, jnp.uint32)
        n_tiles = xs_ref.shape[0] // tmg

        def fill(start):
            return pltpu.make_async_copy(zero_ref, xs_ref.at[pl.ds(pl.multiple_of(start, tmg), tmg), :], sem)

        def fill_tail(j, carry, wait):
            cp = fill(j * tmg)
            cp.wait() if wait else cp.start()
            return carry

        for wait in (False, True):
            for e in range(N_EXPERTS):
                cp = fill(zs_ref[e])
                pl.when(zv_ref[e] > 0)(cp.wait if wait else cp.start)
            lax.fori_loop(nu_ref[0], n_tiles, functools.partial(fill_tail, wait=wait), 0)

    def push(r, carry):
        src = h_ref.at[pl.ds(r, 1), :]
        pltpu.make_async_copy(src, xs_ref.at[pl.ds(d1_ref[0, 0, r], 1), :], sem).start()
        pltpu.make_async_copy(src, xs_ref.at[pl.ds(d2_ref[0, 0, r], 1), :], sem).start()
        return carry

    lax.fori_loop(0, tmc, push, 0, unroll=8)
    one_row = pltpu.make_async_copy(h_ref.at[pl.ds(0, 1), :], xs_ref.at[pl.ds(0, 1), :], sem)
    for _ in range(2 * tmc):
        one_row.wait()


def _dispatch(hp, d1, d2, last_tile, padded, n_used, tmc, tmg):
    t, dw = hp.shape
    n_rows = 2 * t + N_EXPERTS * tmg
    smem_rows = pl.BlockSpec((1, 1, tmc), lambda i, zs, zv, nu: (i, 0, 0), memory_space=pltpu.SMEM)
    return pl.pallas_call(
        functools.partial(_dispatch_kernel, tmc=tmc, tmg=tmg),
        grid_spec=pltpu.PrefetchScalarGridSpec(
            num_scalar_prefetch=3, grid=(t // tmc,),
            in_specs=[smem_rows, smem_rows, pl.BlockSpec((tmc, dw), lambda i, zs, zv, nu: (i, 0))],
            out_specs=pl.BlockSpec(memory_space=pl.ANY),
            scratch_shapes=[pltpu.VMEM((tmg, dw), jnp.uint32), pltpu.SemaphoreType.DMA(())]),
        out_shape=jax.ShapeDtypeStruct((n_rows, dw), jnp.uint32),
        compiler_params=_cparams(("arbitrary",)),
        name="moe_dispatch",
    )(last_tile, padded, n_used, d1.reshape(t // tmc, 1, tmc), d2.reshape(t // tmc, 1, tmc), hp)


def _experts_kernel(te_ref, nu_ref, xs_ref, wg_ref, wu_ref, wd_ref, y_ref):
    used = pl.program_id(0) < nu_ref[0]

    @pl.when(used)
    def _():
        x = _unpack_bf16_pairs(xs_ref[...]).astype(BF16)
        gate = _dot(x, wg_ref[0, 0].astype(BF16))
        hid = gate * _sigmoid(gate) * _dot(x, wu_ref[0, 0].astype(BF16))
        y_ref[...] = _pack_bf16_pairs(_dot(hid.astype(BF16), wd_ref[0, 0].astype(BF16)))

    @pl.when(jnp.logical_not(used))
    def _():
        y_ref[...] = jnp.zeros(y_ref.shape, jnp.uint32)


def _experts(xs, tile_exp, n_used, layer, wg, wu, wd, tmg):
    n_rows, dw = xs.shape
    d = 2 * dw
    tile = lambda i, te, nu: jnp.minimum(i, nu[0] - 1)
    rows = pl.BlockSpec((tmg, dw), lambda i, te, nu: (tile(i, te, nu), 0))
    wspec = lambda shape: pl.BlockSpec((1, 1) + shape,
                                       lambda i, te, nu: (layer, te[tile(i, te, nu)], 0, 0))
    return pl.pallas_call(
        _experts_kernel,
        grid_spec=pltpu.PrefetchScalarGridSpec(
            num_scalar_prefetch=2, grid=(n_rows // tmg,),
            in_specs=[rows, wspec((d, D_EXPERT)), wspec((d, D_EXPERT)), wspec((D_EXPERT, d))],
            out_specs=pl.BlockSpec((tmg, dw), lambda i, te, nu: (i, 0))),
        out_shape=jax.ShapeDtypeStruct((n_rows, dw), jnp.uint32),
        compiler_params=_cparams(("arbitrary",)),
        name="moe_experts",
    )(tile_exp, n_used, xs, wg, wu, wd)


def _combine_kernel(d1_ref, d2_ref, x_ref, meta_ref, gt_ref, y_ref, o_ref, b1_ref, b2_ref, sem, *, tmc):
    def pull(r, carry):
        pltpu.make_async_copy(y_ref.at[pl.ds(d1_ref[0, 0, r], 1), :], b1_ref.at[pl.ds(r, 1), :], sem).start()
        pltpu.make_async_copy(y_ref.at[pl.ds(d2_ref[0, 0, r], 1), :], b2_ref.at[pl.ds(r, 1), :], sem).start()
        return carry

    lax.fori_loop(0, tmc, pull, 0, unroll=8)
    one_row = pltpu.make_async_copy(y_ref.at[pl.ds(0, 1), :], b1_ref.at[pl.ds(0, 1), :], sem)
    for _ in range(2 * tmc):
        one_row.wait()
    meta = meta_ref[0]
    lane = lax.broadcasted_iota(jnp.int32, meta.shape, 1)
    w1 = jnp.sum(jnp.where(lane == 4, meta, 0.0), axis=-1, keepdims=True)
    w2 = jnp.sum(jnp.where(lane == 5, meta, 0.0), axis=-1, keepdims=True)
    y = w1 * _unpack_bf16_pairs(b1_ref[...]) + w2 * _unpack_bf16_pairs(b2_ref[...])
    o_ref[0] = x_ref[0] + gt_ref[0] * y


def _combine(x, meta, gt, y, d1, d2, tmc):
    bsz, s, d = x.shape
    nt = s // tmc
    t = bsz * s
    smem_rows = pl.BlockSpec((1, 1, tmc), lambda b, i: (b * nt + i, 0, 0), memory_space=pltpu.SMEM)
    tok = lambda w: pl.BlockSpec((1, tmc, w), lambda b, i: (b, i, 0))
    buf = pltpu.VMEM((tmc, d // 2), jnp.uint32)
    return pl.pallas_call(
        functools.partial(_combine_kernel, tmc=tmc),
        grid=(bsz, nt),
        in_specs=[smem_rows, smem_rows, tok(d), tok(LANES),
                  pl.BlockSpec((1, 1, d), lambda b, i: (b, 0, 0)),
                  pl.BlockSpec(memory_space=pl.ANY)],
        out_specs=tok(d),
        out_shape=jax.ShapeDtypeStruct((bsz, s, d), F32),
        scratch_shapes=[buf, buf, pltpu.SemaphoreType.DMA(())],
        compiler_params=_cparams(("arbitrary", "arbitrary")),
        name="moe_combine",
    )(d1.reshape(t // tmc, 1, tmc), d2.reshape(t // tmc, 1, tmc), x, meta, gt, y)


def _layer_weights(l, w_in, mla_q_norm_g, mla_w_uq, mla_kv_norm_g, mla_w_ukv, mla_q_qk_g,
                   mla_k_qk_g, lru_w_a, lru_w_x, mix_norm_g, w_out, router_group_w,
                   router_group_b, router_expert_w, router_expert_b):
    half = RET_DK // 2
    perm = np.concatenate([np.arange(half) + HEAD_DIM * h for h in range(N_HEADS)]
                          + [np.arange(half) + half + HEAD_DIM * h for h in range(N_HEADS)])
    w = w_in[l]
    o_mla, o_ret, o_lru = U_CONV, U_CONV + 352, U_CONV + 352 + U_RET
    w_ret = w[:, o_ret:o_ret + U_RET]
    w_ret = jnp.concatenate([w_ret[:, perm], w_ret[:, GROUP_W + perm], w_ret[:, 2 * GROUP_W:]], axis=1)
    w_all = jnp.concatenate([w_ret, w[:, o_lru:o_lru + U_LRU], w[:, :U_CONV], w[:, o_mla:o_mla + 352],
                             jnp.zeros((D_MODEL, U_MLA - 352), F32)], axis=1).astype(BF16)

    hw = N_HEADS * LANES
    r16 = MLA_ROPE // 2
    wq = mla_w_uq[l].reshape(Q_LORA, N_HEADS, MLA_QK)
    zq = jnp.zeros((Q_LORA, N_HEADS, LANES - MLA_QK), F32)
    q_cols = jnp.concatenate([wq, zq], axis=2).reshape(Q_LORA, hw)
    wq_sw = jnp.concatenate([jnp.zeros((Q_LORA, N_HEADS, MLA_NOPE), F32), wq[:, :, MLA_NOPE + r16:],
                             wq[:, :, MLA_NOPE:MLA_NOPE + r16], zq], axis=2).reshape(Q_LORA, hw)
    wkv = mla_w_ukv[l].reshape(KV_LORA, N_HEADS, MLA_NOPE + HEAD_DIM)
    zk = jnp.zeros((KV_LORA, N_HEADS, LANES - MLA_NOPE), F32)
    k_cols = jnp.concatenate([wkv[:, :, :MLA_NOPE], zk], axis=2).reshape(KV_LORA, hw)
    v_cols = jnp.concatenate([wkv[:, :, MLA_NOPE:], zk], axis=2).reshape(KV_LORA, hw)
    eye = jnp.eye(MLA_ROPE, dtype=F32)
    place = jnp.concatenate([jnp.zeros((MLA_ROPE, MLA_NOPE), F32), eye,
                             jnp.zeros((MLA_ROPE, LANES - MLA_QK), F32)], axis=1)
    eye_sw = jnp.concatenate([eye[:, r16:], eye[:, :r16]], axis=1)
    place_sw = jnp.concatenate([jnp.zeros((MLA_ROPE, MLA_NOPE), F32), eye_sw,
                                jnp.zeros((MLA_ROPE, LANES - MLA_QK), F32)], axis=1)
    n_big = 4 * hw + 2 * LANES
    wbig = jnp.zeros((U_MLA, n_big), F32)
    wbig = wbig.at[:Q_LORA, :hw].set(q_cols).at[:Q_LORA, hw:2 * hw].set(wq_sw)
    wbig = wbig.at[Q_LORA:Q_LORA + KV_LORA, 2 * hw:3 * hw].set(k_cols)
    wbig = wbig.at[Q_LORA:Q_LORA + KV_LORA, 3 * hw:4 * hw].set(v_cols)
    wbig = wbig.at[Q_LORA + KV_LORA:352, 4 * hw:4 * hw + LANES].set(place)
    wbig = wbig.at[Q_LORA + KV_LORA:352, 4 * hw + LANES:].set(place_sw)
    wbig = wbig.astype(BF16)

    gu = jnp.concatenate([mla_q_norm_g[l], mla_kv_norm_g[l], mla_k_qk_g[l][MLA_NOPE:],
                          jnp.zeros((U_MLA - 352,), F32)])[None, :]
    qscale = (MLA_QK ** -0.5) * math.log2(math.e)
    gq_full = mla_q_qk_g[l]
    pad = jnp.zeros((LANES - MLA_QK,), F32)
    gq = (jnp.concatenate([gq_full, pad]) * qscale)[None, :]
    gqs = (jnp.concatenate([jnp.zeros((MLA_NOPE,), F32), gq_full[MLA_NOPE + r16:],
                            gq_full[MLA_NOPE:MLA_NOPE + r16], pad]) * qscale)[None, :]
    gk = jnp.concatenate([mla_k_qk_g[l][:MLA_NOPE], jnp.zeros((LANES - MLA_NOPE,), F32)])[None, :]

    def blockdiag(wb):
        out = jnp.zeros((GROUP_W, GROUP_W), F32)
        for n in range(wb.shape[0]):
            out = out.at[n * HEAD_DIM:(n + 1) * HEAD_DIM, n * HEAD_DIM:(n + 1) * HEAD_DIM].set(wb[n])
        return out.astype(BF16)

    gmix = mix_norm_g[l]
    wo = w_out[l].astype(BF16)
    wm = wo[GROUP_W:2 * GROUP_W].reshape(N_HEADS, HEAD_DIM, D_MODEL)
    wm = jnp.concatenate([wm, jnp.zeros((N_HEADS, LANES - HEAD_DIM, D_MODEL), BF16)], axis=1)
    gmla = jnp.concatenate([gmix[GROUP_W:2 * GROUP_W].reshape(N_HEADS, 1, HEAD_DIM),
                            jnp.zeros((N_HEADS, 1, LANES - HEAD_DIM), F32)], axis=2)
    wrt = jnp.concatenate([router_expert_w[l], router_group_w[l],
                           jnp.zeros((D_MODEL, LANES - N_EXPERTS - MOE_GROUPS), F32)], axis=1)
    wrt_hi = wrt.astype(BF16)
    wrt = jnp.stack([wrt_hi, (wrt - wrt_hi.astype(F32)).astype(BF16)])
    brt = jnp.concatenate([router_expert_b[l], router_group_b[l],
                           jnp.zeros((LANES - N_EXPERTS - MOE_GROUPS,), F32)])[None, :]
    return dict(w_all=w_all, wbig=wbig, gu=gu, gq=gq, gqs=gqs, gk=gk,
                wa=blockdiag(lru_w_a[l]), wx=blockdiag(lru_w_x[l]),
                g_conv=gmix[None, :GROUP_W], g_ret=gmix[None, 2 * GROUP_W:3 * GROUP_W],
                g_lru=gmix[None, 3 * GROUP_W:], gmla=gmla,
                wc=wo[:GROUP_W], wm=wm, wr=wo[2 * GROUP_W:3 * GROUP_W], wl=wo[3 * GROUP_W:],
                wrt=wrt, brt=brt)


def _mla_consts():
    seg_u = np.concatenate([np.zeros(Q_LORA), np.ones(KV_LORA), 2 * np.ones(MLA_ROPE),
                            3 * np.ones(U_MLA - 352)])
    mu = jnp.asarray(seg_u[:, None] == seg_u[None, :], BF16)
    invu = jnp.asarray(np.concatenate([np.full(Q_LORA, 1.0 / Q_LORA), np.full(KV_LORA, 1.0 / KV_LORA),
                                       np.full(MLA_ROPE, 1.0 / MLA_ROPE), np.ones(U_MLA - 352)]), F32)[None, :]
    lane = np.arange(N_HEADS * LANES)
    seg_q = (lane // LANES) * 3 + np.where(lane % LANES < MLA_NOPE, 0, np.where(lane % LANES < MLA_QK, 1, 2))
    sq = jnp.asarray(seg_q[:, None] == seg_q[None, :], BF16)
    inv_head = np.concatenate([np.full(MLA_NOPE, 1.0 / MLA_NOPE), np.full(MLA_ROPE, 1.0 / MLA_ROPE),
                               np.ones(LANES - MLA_QK)])
    invq = jnp.asarray(np.tile(inv_head, N_HEADS), F32)[None, :]
    onev = jnp.asarray((np.arange(LANES) == HEAD_DIM).astype(np.float32))[None, :]
    return mu, invu, sq, invq, onev


def kernel(x, c, positions, ada_w, ada_b, norm_mix_g, w_in, conv_w, mla_q_norm_g, mla_w_uq, mla_kv_norm_g, mla_w_ukv, mla_q_qk_g, mla_k_qk_g, lru_conv_w, lru_conv_b, lru_w_a, lru_b_a, lru_w_x, lru_b_x, lru_lambda, mix_norm_g, w_out, norm_ffn_g, router_group_w, router_group_b, router_expert_w, router_expert_b, exp_w_gate, exp_w_up, exp_w_down):
    bsz, s, d = x.shape
    depth = ada_w.shape[0]
    tm = min(512, s)
    chunk = min(256, s)
    tq = min(1024, s)
    tmc = min(512, s)
    tmg = 512

    inv = jnp.concatenate([1.0 / (ROPE_BASE ** (jnp.arange(0, MLA_ROPE, 2, dtype=F32) / MLA_ROPE)),
                           1.0 / (ROPE_BASE ** (jnp.arange(0, RET_DK, 2, dtype=F32) / RET_DK))])[:, None]
    trig = _rope_tables(positions, inv)
    ex_ret, ex_mla = _trig_expanders()

    c_pad = jnp.concatenate([c, jnp.zeros((8 - bsz, d), F32)], axis=0)
    mod = _modulation(c_pad, ada_w, ada_b)[:, :bsz]
    ret_consts = _ret_consts(chunk)
    mu, invu, sq, invq, onev = _mla_consts()

    for l in range(depth):
        sh_m, sc_m, gt_m, sh_f, sc_f, gt_f = [m[:, None, :] for m in jnp.split(mod[l], 6, axis=-1)]
        lw = _layer_weights(l, w_in, mla_q_norm_g, mla_w_uq, mla_kv_norm_g, mla_w_ukv, mla_q_qk_g,
                            mla_k_qk_g, lru_w_a, lru_w_x, mix_norm_g, w_out, router_group_w,
                            router_group_b, router_expert_w, router_expert_b)
        u = _inproj(x, norm_mix_g[l][None, :], sc_m, sh_m, lw["w_all"], tm)
        y_conv = _conv_mixer(u, conv_w[l], lw["g_conv"], tm)
        y_lru = _lru_mixer(u, lru_conv_w[l], lru_conv_b[l][None, :], lw["wa"], lru_b_a[l][None, :],
                           lw["wx"], lru_b_x[l][None, :], lru_lambda[l][None, :], lw["g_lru"], tm)
        y_ret = _ret_mixer(u, trig, ex_ret, ret_consts, lw["g_ret"], chunk)
        q, k, v = _mla_prep(u, trig, ex_mla, mu, invu, lw["gu"], lw["wbig"], sq, invq,
                            lw["gq"], lw["gqs"], lw["gk"], onev, tm)
        o_mla = _flash_attention(q, k, v, tq)
        x, hp, meta, cnt = _outproj(x, y_conv, o_mla, y_ret, y_lru, lw["wc"], lw["wm"], lw["wr"],
                                    lw["wl"], lw["gmla"], gt_m, norm_ffn_g[l][None, :], sc_f, sh_f,
                                    lw["wrt"], lw["brt"], tm)
        d1, d2, tile_exp, n_used, last_tile, padded = _route_plan(meta, cnt, tmg)
        xs = _dispatch(hp.reshape(bsz * s, d // 2), d1, d2, last_tile, padded, n_used, tmc, tmg)
        ys = _experts(xs, tile_exp, n_used, l, exp_w_gate, exp_w_up, exp_w_down, tmg)
        x = _combine(x, meta, gt_f, ys, d1, d2, tmc)
    return x
```

```python
import functools
import math

import jax
import jax.numpy as jnp
import numpy as np
from jax import lax
from jax.experimental import pallas as pl
from jax.experimental.pallas import tpu as pltpu

F32 = jnp.float32
BF16 = jnp.bfloat16
HIGHEST = lax.Precision.HIGHEST

D_MODEL = 1024
GROUP_W = 256
HEAD_DIM = 64
N_HEADS = 4
MLA_NOPE = 64
MLA_ROPE = 32
MLA_QK = 96
Q_LORA = 192
KV_LORA = 128
RET_DK = 64
LRU_C = 8.0
MOE_GROUPS = 4
EXPERTS_PER_GROUP = 8
N_EXPERTS = 32
D_EXPERT = 256
ROPE_BASE = 10000.0
EPS = 1e-6

LANES = 128
MXU_DIM = 256
U_RET, U_LRU, U_CONV, U_MLA = 1024, 512, 768, 384
U_COLS = U_RET + U_LRU + U_CONV + U_MLA
N_FREQ = MLA_ROPE // 2 + RET_DK // 2
NEG_BIG = -1e30
VMEM_LIMIT = 56 * 1024 * 1024


def _cparams(sem):
    return pltpu.CompilerParams(dimension_semantics=sem, vmem_limit_bytes=VMEM_LIMIT)


def _dot(a, b):
    return jnp.dot(a, b, preferred_element_type=F32)


def _dot_nt(a, b):
    return lax.dot_general(a, b, (((1,), (1,)), ((), ())), preferred_element_type=F32)


def _dot_tn(a, b):
    return lax.dot_general(a, b, (((0,), (0,)), ((), ())), preferred_element_type=F32)


def _rms_rows(y, g):
    return y * lax.rsqrt(jnp.mean(y * y, axis=-1, keepdims=True) + EPS) * g


def _sigmoid(x):
    return 1.0 / (1.0 + jnp.exp(-x))


def _pack_bf16_pairs(a):
    k = a.shape[1] // 2
    rounded = a.astype(BF16).astype(F32)
    lo = lax.bitcast_convert_type(rounded[:, :k], jnp.uint32) >> 16
    hi = lax.bitcast_convert_type(rounded[:, k:], jnp.uint32) & jnp.uint32(0xFFFF0000)
    return lo | hi


def _unpack_bf16_pairs(w):
    lo = lax.bitcast_convert_type(w << 16, F32)
    hi = lax.bitcast_convert_type(w & jnp.uint32(0xFFFF0000), F32)
    return jnp.concatenate([lo, hi], axis=1)


def _rope_kernel(pos_ref, inv_ref, tab_ref):
    ang = pos_ref[0].astype(F32) * inv_ref[...]
    row = lax.broadcasted_iota(jnp.int32, (LANES - 2 * N_FREQ, ang.shape[1]), 0)
    pad = jnp.where(row == 0, 1.0, 0.0)
    tab_ref[0] = jnp.concatenate([jnp.cos(ang), jnp.sin(ang), pad], axis=0).T


def _rope_tables(positions, inv):
    bsz, s = positions.shape
    ts = min(s, 2048)
    return pl.pallas_call(
        _rope_kernel,
        grid=(bsz, s // ts),
        in_specs=[pl.BlockSpec((1, 1, ts), lambda b, i: (b, 0, i)),
                  pl.BlockSpec((N_FREQ, 1), lambda b, i: (0, 0))],
        out_specs=pl.BlockSpec((1, ts, LANES), lambda b, i: (b, i, 0)),
        out_shape=jax.ShapeDtypeStruct((bsz, s, LANES), F32),
        compiler_params=_cparams(("parallel", "parallel")),
        name="rope_tables",
    )(positions.reshape(bsz, 1, s), inv)


def _expand_trig(tab, expand):
    hi = tab.astype(BF16)
    lo = (tab - hi.astype(F32)).astype(BF16)
    trig = _dot(hi, expand) + _dot(lo, expand)
    return trig[:, :LANES], trig[:, LANES:]


def _mod_kernel(c_ref, w_ref, b_ref, o_ref):
    c = c_ref[...]
    ca = c * _sigmoid(c)
    o_ref[0] = jnp.dot(ca, w_ref[0], precision=HIGHEST, preferred_element_type=F32) + b_ref[0]


def _modulation(c_pad, ada_w, ada_b):
    nl, d, n = ada_w.shape
    tn = 1536
    return pl.pallas_call(
        _mod_kernel,
        grid=(nl, n // tn),
        in_specs=[pl.BlockSpec((8, d), lambda l, j: (0, 0)),
                  pl.BlockSpec((1, d, tn), lambda l, j: (l, 0, j)),
                  pl.BlockSpec((1, 1, tn), lambda l, j: (l, 0, j))],
        out_specs=pl.BlockSpec((1, 8, tn), lambda l, j: (l, 0, j)),
        out_shape=jax.ShapeDtypeStruct((nl, 8, n), F32),
        compiler_params=_cparams(("parallel", "parallel")),
        name="adaln_mod",
    )(c_pad, ada_w, ada_b.reshape(nl, 1, n))


def _inproj_kernel(x_ref, g_ref, sc_ref, sh_ref, w_ref, u_ref):
    x = x_ref[0]
    h = _rms_rows(x, g_ref[...]) * (1.0 + sc_ref[0]) + sh_ref[0]
    u_ref[0] = _dot(h.astype(BF16), w_ref[...]).astype(BF16)


def _inproj(x, g, sc, sh, w, tm):
    bsz, s, d = x.shape
    vec = pl.BlockSpec((1, 1, d), lambda b, i: (b, 0, 0))
    return pl.pallas_call(
        _inproj_kernel,
        grid=(bsz, s // tm),
        in_specs=[pl.BlockSpec((1, tm, d), lambda b, i: (b, i, 0)),
                  pl.BlockSpec((1, d), lambda b, i: (0, 0)),
                  vec, vec,
                  pl.BlockSpec((d, U_COLS), lambda b, i: (0, 0))],
        out_specs=pl.BlockSpec((1, tm, U_COLS), lambda b, i: (b, i, 0)),
        out_shape=jax.ShapeDtypeStruct((bsz, s, U_COLS), BF16),
        compiler_params=_cparams(("parallel", "parallel")),
        name="inproj",
    )(x, g, sc, sh, w)


def _conv_kernel(u_ref, w_ref, g_ref, y_ref, buf_ref, *, tm):
    @pl.when(pl.program_id(1) == 0)
    def _():
        buf_ref[0:8, :] = jnp.zeros((8, GROUP_W), F32)

    u = u_ref[0].astype(F32)
    b_gate, c_gate, xin = u[:, :GROUP_W], u[:, GROUP_W:2 * GROUP_W], u[:, 2 * GROUP_W:]
    cx = c_gate * xin
    buf_ref[8:8 + tm, :] = cx
    conv = (w_ref[2:3, :] * cx + w_ref[1:2, :] * buf_ref[7:7 + tm, :]
            + w_ref[0:1, :] * buf_ref[6:6 + tm, :])
    buf_ref[0:8, :] = cx[tm - 8:, :]
    y_ref[0] = _rms_rows(b_gate * conv, g_ref[...]).astype(BF16)


def _conv_mixer(u, w, g, tm):
    bsz, s, _ = u.shape
    return pl.pallas_call(
        functools.partial(_conv_kernel, tm=tm),
        grid=(bsz, s // tm),
        in_specs=[pl.BlockSpec((1, tm, U_CONV), lambda b, i: (b, i, (U_RET + U_LRU) // U_CONV)),
                  pl.BlockSpec((3, GROUP_W), lambda b, i: (0, 0)),
                  pl.BlockSpec((1, GROUP_W), lambda b, i: (0, 0))],
        out_specs=pl.BlockSpec((1, tm, GROUP_W), lambda b, i: (b, i, 0)),
        out_shape=jax.ShapeDtypeStruct((bsz, s, GROUP_W), BF16),
        scratch_shapes=[pltpu.VMEM((tm + 8, GROUP_W), F32)],
        compiler_params=_cparams(("parallel", "arbitrary")),
        name="conv_mixer",
    )(u, w, g)


def _lru_kernel(u_ref, cw_ref, cb_ref, wa_ref, ba_ref, wx_ref, bx_ref, lam_ref, g_ref,
                y_ref, buf_ref, h_ref, *, tm):
    @pl.when(pl.program_id(1) == 0)
    def _():
        buf_ref[0:8, :] = jnp.zeros((8, GROUP_W), F32)
        h_ref[...] = jnp.zeros((1, GROUP_W), F32)

    u = u_ref[0].astype(F32)
    xraw, gate = u[:, :GROUP_W], u[:, GROUP_W:]
    buf_ref[8:8 + tm, :] = xraw
    xb = (cw_ref[3:4, :] * xraw + cw_ref[2:3, :] * buf_ref[7:7 + tm, :]
          + cw_ref[1:2, :] * buf_ref[6:6 + tm, :] + cw_ref[0:1, :] * buf_ref[5:5 + tm, :]
          + cb_ref[...])
    buf_ref[0:8, :] = xraw[tm - 8:, :]

    xbb = xb.astype(BF16)
    r = _sigmoid(_dot(xbb, wa_ref[...]) + ba_ref[...])
    i = _sigmoid(_dot(xbb, wx_ref[...]) + bx_ref[...])
    nlam = -lam_ref[...]
    softplus = jnp.maximum(nlam, 0.0) + jnp.log(1.0 + jnp.exp(-jnp.abs(nlam)))
    log_a = (-LRU_C) * r * softplus
    a = jnp.exp(log_a)
    b = jnp.sqrt(1.0 - a * a) * (i * xb)

    row = lax.broadcasted_iota(jnp.int32, (tm, GROUP_W), 0)
    d = 1
    while d < tm:
        keep = row >= d
        a_sh = jnp.where(keep, pltpu.roll(a, d, 0), 1.0)
        b_sh = jnp.where(keep, pltpu.roll(b, d, 0), 0.0)
        b = a * b_sh + b
        a = a * a_sh
        d *= 2
    h = a * h_ref[...] + b
    h_ref[...] = h[tm - 1:tm, :]

    gelu = 0.5 * gate * (1.0 + jnp.tanh(math.sqrt(2.0 / math.pi) * (gate + 0.044715 * gate * gate * gate)))
    y_ref[0] = _rms_rows(h * gelu, g_ref[...]).astype(BF16)


def _lru_mixer(u, cw, cb, wa, ba, wx, bx, lam, g, tm):
    bsz, s, _ = u.shape
    row = pl.BlockSpec((1, GROUP_W), lambda b, i: (0, 0))
    mat = pl.BlockSpec((GROUP_W, GROUP_W), lambda b, i: (0, 0))
    return pl.pallas_call(
        functools.partial(_lru_kernel, tm=tm),
        grid=(bsz, s // tm),
        in_specs=[pl.BlockSpec((1, tm, U_LRU), lambda b, i: (b, i, U_RET // U_LRU)),
                  pl.BlockSpec((4, GROUP_W), lambda b, i: (0, 0)),
                  row, mat, row, mat, row, row, row],
        out_specs=pl.BlockSpec((1, tm, GROUP_W), lambda b, i: (b, i, 0)),
        out_shape=jax.ShapeDtypeStruct((bsz, s, GROUP_W), BF16),
        scratch_shapes=[pltpu.VMEM((tm + 8, GROUP_W), F32), pltpu.VMEM((1, GROUP_W), F32)],
        compiler_params=_cparams(("parallel", "arbitrary")),
        name="lru_mixer",
    )(u, cw, cb, wa, ba, wx, bx, lam, g)


def _ret_kernel(u_ref, tab_ref, ex_ref, inner_ref, qd_ref, kd_ref, cd_ref, bm_ref, gm_ref,
                mq_ref, mv_ref, g_ref, y_ref, st_ref):
    @pl.when(pl.program_id(1) == 0)
    def _():
        st_ref[...] = jnp.zeros((GROUP_W, GROUP_W), F32)

    u = u_ref[0].astype(F32)
    q, k = u[:, :GROUP_W], u[:, GROUP_W:2 * GROUP_W]
    v, gate = u[:, 2 * GROUP_W:3 * GROUP_W], u[:, 3 * GROUP_W:]
    cos, sin = _expand_trig(tab_ref[0], ex_ref[...])

    def rope(t):
        t1, t2 = t[:, :LANES], t[:, LANES:]
        return jnp.concatenate([t1 * cos - t2 * sin, t2 * cos + t1 * sin], axis=-1)

    qr = rope(q)
    kr = rope(k) * (RET_DK ** -0.5)
    krb = kr.astype(BF16)
    vb = v.astype(BF16)
    state = st_ref[...]
    o = _dot(qr.astype(BF16), state.astype(BF16)) * qd_ref[...]
    for h in range(N_HEADS):
        qh = (qr * mq_ref[h]).astype(BF16)
        sc = _dot_nt(qh, krb) * inner_ref[h]
        o = o + _dot(sc.astype(BF16), vb) * mv_ref[h]
    st_ref[...] = state * cd_ref[...] + bm_ref[...] * _dot_tn((kr * kd_ref[...]).astype(BF16), vb)

    gm = gm_ref[...]
    o_hi = o.astype(BF16)
    o_lo = (o - o_hi.astype(F32)).astype(BF16)
    mu = _dot(o_hi, gm) + _dot(o_lo, gm)
    dlt = o - mu
    d2 = dlt * dlt
    d2_hi = d2.astype(BF16)
    d2_lo = (d2 - d2_hi.astype(F32)).astype(BF16)
    var = _dot(d2_hi, gm) + _dot(d2_lo, gm)
    y = dlt * lax.rsqrt(var + EPS)
    y = gate * _sigmoid(gate) * y
    y_ref[0] = _rms_rows(y, g_ref[...]).astype(BF16)


def _ret_mixer(u, trig, expand, consts, g, chunk):
    bsz, s, _ = u.shape
    inner, qd, kd, cd, bm, gm, mq, mv = consts
    full = lambda shape: pl.BlockSpec(shape, lambda b, i: (0,) * len(shape))
    tab = pl.BlockSpec((1, chunk, LANES), lambda b, i: (b, i, 0))
    return pl.pallas_call(
        _ret_kernel,
        grid=(bsz, s // chunk),
        in_specs=[pl.BlockSpec((1, chunk, U_RET), lambda b, i: (b, i, 0)), tab, full(expand.shape),
                  full(inner.shape), full(qd.shape), full(kd.shape), full(cd.shape),
                  full(bm.shape), full(gm.shape), full(mq.shape), full(mv.shape),
                  full((1, GROUP_W))],
        out_specs=pl.BlockSpec((1, chunk, GROUP_W), lambda b, i: (b, i, 0)),
        out_shape=jax.ShapeDtypeStruct((bsz, s, GROUP_W), BF16),
        scratch_shapes=[pltpu.VMEM((GROUP_W, GROUP_W), F32)],
        compiler_params=_cparams(("parallel", "arbitrary")),
        name="ret_mixer",
    )(u, trig, expand, inner, qd, kd, cd, bm, gm, mq, mv, g)


def _ret_consts(chunk):
    nh = N_HEADS
    f32 = np.float32
    log_g = np.log(f32(1.0) - f32(2.0) ** (f32(-5.0) - np.arange(nh, dtype=f32)))
    idx = np.arange(chunk, dtype=f32)
    rel = idx[:, None] - idx[None, :]
    inner = np.where(rel >= 0, np.exp(log_g[:, None, None] * np.maximum(rel, 0.0)), 0.0).astype(f32)
    v_head = np.arange(GROUP_W) // HEAD_DIM
    q_head = (np.arange(GROUP_W) % LANES) // (RET_DK // 2)
    qd = np.exp(log_g[v_head][None, :] * (idx[:, None] + 1.0)).astype(f32)
    kd = np.exp(log_g[q_head][None, :] * (chunk - 1.0 - idx[:, None])).astype(f32)
    cd = np.exp(log_g[v_head] * chunk)[None, :].astype(f32)
    bm = (q_head[:, None] == v_head[None, :]).astype(f32)
    gm = jnp.asarray((v_head[:, None] == v_head[None, :]).astype(f32) / HEAD_DIM, BF16)
    mq = (q_head[None, :] == np.arange(nh)[:, None]).astype(f32)[:, None, :]
    mv = (v_head[None, :] == np.arange(nh)[:, None]).astype(f32)[:, None, :]
    return tuple(jnp.asarray(a) for a in (inner, qd, kd, cd, bm)) + (gm, jnp.asarray(mq), jnp.asarray(mv))


def _trig_expanders():
    r16, r32 = MLA_ROPE // 2, RET_DK // 2
    ret = np.zeros((LANES, 2 * LANES), np.float32)
    mla = np.zeros((LANES, 2 * LANES), np.float32)
    for j in range(r32):
        for h in range(N_HEADS):
            ret[r16 + j, h * r32 + j] = 1.0
            ret[N_FREQ + r16 + j, LANES + h * r32 + j] = 1.0
    for j in range(r16):
        for half, sign in ((0, -1.0), (1, 1.0)):
            lane = MLA_NOPE + half * r16 + j
            mla[j, lane] = 1.0
            mla[N_FREQ + j, LANES + lane] = sign
    mla[2 * N_FREQ, :MLA_NOPE] = 1.0
    return jnp.asarray(ret, BF16), jnp.asarray(mla, BF16)


def _mla_prep_kernel(u_ref, tab_ref, ex_ref, mu_ref, invu_ref, gu_ref, wbig_ref, sq_ref,
                     invq_ref, gq_ref, gqs_ref, gk_ref, onev_ref, q_ref, k_ref, v_ref):
    x = u_ref[0].astype(F32)
    ss = _dot((x * x).astype(BF16), mu_ref[...]) * invu_ref[...]
    xn = (x * lax.rsqrt(ss + EPS) * gu_ref[...]).astype(BF16)
    big = _dot(xn, wbig_ref[...])
    hw = N_HEADS * LANES
    q, qs, kn, v = big[:, :hw], big[:, hw:2 * hw], big[:, 2 * hw:3 * hw], big[:, 3 * hw:4 * hw]
    kr, krs = big[:, 4 * hw:4 * hw + LANES], big[:, 4 * hw + LANES:]
    cos, sin = _expand_trig(tab_ref[0], ex_ref[...])
    rq = lax.rsqrt(_dot((q * q).astype(BF16), sq_ref[...]) * invq_ref[...] + EPS)
    rk = lax.rsqrt(_dot((kn * kn).astype(BF16), sq_ref[...]) * invq_ref[...] + EPS)
    krot = kr * cos + krs * sin
    for h in range(N_HEADS):
        sl = slice(h * LANES, (h + 1) * LANES)
        qh = (q[:, sl] * gq_ref[...] * cos + qs[:, sl] * gqs_ref[...] * sin) * rq[:, sl]
        q_ref[0, h] = qh.astype(BF16)
        k_ref[0, h] = (kn[:, sl] * gk_ref[...] * rk[:, sl] + krot).astype(BF16)
        v_ref[0, h] = (v[:, sl] + onev_ref[...]).astype(BF16)


def _mla_prep(u, trig, expand, mu, invu, gu, wbig, sq, invq, gq, gqs, gk, onev, tm):
    bsz, s, _ = u.shape
    full = lambda a: pl.BlockSpec(a.shape, lambda b, i: (0,) * a.ndim)
    tab = pl.BlockSpec((1, tm, LANES), lambda b, i: (b, i, 0))
    out = jax.ShapeDtypeStruct((bsz, N_HEADS, s, LANES), BF16)
    ospec = pl.BlockSpec((1, N_HEADS, tm, LANES), lambda b, i: (b, 0, i, 0))
    return pl.pallas_call(
        _mla_prep_kernel,
        grid=(bsz, s // tm),
        in_specs=[pl.BlockSpec((1, tm, U_MLA), lambda b, i: (b, i, (U_COLS - U_MLA) // U_MLA)),
                  tab, full(expand), full(mu), full(invu), full(gu), full(wbig), full(sq), full(invq),
                  full(gq), full(gqs), full(gk), full(onev)],
        out_specs=[ospec, ospec, ospec],
        out_shape=[out, out, out],
        compiler_params=_cparams(("parallel", "parallel")),
        name="mla_prep",
    )(u, trig, expand, mu, invu, gu, wbig, sq, invq, gq, gqs, gk, onev)


def _flash_kernel(q_ref, k_ref, v_ref, o_ref, sa_ref, sb_ref, mca_ref, mcb_ref, m_ref, acc_ref, *, tq):
    qi = pl.program_id(2)
    q = q_ref[0, 0]
    bufs = ((sa_ref, mca_ref), (sb_ref, mcb_ref))
    m_ref[...] = jnp.full((tq, LANES), NEG_BIG, F32)
    acc_ref[...] = jnp.zeros((tq, LANES), F32)

    def scores(c, masked, dst):
        s_ref, mc_ref = dst
        start = pl.multiple_of(c * tq, tq)
        s = _dot_nt(q, k_ref[0, 0, pl.ds(start, tq), :])
        if masked:
            row = qi * tq + lax.broadcasted_iota(jnp.int32, (tq, tq), 0)
            col = start + lax.broadcasted_iota(jnp.int32, (tq, tq), 1)
            s = jnp.where(col <= row, s, NEG_BIG)
        s_ref[...] = s
        mc_ref[...] = jnp.broadcast_to(jnp.max(s, axis=-1, keepdims=True), (tq, LANES))

    def accumulate(c, src):
        s_ref, mc_ref = src
        start = pl.multiple_of(c * tq, tq)
        m_prev = m_ref[...]
        m_new = jnp.maximum(m_prev, mc_ref[...])
        alpha = jnp.exp2(m_prev - m_new)
        p = jnp.exp2(s_ref[...] - jnp.tile(m_new, (1, tq // LANES)))
        pv = _dot(p.astype(BF16), v_ref[0, 0, pl.ds(start, tq), :])
        acc_ref[...] = alpha * acc_ref[...] + pv
        m_ref[...] = m_new

    def by_parity(c, fn):
        for par in range(2):
            pl.when(c % 2 == par)(functools.partial(fn, par))

    def pipelined(c, masked, par):
        scores(c + 1, masked, bufs[1 - par])
        accumulate(c, bufs[par])

    scores(0, True, bufs[0])
    n_plain = jnp.maximum(qi - 1, 0)

    def two_steps(i, carry):
        pipelined(2 * i, False, 0)
        pipelined(2 * i + 1, False, 1)
        return carry

    lax.fori_loop(0, n_plain // 2, two_steps, 0)

    @pl.when(n_plain % 2 == 1)
    def _():
        pipelined(n_plain - 1, False, 0)

    @pl.when(qi >= 1)
    def _():
        by_parity(qi - 1, functools.partial(pipelined, qi - 1, True))

    by_parity(qi, lambda par: accumulate(qi, bufs[par]))

    acc = acc_ref[...]
    lane = lax.broadcasted_iota(jnp.int32, (tq, LANES), 1)
    denom = jnp.sum(jnp.where(lane == HEAD_DIM, acc, 0.0), axis=-1, keepdims=True)
    o_ref[0, 0] = jnp.where(lane < HEAD_DIM, acc / denom, 0.0).astype(BF16)


def _flash_attention(q, k, v, tq):
    bsz, nh, s, _ = q.shape
    kv_spec = pl.BlockSpec((1, 1, s, LANES), lambda b, h, i: (b, h, 0, 0))
    blk = pl.BlockSpec((1, 1, tq, LANES), lambda b, h, i: (b, h, i, 0))
    stat = pltpu.VMEM((tq, LANES), F32)
    return pl.pallas_call(
        functools.partial(_flash_kernel, tq=tq),
        grid=(bsz, nh, s // tq),
        in_specs=[blk, kv_spec, kv_spec],
        out_specs=blk,
        out_shape=jax.ShapeDtypeStruct((bsz, nh, s, LANES), BF16),
        scratch_shapes=[pltpu.VMEM((tq, tq), F32), pltpu.VMEM((tq, tq), F32), stat, stat, stat, stat],
        compiler_params=_cparams(("parallel", "parallel", "arbitrary")),
        name="flash_attention",
    )(q, k, v)


def _outproj_kernel(x_ref, yc_ref, om_ref, yr_ref, yl_ref, wc_ref, wm_ref, wr_ref, wl_ref,
                    gmla_ref, gt_ref, gf_ref, scf_ref, shf_ref, wrt_ref, brt_ref, tri_ref,
                    xo_ref, h_ref, meta_ref, cnt_ref, run_ref):
    @pl.when((pl.program_id(0) == 0) & (pl.program_id(1) == 0))
    def _():
        run_ref[...] = jnp.zeros((1, LANES), F32)

    om = [om_ref[0, h].astype(F32) for h in range(N_HEADS)]
    ssq = om[0] * om[0]
    for h in range(1, N_HEADS):
        ssq = ssq + om[h] * om[h]
    r_mla = lax.rsqrt(jnp.sum(ssq, axis=-1, keepdims=True) / GROUP_W + EPS)
    y = _dot(yc_ref[0], wc_ref[...]) + _dot(yr_ref[0], wr_ref[...]) + _dot(yl_ref[0], wl_ref[...])
    for h in range(0, N_HEADS, 2):
        pair = jnp.concatenate([(om[h] * r_mla * gmla_ref[h]).astype(BF16),
                                (om[h + 1] * r_mla * gmla_ref[h + 1]).astype(BF16)], axis=1)
        y = y + _dot(pair, wm_ref[h // 2])
    x = x_ref[0] + gt_ref[0] * y
    xo_ref[0] = x
    hf = _rms_rows(x, gf_ref[...]) * (1.0 + scf_ref[0]) + shf_ref[0]
    h_ref[0] = _pack_bf16_pairs(hf)

    h_hi = hf.astype(BF16)
    h_lo = (hf - h_hi.astype(F32)).astype(BF16)
    both = _dot(h_hi, wrt_ref[...])
    lg = both[:, :LANES] + both[:, LANES:] + _dot(h_lo, wrt_ref[:, :LANES])
    tm = lg.shape[0]
    lane = lax.broadcasted_iota(jnp.int32, (tm, LANES), 1)
    bias = brt_ref[...]
    is_g = (lane >= N_EXPERTS) & (lane < N_EXPERTS + MOE_GROUPS)
    is_e = lane < N_EXPERTS

    def first_argmax(val):
        mx = jnp.max(val, axis=-1, keepdims=True)
        return jnp.min(jnp.where(val == mx, lane, LANES), axis=-1, keepdims=True)

    gl = jnp.where(is_g, lg, NEG_BIG)
    ge = jnp.exp(gl - jnp.max(gl, axis=-1, keepdims=True))
    gp = ge / jnp.sum(ge, axis=-1, keepdims=True)
    g_idx = first_argmax(jnp.where(is_g, gp + bias, NEG_BIG))
    g_weight = jnp.sum(jnp.where(lane == g_idx, gp, 0.0), axis=-1, keepdims=True)
    in_group = is_e & ((lane // EXPERTS_PER_GROUP) == (g_idx - N_EXPERTS))
    el = jnp.where(in_group, lg, NEG_BIG)
    ee = jnp.exp(el - jnp.max(el, axis=-1, keepdims=True))
    ep = ee / jnp.sum(ee, axis=-1, keepdims=True)
    score = jnp.where(in_group, ep + bias, NEG_BIG)
    i1 = first_argmax(score)
    sel1 = lane == i1
    i2 = first_argmax(jnp.where(sel1, NEG_BIG, score))
    sel2 = lane == i2
    p1 = jnp.sum(jnp.where(sel1, ep, 0.0), axis=-1, keepdims=True)
    p2 = jnp.sum(jnp.where(sel2, ep, 0.0), axis=-1, keepdims=True)
    psum = p1 + p2
    w1 = p1 / psum * g_weight
    w2 = p2 / psum * g_weight

    onehot = jnp.where(sel1, 1.0, jnp.where(sel2, 1.0, 0.0)).astype(BF16)
    incl = _dot(tri_ref[...], onehot)
    base = run_ref[...] + incl - 1.0
    r1 = jnp.sum(jnp.where(sel1, base, 0.0), axis=-1, keepdims=True)
    r2 = jnp.sum(jnp.where(sel2, base, 0.0), axis=-1, keepdims=True)
    run_ref[...] = run_ref[...] + incl[tm - 1:tm, :]
    cnt_ref[0] = run_ref[...]
    fields = (i1.astype(F32), i2.astype(F32), r1, r2, w1, w2)
    meta = jnp.zeros((tm, LANES), F32)
    for pos, val in enumerate(fields):
        meta = jnp.where(lane == pos, val, meta)
    meta_ref[0] = meta


def _outproj(x, yc, om, yr, yl, wc, wm, wr, wl, gmla, gt, gf, scf, shf, wrt, brt, tm):
    bsz, s, d = x.shape
    nt = s // tm
    full = lambda a: pl.BlockSpec(a.shape, lambda b, i: (0,) * a.ndim)
    tok = lambda w: pl.BlockSpec((1, tm, w), lambda b, i: (b, i, 0))
    vec = pl.BlockSpec((1, 1, d), lambda b, i: (b, 0, 0))
    tri = jnp.asarray(np.tril(np.ones((tm, tm), np.float32)), BF16)
    return pl.pallas_call(
        _outproj_kernel,
        grid=(bsz, nt),
        in_specs=[tok(d), tok(GROUP_W),
                  pl.BlockSpec((1, N_HEADS, tm, LANES), lambda b, i: (b, 0, i, 0)),
                  tok(GROUP_W), tok(GROUP_W),
                  full(wc), full(wm), full(wr), full(wl), full(gmla), vec, full(gf), vec, vec,
                  full(wrt), full(brt), full(tri)],
        out_specs=[tok(d), tok(d // 2), tok(LANES),
                   pl.BlockSpec((1, 1, LANES), lambda b, i: (b * nt + i, 0, 0))],
        out_shape=[jax.ShapeDtypeStruct((bsz, s, d), F32), jax.ShapeDtypeStruct((bsz, s, d // 2), jnp.uint32),
                   jax.ShapeDtypeStruct((bsz, s, LANES), F32),
                   jax.ShapeDtypeStruct((bsz * nt, 1, LANES), F32)],
        scratch_shapes=[pltpu.VMEM((1, LANES), F32)],
        compiler_params=_cparams(("arbitrary", "arbitrary")),
        name="outproj_router",
    )(x, yc, om, yr, yl, wc, wm, wr, wl, gmla, gt, gf, scf, shf, wrt, brt, tri)


def _route_plan(meta, cnt, tmg):
    t = meta.shape[0] * meta.shape[1]
    m = meta.reshape(t, LANES)
    e = m[:, 0:2].astype(jnp.int32)
    r = m[:, 2:4].astype(jnp.int32)
    counts = cnt[-1, 0, :N_EXPERTS].astype(jnp.int32)
    padded = (counts + tmg - 1) // tmg * tmg
    ends = jnp.cumsum(padded)
    starts = ends - padded
    dest = jnp.take(starts, e) + r
    n_tiles = (2 * t) // tmg + N_EXPERTS
    tile_start = jnp.arange(n_tiles, dtype=jnp.int32) * tmg
    tile_exp = jnp.sum((ends[None, :] <= tile_start[:, None]).astype(jnp.int32), axis=1)
    tile_exp = jnp.minimum(tile_exp, N_EXPERTS - 1)
    n_used = (ends[-1:] // tmg).astype(jnp.int32)
    last_tile = jnp.maximum(ends - tmg, 0).astype(jnp.int32)
    return dest[:, 0], dest[:, 1], tile_exp, n_used, last_tile, padded.astype(jnp.int32)


def _dispatch_kernel(zs_ref, zv_ref, nu_ref, d1_ref, d2_ref, h_ref, xs_ref, zero_ref, sem, *, tmc, tmg):
    @pl.when(pl.program_id(0) == 0)
    def _():
        zero_ref[...] = jnp.zeros(zero_ref.shape, jnp.uint32)
        n_tiles = xs_ref.shape[0] // tmg

        def fill(start):
            return pltpu.make_async_copy(zero_ref, xs_ref.at[pl.ds(pl.multiple_of(start, tmg), tmg), :], sem)

        def fill_tail(j, carry, wait):
            cp = fill(j * tmg)
            cp.wait() if wait else cp.start()
            return carry

        for wait in (False, True):
            for e in range(N_EXPERTS):
                cp = fill(zs_ref[e])
                pl.when(zv_ref[e] > 0)(cp.wait if wait else cp.start)
            lax.fori_loop(nu_ref[0], n_tiles, functools.partial(fill_tail, wait=wait), 0)

    def push(r, carry):
        src = h_ref.at[pl.ds(r, 1), :]
        pltpu.make_async_copy(src, xs_ref.at[pl.ds(d1_ref[0, 0, r], 1), :], sem).start()
        pltpu.make_async_copy(src, xs_ref.at[pl.ds(d2_ref[0, 0, r], 1), :], sem).start()
        return carry

    lax.fori_loop(0, tmc, push, 0, unroll=8)
    one_row = pltpu.make_async_copy(h_ref.at[pl.ds(0, 1), :], xs_ref.at[pl.ds(0, 1), :], sem)
    for _ in range(2 * tmc):
        one_row.wait()


def _dispatch(hp, d1, d2, last_tile, padded, n_used, tmc, tmg):
    t, dw = hp.shape
    n_rows = 2 * t + N_EXPERTS * tmg
    smem_rows = pl.BlockSpec((1, 1, tmc), lambda i, zs, zv, nu: (i, 0, 0), memory_space=pltpu.SMEM)
    return pl.pallas_call(
        functools.partial(_dispatch_kernel, tmc=tmc, tmg=tmg),
        grid_spec=pltpu.PrefetchScalarGridSpec(
            num_scalar_prefetch=3, grid=(t // tmc,),
            in_specs=[smem_rows, smem_rows, pl.BlockSpec((tmc, dw), lambda i, zs, zv, nu: (i, 0))],
            out_specs=pl.BlockSpec(memory_space=pl.ANY),
            scratch_shapes=[pltpu.VMEM((tmg, dw), jnp.uint32), pltpu.SemaphoreType.DMA(())]),
        out_shape=jax.ShapeDtypeStruct((n_rows, dw), jnp.uint32),
        compiler_params=_cparams(("arbitrary",)),
        name="moe_dispatch",
    )(last_tile, padded, n_used, d1.reshape(t // tmc, 1, tmc), d2.reshape(t // tmc, 1, tmc), hp)


def _experts_kernel(te_ref, nu_ref, xs_ref, wg_ref, wu_ref, wd_ref, y_ref):
    used = pl.program_id(0) < nu_ref[0]

    @pl.when(used)
    def _():
        x = _unpack_bf16_pairs(xs_ref[...]).astype(BF16)
        gate = _dot(x, wg_ref[0, 0].astype(BF16))
        hid = gate * _sigmoid(gate) * _dot(x, wu_ref[0, 0].astype(BF16))
        y_ref[...] = _pack_bf16_pairs(_dot(hid.astype(BF16), wd_ref[0, 0].astype(BF16)))

    @pl.when(jnp.logical_not(used))
    def _():
        y_ref[...] = jnp.zeros(y_ref.shape, jnp.uint32)


def _experts(xs, tile_exp, n_used, layer, wg, wu, wd, tmg):
    n_rows, dw = xs.shape
    d = 2 * dw
    tile = lambda i, te, nu: jnp.minimum(i, nu[0] - 1)
    rows = pl.BlockSpec((tmg, dw), lambda i, te, nu: (tile(i, te, nu), 0))
    wspec = lambda shape: pl.BlockSpec((1, 1) + shape,
                                       lambda i, te, nu: (layer, te[tile(i, te, nu)], 0, 0))
    return pl.pallas_call(
        _experts_kernel,
        grid_spec=pltpu.PrefetchScalarGridSpec(
            num_scalar_prefetch=2, grid=(n_rows // tmg,),
            in_specs=[rows, wspec((d, D_EXPERT)), wspec((d, D_EXPERT)), wspec((D_EXPERT, d))],
            out_specs=pl.BlockSpec((tmg, dw), lambda i, te, nu: (i, 0))),
        out_shape=jax.ShapeDtypeStruct((n_rows, dw), jnp.uint32),
        compiler_params=_cparams(("arbitrary",)),
        name="moe_experts",
    )(tile_exp, n_used, xs, wg, wu, wd)


def _combine_kernel(d1_ref, d2_ref, x_ref, meta_ref, gt_ref, y_ref, o_ref, b1_ref, b2_ref, sem, *, tmc):
    def pull(r, carry):
        pltpu.make_async_copy(y_ref.at[pl.ds(d1_ref[0, 0, r], 1), :], b1_ref.at[pl.ds(r, 1), :], sem).start()
        pltpu.make_async_copy(y_ref.at[pl.ds(d2_ref[0, 0, r], 1), :], b2_ref.at[pl.ds(r, 1), :], sem).start()
        return carry

    lax.fori_loop(0, tmc, pull, 0, unroll=8)
    one_row = pltpu.make_async_copy(y_ref.at[pl.ds(0, 1), :], b1_ref.at[pl.ds(0, 1), :], sem)
    for _ in range(2 * tmc):
        one_row.wait()
    meta = meta_ref[0]
    lane = lax.broadcasted_iota(jnp.int32, meta.shape, 1)
    w1 = jnp.sum(jnp.where(lane == 4, meta, 0.0), axis=-1, keepdims=True)
    w2 = jnp.sum(jnp.where(lane == 5, meta, 0.0), axis=-1, keepdims=True)
    y = w1 * _unpack_bf16_pairs(b1_ref[...]) + w2 * _unpack_bf16_pairs(b2_ref[...])
    o_ref[0] = x_ref[0] + gt_ref[0] * y


def _combine(x, meta, gt, y, d1, d2, tmc):
    bsz, s, d = x.shape
    nt = s // tmc
    t = bsz * s
    smem_rows = pl.BlockSpec((1, 1, tmc), lambda b, i: (b * nt + i, 0, 0), memory_space=pltpu.SMEM)
    tok = lambda w: pl.BlockSpec((1, tmc, w), lambda b, i: (b, i, 0))
    buf = pltpu.VMEM((tmc, d // 2), jnp.uint32)
    return pl.pallas_call(
        functools.partial(_combine_kernel, tmc=tmc),
        grid=(bsz, nt),
        in_specs=[smem_rows, smem_rows, tok(d), tok(LANES),
                  pl.BlockSpec((1, 1, d), lambda b, i: (b, 0, 0)),
                  pl.BlockSpec(memory_space=pl.ANY)],
        out_specs=tok(d),
        out_shape=jax.ShapeDtypeStruct((bsz, s, d), F32),
        scratch_shapes=[buf, buf, pltpu.SemaphoreType.DMA(())],
        compiler_params=_cparams(("arbitrary", "arbitrary")),
        name="moe_combine",
    )(d1.reshape(t // tmc, 1, tmc), d2.reshape(t // tmc, 1, tmc), x, meta, gt, y)


def _layer_weights(l, w_in, mla_q_norm_g, mla_w_uq, mla_kv_norm_g, mla_w_ukv, mla_q_qk_g,
                   mla_k_qk_g, lru_w_a, lru_w_x, mix_norm_g, w_out, router_group_w,
                   router_group_b, router_expert_w, router_expert_b):
    half = RET_DK // 2
    perm = np.concatenate([np.arange(half) + HEAD_DIM * h for h in range(N_HEADS)]
                          + [np.arange(half) + half + HEAD_DIM * h for h in range(N_HEADS)])
    w = w_in[l]
    o_mla, o_ret, o_lru = U_CONV, U_CONV + 352, U_CONV + 352 + U_RET
    w_ret = w[:, o_ret:o_ret + U_RET]
    w_ret = jnp.concatenate([w_ret[:, perm], w_ret[:, GROUP_W + perm], w_ret[:, 2 * GROUP_W:]], axis=1)
    w_all = jnp.concatenate([w_ret, w[:, o_lru:o_lru + U_LRU], w[:, :U_CONV], w[:, o_mla:o_mla + 352],
                             jnp.zeros((D_MODEL, U_MLA - 352), F32)], axis=1).astype(BF16)

    hw = N_HEADS * LANES
    r16 = MLA_ROPE // 2
    wq = mla_w_uq[l].reshape(Q_LORA, N_HEADS, MLA_QK)
    zq = jnp.zeros((Q_LORA, N_HEADS, LANES - MLA_QK), F32)
    q_cols = jnp.concatenate([wq, zq], axis=2).reshape(Q_LORA, hw)
    wq_sw = jnp.concatenate([jnp.zeros((Q_LORA, N_HEADS, MLA_NOPE), F32), wq[:, :, MLA_NOPE + r16:],
                             wq[:, :, MLA_NOPE:MLA_NOPE + r16], zq], axis=2).reshape(Q_LORA, hw)
    wkv = mla_w_ukv[l].reshape(KV_LORA, N_HEADS, MLA_NOPE + HEAD_DIM)
    zk = jnp.zeros((KV_LORA, N_HEADS, LANES - MLA_NOPE), F32)
    k_cols = jnp.concatenate([wkv[:, :, :MLA_NOPE], zk], axis=2).reshape(KV_LORA, hw)
    v_cols = jnp.concatenate([wkv[:, :, MLA_NOPE:], zk], axis=2).reshape(KV_LORA, hw)
    eye = jnp.eye(MLA_ROPE, dtype=F32)
    place = jnp.concatenate([jnp.zeros((MLA_ROPE, MLA_NOPE), F32), eye,
                             jnp.zeros((MLA_ROPE, LANES - MLA_QK), F32)], axis=1)
    eye_sw = jnp.concatenate([eye[:, r16:], eye[:, :r16]], axis=1)
    place_sw = jnp.concatenate([jnp.zeros((MLA_ROPE, MLA_NOPE), F32), eye_sw,
                                jnp.zeros((MLA_ROPE, LANES - MLA_QK), F32)], axis=1)
    n_big = 4 * hw + 2 * LANES
    wbig = jnp.zeros((U_MLA, n_big), F32)
    wbig = wbig.at[:Q_LORA, :hw].set(q_cols).at[:Q_LORA, hw:2 * hw].set(wq_sw)
    wbig = wbig.at[Q_LORA:Q_LORA + KV_LORA, 2 * hw:3 * hw].set(k_cols)
    wbig = wbig.at[Q_LORA:Q_LORA + KV_LORA, 3 * hw:4 * hw].set(v_cols)
    wbig = wbig.at[Q_LORA + KV_LORA:352, 4 * hw:4 * hw + LANES].set(place)
    wbig = wbig.at[Q_LORA + KV_LORA:352, 4 * hw + LANES:].set(place_sw)
    wbig = wbig.astype(BF16)

    gu = jnp.concatenate([mla_q_norm_g[l], mla_kv_norm_g[l], mla_k_qk_g[l][MLA_NOPE:],
                          jnp.zeros((U_MLA - 352,), F32)])[None, :]
    qscale = (MLA_QK ** -0.5) * math.log2(math.e)
    gq_full = mla_q_qk_g[l]
    pad = jnp.zeros((LANES - MLA_QK,), F32)
    gq = (jnp.concatenate([gq_full, pad]) * qscale)[None, :]
    gqs = (jnp.concatenate([jnp.zeros((MLA_NOPE,), F32), gq_full[MLA_NOPE + r16:],
                            gq_full[MLA_NOPE:MLA_NOPE + r16], pad]) * qscale)[None, :]
    gk = jnp.concatenate([mla_k_qk_g[l][:MLA_NOPE], jnp.zeros((LANES - MLA_NOPE,), F32)])[None, :]

    def blockdiag(wb):
        out = jnp.zeros((GROUP_W, GROUP_W), F32)
        for n in range(wb.shape[0]):
            out = out.at[n * HEAD_DIM:(n + 1) * HEAD_DIM, n * HEAD_DIM:(n + 1) * HEAD_DIM].set(wb[n])
        return out.astype(BF16)

    gmix = mix_norm_g[l]
    wo = w_out[l].astype(BF16)
    wm = wo[GROUP_W:2 * GROUP_W].reshape(N_HEADS, HEAD_DIM, D_MODEL)
    wm = jnp.concatenate([wm, jnp.zeros((N_HEADS, LANES - HEAD_DIM, D_MODEL), BF16)], axis=1)
    wm = wm.reshape(N_HEADS // 2, 2 * LANES, D_MODEL)
    gmla = jnp.concatenate([gmix[GROUP_W:2 * GROUP_W].reshape(N_HEADS, 1, HEAD_DIM),
                            jnp.zeros((N_HEADS, 1, LANES - HEAD_DIM), F32)], axis=2)
    wrt = jnp.concatenate([router_expert_w[l], router_group_w[l],
                           jnp.zeros((D_MODEL, LANES - N_EXPERTS - MOE_GROUPS), F32)], axis=1)
    wrt_hi = wrt.astype(BF16)
    wrt = jnp.concatenate([wrt_hi, (wrt - wrt_hi.astype(F32)).astype(BF16)], axis=1)
    brt = jnp.concatenate([router_expert_b[l], router_group_b[l],
                           jnp.zeros((LANES - N_EXPERTS - MOE_GROUPS,), F32)])[None, :]
    return dict(w_all=w_all, wbig=wbig, gu=gu, gq=gq, gqs=gqs, gk=gk,
                wa=blockdiag(lru_w_a[l]), wx=blockdiag(lru_w_x[l]),
                g_conv=gmix[None, :GROUP_W], g_ret=gmix[None, 2 * GROUP_W:3 * GROUP_W],
                g_lru=gmix[None, 3 * GROUP_W:], gmla=gmla,
                wc=wo[:GROUP_W], wm=wm, wr=wo[2 * GROUP_W:3 * GROUP_W], wl=wo[3 * GROUP_W:],
                wrt=wrt, brt=brt)


def _mla_consts():
    seg_u = np.concatenate([np.zeros(Q_LORA), np.ones(KV_LORA), 2 * np.ones(MLA_ROPE),
                            3 * np.ones(U_MLA - 352)])
    mu = jnp.asarray(seg_u[:, None] == seg_u[None, :], BF16)
    invu = jnp.asarray(np.concatenate([np.full(Q_LORA, 1.0 / Q_LORA), np.full(KV_LORA, 1.0 / KV_LORA),
                                       np.full(MLA_ROPE, 1.0 / MLA_ROPE), np.ones(U_MLA - 352)]), F32)[None, :]
    lane = np.arange(N_HEADS * LANES)
    seg_q = (lane // LANES) * 3 + np.where(lane % LANES < MLA_NOPE, 0, np.where(lane % LANES < MLA_QK, 1, 2))
    sq = jnp.asarray(seg_q[:, None] == seg_q[None, :], BF16)
    inv_head = np.concatenate([np.full(MLA_NOPE, 1.0 / MLA_NOPE), np.full(MLA_ROPE, 1.0 / MLA_ROPE),
                               np.ones(LANES - MLA_QK)])
    invq = jnp.asarray(np.tile(inv_head, N_HEADS), F32)[None, :]
    onev = jnp.asarray((np.arange(LANES) == HEAD_DIM).astype(np.float32))[None, :]
    return mu, invu, sq, invq, onev


def kernel(x, c, positions, ada_w, ada_b, norm_mix_g, w_in, conv_w, mla_q_norm_g, mla_w_uq, mla_kv_norm_g, mla_w_ukv, mla_q_qk_g, mla_k_qk_g, lru_conv_w, lru_conv_b, lru_w_a, lru_b_a, lru_w_x, lru_b_x, lru_lambda, mix_norm_g, w_out, norm_ffn_g, router_group_w, router_group_b, router_expert_w, router_expert_b, exp_w_gate, exp_w_up, exp_w_down):
    bsz, s, d = x.shape
    depth = ada_w.shape[0]
    tm = min(512, s)
    chunk = min(256, s)
    tq = min(1024, s)
    tmc = min(512, s)
    tmg = 512

    inv = jnp.concatenate([1.0 / (ROPE_BASE ** (jnp.arange(0, MLA_ROPE, 2, dtype=F32) / MLA_ROPE)),
                           1.0 / (ROPE_BASE ** (jnp.arange(0, RET_DK, 2, dtype=F32) / RET_DK))])[:, None]
    trig = _rope_tables(positions, inv)
    ex_ret, ex_mla = _trig_expanders()

    c_pad = jnp.concatenate([c, jnp.zeros((8 - bsz, d), F32)], axis=0)
    mod = _modulation(c_pad, ada_w, ada_b)[:, :bsz]
    ret_consts = _ret_consts(chunk)
    mu, invu, sq, invq, onev = _mla_consts()

    for l in range(depth):
        sh_m, sc_m, gt_m, sh_f, sc_f, gt_f = [m[:, None, :] for m in jnp.split(mod[l], 6, axis=-1)]
        lw = _layer_weights(l, w_in, mla_q_norm_g, mla_w_uq, mla_kv_norm_g, mla_w_ukv, mla_q_qk_g,
                            mla_k_qk_g, lru_w_a, lru_w_x, mix_norm_g, w_out, router_group_w,
                            router_group_b, router_expert_w, router_expert_b)
        u = _inproj(x, norm_mix_g[l][None, :], sc_m, sh_m, lw["w_all"], tm)
        y_conv = _conv_mixer(u, conv_w[l], lw["g_conv"], tm)
        y_lru = _lru_mixer(u, lru_conv_w[l], lru_conv_b[l][None, :], lw["wa"], lru_b_a[l][None, :],
                           lw["wx"], lru_b_x[l][None, :], lru_lambda[l][None, :], lw["g_lru"], tm)
        y_ret = _ret_mixer(u, trig, ex_ret, ret_consts, lw["g_ret"], chunk)
        q, k, v = _mla_prep(u, trig, ex_mla, mu, invu, lw["gu"], lw["wbig"], sq, invq,
                            lw["gq"], lw["gqs"], lw["gk"], onev, tm)
        o_mla = _flash_attention(q, k, v, tq)
        x, hp, meta, cnt = _outproj(x, y_conv, o_mla, y_ret, y_lru, lw["wc"], lw["wm"], lw["wr"],
                                    lw["wl"], lw["gmla"], gt_m, norm_ffn_g[l][None, :], sc_f, sh_f,
                                    lw["wrt"], lw["brt"], tm)
        d1, d2, tile_exp, n_used, last_tile, padded = _route_plan(meta, cnt, tmg)
        xs = _dispatch(hp.reshape(bsz * s, d // 2), d1, d2, last_tile, padded, n_used, tmc, tmg)
        ys = _experts(xs, tile_exp, n_used, l, exp_w_gate, exp_w_up, exp_w_down, tmg)
        x = _combine(x, meta, gt_f, ys, d1, d2, tmc)
    return x
```

```python
import functools
import math

import jax
import jax.numpy as jnp
import numpy as np
from jax import lax
from jax.experimental import pallas as pl
from jax.experimental.pallas import tpu as pltpu

F32 = jnp.float32
BF16 = jnp.bfloat16
HIGHEST = lax.Precision.HIGHEST

D_MODEL = 1024
GROUP_W = 256
HEAD_DIM = 64
N_HEADS = 4
MLA_NOPE = 64
MLA_ROPE = 32
MLA_QK = 96
Q_LORA = 192
KV_LORA = 128
RET_DK = 64
LRU_C = 8.0
MOE_GROUPS = 4
EXPERTS_PER_GROUP = 8
N_EXPERTS = 32
D_EXPERT = 256
ROPE_BASE = 10000.0
EPS = 1e-6

LANES = 128
MXU_DIM = 256
U_RET, U_LRU, U_CONV, U_MLA = 1024, 512, 768, 384
U_COLS = U_RET + U_LRU + U_CONV + U_MLA
N_FREQ = MLA_ROPE // 2 + RET_DK // 2
NEG_BIG = -1e30
VMEM_LIMIT = 56 * 1024 * 1024


def _cparams(sem):
    return pltpu.CompilerParams(dimension_semantics=sem, vmem_limit_bytes=VMEM_LIMIT)


def _dot(a, b):
    return jnp.dot(a, b, preferred_element_type=F32)


def _dot_nt(a, b):
    return lax.dot_general(a, b, (((1,), (1,)), ((), ())), preferred_element_type=F32)


def _dot_tn(a, b):
    return lax.dot_general(a, b, (((0,), (0,)), ((), ())), preferred_element_type=F32)


def _rms_rows(y, g):
    return y * lax.rsqrt(jnp.mean(y * y, axis=-1, keepdims=True) + EPS) * g


def _sigmoid(x):
    return 1.0 / (1.0 + jnp.exp(-x))


def _pack_bf16_pairs(a):
    k = a.shape[1] // 2
    rounded = a.astype(BF16).astype(F32)
    lo = lax.bitcast_convert_type(rounded[:, :k], jnp.uint32) >> 16
    hi = lax.bitcast_convert_type(rounded[:, k:], jnp.uint32) & jnp.uint32(0xFFFF0000)
    return lo | hi


def _unpack_bf16_pairs(w):
    lo = lax.bitcast_convert_type(w << 16, F32)
    hi = lax.bitcast_convert_type(w & jnp.uint32(0xFFFF0000), F32)
    return jnp.concatenate([lo, hi], axis=1)


def _rope_kernel(pos_ref, inv_ref, tab_ref):
    ang = pos_ref[0].astype(F32) * inv_ref[...]
    row = lax.broadcasted_iota(jnp.int32, (LANES - 2 * N_FREQ, ang.shape[1]), 0)
    pad = jnp.where(row == 0, 1.0, 0.0)
    tab_ref[0] = jnp.concatenate([jnp.cos(ang), jnp.sin(ang), pad], axis=0).T


def _rope_tables(positions, inv):
    bsz, s = positions.shape
    ts = min(s, 2048)
    return pl.pallas_call(
        _rope_kernel,
        grid=(bsz, s // ts),
        in_specs=[pl.BlockSpec((1, 1, ts), lambda b, i: (b, 0, i)),
                  pl.BlockSpec((N_FREQ, 1), lambda b, i: (0, 0))],
        out_specs=pl.BlockSpec((1, ts, LANES), lambda b, i: (b, i, 0)),
        out_shape=jax.ShapeDtypeStruct((bsz, s, LANES), F32),
        compiler_params=_cparams(("parallel", "parallel")),
        name="rope_tables",
    )(positions.reshape(bsz, 1, s), inv)


def _expand_trig(tab, expand):
    hi = tab.astype(BF16)
    lo = (tab - hi.astype(F32)).astype(BF16)
    trig = _dot(hi, expand) + _dot(lo, expand)
    return trig[:, :LANES], trig[:, LANES:]


def _mod_kernel(c_ref, w_ref, b_ref, o_ref):
    c = c_ref[...]
    ca = c * _sigmoid(c)
    o_ref[0] = jnp.dot(ca, w_ref[0], precision=HIGHEST, preferred_element_type=F32) + b_ref[0]


def _modulation(c_pad, ada_w, ada_b):
    nl, d, n = ada_w.shape
    tn = 1536
    return pl.pallas_call(
        _mod_kernel,
        grid=(nl, n // tn),
        in_specs=[pl.BlockSpec((8, d), lambda l, j: (0, 0)),
                  pl.BlockSpec((1, d, tn), lambda l, j: (l, 0, j)),
                  pl.BlockSpec((1, 1, tn), lambda l, j: (l, 0, j))],
        out_specs=pl.BlockSpec((1, 8, tn), lambda l, j: (l, 0, j)),
        out_shape=jax.ShapeDtypeStruct((nl, 8, n), F32),
        compiler_params=_cparams(("parallel", "parallel")),
        name="adaln_mod",
    )(c_pad, ada_w, ada_b.reshape(nl, 1, n))


def _norm_project(x, g_ref, sc_ref, sh_ref, w_ref):
    h = _rms_rows(x, g_ref[...]) * (1.0 + sc_ref[0]) + sh_ref[0]
    return _dot(h.astype(BF16), w_ref[...]).astype(BF16)


def _inproj_kernel(x_ref, g_ref, sc_ref, sh_ref, w_ref, u_ref):
    u_ref[0] = _norm_project(x_ref[0], g_ref, sc_ref, sh_ref, w_ref)


def _router_weights(meta):
    lane = lax.broadcasted_iota(jnp.int32, meta.shape, 1)
    w1 = jnp.sum(jnp.where(lane == 4, meta, 0.0), axis=-1, keepdims=True)
    w2 = jnp.sum(jnp.where(lane == 5, meta, 0.0), axis=-1, keepdims=True)
    return w1, w2


def _inproj_combine_kernel(d1c_ref, d2c_ref, d1n_ref, d2n_ref, x_ref, meta_ref, gtf_ref, y_hbm,
                           g_ref, sc_ref, sh_ref, w_ref, xo_ref, u_ref,
                           a1_ref, a2_ref, b1_ref, b2_ref, sem_a, sem_b, *, tm):
    n = pl.program_id(0)
    bufs = ((a1_ref, a2_ref, sem_a), (b1_ref, b2_ref, sem_b))

    def pull(d_ref, buf, sem, r):
        return pltpu.make_async_copy(y_hbm.at[pl.ds(d_ref[0, 0, r], 1), :], buf.at[pl.ds(r, 1), :], sem)

    def wait_tile(buf, sem):
        one_row = pltpu.make_async_copy(y_hbm.at[pl.ds(0, 1), :], buf.at[pl.ds(0, 1), :], sem)
        for _ in range(2 * tm):
            one_row.wait()

    @pl.when(n == 0)
    def _():
        def first(r, carry):
            pull(d1c_ref, a1_ref, sem_a, r).start()
            pull(d2c_ref, a2_ref, sem_a, r).start()
            return carry
        lax.fori_loop(0, tm, first, 0, unroll=8)

    def step(par):
        y1_ref, y2_ref, sem = bufs[par]
        n1_ref, n2_ref, nsem = bufs[1 - par]
        wait_tile(y1_ref, sem)
        for r in range(tm):
            pull(d1n_ref, n1_ref, nsem, r).start()
            pull(d2n_ref, n2_ref, nsem, r).start()
        w1, w2 = _router_weights(meta_ref[0])
        y = w1 * _unpack_bf16_pairs(y1_ref[...]) + w2 * _unpack_bf16_pairs(y2_ref[...])
        x = x_ref[0] + gtf_ref[0] * y
        xo_ref[0] = x
        u_ref[0] = _norm_project(x, g_ref, sc_ref, sh_ref, w_ref)

        @pl.when(n == pl.num_programs(0) - 1)
        def _():
            wait_tile(n1_ref, nsem)

    for par in range(2):
        pl.when(n % 2 == par)(functools.partial(step, par))


def _inproj_combine(x, meta, gtf, ys, d1, d2, g, sc, sh, w, tm):
    bsz, s, d = x.shape
    nt = s // tm
    n_steps = bsz * nt
    nxt = lambda n: jnp.minimum(n + 1, n_steps - 1)
    smem = lambda f: pl.BlockSpec((1, 1, tm), lambda n: (f(n), 0, 0), memory_space=pltpu.SMEM)
    tok = lambda wdt: pl.BlockSpec((1, tm, wdt), lambda n: (n // nt, n % nt, 0))
    vec = pl.BlockSpec((1, 1, d), lambda n: (n // nt, 0, 0))
    buf = pltpu.VMEM((tm, d // 2), jnp.uint32)
    dd1, dd2 = d1.reshape(n_steps, 1, tm), d2.reshape(n_steps, 1, tm)
    return pl.pallas_call(
        functools.partial(_inproj_combine_kernel, tm=tm),
        grid=(n_steps,),
        in_specs=[smem(lambda n: n), smem(lambda n: n), smem(nxt), smem(nxt),
                  tok(d), tok(LANES), vec, pl.BlockSpec(memory_space=pl.ANY),
                  pl.BlockSpec((1, d), lambda n: (0, 0)), vec, vec,
                  pl.BlockSpec((d, U_COLS), lambda n: (0, 0))],
        out_specs=[tok(d), tok(U_COLS)],
        out_shape=[jax.ShapeDtypeStruct((bsz, s, d), F32), jax.ShapeDtypeStruct((bsz, s, U_COLS), BF16)],
        scratch_shapes=[buf, buf, buf, buf, pltpu.SemaphoreType.DMA(()), pltpu.SemaphoreType.DMA(())],
        compiler_params=_cparams(("arbitrary",)),
        name="inproj_combine",
    )(dd1, dd2, dd1, dd2, x, meta, gtf, ys, g, sc, sh, w)


def _inproj(x, g, sc, sh, w, tm):
    bsz, s, d = x.shape
    vec = pl.BlockSpec((1, 1, d), lambda b, i: (b, 0, 0))
    return pl.pallas_call(
        _inproj_kernel,
        grid=(bsz, s // tm),
        in_specs=[pl.BlockSpec((1, tm, d), lambda b, i: (b, i, 0)),
                  pl.BlockSpec((1, d), lambda b, i: (0, 0)),
                  vec, vec,
                  pl.BlockSpec((d, U_COLS), lambda b, i: (0, 0))],
        out_specs=pl.BlockSpec((1, tm, U_COLS), lambda b, i: (b, i, 0)),
        out_shape=jax.ShapeDtypeStruct((bsz, s, U_COLS), BF16),
        compiler_params=_cparams(("parallel", "parallel")),
        name="inproj",
    )(x, g, sc, sh, w)


def _conv_kernel(u_ref, w_ref, g_ref, y_ref, buf_ref, *, tm):
    @pl.when(pl.program_id(1) == 0)
    def _():
        buf_ref[0:8, :] = jnp.zeros((8, GROUP_W), F32)

    u = u_ref[0].astype(F32)
    b_gate, c_gate, xin = u[:, :GROUP_W], u[:, GROUP_W:2 * GROUP_W], u[:, 2 * GROUP_W:]
    cx = c_gate * xin
    buf_ref[8:8 + tm, :] = cx
    conv = (w_ref[2:3, :] * cx + w_ref[1:2, :] * buf_ref[7:7 + tm, :]
            + w_ref[0:1, :] * buf_ref[6:6 + tm, :])
    buf_ref[0:8, :] = cx[tm - 8:, :]
    y_ref[0] = _rms_rows(b_gate * conv, g_ref[...]).astype(BF16)


def _conv_mixer(u, w, g, tm):
    bsz, s, _ = u.shape
    return pl.pallas_call(
        functools.partial(_conv_kernel, tm=tm),
        grid=(bsz, s // tm),
        in_specs=[pl.BlockSpec((1, tm, U_CONV), lambda b, i: (b, i, (U_RET + U_LRU) // U_CONV)),
                  pl.BlockSpec((3, GROUP_W), lambda b, i: (0, 0)),
                  pl.BlockSpec((1, GROUP_W), lambda b, i: (0, 0))],
        out_specs=pl.BlockSpec((1, tm, GROUP_W), lambda b, i: (b, i, 0)),
        out_shape=jax.ShapeDtypeStruct((bsz, s, GROUP_W), BF16),
        scratch_shapes=[pltpu.VMEM((tm + 8, GROUP_W), F32)],
        compiler_params=_cparams(("parallel", "arbitrary")),
        name="conv_mixer",
    )(u, w, g)


def _lru_kernel(u_ref, cw_ref, cb_ref, wa_ref, ba_ref, wx_ref, bx_ref, lam_ref, g_ref,
                y_ref, buf_ref, h_ref, *, tm):
    @pl.when(pl.program_id(1) == 0)
    def _():
        buf_ref[0:8, :] = jnp.zeros((8, GROUP_W), F32)
        h_ref[...] = jnp.zeros((1, GROUP_W), F32)

    u = u_ref[0].astype(F32)
    xraw, gate = u[:, :GROUP_W], u[:, GROUP_W:]
    buf_ref[8:8 + tm, :] = xraw
    xb = (cw_ref[3:4, :] * xraw + cw_ref[2:3, :] * buf_ref[7:7 + tm, :]
          + cw_ref[1:2, :] * buf_ref[6:6 + tm, :] + cw_ref[0:1, :] * buf_ref[5:5 + tm, :]
          + cb_ref[...])
    buf_ref[0:8, :] = xraw[tm - 8:, :]

    xbb = xb.astype(BF16)
    r = _sigmoid(_dot(xbb, wa_ref[...]) + ba_ref[...])
    i = _sigmoid(_dot(xbb, wx_ref[...]) + bx_ref[...])
    nlam = -lam_ref[...]
    softplus = jnp.maximum(nlam, 0.0) + jnp.log(1.0 + jnp.exp(-jnp.abs(nlam)))
    log_a = (-LRU_C) * r * softplus
    a = jnp.exp(log_a)
    b = jnp.sqrt(1.0 - a * a) * (i * xb)

    row = lax.broadcasted_iota(jnp.int32, (tm, GROUP_W), 0)
    d = 1
    while d < tm:
        keep = row >= d
        a_sh = jnp.where(keep, pltpu.roll(a, d, 0), 1.0)
        b_sh = jnp.where(keep, pltpu.roll(b, d, 0), 0.0)
        b = a * b_sh + b
        a = a * a_sh
        d *= 2
    h = a * h_ref[...] + b
    h_ref[...] = h[tm - 1:tm, :]

    gelu = 0.5 * gate * (1.0 + jnp.tanh(math.sqrt(2.0 / math.pi) * (gate + 0.044715 * gate * gate * gate)))
    y_ref[0] = _rms_rows(h * gelu, g_ref[...]).astype(BF16)


def _lru_mixer(u, cw, cb, wa, ba, wx, bx, lam, g, tm):
    bsz, s, _ = u.shape
    row = pl.BlockSpec((1, GROUP_W), lambda b, i: (0, 0))
    mat = pl.BlockSpec((GROUP_W, GROUP_W), lambda b, i: (0, 0))
    return pl.pallas_call(
        functools.partial(_lru_kernel, tm=tm),
        grid=(bsz, s // tm),
        in_specs=[pl.BlockSpec((1, tm, U_LRU), lambda b, i: (b, i, U_RET // U_LRU)),
                  pl.BlockSpec((4, GROUP_W), lambda b, i: (0, 0)),
                  row, mat, row, mat, row, row, row],
        out_specs=pl.BlockSpec((1, tm, GROUP_W), lambda b, i: (b, i, 0)),
        out_shape=jax.ShapeDtypeStruct((bsz, s, GROUP_W), BF16),
        scratch_shapes=[pltpu.VMEM((tm + 8, GROUP_W), F32), pltpu.VMEM((1, GROUP_W), F32)],
        compiler_params=_cparams(("parallel", "arbitrary")),
        name="lru_mixer",
    )(u, cw, cb, wa, ba, wx, bx, lam, g)


def _ret_kernel(u_ref, tab_ref, ex_ref, inner_ref, qd_ref, kd_ref, cd_ref, bm_ref, gm_ref,
                mq_ref, mv_ref, g_ref, y_ref, st_ref):
    @pl.when(pl.program_id(1) == 0)
    def _():
        st_ref[...] = jnp.zeros((GROUP_W, GROUP_W), F32)

    u = u_ref[0].astype(F32)
    q, k = u[:, :GROUP_W], u[:, GROUP_W:2 * GROUP_W]
    v, gate = u[:, 2 * GROUP_W:3 * GROUP_W], u[:, 3 * GROUP_W:]
    cos, sin = _expand_trig(tab_ref[0], ex_ref[...])

    def rope(t):
        t1, t2 = t[:, :LANES], t[:, LANES:]
        return jnp.concatenate([t1 * cos - t2 * sin, t2 * cos + t1 * sin], axis=-1)

    qr = rope(q)
    kr = rope(k) * (RET_DK ** -0.5)
    krb = kr.astype(BF16)
    vb = v.astype(BF16)
    state = st_ref[...]
    o = _dot(qr.astype(BF16), state.astype(BF16)) * qd_ref[...]
    for h in range(N_HEADS):
        qh = (qr * mq_ref[h]).astype(BF16)
        sc = _dot_nt(qh, krb) * inner_ref[h]
        o = o + _dot(sc.astype(BF16), vb) * mv_ref[h]
    st_ref[...] = state * cd_ref[...] + bm_ref[...] * _dot_tn((kr * kd_ref[...]).astype(BF16), vb)

    gm = gm_ref[...]
    o_hi = o.astype(BF16)
    o_lo = (o - o_hi.astype(F32)).astype(BF16)
    mu = _dot(o_hi, gm) + _dot(o_lo, gm)
    dlt = o - mu
    d2 = dlt * dlt
    d2_hi = d2.astype(BF16)
    d2_lo = (d2 - d2_hi.astype(F32)).astype(BF16)
    var = _dot(d2_hi, gm) + _dot(d2_lo, gm)
    y = dlt * lax.rsqrt(var + EPS)
    y = gate * _sigmoid(gate) * y
    y_ref[0] = _rms_rows(y, g_ref[...]).astype(BF16)


def _ret_mixer(u, trig, expand, consts, g, chunk):
    bsz, s, _ = u.shape
    inner, qd, kd, cd, bm, gm, mq, mv = consts
    full = lambda shape: pl.BlockSpec(shape, lambda b, i: (0,) * len(shape))
    tab = pl.BlockSpec((1, chunk, LANES), lambda b, i: (b, i, 0))
    return pl.pallas_call(
        _ret_kernel,
        grid=(bsz, s // chunk),
        in_specs=[pl.BlockSpec((1, chunk, U_RET), lambda b, i: (b, i, 0)), tab, full(expand.shape),
                  full(inner.shape), full(qd.shape), full(kd.shape), full(cd.shape),
                  full(bm.shape), full(gm.shape), full(mq.shape), full(mv.shape),
                  full((1, GROUP_W))],
        out_specs=pl.BlockSpec((1, chunk, GROUP_W), lambda b, i: (b, i, 0)),
        out_shape=jax.ShapeDtypeStruct((bsz, s, GROUP_W), BF16),
        scratch_shapes=[pltpu.VMEM((GROUP_W, GROUP_W), F32)],
        compiler_params=_cparams(("parallel", "arbitrary")),
        name="ret_mixer",
    )(u, trig, expand, inner, qd, kd, cd, bm, gm, mq, mv, g)


def _ret_consts(chunk):
    nh = N_HEADS
    f32 = np.float32
    log_g = np.log(f32(1.0) - f32(2.0) ** (f32(-5.0) - np.arange(nh, dtype=f32)))
    idx = np.arange(chunk, dtype=f32)
    rel = idx[:, None] - idx[None, :]
    inner = np.where(rel >= 0, np.exp(log_g[:, None, None] * np.maximum(rel, 0.0)), 0.0).astype(f32)
    v_head = np.arange(GROUP_W) // HEAD_DIM
    q_head = (np.arange(GROUP_W) % LANES) // (RET_DK // 2)
    qd = np.exp(log_g[v_head][None, :] * (idx[:, None] + 1.0)).astype(f32)
    kd = np.exp(log_g[q_head][None, :] * (chunk - 1.0 - idx[:, None])).astype(f32)
    cd = np.exp(log_g[v_head] * chunk)[None, :].astype(f32)
    bm = (q_head[:, None] == v_head[None, :]).astype(f32)
    gm = jnp.asarray((v_head[:, None] == v_head[None, :]).astype(f32) / HEAD_DIM, BF16)
    mq = (q_head[None, :] == np.arange(nh)[:, None]).astype(f32)[:, None, :]
    mv = (v_head[None, :] == np.arange(nh)[:, None]).astype(f32)[:, None, :]
    return tuple(jnp.asarray(a) for a in (inner, qd, kd, cd, bm)) + (gm, jnp.asarray(mq), jnp.asarray(mv))


def _trig_expanders():
    r16, r32 = MLA_ROPE // 2, RET_DK // 2
    ret = np.zeros((LANES, 2 * LANES), np.float32)
    mla = np.zeros((LANES, 2 * LANES), np.float32)
    for j in range(r32):
        for h in range(N_HEADS):
            ret[r16 + j, h * r32 + j] = 1.0
            ret[N_FREQ + r16 + j, LANES + h * r32 + j] = 1.0
    for j in range(r16):
        for half, sign in ((0, -1.0), (1, 1.0)):
            lane = MLA_NOPE + half * r16 + j
            mla[j, lane] = 1.0
            mla[N_FREQ + j, LANES + lane] = sign
    mla[2 * N_FREQ, :MLA_NOPE] = 1.0
    return jnp.asarray(ret, BF16), jnp.asarray(mla, BF16)


def _mla_prep_kernel(u_ref, tab_ref, ex_ref, mu_ref, invu_ref, gu_ref, wbig_ref, sq_ref,
                     invq_ref, gq_ref, gqs_ref, gk_ref, onev_ref, q_ref, k_ref, v_ref):
    x = u_ref[0].astype(F32)
    ss = _dot((x * x).astype(BF16), mu_ref[...]) * invu_ref[...]
    xn = (x * lax.rsqrt(ss + EPS) * gu_ref[...]).astype(BF16)
    big = _dot(xn, wbig_ref[...])
    hw = N_HEADS * LANES
    q, qs, kn, v = big[:, :hw], big[:, hw:2 * hw], big[:, 2 * hw:3 * hw], big[:, 3 * hw:4 * hw]
    kr, krs = big[:, 4 * hw:4 * hw + LANES], big[:, 4 * hw + LANES:]
    cos, sin = _expand_trig(tab_ref[0], ex_ref[...])
    rq = lax.rsqrt(_dot((q * q).astype(BF16), sq_ref[...]) * invq_ref[...] + EPS)
    rk = lax.rsqrt(_dot((kn * kn).astype(BF16), sq_ref[...]) * invq_ref[...] + EPS)
    krot = kr * cos + krs * sin
    for h in range(N_HEADS):
        sl = slice(h * LANES, (h + 1) * LANES)
        qh = (q[:, sl] * gq_ref[...] * cos + qs[:, sl] * gqs_ref[...] * sin) * rq[:, sl]
        q_ref[0, h] = qh.astype(BF16)
        k_ref[0, h] = (kn[:, sl] * gk_ref[...] * rk[:, sl] + krot).astype(BF16)
        v_ref[0, h] = (v[:, sl] + onev_ref[...]).astype(BF16)


def _mla_prep(u, trig, expand, mu, invu, gu, wbig, sq, invq, gq, gqs, gk, onev, tm):
    bsz, s, _ = u.shape
    full = lambda a: pl.BlockSpec(a.shape, lambda b, i: (0,) * a.ndim)
    tab = pl.BlockSpec((1, tm, LANES), lambda b, i: (b, i, 0))
    out = jax.ShapeDtypeStruct((bsz, N_HEADS, s, LANES), BF16)
    ospec = pl.BlockSpec((1, N_HEADS, tm, LANES), lambda b, i: (b, 0, i, 0))
    return pl.pallas_call(
        _mla_prep_kernel,
        grid=(bsz, s // tm),
        in_specs=[pl.BlockSpec((1, tm, U_MLA), lambda b, i: (b, i, (U_COLS - U_MLA) // U_MLA)),
                  tab, full(expand), full(mu), full(invu), full(gu), full(wbig), full(sq), full(invq),
                  full(gq), full(gqs), full(gk), full(onev)],
        out_specs=[ospec, ospec, ospec],
        out_shape=[out, out, out],
        compiler_params=_cparams(("parallel", "parallel")),
        name="mla_prep",
    )(u, trig, expand, mu, invu, gu, wbig, sq, invq, gq, gqs, gk, onev)


def _flash_kernel(q_ref, k_ref, v_ref, o_ref, sa_ref, sb_ref, mca_ref, mcb_ref, m_ref, acc_ref, *, tq):
    qi = pl.program_id(2)
    q = q_ref[0, 0]
    bufs = ((sa_ref, mca_ref), (sb_ref, mcb_ref))
    m_ref[...] = jnp.full((tq, LANES), NEG_BIG, F32)
    acc_ref[...] = jnp.zeros((tq, LANES), F32)

    def scores(c, masked, dst):
        s_ref, mc_ref = dst
        start = pl.multiple_of(c * tq, tq)
        s = _dot_nt(q, k_ref[0, 0, pl.ds(start, tq), :])
        if masked:
            row = qi * tq + lax.broadcasted_iota(jnp.int32, (tq, tq), 0)
            col = start + lax.broadcasted_iota(jnp.int32, (tq, tq), 1)
            s = jnp.where(col <= row, s, NEG_BIG)
        s_ref[...] = s
        mc_ref[...] = jnp.broadcast_to(jnp.max(s, axis=-1, keepdims=True), (tq, LANES))

    def accumulate(c, src):
        s_ref, mc_ref = src
        start = pl.multiple_of(c * tq, tq)
        m_prev = m_ref[...]
        m_new = jnp.maximum(m_prev, mc_ref[...])
        alpha = jnp.exp2(m_prev - m_new)
        p = jnp.exp2(s_ref[...] - jnp.tile(m_new, (1, tq // LANES)))
        pv = _dot(p.astype(BF16), v_ref[0, 0, pl.ds(start, tq), :])
        acc_ref[...] = alpha * acc_ref[...] + pv
        m_ref[...] = m_new

    def by_parity(c, fn):
        for par in range(2):
            pl.when(c % 2 == par)(functools.partial(fn, par))

    def pipelined(c, masked, par):
        scores(c + 1, masked, bufs[1 - par])
        accumulate(c, bufs[par])

    scores(0, True, bufs[0])
    n_plain = jnp.maximum(qi - 1, 0)

    def two_steps(i, carry):
        pipelined(2 * i, False, 0)
        pipelined(2 * i + 1, False, 1)
        return carry

    lax.fori_loop(0, n_plain // 2, two_steps, 0)

    @pl.when(n_plain % 2 == 1)
    def _():
        pipelined(n_plain - 1, False, 0)

    @pl.when(qi >= 1)
    def _():
        by_parity(qi - 1, functools.partial(pipelined, qi - 1, True))

    by_parity(qi, lambda par: accumulate(qi, bufs[par]))

    acc = acc_ref[...]
    lane = lax.broadcasted_iota(jnp.int32, (tq, LANES), 1)
    denom = jnp.sum(jnp.where(lane == HEAD_DIM, acc, 0.0), axis=-1, keepdims=True)
    o_ref[0, 0] = jnp.where(lane < HEAD_DIM, acc / denom, 0.0).astype(BF16)


def _flash_attention(q, k, v, tq):
    bsz, nh, s, _ = q.shape
    kv_spec = pl.BlockSpec((1, 1, s, LANES), lambda b, h, i: (b, h, 0, 0))
    blk = pl.BlockSpec((1, 1, tq, LANES), lambda b, h, i: (b, h, i, 0))
    stat = pltpu.VMEM((tq, LANES), F32)
    return pl.pallas_call(
        functools.partial(_flash_kernel, tq=tq),
        grid=(bsz, nh, s // tq),
        in_specs=[blk, kv_spec, kv_spec],
        out_specs=blk,
        out_shape=jax.ShapeDtypeStruct((bsz, nh, s, LANES), BF16),
        scratch_shapes=[pltpu.VMEM((tq, tq), F32), pltpu.VMEM((tq, tq), F32), stat, stat, stat, stat],
        compiler_params=_cparams(("parallel", "parallel", "arbitrary")),
        name="flash_attention",
    )(q, k, v)


def _outproj_kernel(x_ref, yc_ref, om_ref, yr_ref, yl_ref, wc_ref, wm_ref, wr_ref, wl_ref,
                    gmla_ref, gt_ref, gf_ref, scf_ref, shf_ref, wrt_ref, brt_ref, tri_ref,
                    xo_ref, h_ref, meta_ref, cnt_ref, run_ref):
    @pl.when((pl.program_id(0) == 0) & (pl.program_id(1) == 0))
    def _():
        run_ref[...] = jnp.zeros((1, LANES), F32)

    om = [om_ref[0, h].astype(F32) for h in range(N_HEADS)]
    ssq = om[0] * om[0]
    for h in range(1, N_HEADS):
        ssq = ssq + om[h] * om[h]
    r_mla = lax.rsqrt(jnp.sum(ssq, axis=-1, keepdims=True) / GROUP_W + EPS)
    y = _dot(yc_ref[0], wc_ref[...]) + _dot(yr_ref[0], wr_ref[...]) + _dot(yl_ref[0], wl_ref[...])
    for h in range(0, N_HEADS, 2):
        pair = jnp.concatenate([(om[h] * r_mla * gmla_ref[h]).astype(BF16),
                                (om[h + 1] * r_mla * gmla_ref[h + 1]).astype(BF16)], axis=1)
        y = y + _dot(pair, wm_ref[h // 2])
    x = x_ref[0] + gt_ref[0] * y
    xo_ref[0] = x
    hf = _rms_rows(x, gf_ref[...]) * (1.0 + scf_ref[0]) + shf_ref[0]
    h_ref[0] = _pack_bf16_pairs(hf)

    h_hi = hf.astype(BF16)
    h_lo = (hf - h_hi.astype(F32)).astype(BF16)
    both = _dot(h_hi, wrt_ref[...])
    lg = both[:, :LANES] + both[:, LANES:] + _dot(h_lo, wrt_ref[:, :LANES])
    tm = lg.shape[0]
    lane = lax.broadcasted_iota(jnp.int32, (tm, LANES), 1)
    bias = brt_ref[...]
    is_g = (lane >= N_EXPERTS) & (lane < N_EXPERTS + MOE_GROUPS)
    is_e = lane < N_EXPERTS

    def first_argmax(val):
        mx = jnp.max(val, axis=-1, keepdims=True)
        return jnp.min(jnp.where(val == mx, lane, LANES), axis=-1, keepdims=True)

    gl = jnp.where(is_g, lg, NEG_BIG)
    ge = jnp.exp(gl - jnp.max(gl, axis=-1, keepdims=True))
    gp = ge / jnp.sum(ge, axis=-1, keepdims=True)
    g_idx = first_argmax(jnp.where(is_g, gp + bias, NEG_BIG))
    g_weight = jnp.sum(jnp.where(lane == g_idx, gp, 0.0), axis=-1, keepdims=True)
    in_group = is_e & ((lane // EXPERTS_PER_GROUP) == (g_idx - N_EXPERTS))
    el = jnp.where(in_group, lg, NEG_BIG)
    ee = jnp.exp(el - jnp.max(el, axis=-1, keepdims=True))
    ep = ee / jnp.sum(ee, axis=-1, keepdims=True)
    score = jnp.where(in_group, ep + bias, NEG_BIG)
    i1 = first_argmax(score)
    sel1 = lane == i1
    i2 = first_argmax(jnp.where(sel1, NEG_BIG, score))
    sel2 = lane == i2
    p1 = jnp.sum(jnp.where(sel1, ep, 0.0), axis=-1, keepdims=True)
    p2 = jnp.sum(jnp.where(sel2, ep, 0.0), axis=-1, keepdims=True)
    psum = p1 + p2
    w1 = p1 / psum * g_weight
    w2 = p2 / psum * g_weight

    onehot = jnp.where(sel1, 1.0, jnp.where(sel2, 1.0, 0.0)).astype(BF16)
    incl = _dot(tri_ref[...], onehot)
    base = run_ref[...] + incl - 1.0
    r1 = jnp.sum(jnp.where(sel1, base, 0.0), axis=-1, keepdims=True)
    r2 = jnp.sum(jnp.where(sel2, base, 0.0), axis=-1, keepdims=True)
    run_ref[...] = run_ref[...] + incl[tm - 1:tm, :]
    cnt_ref[0] = run_ref[...]
    fields = (i1.astype(F32), i2.astype(F32), r1, r2, w1, w2)
    meta = jnp.zeros((tm, LANES), F32)
    for pos, val in enumerate(fields):
        meta = jnp.where(lane == pos, val, meta)
    meta_ref[0] = meta


def _outproj(x, yc, om, yr, yl, wc, wm, wr, wl, gmla, gt, gf, scf, shf, wrt, brt, tm):
    bsz, s, d = x.shape
    nt = s // tm
    full = lambda a: pl.BlockSpec(a.shape, lambda b, i: (0,) * a.ndim)
    tok = lambda w: pl.BlockSpec((1, tm, w), lambda b, i: (b, i, 0))
    vec = pl.BlockSpec((1, 1, d), lambda b, i: (b, 0, 0))
    tri = jnp.asarray(np.tril(np.ones((tm, tm), np.float32)), BF16)
    return pl.pallas_call(
        _outproj_kernel,
        grid=(bsz, nt),
        in_specs=[tok(d), tok(GROUP_W),
                  pl.BlockSpec((1, N_HEADS, tm, LANES), lambda b, i: (b, 0, i, 0)),
                  tok(GROUP_W), tok(GROUP_W),
                  full(wc), full(wm), full(wr), full(wl), full(gmla), vec, full(gf), vec, vec,
                  full(wrt), full(brt), full(tri)],
        out_specs=[tok(d), tok(d // 2), tok(LANES),
                   pl.BlockSpec((1, 1, LANES), lambda b, i: (b * nt + i, 0, 0))],
        out_shape=[jax.ShapeDtypeStruct((bsz, s, d), F32), jax.ShapeDtypeStruct((bsz, s, d // 2), jnp.uint32),
                   jax.ShapeDtypeStruct((bsz, s, LANES), F32),
                   jax.ShapeDtypeStruct((bsz * nt, 1, LANES), F32)],
        scratch_shapes=[pltpu.VMEM((1, LANES), F32)],
        compiler_params=_cparams(("arbitrary", "arbitrary")),
        name="outproj_router",
    )(x, yc, om, yr, yl, wc, wm, wr, wl, gmla, gt, gf, scf, shf, wrt, brt, tri)


def _route_plan(meta, cnt, tmg):
    t = meta.shape[0] * meta.shape[1]
    m = meta.reshape(t, LANES)
    e = m[:, 0:2].astype(jnp.int32)
    r = m[:, 2:4].astype(jnp.int32)
    counts = cnt[-1, 0, :N_EXPERTS].astype(jnp.int32)
    padded = (counts + tmg - 1) // tmg * tmg
    ends = jnp.cumsum(padded)
    starts = ends - padded
    dest = jnp.take(starts, e) + r
    n_tiles = (2 * t) // tmg + N_EXPERTS
    tile_start = jnp.arange(n_tiles, dtype=jnp.int32) * tmg
    tile_exp = jnp.sum((ends[None, :] <= tile_start[:, None]).astype(jnp.int32), axis=1)
    tile_exp = jnp.minimum(tile_exp, N_EXPERTS - 1)
    n_used = (ends[-1:] // tmg).astype(jnp.int32)
    last_tile = jnp.maximum(ends - tmg, 0).astype(jnp.int32)
    return dest[:, 0], dest[:, 1], tile_exp, n_used, last_tile, padded.astype(jnp.int32)


def _dispatch_kernel(zs_ref, zv_ref, nu_ref, d1_ref, d2_ref, h_ref, xs_ref, zero_ref, sem, *, tmc, tmg):
    @pl.when(pl.program_id(0) == 0)
    def _():
        zero_ref[...] = jnp.zeros(zero_ref.shape, jnp.uint32)
        n_tiles = xs_ref.shape[0] // tmg

        def fill(start):
            return pltpu.make_async_copy(zero_ref, xs_ref.at[pl.ds(pl.multiple_of(start, tmg), tmg), :], sem)

        def fill_tail(j, carry, wait):
            cp = fill(j * tmg)
            cp.wait() if wait else cp.start()
            return carry

        for wait in (False, True):
            for e in range(N_EXPERTS):
                cp = fill(zs_ref[e])
                pl.when(zv_ref[e] > 0)(cp.wait if wait else cp.start)
            lax.fori_loop(nu_ref[0], n_tiles, functools.partial(fill_tail, wait=wait), 0)

    def push(r, carry):
        src = h_ref.at[pl.ds(r, 1), :]
        pltpu.make_async_copy(src, xs_ref.at[pl.ds(d1_ref[0, 0, r], 1), :], sem).start()
        pltpu.make_async_copy(src, xs_ref.at[pl.ds(d2_ref[0, 0, r], 1), :], sem).start()
        return carry

    lax.fori_loop(0, tmc, push, 0, unroll=8)
    one_row = pltpu.make_async_copy(h_ref.at[pl.ds(0, 1), :], xs_ref.at[pl.ds(0, 1), :], sem)
    for _ in range(2 * tmc):
        one_row.wait()


def _dispatch(hp, d1, d2, last_tile, padded, n_used, tmc, tmg):
    t, dw = hp.shape
    n_rows = 2 * t + N_EXPERTS * tmg
    smem_rows = pl.BlockSpec((1, 1, tmc), lambda i, zs, zv, nu: (i, 0, 0), memory_space=pltpu.SMEM)
    return pl.pallas_call(
        functools.partial(_dispatch_kernel, tmc=tmc, tmg=tmg),
        grid_spec=pltpu.PrefetchScalarGridSpec(
            num_scalar_prefetch=3, grid=(t // tmc,),
            in_specs=[smem_rows, smem_rows, pl.BlockSpec((tmc, dw), lambda i, zs, zv, nu: (i, 0))],
            out_specs=pl.BlockSpec(memory_space=pl.ANY),
            scratch_shapes=[pltpu.VMEM((tmg, dw), jnp.uint32), pltpu.SemaphoreType.DMA(())]),
        out_shape=jax.ShapeDtypeStruct((n_rows, dw), jnp.uint32),
        compiler_params=_cparams(("arbitrary",)),
        name="moe_dispatch",
    )(last_tile, padded, n_used, d1.reshape(t // tmc, 1, tmc), d2.reshape(t // tmc, 1, tmc), hp)


def _experts_kernel(te_ref, nu_ref, xs_ref, wg_ref, wu_ref, wd_ref, y_ref):
    used = pl.program_id(0) < nu_ref[0]

    @pl.when(used)
    def _():
        x = _unpack_bf16_pairs(xs_ref[...]).astype(BF16)
        gate = _dot(x, wg_ref[0, 0].astype(BF16))
        hid = gate * _sigmoid(gate) * _dot(x, wu_ref[0, 0].astype(BF16))
        y_ref[...] = _pack_bf16_pairs(_dot(hid.astype(BF16), wd_ref[0, 0].astype(BF16)))

    @pl.when(jnp.logical_not(used))
    def _():
        y_ref[...] = jnp.zeros(y_ref.shape, jnp.uint32)


def _experts(xs, tile_exp, n_used, layer, wg, wu, wd, tmg):
    n_rows, dw = xs.shape
    d = 2 * dw
    tile = lambda i, te, nu: jnp.minimum(i, nu[0] - 1)
    rows = pl.BlockSpec((tmg, dw), lambda i, te, nu: (tile(i, te, nu), 0))
    wspec = lambda shape: pl.BlockSpec((1, 1) + shape,
                                       lambda i, te, nu: (layer, te[tile(i, te, nu)], 0, 0))
    return pl.pallas_call(
        _experts_kernel,
        grid_spec=pltpu.PrefetchScalarGridSpec(
            num_scalar_prefetch=2, grid=(n_rows // tmg,),
            in_specs=[rows, wspec((d, D_EXPERT)), wspec((d, D_EXPERT)), wspec((D_EXPERT, d))],
            out_specs=pl.BlockSpec((tmg, dw), lambda i, te, nu: (i, 0))),
        out_shape=jax.ShapeDtypeStruct((n_rows, dw), jnp.uint32),
        compiler_params=_cparams(("arbitrary",)),
        name="moe_experts",
    )(tile_exp, n_used, xs, wg, wu, wd)


def _combine_kernel(d1_ref, d2_ref, x_ref, meta_ref, gt_ref, y_ref, o_ref, b1_ref, b2_ref, sem, *, tmc):
    def pull(r, carry):
        pltpu.make_async_copy(y_ref.at[pl.ds(d1_ref[0, 0, r], 1), :], b1_ref.at[pl.ds(r, 1), :], sem).start()
        pltpu.make_async_copy(y_ref.at[pl.ds(d2_ref[0, 0, r], 1), :], b2_ref.at[pl.ds(r, 1), :], sem).start()
        return carry

    lax.fori_loop(0, tmc, pull, 0, unroll=8)
    one_row = pltpu.make_async_copy(y_ref.at[pl.ds(0, 1), :], b1_ref.at[pl.ds(0, 1), :], sem)
    for _ in range(2 * tmc):
        one_row.wait()
    w1, w2 = _router_weights(meta_ref[0])
    y = w1 * _unpack_bf16_pairs(b1_ref[...]) + w2 * _unpack_bf16_pairs(b2_ref[...])
    o_ref[0] = x_ref[0] + gt_ref[0] * y


def _combine(x, meta, gt, y, d1, d2, tmc):
    bsz, s, d = x.shape
    nt = s // tmc
    t = bsz * s
    smem_rows = pl.BlockSpec((1, 1, tmc), lambda b, i: (b * nt + i, 0, 0), memory_space=pltpu.SMEM)
    tok = lambda w: pl.BlockSpec((1, tmc, w), lambda b, i: (b, i, 0))
    buf = pltpu.VMEM((tmc, d // 2), jnp.uint32)
    return pl.pallas_call(
        functools.partial(_combine_kernel, tmc=tmc),
        grid=(bsz, nt),
        in_specs=[smem_rows, smem_rows, tok(d), tok(LANES),
                  pl.BlockSpec((1, 1, d), lambda b, i: (b, 0, 0)),
                  pl.BlockSpec(memory_space=pl.ANY)],
        out_specs=tok(d),
        out_shape=jax.ShapeDtypeStruct((bsz, s, d), F32),
        scratch_shapes=[buf, buf, pltpu.SemaphoreType.DMA(())],
        compiler_params=_cparams(("arbitrary", "arbitrary")),
        name="moe_combine",
    )(d1.reshape(t // tmc, 1, tmc), d2.reshape(t // tmc, 1, tmc), x, meta, gt, y)


def _layer_weights(l, w_in, mla_q_norm_g, mla_w_uq, mla_kv_norm_g, mla_w_ukv, mla_q_qk_g,
                   mla_k_qk_g, lru_w_a, lru_w_x, mix_norm_g, w_out, router_group_w,
                   router_group_b, router_expert_w, router_expert_b):
    half = RET_DK // 2
    perm = np.concatenate([np.arange(half) + HEAD_DIM * h for h in range(N_HEADS)]
                          + [np.arange(half) + half + HEAD_DIM * h for h in range(N_HEADS)])
    w = w_in[l]
    o_mla, o_ret, o_lru = U_CONV, U_CONV + 352, U_CONV + 352 + U_RET
    w_ret = w[:, o_ret:o_ret + U_RET]
    w_ret = jnp.concatenate([w_ret[:, perm], w_ret[:, GROUP_W + perm], w_ret[:, 2 * GROUP_W:]], axis=1)
    w_all = jnp.concatenate([w_ret, w[:, o_lru:o_lru + U_LRU], w[:, :U_CONV], w[:, o_mla:o_mla + 352],
                             jnp.zeros((D_MODEL, U_MLA - 352), F32)], axis=1).astype(BF16)

    hw = N_HEADS * LANES
    r16 = MLA_ROPE // 2
    wq = mla_w_uq[l].reshape(Q_LORA, N_HEADS, MLA_QK)
    zq = jnp.zeros((Q_LORA, N_HEADS, LANES - MLA_QK), F32)
    q_cols = jnp.concatenate([wq, zq], axis=2).reshape(Q_LORA, hw)
    wq_sw = jnp.concatenate([jnp.zeros((Q_LORA, N_HEADS, MLA_NOPE), F32), wq[:, :, MLA_NOPE + r16:],
                             wq[:, :, MLA_NOPE:MLA_NOPE + r16], zq], axis=2).reshape(Q_LORA, hw)
    wkv = mla_w_ukv[l].reshape(KV_LORA, N_HEADS, MLA_NOPE + HEAD_DIM)
    zk = jnp.zeros((KV_LORA, N_HEADS, LANES - MLA_NOPE), F32)
    k_cols = jnp.concatenate([wkv[:, :, :MLA_NOPE], zk], axis=2).reshape(KV_LORA, hw)
    v_cols = jnp.concatenate([wkv[:, :, MLA_NOPE:], zk], axis=2).reshape(KV_LORA, hw)
    eye = jnp.eye(MLA_ROPE, dtype=F32)
    place = jnp.concatenate([jnp.zeros((MLA_ROPE, MLA_NOPE), F32), eye,
                             jnp.zeros((MLA_ROPE, LANES - MLA_QK), F32)], axis=1)
    eye_sw = jnp.concatenate([eye[:, r16:], eye[:, :r16]], axis=1)
    place_sw = jnp.concatenate([jnp.zeros((MLA_ROPE, MLA_NOPE), F32), eye_sw,
                                jnp.zeros((MLA_ROPE, LANES - MLA_QK), F32)], axis=1)
    n_big = 4 * hw + 2 * LANES
    wbig = jnp.zeros((U_MLA, n_big), F32)
    wbig = wbig.at[:Q_LORA, :hw].set(q_cols).at[:Q_LORA, hw:2 * hw].set(wq_sw)
    wbig = wbig.at[Q_LORA:Q_LORA + KV_LORA, 2 * hw:3 * hw].set(k_cols)
    wbig = wbig.at[Q_LORA:Q_LORA + KV_LORA, 3 * hw:4 * hw].set(v_cols)
    wbig = wbig.at[Q_LORA + KV_LORA:352, 4 * hw:4 * hw + LANES].set(place)
    wbig = wbig.at[Q_LORA + KV_LORA:352, 4 * hw + LANES:].set(place_sw)
    wbig = wbig.astype(BF16)

    gu = jnp.concatenate([mla_q_norm_g[l], mla_kv_norm_g[l], mla_k_qk_g[l][MLA_NOPE:],
                          jnp.zeros((U_MLA - 352,), F32)])[None, :]
    qscale = (MLA_QK ** -0.5) * math.log2(math.e)
    gq_full = mla_q_qk_g[l]
    pad = jnp.zeros((LANES - MLA_QK,), F32)
    gq = (jnp.concatenate([gq_full, pad]) * qscale)[None, :]
    gqs = (jnp.concatenate([jnp.zeros((MLA_NOPE,), F32), gq_full[MLA_NOPE + r16:],
                            gq_full[MLA_NOPE:MLA_NOPE + r16], pad]) * qscale)[None, :]
    gk = jnp.concatenate([mla_k_qk_g[l][:MLA_NOPE], jnp.zeros((LANES - MLA_NOPE,), F32)])[None, :]

    def blockdiag(wb):
        out = jnp.zeros((GROUP_W, GROUP_W), F32)
        for n in range(wb.shape[0]):
            out = out.at[n * HEAD_DIM:(n + 1) * HEAD_DIM, n * HEAD_DIM:(n + 1) * HEAD_DIM].set(wb[n])
        return out.astype(BF16)

    gmix = mix_norm_g[l]
    wo = w_out[l].astype(BF16)
    wm = wo[GROUP_W:2 * GROUP_W].reshape(N_HEADS, HEAD_DIM, D_MODEL)
    wm = jnp.concatenate([wm, jnp.zeros((N_HEADS, LANES - HEAD_DIM, D_MODEL), BF16)], axis=1)
    wm = wm.reshape(N_HEADS // 2, 2 * LANES, D_MODEL)
    gmla = jnp.concatenate([gmix[GROUP_W:2 * GROUP_W].reshape(N_HEADS, 1, HEAD_DIM),
                            jnp.zeros((N_HEADS, 1, LANES - HEAD_DIM), F32)], axis=2)
    wrt = jnp.concatenate([router_expert_w[l], router_group_w[l],
                           jnp.zeros((D_MODEL, LANES - N_EXPERTS - MOE_GROUPS), F32)], axis=1)
    wrt_hi = wrt.astype(BF16)
    wrt = jnp.concatenate([wrt_hi, (wrt - wrt_hi.astype(F32)).astype(BF16)], axis=1)
    brt = jnp.concatenate([router_expert_b[l], router_group_b[l],
                           jnp.zeros((LANES - N_EXPERTS - MOE_GROUPS,), F32)])[None, :]
    return dict(w_all=w_all, wbig=wbig, gu=gu, gq=gq, gqs=gqs, gk=gk,
                wa=blockdiag(lru_w_a[l]), wx=blockdiag(lru_w_x[l]),
                g_conv=gmix[None, :GROUP_W], g_ret=gmix[None, 2 * GROUP_W:3 * GROUP_W],
                g_lru=gmix[None, 3 * GROUP_W:], gmla=gmla,
                wc=wo[:GROUP_W], wm=wm, wr=wo[2 * GROUP_W:3 * GROUP_W], wl=wo[3 * GROUP_W:],
                wrt=wrt, brt=brt)


def _mla_consts():
    seg_u = np.concatenate([np.zeros(Q_LORA), np.ones(KV_LORA), 2 * np.ones(MLA_ROPE),
                            3 * np.ones(U_MLA - 352)])
    mu = jnp.asarray(seg_u[:, None] == seg_u[None, :], BF16)
    invu = jnp.asarray(np.concatenate([np.full(Q_LORA, 1.0 / Q_LORA), np.full(KV_LORA, 1.0 / KV_LORA),
                                       np.full(MLA_ROPE, 1.0 / MLA_ROPE), np.ones(U_MLA - 352)]), F32)[None, :]
    lane = np.arange(N_HEADS * LANES)
    seg_q = (lane // LANES) * 3 + np.where(lane % LANES < MLA_NOPE, 0, np.where(lane % LANES < MLA_QK, 1, 2))
    sq = jnp.asarray(seg_q[:, None] == seg_q[None, :], BF16)
    inv_head = np.concatenate([np.full(MLA_NOPE, 1.0 / MLA_NOPE), np.full(MLA_ROPE, 1.0 / MLA_ROPE),
                               np.ones(LANES - MLA_QK)])
    invq = jnp.asarray(np.tile(inv_head, N_HEADS), F32)[None, :]
    onev = jnp.asarray((np.arange(LANES) == HEAD_DIM).astype(np.float32))[None, :]
    return mu, invu, sq, invq, onev


def kernel(x, c, positions, ada_w, ada_b, norm_mix_g, w_in, conv_w, mla_q_norm_g, mla_w_uq, mla_kv_norm_g, mla_w_ukv, mla_q_qk_g, mla_k_qk_g, lru_conv_w, lru_conv_b, lru_w_a, lru_b_a, lru_w_x, lru_b_x, lru_lambda, mix_norm_g, w_out, norm_ffn_g, router_group_w, router_group_b, router_expert_w, router_expert_b, exp_w_gate, exp_w_up, exp_w_down):
    bsz, s, d = x.shape
    depth = ada_w.shape[0]
    tm = min(512, s)
    chunk = min(256, s)
    tq = min(1024, s)
    tmc = min(512, s)
    tmg = 512

    inv = jnp.concatenate([1.0 / (ROPE_BASE ** (jnp.arange(0, MLA_ROPE, 2, dtype=F32) / MLA_ROPE)),
                           1.0 / (ROPE_BASE ** (jnp.arange(0, RET_DK, 2, dtype=F32) / RET_DK))])[:, None]
    trig = _rope_tables(positions, inv)
    ex_ret, ex_mla = _trig_expanders()

    c_pad = jnp.concatenate([c, jnp.zeros((8 - bsz, d), F32)], axis=0)
    mod = _modulation(c_pad, ada_w, ada_b)[:, :bsz]
    ret_consts = _ret_consts(chunk)
    mu, invu, sq, invq, onev = _mla_consts()

    pending = None
    for l in range(depth):
        sh_m, sc_m, gt_m, sh_f, sc_f, gt_f = [m[:, None, :] for m in jnp.split(mod[l], 6, axis=-1)]
        lw = _layer_weights(l, w_in, mla_q_norm_g, mla_w_uq, mla_kv_norm_g, mla_w_ukv, mla_q_qk_g,
                            mla_k_qk_g, lru_w_a, lru_w_x, mix_norm_g, w_out, router_group_w,
                            router_group_b, router_expert_w, router_expert_b)
        if pending is None:
            u = _inproj(x, norm_mix_g[l][None, :], sc_m, sh_m, lw["w_all"], tm)
        else:
            x, u = _inproj_combine(x, *pending, norm_mix_g[l][None, :], sc_m, sh_m, lw["w_all"], tm)
        y_conv = _conv_mixer(u, conv_w[l], lw["g_conv"], tm)
        y_lru = _lru_mixer(u, lru_conv_w[l], lru_conv_b[l][None, :], lw["wa"], lru_b_a[l][None, :],
                           lw["wx"], lru_b_x[l][None, :], lru_lambda[l][None, :], lw["g_lru"], tm)
        y_ret = _ret_mixer(u, trig, ex_ret, ret_consts, lw["g_ret"], chunk)
        q, k, v = _mla_prep(u, trig, ex_mla, mu, invu, lw["gu"], lw["wbig"], sq, invq,
                            lw["gq"], lw["gqs"], lw["gk"], onev, tm)
        o_mla = _flash_attention(q, k, v, tq)
        x, hp, meta, cnt = _outproj(x, y_conv, o_mla, y_ret, y_lru, lw["wc"], lw["wm"], lw["wr"],
                                    lw["wl"], lw["gmla"], gt_m, norm_ffn_g[l][None, :], sc_f, sh_f,
                                    lw["wrt"], lw["brt"], tm)
        d1, d2, tile_exp, n_used, last_tile, padded = _route_plan(meta, cnt, tmg)
        xs = _dispatch(hp.reshape(bsz * s, d // 2), d1, d2, last_tile, padded, n_used, tmc, tmg)
        ys = _experts(xs, tile_exp, n_used, l, exp_w_gate, exp_w_up, exp_w_down, tmg)
        pending = (meta, gt_f, ys, d1, d2)
    meta, gt_f, ys, d1, d2 = pending
    return _combine(x, meta, gt_f, ys, d1, d2, tmc)
```

```python
import functools
import math

import jax
import jax.numpy as jnp
import numpy as np
from jax import lax
from jax.experimental import pallas as pl
from jax.experimental.pallas import tpu as pltpu

F32 = jnp.float32
BF16 = jnp.bfloat16
HIGHEST = lax.Precision.HIGHEST

D_MODEL = 1024
GROUP_W = 256
HEAD_DIM = 64
N_HEADS = 4
MLA_NOPE = 64
MLA_ROPE = 32
MLA_QK = 96
Q_LORA = 192
KV_LORA = 128
RET_DK = 64
LRU_C = 8.0
MOE_GROUPS = 4
EXPERTS_PER_GROUP = 8
N_EXPERTS = 32
D_EXPERT = 256
ROPE_BASE = 10000.0
EPS = 1e-6

LANES = 128
MXU_DIM = 256
U_RET, U_LRU, U_CONV, U_MLA = 1024, 512, 768, 384
U_COLS = U_RET + U_LRU + U_CONV + U_MLA
N_FREQ = MLA_ROPE // 2 + RET_DK // 2
NEG_BIG = -1e30
VMEM_LIMIT = 56 * 1024 * 1024


def _cparams(sem):
    return pltpu.CompilerParams(dimension_semantics=sem, vmem_limit_bytes=VMEM_LIMIT)


def _dot(a, b):
    return jnp.dot(a, b, preferred_element_type=F32)


def _dot_nt(a, b):
    return lax.dot_general(a, b, (((1,), (1,)), ((), ())), preferred_element_type=F32)


def _dot_tn(a, b):
    return lax.dot_general(a, b, (((0,), (0,)), ((), ())), preferred_element_type=F32)


def _rms_rows(y, g):
    return y * lax.rsqrt(jnp.mean(y * y, axis=-1, keepdims=True) + EPS) * g


def _sigmoid(x):
    return 0.5 * jnp.tanh(0.5 * x) + 0.5


def _pack_bf16_pairs(a):
    k = a.shape[1] // 2
    rounded = a.astype(BF16).astype(F32)
    lo = lax.bitcast_convert_type(rounded[:, :k], jnp.uint32) >> 16
    hi = lax.bitcast_convert_type(rounded[:, k:], jnp.uint32) & jnp.uint32(0xFFFF0000)
    return lo | hi


ROW_TILE = (4, LANES)


def _load_rows(ref):
    return jnp.concatenate([ref[:, j, :] for j in range(ROW_TILE[0])], axis=1)


def _store_rows(ref, val):
    for j in range(ROW_TILE[0]):
        ref[:, j, :] = val[:, j * LANES:(j + 1) * LANES]


def _unpack_bf16_pairs(w):
    lo = lax.bitcast_convert_type(w << 16, F32)
    hi = lax.bitcast_convert_type(w & jnp.uint32(0xFFFF0000), F32)
    return jnp.concatenate([lo, hi], axis=1)


def _rope_kernel(pos_ref, inv_ref, tab_ref):
    ang = pos_ref[0].astype(F32) * inv_ref[...]
    row = lax.broadcasted_iota(jnp.int32, (LANES - 2 * N_FREQ, ang.shape[1]), 0)
    pad = jnp.where(row == 0, 1.0, 0.0)
    tab_ref[0] = jnp.concatenate([jnp.cos(ang), jnp.sin(ang), pad], axis=0).T


def _rope_tables(positions, inv):
    bsz, s = positions.shape
    ts = min(s, 2048)
    return pl.pallas_call(
        _rope_kernel,
        grid=(bsz, s // ts),
        in_specs=[pl.BlockSpec((1, 1, ts), lambda b, i: (b, 0, i)),
                  pl.BlockSpec((N_FREQ, 1), lambda b, i: (0, 0))],
        out_specs=pl.BlockSpec((1, ts, LANES), lambda b, i: (b, i, 0)),
        out_shape=jax.ShapeDtypeStruct((bsz, s, LANES), F32),
        compiler_params=_cparams(("parallel", "parallel")),
        name="rope_tables",
    )(positions.reshape(bsz, 1, s), inv)


def _expand_trig(tab, expand):
    hi = tab.astype(BF16)
    lo = (tab - hi.astype(F32)).astype(BF16)
    trig = _dot(hi, expand) + _dot(lo, expand)
    return trig[:, :LANES], trig[:, LANES:]


def _mod_kernel(c_ref, w_ref, b_ref, o_ref):
    c = c_ref[...]
    ca = c * _sigmoid(c)
    o_ref[0] = jnp.dot(ca, w_ref[0], precision=HIGHEST, preferred_element_type=F32) + b_ref[0]


def _modulation(c_pad, ada_w, ada_b):
    nl, d, n = ada_w.shape
    tn = 1536
    return pl.pallas_call(
        _mod_kernel,
        grid=(nl, n // tn),
        in_specs=[pl.BlockSpec((8, d), lambda l, j: (0, 0)),
                  pl.BlockSpec((1, d, tn), lambda l, j: (l, 0, j)),
                  pl.BlockSpec((1, 1, tn), lambda l, j: (l, 0, j))],
        out_specs=pl.BlockSpec((1, 8, tn), lambda l, j: (l, 0, j)),
        out_shape=jax.ShapeDtypeStruct((nl, 8, n), F32),
        compiler_params=_cparams(("parallel", "parallel")),
        name="adaln_mod",
    )(c_pad, ada_w, ada_b.reshape(nl, 1, n))


def _norm_project(x, g_ref, sc_ref, sh_ref, w_ref):
    h = _rms_rows(x, g_ref[...]) * (1.0 + sc_ref[0]) + sh_ref[0]
    return _dot(h.astype(BF16), w_ref[...]).astype(BF16)


def _inproj_kernel(x_ref, g_ref, sc_ref, sh_ref, w_ref, u_ref):
    u_ref[0] = _norm_project(x_ref[0], g_ref, sc_ref, sh_ref, w_ref)


def _router_weights(meta):
    lane = lax.broadcasted_iota(jnp.int32, meta.shape, 1)
    w1 = jnp.sum(jnp.where(lane == 4, meta, 0.0), axis=-1, keepdims=True)
    w2 = jnp.sum(jnp.where(lane == 5, meta, 0.0), axis=-1, keepdims=True)
    return w1, w2


def _inproj_combine_kernel(d1c_ref, d2c_ref, d1n_ref, d2n_ref, x_ref, meta_ref, gtf_ref, y_hbm,
                           g_ref, sc_ref, sh_ref, w_ref, xo_ref, u_ref,
                           a1_ref, a2_ref, b1_ref, b2_ref, sem_a, sem_b, *, tm):
    n = pl.program_id(0)
    bufs = ((a1_ref, a2_ref, sem_a), (b1_ref, b2_ref, sem_b))

    def pull(d_ref, buf, sem, r):
        return pltpu.make_async_copy(y_hbm.at[d_ref[0, 0, r]], buf.at[r], sem)

    def wait_tile(buf, sem):
        one_row = pltpu.make_async_copy(y_hbm.at[0], buf.at[0], sem)
        for _ in range(2 * tm):
            one_row.wait()

    @pl.when(n == 0)
    def _():
        def first(r, carry):
            pull(d1c_ref, a1_ref, sem_a, r).start()
            pull(d2c_ref, a2_ref, sem_a, r).start()
            return carry
        lax.fori_loop(0, tm, first, 0, unroll=8)

    def step(par):
        y1_ref, y2_ref, sem = bufs[par]
        n1_ref, n2_ref, nsem = bufs[1 - par]
        wait_tile(y1_ref, sem)
        for r in range(tm):
            pull(d1n_ref, n1_ref, nsem, r).start()
            pull(d2n_ref, n2_ref, nsem, r).start()
        w1, w2 = _router_weights(meta_ref[0])
        y = w1 * _unpack_bf16_pairs(_load_rows(y1_ref)) + w2 * _unpack_bf16_pairs(_load_rows(y2_ref))
        x = x_ref[0] + gtf_ref[0] * y
        xo_ref[0] = x
        u_ref[0] = _norm_project(x, g_ref, sc_ref, sh_ref, w_ref)

        @pl.when(n == pl.num_programs(0) - 1)
        def _():
            wait_tile(n1_ref, nsem)

    for par in range(2):
        pl.when(n % 2 == par)(functools.partial(step, par))


def _inproj_combine(x, meta, gtf, ys, d1, d2, g, sc, sh, w, tm):
    bsz, s, d = x.shape
    nt = s // tm
    n_steps = bsz * nt
    nxt = lambda n: jnp.minimum(n + 1, n_steps - 1)
    smem = lambda f: pl.BlockSpec((1, 1, tm), lambda n: (f(n), 0, 0), memory_space=pltpu.SMEM)
    tok = lambda wdt: pl.BlockSpec((1, tm, wdt), lambda n: (n // nt, n % nt, 0))
    vec = pl.BlockSpec((1, 1, d), lambda n: (n // nt, 0, 0))
    buf = pltpu.VMEM((tm,) + ROW_TILE, jnp.uint32)
    dd1, dd2 = d1.reshape(n_steps, 1, tm), d2.reshape(n_steps, 1, tm)
    return pl.pallas_call(
        functools.partial(_inproj_combine_kernel, tm=tm),
        grid=(n_steps,),
        in_specs=[smem(lambda n: n), smem(lambda n: n), smem(nxt), smem(nxt),
                  tok(d), tok(LANES), vec, pl.BlockSpec(memory_space=pl.ANY),
                  pl.BlockSpec((1, d), lambda n: (0, 0)), vec, vec,
                  pl.BlockSpec((d, U_COLS), lambda n: (0, 0))],
        out_specs=[tok(d), tok(U_COLS)],
        out_shape=[jax.ShapeDtypeStruct((bsz, s, d), F32), jax.ShapeDtypeStruct((bsz, s, U_COLS), BF16)],
        scratch_shapes=[buf, buf, buf, buf, pltpu.SemaphoreType.DMA(()), pltpu.SemaphoreType.DMA(())],
        compiler_params=_cparams(("arbitrary",)),
        name="inproj_combine",
    )(dd1, dd2, dd1, dd2, x, meta, gtf, ys, g, sc, sh, w)


def _inproj(x, g, sc, sh, w, tm):
    bsz, s, d = x.shape
    vec = pl.BlockSpec((1, 1, d), lambda b, i: (b, 0, 0))
    return pl.pallas_call(
        _inproj_kernel,
        grid=(bsz, s // tm),
        in_specs=[pl.BlockSpec((1, tm, d), lambda b, i: (b, i, 0)),
                  pl.BlockSpec((1, d), lambda b, i: (0, 0)),
                  vec, vec,
                  pl.BlockSpec((d, U_COLS), lambda b, i: (0, 0))],
        out_specs=pl.BlockSpec((1, tm, U_COLS), lambda b, i: (b, i, 0)),
        out_shape=jax.ShapeDtypeStruct((bsz, s, U_COLS), BF16),
        compiler_params=_cparams(("parallel", "parallel")),
        name="inproj",
    )(x, g, sc, sh, w)


def _conv_kernel(u_ref, w_ref, g_ref, y_ref, buf_ref, *, tm):
    @pl.when(pl.program_id(1) == 0)
    def _():
        buf_ref[0:8, :] = jnp.zeros((8, GROUP_W), F32)

    u = u_ref[0].astype(F32)
    b_gate, c_gate, xin = u[:, :GROUP_W], u[:, GROUP_W:2 * GROUP_W], u[:, 2 * GROUP_W:]
    cx = c_gate * xin
    buf_ref[8:8 + tm, :] = cx
    conv = (w_ref[2:3, :] * cx + w_ref[1:2, :] * buf_ref[7:7 + tm, :]
            + w_ref[0:1, :] * buf_ref[6:6 + tm, :])
    buf_ref[0:8, :] = cx[tm - 8:, :]
    y_ref[0] = _rms_rows(b_gate * conv, g_ref[...]).astype(BF16)


def _conv_mixer(u, w, g, tm):
    bsz, s, _ = u.shape
    return pl.pallas_call(
        functools.partial(_conv_kernel, tm=tm),
        grid=(bsz, s // tm),
        in_specs=[pl.BlockSpec((1, tm, U_CONV), lambda b, i: (b, i, (U_RET + U_LRU) // U_CONV)),
                  pl.BlockSpec((3, GROUP_W), lambda b, i: (0, 0)),
                  pl.BlockSpec((1, GROUP_W), lambda b, i: (0, 0))],
        out_specs=pl.BlockSpec((1, tm, GROUP_W), lambda b, i: (b, i, 0)),
        out_shape=jax.ShapeDtypeStruct((bsz, s, GROUP_W), BF16),
        scratch_shapes=[pltpu.VMEM((tm + 8, GROUP_W), F32)],
        compiler_params=_cparams(("parallel", "arbitrary")),
        name="conv_mixer",
    )(u, w, g)


def _lru_kernel(u_ref, cw_ref, cb_ref, wa_ref, ba_ref, wx_ref, bx_ref, lam_ref, g_ref,
                y_ref, buf_ref, h_ref, *, tm):
    @pl.when(pl.program_id(1) == 0)
    def _():
        buf_ref[0:8, :] = jnp.zeros((8, GROUP_W), F32)
        h_ref[...] = jnp.zeros((1, GROUP_W), F32)

    u = u_ref[0].astype(F32)
    xraw, gate = u[:, :GROUP_W], u[:, GROUP_W:]
    buf_ref[8:8 + tm, :] = xraw
    xb = (cw_ref[3:4, :] * xraw + cw_ref[2:3, :] * buf_ref[7:7 + tm, :]
          + cw_ref[1:2, :] * buf_ref[6:6 + tm, :] + cw_ref[0:1, :] * buf_ref[5:5 + tm, :]
          + cb_ref[...])
    buf_ref[0:8, :] = xraw[tm - 8:, :]

    xbb = xb.astype(BF16)
    r = _sigmoid(_dot(xbb, wa_ref[...]) + ba_ref[...])
    i = _sigmoid(_dot(xbb, wx_ref[...]) + bx_ref[...])
    nlam = -lam_ref[...]
    softplus = jnp.maximum(nlam, 0.0) + jnp.log(1.0 + jnp.exp(-jnp.abs(nlam)))
    log_a = (-LRU_C) * r * softplus
    a = jnp.exp(log_a)
    b = jnp.sqrt(1.0 - a * a) * (i * xb)

    row = lax.broadcasted_iota(jnp.int32, (tm, GROUP_W), 0)
    d = 1
    while d < tm:
        keep = row >= d
        a_sh = jnp.where(keep, pltpu.roll(a, d, 0), 1.0)
        b_sh = jnp.where(keep, pltpu.roll(b, d, 0), 0.0)
        b = a * b_sh + b
        a = a * a_sh
        d *= 2
    h = a * h_ref[...] + b
    h_ref[...] = h[tm - 1:tm, :]

    gelu = 0.5 * gate * (1.0 + jnp.tanh(math.sqrt(2.0 / math.pi) * (gate + 0.044715 * gate * gate * gate)))
    y_ref[0] = _rms_rows(h * gelu, g_ref[...]).astype(BF16)


def _lru_mixer(u, cw, cb, wa, ba, wx, bx, lam, g, tm):
    bsz, s, _ = u.shape
    row = pl.BlockSpec((1, GROUP_W), lambda b, i: (0, 0))
    mat = pl.BlockSpec((GROUP_W, GROUP_W), lambda b, i: (0, 0))
    return pl.pallas_call(
        functools.partial(_lru_kernel, tm=tm),
        grid=(bsz, s // tm),
        in_specs=[pl.BlockSpec((1, tm, U_LRU), lambda b, i: (b, i, U_RET // U_LRU)),
                  pl.BlockSpec((4, GROUP_W), lambda b, i: (0, 0)),
                  row, mat, row, mat, row, row, row],
        out_specs=pl.BlockSpec((1, tm, GROUP_W), lambda b, i: (b, i, 0)),
        out_shape=jax.ShapeDtypeStruct((bsz, s, GROUP_W), BF16),
        scratch_shapes=[pltpu.VMEM((tm + 8, GROUP_W), F32), pltpu.VMEM((1, GROUP_W), F32)],
        compiler_params=_cparams(("parallel", "arbitrary")),
        name="lru_mixer",
    )(u, cw, cb, wa, ba, wx, bx, lam, g)


def _ret_kernel(u_ref, tab_ref, ex_ref, inner_ref, qd_ref, kd_ref, cd_ref, bm_ref, gm_ref,
                mq_ref, mv_ref, g_ref, y_ref, st_ref):
    @pl.when(pl.program_id(0) == 0)
    def _():
        st_ref[...] = jnp.zeros(st_ref.shape, F32)

    for b in range(u_ref.shape[0]):
        u = u_ref[b].astype(F32)
        q, k = u[:, :GROUP_W], u[:, GROUP_W:2 * GROUP_W]
        v, gate = u[:, 2 * GROUP_W:3 * GROUP_W], u[:, 3 * GROUP_W:]
        cos, sin = _expand_trig(tab_ref[b], ex_ref[...])

        def rope(t):
            t1, t2 = t[:, :LANES], t[:, LANES:]
            return jnp.concatenate([t1 * cos - t2 * sin, t2 * cos + t1 * sin], axis=-1)

        qr = rope(q)
        kr = rope(k) * (RET_DK ** -0.5)
        krb = kr.astype(BF16)
        vb = v.astype(BF16)
        state = st_ref[b]
        o = _dot(qr.astype(BF16), state.astype(BF16)) * qd_ref[...]
        for h in range(N_HEADS):
            qh = (qr * mq_ref[h]).astype(BF16)
            sc = _dot_nt(qh, krb) * inner_ref[h]
            o = o + _dot(sc.astype(BF16), vb) * mv_ref[h]
        st_ref[b] = state * cd_ref[...] + bm_ref[...] * _dot_tn((kr * kd_ref[...]).astype(BF16), vb)

        gm = gm_ref[...]
        o_hi = o.astype(BF16)
        o_lo = (o - o_hi.astype(F32)).astype(BF16)
        mu = _dot(o_hi, gm) + _dot(o_lo, gm)
        dlt = o - mu
        var = _dot((dlt * dlt).astype(BF16), gm)
        y = dlt * lax.rsqrt(var + EPS)
        y = gate * _sigmoid(gate) * y
        y_ref[b] = _rms_rows(y, g_ref[...]).astype(BF16)


def _ret_mixer(u, trig, expand, consts, g, chunk):
    bsz, s, _ = u.shape
    inner, qd, kd, cd, bm, gm, mq, mv = consts
    full = lambda shape: pl.BlockSpec(shape, lambda i: (0,) * len(shape))
    tok = lambda w: pl.BlockSpec((bsz, chunk, w), lambda i: (0, i, 0))
    return pl.pallas_call(
        _ret_kernel,
        grid=(s // chunk,),
        in_specs=[tok(U_RET), tok(LANES), full(expand.shape),
                  full(inner.shape), full(qd.shape), full(kd.shape), full(cd.shape),
                  full(bm.shape), full(gm.shape), full(mq.shape), full(mv.shape),
                  full((1, GROUP_W))],
        out_specs=tok(GROUP_W),
        out_shape=jax.ShapeDtypeStruct((bsz, s, GROUP_W), BF16),
        scratch_shapes=[pltpu.VMEM((bsz, GROUP_W, GROUP_W), F32)],
        compiler_params=_cparams(("arbitrary",)),
        name="ret_mixer",
    )(u, trig, expand, inner, qd, kd, cd, bm, gm, mq, mv, g)


def _ret_consts(chunk):
    nh = N_HEADS
    f32 = np.float32
    log_g = np.log(f32(1.0) - f32(2.0) ** (f32(-5.0) - np.arange(nh, dtype=f32)))
    idx = np.arange(chunk, dtype=f32)
    rel = idx[:, None] - idx[None, :]
    inner = np.where(rel >= 0, np.exp(log_g[:, None, None] * np.maximum(rel, 0.0)), 0.0).astype(f32)
    v_head = np.arange(GROUP_W) // HEAD_DIM
    q_head = (np.arange(GROUP_W) % LANES) // (RET_DK // 2)
    qd = np.exp(log_g[v_head][None, :] * (idx[:, None] + 1.0)).astype(f32)
    kd = np.exp(log_g[q_head][None, :] * (chunk - 1.0 - idx[:, None])).astype(f32)
    cd = np.exp(log_g[v_head] * chunk)[None, :].astype(f32)
    bm = (q_head[:, None] == v_head[None, :]).astype(f32)
    gm = jnp.asarray((v_head[:, None] == v_head[None, :]).astype(f32) / HEAD_DIM, BF16)
    mq = (q_head[None, :] == np.arange(nh)[:, None]).astype(f32)[:, None, :]
    mv = (v_head[None, :] == np.arange(nh)[:, None]).astype(f32)[:, None, :]
    return tuple(jnp.asarray(a) for a in (inner, qd, kd, cd, bm)) + (gm, jnp.asarray(mq), jnp.asarray(mv))


def _trig_expanders():
    r16, r32 = MLA_ROPE // 2, RET_DK // 2
    ret = np.zeros((LANES, 2 * LANES), np.float32)
    mla = np.zeros((LANES, 2 * LANES), np.float32)
    for j in range(r32):
        for h in range(N_HEADS):
            ret[r16 + j, h * r32 + j] = 1.0
            ret[N_FREQ + r16 + j, LANES + h * r32 + j] = 1.0
    for j in range(r16):
        for half, sign in ((0, -1.0), (1, 1.0)):
            lane = MLA_NOPE + half * r16 + j
            mla[j, lane] = 1.0
            mla[N_FREQ + j, LANES + lane] = sign
    mla[2 * N_FREQ, :MLA_NOPE] = 1.0
    return jnp.asarray(ret, BF16), jnp.asarray(mla, BF16)


def _mla_prep_kernel(u_ref, tab_ref, ex_ref, mu_ref, invu_ref, gu_ref, wbig_ref, sq_ref,
                     invq_ref, gq_ref, gqs_ref, gk_ref, onev_ref, q_ref, k_ref, v_ref):
    x = u_ref[0].astype(F32)
    ss = _dot((x * x).astype(BF16), mu_ref[...]) * invu_ref[...]
    xn = (x * lax.rsqrt(ss + EPS) * gu_ref[...]).astype(BF16)
    big = _dot(xn, wbig_ref[...])
    hw = N_HEADS * LANES
    q, qs, kn, v = big[:, :hw], big[:, hw:2 * hw], big[:, 2 * hw:3 * hw], big[:, 3 * hw:4 * hw]
    kr, krs = big[:, 4 * hw:4 * hw + LANES], big[:, 4 * hw + LANES:]
    cos, sin = _expand_trig(tab_ref[0], ex_ref[...])
    rq = lax.rsqrt(_dot((q * q).astype(BF16), sq_ref[...]) * invq_ref[...] + EPS)
    rk = lax.rsqrt(_dot((kn * kn).astype(BF16), sq_ref[...]) * invq_ref[...] + EPS)
    krot = kr * cos + krs * sin
    for h in range(N_HEADS):
        sl = slice(h * LANES, (h + 1) * LANES)
        qh = (q[:, sl] * gq_ref[...] * cos + qs[:, sl] * gqs_ref[...] * sin) * rq[:, sl]
        q_ref[0, h] = qh.astype(BF16)
        k_ref[0, h] = (kn[:, sl] * gk_ref[...] * rk[:, sl] + krot).astype(BF16)
        v_ref[0, h] = (v[:, sl] + onev_ref[...]).astype(BF16)


def _mla_prep(u, trig, expand, mu, invu, gu, wbig, sq, invq, gq, gqs, gk, onev, tm):
    bsz, s, _ = u.shape
    full = lambda a: pl.BlockSpec(a.shape, lambda b, i: (0,) * a.ndim)
    tab = pl.BlockSpec((1, tm, LANES), lambda b, i: (b, i, 0))
    out = jax.ShapeDtypeStruct((bsz, N_HEADS, s, LANES), BF16)
    ospec = pl.BlockSpec((1, N_HEADS, tm, LANES), lambda b, i: (b, 0, i, 0))
    return pl.pallas_call(
        _mla_prep_kernel,
        grid=(bsz, s // tm),
        in_specs=[pl.BlockSpec((1, tm, U_MLA), lambda b, i: (b, i, (U_COLS - U_MLA) // U_MLA)),
                  tab, full(expand), full(mu), full(invu), full(gu), full(wbig), full(sq), full(invq),
                  full(gq), full(gqs), full(gk), full(onev)],
        out_specs=[ospec, ospec, ospec],
        out_shape=[out, out, out],
        compiler_params=_cparams(("parallel", "parallel")),
        name="mla_prep",
    )(u, trig, expand, mu, invu, gu, wbig, sq, invq, gq, gqs, gk, onev)


def _flash_kernel(q_ref, k_ref, v_ref, o_ref, sa_ref, sb_ref, mca_ref, mcb_ref, m_ref, acc_ref, *, tq):
    qi = pl.program_id(2)
    q = q_ref[0, 0]
    bufs = ((sa_ref, mca_ref), (sb_ref, mcb_ref))
    m_ref[...] = jnp.full((tq, LANES), NEG_BIG, F32)
    acc_ref[...] = jnp.zeros((tq, LANES), F32)

    def scores(c, masked, dst):
        s_ref, mc_ref = dst
        start = pl.multiple_of(c * tq, tq)
        s = _dot_nt(q, k_ref[0, 0, pl.ds(start, tq), :])
        if masked:
            row = qi * tq + lax.broadcasted_iota(jnp.int32, (tq, tq), 0)
            col = start + lax.broadcasted_iota(jnp.int32, (tq, tq), 1)
            s = jnp.where(col <= row, s, NEG_BIG)
        s_ref[...] = s
        mc_ref[...] = jnp.broadcast_to(jnp.max(s, axis=-1, keepdims=True), (tq, LANES))

    def accumulate(c, src):
        s_ref, mc_ref = src
        start = pl.multiple_of(c * tq, tq)
        m_prev = m_ref[...]
        m_new = jnp.maximum(m_prev, mc_ref[...])
        alpha = jnp.exp2(m_prev - m_new)
        p = jnp.exp2(s_ref[...] - jnp.tile(m_new, (1, tq // LANES)))
        pv = _dot(p.astype(BF16), v_ref[0, 0, pl.ds(start, tq), :])
        acc_ref[...] = alpha * acc_ref[...] + pv
        m_ref[...] = m_new

    def by_parity(c, fn):
        for par in range(2):
            pl.when(c % 2 == par)(functools.partial(fn, par))

    def pipelined(c, masked, par):
        scores(c + 1, masked, bufs[1 - par])
        accumulate(c, bufs[par])

    scores(0, True, bufs[0])
    n_plain = jnp.maximum(qi - 1, 0)

    def two_steps(i, carry):
        pipelined(2 * i, False, 0)
        pipelined(2 * i + 1, False, 1)
        return carry

    lax.fori_loop(0, n_plain // 2, two_steps, 0)

    @pl.when(n_plain % 2 == 1)
    def _():
        pipelined(n_plain - 1, False, 0)

    @pl.when(qi >= 1)
    def _():
        by_parity(qi - 1, functools.partial(pipelined, qi - 1, True))

    by_parity(qi, lambda par: accumulate(qi, bufs[par]))

    acc = acc_ref[...]
    lane = lax.broadcasted_iota(jnp.int32, (tq, LANES), 1)
    denom = jnp.sum(jnp.where(lane == HEAD_DIM, acc, 0.0), axis=-1, keepdims=True)
    o_ref[0, 0] = jnp.where(lane < HEAD_DIM, acc / denom, 0.0).astype(BF16)


def _flash_attention(q, k, v, tq):
    bsz, nh, s, _ = q.shape
    kv_spec = pl.BlockSpec((1, 1, s, LANES), lambda b, h, i: (b, h, 0, 0))
    blk = pl.BlockSpec((1, 1, tq, LANES), lambda b, h, i: (b, h, i, 0))
    stat = pltpu.VMEM((tq, LANES), F32)
    return pl.pallas_call(
        functools.partial(_flash_kernel, tq=tq),
        grid=(bsz, nh, s // tq),
        in_specs=[blk, kv_spec, kv_spec],
        out_specs=blk,
        out_shape=jax.ShapeDtypeStruct((bsz, nh, s, LANES), BF16),
        scratch_shapes=[pltpu.VMEM((tq, tq), F32), pltpu.VMEM((tq, tq), F32), stat, stat, stat, stat],
        compiler_params=_cparams(("parallel", "parallel", "arbitrary")),
        name="flash_attention",
    )(q, k, v)


def _outproj_kernel(x_ref, yc_ref, om_ref, yr_ref, yl_ref, wc_ref, wm_ref, wr_ref, wl_ref,
                    gmla_ref, gt_ref, gf_ref, scf_ref, shf_ref, wrt_ref, brt_ref, tri_ref,
                    xo_ref, h_ref, meta_ref, cnt_ref, run_ref):
    @pl.when((pl.program_id(0) == 0) & (pl.program_id(1) == 0))
    def _():
        run_ref[...] = jnp.zeros((1, LANES), F32)

    om = [om_ref[0, h].astype(F32) for h in range(N_HEADS)]
    ssq = om[0] * om[0]
    for h in range(1, N_HEADS):
        ssq = ssq + om[h] * om[h]
    r_mla = lax.rsqrt(jnp.sum(ssq, axis=-1, keepdims=True) / GROUP_W + EPS)
    y = _dot(yc_ref[0], wc_ref[...]) + _dot(yr_ref[0], wr_ref[...]) + _dot(yl_ref[0], wl_ref[...])
    for h in range(0, N_HEADS, 2):
        pair = jnp.concatenate([(om[h] * r_mla * gmla_ref[h]).astype(BF16),
                                (om[h + 1] * r_mla * gmla_ref[h + 1]).astype(BF16)], axis=1)
        y = y + _dot(pair, wm_ref[h // 2])
    x = x_ref[0] + gt_ref[0] * y
    xo_ref[0] = x
    hf = _rms_rows(x, gf_ref[...]) * (1.0 + scf_ref[0]) + shf_ref[0]
    _store_rows(h_ref.at[0], _pack_bf16_pairs(hf))

    h_hi = hf.astype(BF16)
    h_lo = (hf - h_hi.astype(F32)).astype(BF16)
    both = _dot(h_hi, wrt_ref[...])
    lg = both[:, :LANES] + both[:, LANES:] + _dot(h_lo, wrt_ref[:, :LANES])
    tm = lg.shape[0]
    lane = lax.broadcasted_iota(jnp.int32, (tm, LANES), 1)
    bias = brt_ref[...]
    is_g = (lane >= N_EXPERTS) & (lane < N_EXPERTS + MOE_GROUPS)
    is_e = lane < N_EXPERTS

    def first_argmax(val):
        mx = jnp.max(val, axis=-1, keepdims=True)
        return jnp.min(jnp.where(val == mx, lane, LANES), axis=-1, keepdims=True)

    gl = jnp.where(is_g, lg, NEG_BIG)
    ge = jnp.exp(gl - jnp.max(gl, axis=-1, keepdims=True))
    gp = ge / jnp.sum(ge, axis=-1, keepdims=True)
    g_idx = first_argmax(jnp.where(is_g, gp + bias, NEG_BIG))
    g_weight = jnp.sum(jnp.where(lane == g_idx, gp, 0.0), axis=-1, keepdims=True)
    in_group = is_e & ((lane // EXPERTS_PER_GROUP) == (g_idx - N_EXPERTS))
    el = jnp.where(in_group, lg, NEG_BIG)
    ee = jnp.exp(el - jnp.max(el, axis=-1, keepdims=True))
    ep = ee / jnp.sum(ee, axis=-1, keepdims=True)
    score = jnp.where(in_group, ep + bias, NEG_BIG)
    i1 = first_argmax(score)
    sel1 = lane == i1
    i2 = first_argmax(jnp.where(sel1, NEG_BIG, score))
    sel2 = lane == i2
    p1 = jnp.sum(jnp.where(sel1, ep, 0.0), axis=-1, keepdims=True)
    p2 = jnp.sum(jnp.where(sel2, ep, 0.0), axis=-1, keepdims=True)
    psum = p1 + p2
    w1 = p1 / psum * g_weight
    w2 = p2 / psum * g_weight

    onehot = jnp.where(sel1, 1.0, jnp.where(sel2, 1.0, 0.0)).astype(BF16)
    incl = _dot(tri_ref[...], onehot)
    base = run_ref[...] + incl - 1.0
    r1 = jnp.sum(jnp.where(sel1, base, 0.0), axis=-1, keepdims=True)
    r2 = jnp.sum(jnp.where(sel2, base, 0.0), axis=-1, keepdims=True)
    run_ref[...] = run_ref[...] + incl[tm - 1:tm, :]
    cnt_ref[0] = run_ref[...]
    fields = (i1.astype(F32), i2.astype(F32), r1, r2, w1, w2)
    meta = jnp.zeros((tm, LANES), F32)
    for pos, val in enumerate(fields):
        meta = jnp.where(lane == pos, val, meta)
    meta_ref[0] = meta


def _outproj(x, yc, om, yr, yl, wc, wm, wr, wl, gmla, gt, gf, scf, shf, wrt, brt, tm):
    bsz, s, d = x.shape
    nt = s // tm
    full = lambda a: pl.BlockSpec(a.shape, lambda b, i: (0,) * a.ndim)
    tok = lambda w: pl.BlockSpec((1, tm, w), lambda b, i: (b, i, 0))
    vec = pl.BlockSpec((1, 1, d), lambda b, i: (b, 0, 0))
    tri = jnp.asarray(np.tril(np.ones((tm, tm), np.float32)), BF16)
    return pl.pallas_call(
        _outproj_kernel,
        grid=(bsz, nt),
        in_specs=[tok(d), tok(GROUP_W),
                  pl.BlockSpec((1, N_HEADS, tm, LANES), lambda b, i: (b, 0, i, 0)),
                  tok(GROUP_W), tok(GROUP_W),
                  full(wc), full(wm), full(wr), full(wl), full(gmla), vec, full(gf), vec, vec,
                  full(wrt), full(brt), full(tri)],
        out_specs=[tok(d), pl.BlockSpec((1, tm) + ROW_TILE, lambda b, i: (b, i, 0, 0)), tok(LANES),
                   pl.BlockSpec((1, 1, LANES), lambda b, i: (b * nt + i, 0, 0))],
        out_shape=[jax.ShapeDtypeStruct((bsz, s, d), F32), jax.ShapeDtypeStruct((bsz, s) + ROW_TILE, jnp.uint32),
                   jax.ShapeDtypeStruct((bsz, s, LANES), F32),
                   jax.ShapeDtypeStruct((bsz * nt, 1, LANES), F32)],
        scratch_shapes=[pltpu.VMEM((1, LANES), F32)],
        compiler_params=_cparams(("arbitrary", "arbitrary")),
        name="outproj_router",
    )(x, yc, om, yr, yl, wc, wm, wr, wl, gmla, gt, gf, scf, shf, wrt, brt, tri)


def _route_plan(meta, cnt, tmg):
    t = meta.shape[0] * meta.shape[1]
    m = meta.reshape(t, LANES)
    e = m[:, 0:2].astype(jnp.int32)
    r = m[:, 2:4].astype(jnp.int32)
    counts = cnt[-1, 0, :N_EXPERTS].astype(jnp.int32)
    padded = (counts + tmg - 1) // tmg * tmg
    ends = jnp.cumsum(padded)
    starts = ends - padded
    dest = jnp.take(starts, e) + r
    n_tiles = (2 * t) // tmg + N_EXPERTS
    tile_start = jnp.arange(n_tiles, dtype=jnp.int32) * tmg
    tile_exp = jnp.sum((ends[None, :] <= tile_start[:, None]).astype(jnp.int32), axis=1)
    tile_exp = jnp.minimum(tile_exp, N_EXPERTS - 1)
    n_used = (ends[-1:] // tmg).astype(jnp.int32)
    last_tile = jnp.maximum(ends - tmg, 0).astype(jnp.int32)
    return dest[:, 0], dest[:, 1], tile_exp, n_used, last_tile, padded.astype(jnp.int32)


def _dispatch_kernel(zs_ref, zv_ref, nu_ref, d1_ref, d2_ref, h_ref, xs_ref, zero_ref, sem, *, tmc, tmg):
    @pl.when(pl.program_id(0) == 0)
    def _():
        zero_ref[...] = jnp.zeros(zero_ref.shape, jnp.uint32)
        n_tiles = xs_ref.shape[0] // tmg

        def fill(start):
            return pltpu.make_async_copy(zero_ref, xs_ref.at[pl.ds(pl.multiple_of(start, tmg), tmg)], sem)

        def fill_tail(j, carry, wait):
            cp = fill(j * tmg)
            cp.wait() if wait else cp.start()
            return carry

        for wait in (False, True):
            for e in range(N_EXPERTS):
                cp = fill(zs_ref[e])
                pl.when(zv_ref[e] > 0)(cp.wait if wait else cp.start)
            lax.fori_loop(nu_ref[0], n_tiles, functools.partial(fill_tail, wait=wait), 0)

    def push(r, carry):
        src = h_ref.at[r]
        pltpu.make_async_copy(src, xs_ref.at[d1_ref[0, 0, r]], sem).start()
        pltpu.make_async_copy(src, xs_ref.at[d2_ref[0, 0, r]], sem).start()
        return carry

    lax.fori_loop(0, tmc, push, 0, unroll=8)
    one_row = pltpu.make_async_copy(h_ref.at[0], xs_ref.at[0], sem)
    for _ in range(2 * tmc):
        one_row.wait()


def _dispatch(hp, d1, d2, last_tile, padded, n_used, tmc, tmg):
    t = hp.shape[0]
    n_rows = 2 * t + N_EXPERTS * tmg
    smem_rows = pl.BlockSpec((1, 1, tmc), lambda i, zs, zv, nu: (i, 0, 0), memory_space=pltpu.SMEM)
    return pl.pallas_call(
        functools.partial(_dispatch_kernel, tmc=tmc, tmg=tmg),
        grid_spec=pltpu.PrefetchScalarGridSpec(
            num_scalar_prefetch=3, grid=(t // tmc,),
            in_specs=[smem_rows, smem_rows, pl.BlockSpec((tmc,) + ROW_TILE, lambda i, zs, zv, nu: (i, 0, 0))],
            out_specs=pl.BlockSpec(memory_space=pl.ANY),
            scratch_shapes=[pltpu.VMEM((tmg,) + ROW_TILE, jnp.uint32), pltpu.SemaphoreType.DMA(())]),
        out_shape=jax.ShapeDtypeStruct((n_rows,) + ROW_TILE, jnp.uint32),
        compiler_params=_cparams(("arbitrary",)),
        name="moe_dispatch",
    )(last_tile, padded, n_used, d1.reshape(t // tmc, 1, tmc), d2.reshape(t // tmc, 1, tmc), hp)


def _experts_kernel(te_ref, nu_ref, xs_ref, wg_ref, wu_ref, wd_ref, y_ref):
    used = pl.program_id(0) < nu_ref[0]

    @pl.when(used)
    def _():
        x = _unpack_bf16_pairs(_load_rows(xs_ref)).astype(BF16)
        gate = _dot(x, wg_ref[0, 0].astype(BF16))
        hid = gate * _sigmoid(gate) * _dot(x, wu_ref[0, 0].astype(BF16))
        _store_rows(y_ref, _pack_bf16_pairs(_dot(hid.astype(BF16), wd_ref[0, 0].astype(BF16))))

    @pl.when(jnp.logical_not(used))
    def _():
        y_ref[...] = jnp.zeros(y_ref.shape, jnp.uint32)


def _experts(xs, tile_exp, n_used, layer, wg, wu, wd, tmg):
    n_rows = xs.shape[0]
    d = D_MODEL
    tile = lambda i, te, nu: jnp.minimum(i, nu[0] - 1)
    rows = pl.BlockSpec((tmg,) + ROW_TILE, lambda i, te, nu: (tile(i, te, nu), 0, 0))
    wspec = lambda shape: pl.BlockSpec((1, 1) + shape,
                                       lambda i, te, nu: (layer, te[tile(i, te, nu)], 0, 0))
    return pl.pallas_call(
        _experts_kernel,
        grid_spec=pltpu.PrefetchScalarGridSpec(
            num_scalar_prefetch=2, grid=(n_rows // tmg,),
            in_specs=[rows, wspec((d, D_EXPERT)), wspec((d, D_EXPERT)), wspec((D_EXPERT, d))],
            out_specs=pl.BlockSpec((tmg,) + ROW_TILE, lambda i, te, nu: (i, 0, 0))),
        out_shape=jax.ShapeDtypeStruct((n_rows,) + ROW_TILE, jnp.uint32),
        compiler_params=_cparams(("arbitrary",)),
        name="moe_experts",
    )(tile_exp, n_used, xs, wg, wu, wd)


def _combine_kernel(d1_ref, d2_ref, x_ref, meta_ref, gt_ref, y_ref, o_ref, b1_ref, b2_ref, sem, *, tmc):
    def pull(r, carry):
        pltpu.make_async_copy(y_ref.at[d1_ref[0, 0, r]], b1_ref.at[r], sem).start()
        pltpu.make_async_copy(y_ref.at[d2_ref[0, 0, r]], b2_ref.at[r], sem).start()
        return carry

    lax.fori_loop(0, tmc, pull, 0, unroll=8)
    one_row = pltpu.make_async_copy(y_ref.at[0], b1_ref.at[0], sem)
    for _ in range(2 * tmc):
        one_row.wait()
    w1, w2 = _router_weights(meta_ref[0])
    y = w1 * _unpack_bf16_pairs(_load_rows(b1_ref)) + w2 * _unpack_bf16_pairs(_load_rows(b2_ref))
    o_ref[0] = x_ref[0] + gt_ref[0] * y


def _combine(x, meta, gt, y, d1, d2, tmc):
    bsz, s, d = x.shape
    nt = s // tmc
    t = bsz * s
    smem_rows = pl.BlockSpec((1, 1, tmc), lambda b, i: (b * nt + i, 0, 0), memory_space=pltpu.SMEM)
    tok = lambda w: pl.BlockSpec((1, tmc, w), lambda b, i: (b, i, 0))
    buf = pltpu.VMEM((tmc,) + ROW_TILE, jnp.uint32)
    return pl.pallas_call(
        functools.partial(_combine_kernel, tmc=tmc),
        grid=(bsz, nt),
        in_specs=[smem_rows, smem_rows, tok(d), tok(LANES),
                  pl.BlockSpec((1, 1, d), lambda b, i: (b, 0, 0)),
                  pl.BlockSpec(memory_space=pl.ANY)],
        out_specs=tok(d),
        out_shape=jax.ShapeDtypeStruct((bsz, s, d), F32),
        scratch_shapes=[buf, buf, pltpu.SemaphoreType.DMA(())],
        compiler_params=_cparams(("arbitrary", "arbitrary")),
        name="moe_combine",
    )(d1.reshape(t // tmc, 1, tmc), d2.reshape(t // tmc, 1, tmc), x, meta, gt, y)


def _layer_weights(l, w_in, mla_q_norm_g, mla_w_uq, mla_kv_norm_g, mla_w_ukv, mla_q_qk_g,
                   mla_k_qk_g, lru_w_a, lru_w_x, mix_norm_g, w_out, router_group_w,
                   router_group_b, router_expert_w, router_expert_b):
    half = RET_DK // 2
    perm = np.concatenate([np.arange(half) + HEAD_DIM * h for h in range(N_HEADS)]
                          + [np.arange(half) + half + HEAD_DIM * h for h in range(N_HEADS)])
    w = w_in[l]
    o_mla, o_ret, o_lru = U_CONV, U_CONV + 352, U_CONV + 352 + U_RET
    w_ret = w[:, o_ret:o_ret + U_RET]
    w_ret = jnp.concatenate([w_ret[:, perm], w_ret[:, GROUP_W + perm], w_ret[:, 2 * GROUP_W:]], axis=1)
    w_all = jnp.concatenate([w_ret, w[:, o_lru:o_lru + U_LRU], w[:, :U_CONV], w[:, o_mla:o_mla + 352],
                             jnp.zeros((D_MODEL, U_MLA - 352), F32)], axis=1).astype(BF16)

    hw = N_HEADS * LANES
    r16 = MLA_ROPE // 2
    wq = mla_w_uq[l].reshape(Q_LORA, N_HEADS, MLA_QK)
    zq = jnp.zeros((Q_LORA, N_HEADS, LANES - MLA_QK), F32)
    q_cols = jnp.concatenate([wq, zq], axis=2).reshape(Q_LORA, hw)
    wq_sw = jnp.concatenate([jnp.zeros((Q_LORA, N_HEADS, MLA_NOPE), F32), wq[:, :, MLA_NOPE + r16:],
                             wq[:, :, MLA_NOPE:MLA_NOPE + r16], zq], axis=2).reshape(Q_LORA, hw)
    wkv = mla_w_ukv[l].reshape(KV_LORA, N_HEADS, MLA_NOPE + HEAD_DIM)
    zk = jnp.zeros((KV_LORA, N_HEADS, LANES - MLA_NOPE), F32)
    k_cols = jnp.concatenate([wkv[:, :, :MLA_NOPE], zk], axis=2).reshape(KV_LORA, hw)
    v_cols = jnp.concatenate([wkv[:, :, MLA_NOPE:], zk], axis=2).reshape(KV_LORA, hw)
    eye = jnp.eye(MLA_ROPE, dtype=F32)
    place = jnp.concatenate([jnp.zeros((MLA_ROPE, MLA_NOPE), F32), eye,
                             jnp.zeros((MLA_ROPE, LANES - MLA_QK), F32)], axis=1)
    eye_sw = jnp.concatenate([eye[:, r16:], eye[:, :r16]], axis=1)
    place_sw = jnp.concatenate([jnp.zeros((MLA_ROPE, MLA_NOPE), F32), eye_sw,
                                jnp.zeros((MLA_ROPE, LANES - MLA_QK), F32)], axis=1)
    n_big = 4 * hw + 2 * LANES
    wbig = jnp.zeros((U_MLA, n_big), F32)
    wbig = wbig.at[:Q_LORA, :hw].set(q_cols).at[:Q_LORA, hw:2 * hw].set(wq_sw)
    wbig = wbig.at[Q_LORA:Q_LORA + KV_LORA, 2 * hw:3 * hw].set(k_cols)
    wbig = wbig.at[Q_LORA:Q_LORA + KV_LORA, 3 * hw:4 * hw].set(v_cols)
    wbig = wbig.at[Q_LORA + KV_LORA:352, 4 * hw:4 * hw + LANES].set(place)
    wbig = wbig.at[Q_LORA + KV_LORA:352, 4 * hw + LANES:].set(place_sw)
    wbig = wbig.astype(BF16)

    gu = jnp.concatenate([mla_q_norm_g[l], mla_kv_norm_g[l], mla_k_qk_g[l][MLA_NOPE:],
                          jnp.zeros((U_MLA - 352,), F32)])[None, :]
    qscale = (MLA_QK ** -0.5) * math.log2(math.e)
    gq_full = mla_q_qk_g[l]
    pad = jnp.zeros((LANES - MLA_QK,), F32)
    gq = (jnp.concatenate([gq_full, pad]) * qscale)[None, :]
    gqs = (jnp.concatenate([jnp.zeros((MLA_NOPE,), F32), gq_full[MLA_NOPE + r16:],
                            gq_full[MLA_NOPE:MLA_NOPE + r16], pad]) * qscale)[None, :]
    gk = jnp.concatenate([mla_k_qk_g[l][:MLA_NOPE], jnp.zeros((LANES - MLA_NOPE,), F32)])[None, :]

    def blockdiag(wb):
        out = jnp.zeros((GROUP_W, GROUP_W), F32)
        for n in range(wb.shape[0]):
            out = out.at[n * HEAD_DIM:(n + 1) * HEAD_DIM, n * HEAD_DIM:(n + 1) * HEAD_DIM].set(wb[n])
        return out.astype(BF16)

    gmix = mix_norm_g[l]
    wo = w_out[l].astype(BF16)
    wm = wo[GROUP_W:2 * GROUP_W].reshape(N_HEADS, HEAD_DIM, D_MODEL)
    wm = jnp.concatenate([wm, jnp.zeros((N_HEADS, LANES - HEAD_DIM, D_MODEL), BF16)], axis=1)
    wm = wm.reshape(N_HEADS // 2, 2 * LANES, D_MODEL)
    gmla = jnp.concatenate([gmix[GROUP_W:2 * GROUP_W].reshape(N_HEADS, 1, HEAD_DIM),
                            jnp.zeros((N_HEADS, 1, LANES - HEAD_DIM), F32)], axis=2)
    wrt = jnp.concatenate([router_expert_w[l], router_group_w[l],
                           jnp.zeros((D_MODEL, LANES - N_EXPERTS - MOE_GROUPS), F32)], axis=1)
    wrt_hi = wrt.astype(BF16)
    wrt = jnp.concatenate([wrt_hi, (wrt - wrt_hi.astype(F32)).astype(BF16)], axis=1)
    brt = jnp.concatenate([router_expert_b[l], router_group_b[l],
                           jnp.zeros((LANES - N_EXPERTS - MOE_GROUPS,), F32)])[None, :]
    return dict(w_all=w_all, wbig=wbig, gu=gu, gq=gq, gqs=gqs, gk=gk,
                wa=blockdiag(lru_w_a[l]), wx=blockdiag(lru_w_x[l]),
                g_conv=gmix[None, :GROUP_W], g_ret=gmix[None, 2 * GROUP_W:3 * GROUP_W],
                g_lru=gmix[None, 3 * GROUP_W:], gmla=gmla,
                wc=wo[:GROUP_W], wm=wm, wr=wo[2 * GROUP_W:3 * GROUP_W], wl=wo[3 * GROUP_W:],
                wrt=wrt, brt=brt)


def _mla_consts():
    seg_u = np.concatenate([np.zeros(Q_LORA), np.ones(KV_LORA), 2 * np.ones(MLA_ROPE),
                            3 * np.ones(U_MLA - 352)])
    mu = jnp.asarray(seg_u[:, None] == seg_u[None, :], BF16)
    invu = jnp.asarray(np.concatenate([np.full(Q_LORA, 1.0 / Q_LORA), np.full(KV_LORA, 1.0 / KV_LORA),
                                       np.full(MLA_ROPE, 1.0 / MLA_ROPE), np.ones(U_MLA - 352)]), F32)[None, :]
    lane = np.arange(N_HEADS * LANES)
    seg_q = (lane // LANES) * 3 + np.where(lane % LANES < MLA_NOPE, 0, np.where(lane % LANES < MLA_QK, 1, 2))
    sq = jnp.asarray(seg_q[:, None] == seg_q[None, :], BF16)
    inv_head = np.concatenate([np.full(MLA_NOPE, 1.0 / MLA_NOPE), np.full(MLA_ROPE, 1.0 / MLA_ROPE),
                               np.ones(LANES - MLA_QK)])
    invq = jnp.asarray(np.tile(inv_head, N_HEADS), F32)[None, :]
    onev = jnp.asarray((np.arange(LANES) == HEAD_DIM).astype(np.float32))[None, :]
    return mu, invu, sq, invq, onev


def kernel(x, c, positions, ada_w, ada_b, norm_mix_g, w_in, conv_w, mla_q_norm_g, mla_w_uq, mla_kv_norm_g, mla_w_ukv, mla_q_qk_g, mla_k_qk_g, lru_conv_w, lru_conv_b, lru_w_a, lru_b_a, lru_w_x, lru_b_x, lru_lambda, mix_norm_g, w_out, norm_ffn_g, router_group_w, router_group_b, router_expert_w, router_expert_b, exp_w_gate, exp_w_up, exp_w_down):
    bsz, s, d = x.shape
    depth = ada_w.shape[0]
    tm = min(512, s)
    chunk = min(256, s)
    tq = min(1024, s)
    tmc = min(512, s)
    tmg = 512

    inv = jnp.concatenate([1.0 / (ROPE_BASE ** (jnp.arange(0, MLA_ROPE, 2, dtype=F32) / MLA_ROPE)),
                           1.0 / (ROPE_BASE ** (jnp.arange(0, RET_DK, 2, dtype=F32) / RET_DK))])[:, None]
    trig = _rope_tables(positions, inv)
    ex_ret, ex_mla = _trig_expanders()

    c_pad = jnp.concatenate([c, jnp.zeros((8 - bsz, d), F32)], axis=0)
    mod = _modulation(c_pad, ada_w, ada_b)[:, :bsz]
    ret_consts = _ret_consts(chunk)
    mu, invu, sq, invq, onev = _mla_consts()

    pending = None
    for l in range(depth):
        sh_m, sc_m, gt_m, sh_f, sc_f, gt_f = [m[:, None, :] for m in jnp.split(mod[l], 6, axis=-1)]
        lw = _layer_weights(l, w_in, mla_q_norm_g, mla_w_uq, mla_kv_norm_g, mla_w_ukv, mla_q_qk_g,
                            mla_k_qk_g, lru_w_a, lru_w_x, mix_norm_g, w_out, router_group_w,
                            router_group_b, router_expert_w, router_expert_b)
        if pending is None:
            u = _inproj(x, norm_mix_g[l][None, :], sc_m, sh_m, lw["w_all"], tm)
        else:
            x, u = _inproj_combine(x, *pending, norm_mix_g[l][None, :], sc_m, sh_m, lw["w_all"], tm)
        y_conv = _conv_mixer(u, conv_w[l], lw["g_conv"], tm)
        y_lru = _lru_mixer(u, lru_conv_w[l], lru_conv_b[l][None, :], lw["wa"], lru_b_a[l][None, :],
                           lw["wx"], lru_b_x[l][None, :], lru_lambda[l][None, :], lw["g_lru"], tm)
        y_ret = _ret_mixer(u, trig, ex_ret, ret_consts, lw["g_ret"], chunk)
        q, k, v = _mla_prep(u, trig, ex_mla, mu, invu, lw["gu"], lw["wbig"], sq, invq,
                            lw["gq"], lw["gqs"], lw["gk"], onev, tm)
        o_mla = _flash_attention(q, k, v, tq)
        x, hp, meta, cnt = _outproj(x, y_conv, o_mla, y_ret, y_lru, lw["wc"], lw["wm"], lw["wr"],
                                    lw["wl"], lw["gmla"], gt_m, norm_ffn_g[l][None, :], sc_f, sh_f,
                                    lw["wrt"], lw["brt"], tm)
        d1, d2, tile_exp, n_used, last_tile, padded = _route_plan(meta, cnt, tmg)
        xs = _dispatch(hp.reshape((bsz * s,) + ROW_TILE), d1, d2, last_tile, padded, n_used, tmc, tmg)
        ys = _experts(xs, tile_exp, n_used, l, exp_w_gate, exp_w_up, exp_w_down, tmg)
        pending = (meta, gt_f, ys, d1, d2)
    meta, gt_f, ys, d1, d2 = pending
    return _combine(x, meta, gt_f, ys, d1, d2, tmc)
```

```python
import functools
import math

import jax
import jax.numpy as jnp
import numpy as np
from jax import lax
from jax.experimental import pallas as pl
from jax.experimental.pallas import tpu as pltpu

F32 = jnp.float32
BF16 = jnp.bfloat16
HIGHEST = lax.Precision.HIGHEST

D_MODEL = 1024
GROUP_W = 256
HEAD_DIM = 64
N_HEADS = 4
MLA_NOPE = 64
MLA_ROPE = 32
MLA_QK = 96
Q_LORA = 192
KV_LORA = 128
RET_DK = 64
LRU_C = 8.0
MOE_GROUPS = 4
EXPERTS_PER_GROUP = 8
N_EXPERTS = 32
D_EXPERT = 256
ROPE_BASE = 10000.0
EPS = 1e-6

LANES = 128
SUBLANES = 8
MXU_DIM = 256
U_RET, U_LRU, U_CONV, U_MLA = 1024, 512, 768, 384
U_COLS = U_RET + U_LRU + U_CONV + U_MLA
N_FREQ = MLA_ROPE // 2 + RET_DK // 2
NEG_BIG = -1e30
VMEM_LIMIT = 56 * 1024 * 1024


def _cparams(sem):
    return pltpu.CompilerParams(dimension_semantics=sem, vmem_limit_bytes=VMEM_LIMIT)


def _dot(a, b):
    return jnp.dot(a, b, preferred_element_type=F32)


def _dot_nt(a, b):
    return lax.dot_general(a, b, (((1,), (1,)), ((), ())), preferred_element_type=F32)


def _dot_tn(a, b):
    return lax.dot_general(a, b, (((0,), (0,)), ((), ())), preferred_element_type=F32)


def _rms_rows(y, g):
    return y * lax.rsqrt(jnp.mean(y * y, axis=-1, keepdims=True) + EPS) * g


def _sigmoid(x):
    return 0.5 * jnp.tanh(0.5 * x) + 0.5


def _pack_bf16_pairs(a):
    k = a.shape[1] // 2
    rounded = a.astype(BF16).astype(F32)
    lo = lax.bitcast_convert_type(rounded[:, :k], jnp.uint32) >> 16
    hi = lax.bitcast_convert_type(rounded[:, k:], jnp.uint32) & jnp.uint32(0xFFFF0000)
    return lo | hi


def _unpack_bf16_pairs(w):
    lo = lax.bitcast_convert_type(w << 16, F32)
    hi = lax.bitcast_convert_type(w & jnp.uint32(0xFFFF0000), F32)
    return jnp.concatenate([lo, hi], axis=1)


def _rope_kernel(pos_ref, inv_ref, tab_ref):
    ang = pos_ref[0].astype(F32) * inv_ref[...]
    row = lax.broadcasted_iota(jnp.int32, (LANES - 2 * N_FREQ, ang.shape[1]), 0)
    pad = jnp.where(row == 0, 1.0, 0.0)
    tab_ref[0] = jnp.concatenate([jnp.cos(ang), jnp.sin(ang), pad], axis=0).T


def _rope_tables(positions, inv):
    bsz, s = positions.shape
    ts = min(s, 2048)
    return pl.pallas_call(
        _rope_kernel,
        grid=(bsz, s // ts),
        in_specs=[pl.BlockSpec((1, 1, ts), lambda b, i: (b, 0, i)),
                  pl.BlockSpec((N_FREQ, 1), lambda b, i: (0, 0))],
        out_specs=pl.BlockSpec((1, ts, LANES), lambda b, i: (b, i, 0)),
        out_shape=jax.ShapeDtypeStruct((bsz, s, LANES), F32),
        compiler_params=_cparams(("parallel", "parallel")),
        name="rope_tables",
    )(positions.reshape(bsz, 1, s), inv)


def _expand_trig(tab, expand):
    hi = tab.astype(BF16)
    lo = (tab - hi.astype(F32)).astype(BF16)
    trig = _dot(hi, expand) + _dot(lo, expand)
    return trig[:, :LANES], trig[:, LANES:]


def _mod_kernel(c_ref, w_ref, b_ref, o_ref):
    c = c_ref[...]
    ca = c * _sigmoid(c)
    o_ref[0] = jnp.dot(ca, w_ref[0], precision=HIGHEST, preferred_element_type=F32) + b_ref[0]


def _modulation(c_pad, ada_w, ada_b):
    nl, d, n = ada_w.shape
    tn = 1536
    return pl.pallas_call(
        _mod_kernel,
        grid=(nl, n // tn),
        in_specs=[pl.BlockSpec((8, d), lambda l, j: (0, 0)),
                  pl.BlockSpec((1, d, tn), lambda l, j: (l, 0, j)),
                  pl.BlockSpec((1, 1, tn), lambda l, j: (l, 0, j))],
        out_specs=pl.BlockSpec((1, 8, tn), lambda l, j: (l, 0, j)),
        out_shape=jax.ShapeDtypeStruct((nl, 8, n), F32),
        compiler_params=_cparams(("parallel", "parallel")),
        name="adaln_mod",
    )(c_pad, ada_w, ada_b.reshape(nl, 1, n))


def _norm_project(x, g_ref, sc_ref, sh_ref, w_ref):
    h = _rms_rows(x, g_ref[...]) * (1.0 + sc_ref[0]) + sh_ref[0]
    return _dot(h.astype(BF16), w_ref[...]).astype(BF16)


def _inproj_kernel(x_ref, g_ref, sc_ref, sh_ref, w_ref, u_ref):
    u_ref[0] = _norm_project(x_ref[0], g_ref, sc_ref, sh_ref, w_ref)


def _router_weights(meta):
    lane = lax.broadcasted_iota(jnp.int32, meta.shape, 1)
    w1 = jnp.sum(jnp.where(lane == 4, meta, 0.0), axis=-1, keepdims=True)
    w2 = jnp.sum(jnp.where(lane == 5, meta, 0.0), axis=-1, keepdims=True)
    return w1, w2


def _inproj_combine_kernel(d1c_ref, d2c_ref, d1n_ref, d2n_ref, x_ref, meta_ref, gtf_ref, y_hbm,
                           g_ref, sc_ref, sh_ref, w_ref, xo_ref, u_ref,
                           a1_ref, a2_ref, b1_ref, b2_ref, sem_a, sem_b, *, tm):
    n = pl.program_id(0)
    bufs = ((a1_ref, a2_ref, sem_a), (b1_ref, b2_ref, sem_b))

    def pull(d_ref, buf, sem, r):
        return pltpu.make_async_copy(y_hbm.at[pl.ds(d_ref[0, 0, r], 1), :], buf.at[pl.ds(r, 1), :], sem)

    def wait_tile(buf, sem):
        one_row = pltpu.make_async_copy(y_hbm.at[pl.ds(0, 1), :], buf.at[pl.ds(0, 1), :], sem)
        for _ in range(2 * tm):
            one_row.wait()

    @pl.when(n == 0)
    def _():
        def first(r, carry):
            pull(d1c_ref, a1_ref, sem_a, r).start()
            pull(d2c_ref, a2_ref, sem_a, r).start()
            return carry
        lax.fori_loop(0, tm, first, 0, unroll=8)

    def step(par):
        y1_ref, y2_ref, sem = bufs[par]
        n1_ref, n2_ref, nsem = bufs[1 - par]
        wait_tile(y1_ref, sem)
        for r in range(tm):
            pull(d1n_ref, n1_ref, nsem, r).start()
            pull(d2n_ref, n2_ref, nsem, r).start()
        w1, w2 = _router_weights(meta_ref[0])
        y = w1 * _unpack_bf16_pairs(y1_ref[...]) + w2 * _unpack_bf16_pairs(y2_ref[...])
        x = x_ref[0] + gtf_ref[0] * y
        xo_ref[0] = x
        u_ref[0] = _norm_project(x, g_ref, sc_ref, sh_ref, w_ref)

        @pl.when(n == pl.num_programs(0) - 1)
        def _():
            wait_tile(n1_ref, nsem)

    for par in range(2):
        pl.when(n % 2 == par)(functools.partial(step, par))


def _inproj_combine(x, meta, gtf, ys, d1, d2, g, sc, sh, w, tm):
    bsz, s, d = x.shape
    nt = s // tm
    n_steps = bsz * nt
    nxt = lambda n: jnp.minimum(n + 1, n_steps - 1)
    smem = lambda f: pl.BlockSpec((1, 1, tm), lambda n: (f(n), 0, 0), memory_space=pltpu.SMEM)
    tok = lambda wdt: pl.BlockSpec((1, tm, wdt), lambda n: (n // nt, n % nt, 0))
    vec = pl.BlockSpec((1, 1, d), lambda n: (n // nt, 0, 0))
    buf = pltpu.VMEM((tm, d // 2), jnp.uint32)
    dd1, dd2 = d1.reshape(n_steps, 1, tm), d2.reshape(n_steps, 1, tm)
    return pl.pallas_call(
        functools.partial(_inproj_combine_kernel, tm=tm),
        grid=(n_steps,),
        in_specs=[smem(lambda n: n), smem(lambda n: n), smem(nxt), smem(nxt),
                  tok(d), tok(LANES), vec, pl.BlockSpec(memory_space=pl.ANY),
                  pl.BlockSpec((1, d), lambda n: (0, 0)), vec, vec,
                  pl.BlockSpec((d, U_COLS), lambda n: (0, 0))],
        out_specs=[tok(d), tok(U_COLS)],
        out_shape=[jax.ShapeDtypeStruct((bsz, s, d), F32), jax.ShapeDtypeStruct((bsz, s, U_COLS), BF16)],
        scratch_shapes=[buf, buf, buf, buf, pltpu.SemaphoreType.DMA(()), pltpu.SemaphoreType.DMA(())],
        compiler_params=_cparams(("arbitrary",)),
        name="inproj_combine",
    )(dd1, dd2, dd1, dd2, x, meta, gtf, ys, g, sc, sh, w)


def _inproj(x, g, sc, sh, w, tm):
    bsz, s, d = x.shape
    vec = pl.BlockSpec((1, 1, d), lambda b, i: (b, 0, 0))
    return pl.pallas_call(
        _inproj_kernel,
        grid=(bsz, s // tm),
        in_specs=[pl.BlockSpec((1, tm, d), lambda b, i: (b, i, 0)),
                  pl.BlockSpec((1, d), lambda b, i: (0, 0)),
                  vec, vec,
                  pl.BlockSpec((d, U_COLS), lambda b, i: (0, 0))],
        out_specs=pl.BlockSpec((1, tm, U_COLS), lambda b, i: (b, i, 0)),
        out_shape=jax.ShapeDtypeStruct((bsz, s, U_COLS), BF16),
        compiler_params=_cparams(("parallel", "parallel")),
        name="inproj",
    )(x, g, sc, sh, w)


def _conv_kernel(u_ref, w_ref, g_ref, y_ref, buf_ref, *, tm):
    @pl.when(pl.program_id(1) == 0)
    def _():
        buf_ref[0:8, :] = jnp.zeros((8, GROUP_W), F32)

    u = u_ref[0].astype(F32)
    b_gate, c_gate, xin = u[:, :GROUP_W], u[:, GROUP_W:2 * GROUP_W], u[:, 2 * GROUP_W:]
    cx = c_gate * xin
    buf_ref[8:8 + tm, :] = cx
    conv = (w_ref[2:3, :] * cx + w_ref[1:2, :] * buf_ref[7:7 + tm, :]
            + w_ref[0:1, :] * buf_ref[6:6 + tm, :])
    buf_ref[0:8, :] = cx[tm - 8:, :]
    y_ref[0] = _rms_rows(b_gate * conv, g_ref[...]).astype(BF16)


def _conv_mixer(u, w, g, tm):
    bsz, s, _ = u.shape
    return pl.pallas_call(
        functools.partial(_conv_kernel, tm=tm),
        grid=(bsz, s // tm),
        in_specs=[pl.BlockSpec((1, tm, U_CONV), lambda b, i: (b, i, (U_RET + U_LRU) // U_CONV)),
                  pl.BlockSpec((3, GROUP_W), lambda b, i: (0, 0)),
                  pl.BlockSpec((1, GROUP_W), lambda b, i: (0, 0))],
        out_specs=pl.BlockSpec((1, tm, GROUP_W), lambda b, i: (b, i, 0)),
        out_shape=jax.ShapeDtypeStruct((bsz, s, GROUP_W), BF16),
        scratch_shapes=[pltpu.VMEM((tm + 8, GROUP_W), F32)],
        compiler_params=_cparams(("parallel", "arbitrary")),
        name="conv_mixer",
    )(u, w, g)


def _lru_kernel(u_ref, cw_ref, cb_ref, wa_ref, ba_ref, wx_ref, bx_ref, lam_ref, g_ref,
                y_ref, buf_ref, h_ref, *, tm):
    @pl.when(pl.program_id(1) == 0)
    def _():
        buf_ref[0:8, :] = jnp.zeros((8, GROUP_W), F32)
        h_ref[...] = jnp.zeros((1, GROUP_W), F32)

    u = u_ref[0].astype(F32)
    xraw, gate = u[:, :GROUP_W], u[:, GROUP_W:]
    buf_ref[8:8 + tm, :] = xraw
    xb = (cw_ref[3:4, :] * xraw + cw_ref[2:3, :] * buf_ref[7:7 + tm, :]
          + cw_ref[1:2, :] * buf_ref[6:6 + tm, :] + cw_ref[0:1, :] * buf_ref[5:5 + tm, :]
          + cb_ref[...])
    buf_ref[0:8, :] = xraw[tm - 8:, :]

    xbb = xb.astype(BF16)
    r = _sigmoid(_dot(xbb, wa_ref[...]) + ba_ref[...])
    i = _sigmoid(_dot(xbb, wx_ref[...]) + bx_ref[...])
    nlam = -lam_ref[...]
    softplus = jnp.maximum(nlam, 0.0) + jnp.log(1.0 + jnp.exp(-jnp.abs(nlam)))
    log_a = (-LRU_C) * r * softplus
    a = jnp.exp(log_a)
    b = jnp.sqrt(1.0 - a * a) * (i * xb)

    n_groups = tm // SUBLANES
    a = a.reshape(n_groups, SUBLANES, GROUP_W)
    b = b.reshape(n_groups, SUBLANES, GROUP_W)
    sub = lax.broadcasted_iota(jnp.int32, a.shape, 1)
    d = 1
    while d < SUBLANES:
        keep = sub >= d
        a_sh = jnp.where(keep, pltpu.roll(a, d, 1), 1.0)
        b_sh = jnp.where(keep, pltpu.roll(b, d, 1), 0.0)
        b = a * b_sh + b
        a = a * a_sh
        d *= 2
    carry = h_ref[...]
    groups = []
    for g in range(n_groups):
        hg = a[g] * carry + b[g]
        carry = hg[SUBLANES - 1:, :]
        groups.append(hg)
    h = jnp.concatenate(groups, axis=0)
    h_ref[...] = carry

    gelu = 0.5 * gate * (1.0 + jnp.tanh(math.sqrt(2.0 / math.pi) * (gate + 0.044715 * gate * gate * gate)))
    y_ref[0] = _rms_rows(h * gelu, g_ref[...]).astype(BF16)


def _lru_mixer(u, cw, cb, wa, ba, wx, bx, lam, g, tm):
    bsz, s, _ = u.shape
    row = pl.BlockSpec((1, GROUP_W), lambda b, i: (0, 0))
    mat = pl.BlockSpec((GROUP_W, GROUP_W), lambda b, i: (0, 0))
    return pl.pallas_call(
        functools.partial(_lru_kernel, tm=tm),
        grid=(bsz, s // tm),
        in_specs=[pl.BlockSpec((1, tm, U_LRU), lambda b, i: (b, i, U_RET // U_LRU)),
                  pl.BlockSpec((4, GROUP_W), lambda b, i: (0, 0)),
                  row, mat, row, mat, row, row, row],
        out_specs=pl.BlockSpec((1, tm, GROUP_W), lambda b, i: (b, i, 0)),
        out_shape=jax.ShapeDtypeStruct((bsz, s, GROUP_W), BF16),
        scratch_shapes=[pltpu.VMEM((tm + 8, GROUP_W), F32), pltpu.VMEM((1, GROUP_W), F32)],
        compiler_params=_cparams(("parallel", "arbitrary")),
        name="lru_mixer",
    )(u, cw, cb, wa, ba, wx, bx, lam, g)


def _ret_kernel(u_ref, tab_ref, ex_ref, inner_ref, qd_ref, kd_ref, cd_ref, bm_ref, gm_ref,
                mq_ref, mv_ref, g_ref, y_ref, st_ref):
    @pl.when(pl.program_id(0) == 0)
    def _():
        st_ref[...] = jnp.zeros(st_ref.shape, F32)

    for b in range(u_ref.shape[0]):
        u = u_ref[b].astype(F32)
        q, k = u[:, :GROUP_W], u[:, GROUP_W:2 * GROUP_W]
        v, gate = u[:, 2 * GROUP_W:3 * GROUP_W], u[:, 3 * GROUP_W:]
        cos, sin = _expand_trig(tab_ref[b], ex_ref[...])

        def rope(t):
            t1, t2 = t[:, :LANES], t[:, LANES:]
            return jnp.concatenate([t1 * cos - t2 * sin, t2 * cos + t1 * sin], axis=-1)

        qr = rope(q)
        kr = rope(k) * (RET_DK ** -0.5)
        krb = kr.astype(BF16)
        vb = v.astype(BF16)
        state = st_ref[b]
        o = _dot(qr.astype(BF16), state.astype(BF16)) * qd_ref[...]
        for h in range(N_HEADS):
            qh = (qr * mq_ref[h]).astype(BF16)
            sc = _dot_nt(qh, krb) * inner_ref[h]
            o = o + _dot(sc.astype(BF16), vb) * mv_ref[h]
        st_ref[b] = state * cd_ref[...] + bm_ref[...] * _dot_tn((kr * kd_ref[...]).astype(BF16), vb)

        gm = gm_ref[...]
        o_hi = o.astype(BF16)
        o_lo = (o - o_hi.astype(F32)).astype(BF16)
        mu = _dot(o_hi, gm) + _dot(o_lo, gm)
        dlt = o - mu
        var = _dot((dlt * dlt).astype(BF16), gm)
        y = dlt * lax.rsqrt(var + EPS)
        y = gate * _sigmoid(gate) * y
        y_ref[b] = _rms_rows(y, g_ref[...]).astype(BF16)


def _ret_mixer(u, trig, expand, consts, g, chunk):
    bsz, s, _ = u.shape
    inner, qd, kd, cd, bm, gm, mq, mv = consts
    full = lambda shape: pl.BlockSpec(shape, lambda i: (0,) * len(shape))
    tok = lambda w: pl.BlockSpec((bsz, chunk, w), lambda i: (0, i, 0))
    return pl.pallas_call(
        _ret_kernel,
        grid=(s // chunk,),
        in_specs=[tok(U_RET), tok(LANES), full(expand.shape),
                  full(inner.shape), full(qd.shape), full(kd.shape), full(cd.shape),
                  full(bm.shape), full(gm.shape), full(mq.shape), full(mv.shape),
                  full((1, GROUP_W))],
        out_specs=tok(GROUP_W),
        out_shape=jax.ShapeDtypeStruct((bsz, s, GROUP_W), BF16),
        scratch_shapes=[pltpu.VMEM((bsz, GROUP_W, GROUP_W), F32)],
        compiler_params=_cparams(("arbitrary",)),
        name="ret_mixer",
    )(u, trig, expand, inner, qd, kd, cd, bm, gm, mq, mv, g)


def _ret_consts(chunk):
    nh = N_HEADS
    f32 = np.float32
    log_g = np.log(f32(1.0) - f32(2.0) ** (f32(-5.0) - np.arange(nh, dtype=f32)))
    idx = np.arange(chunk, dtype=f32)
    rel = idx[:, None] - idx[None, :]
    inner = np.where(rel >= 0, np.exp(log_g[:, None, None] * np.maximum(rel, 0.0)), 0.0).astype(f32)
    v_head = np.arange(GROUP_W) // HEAD_DIM
    q_head = (np.arange(GROUP_W) % LANES) // (RET_DK // 2)
    qd = np.exp(log_g[v_head][None, :] * (idx[:, None] + 1.0)).astype(f32)
    kd = np.exp(log_g[q_head][None, :] * (chunk - 1.0 - idx[:, None])).astype(f32)
    cd = np.exp(log_g[v_head] * chunk)[None, :].astype(f32)
    bm = (q_head[:, None] == v_head[None, :]).astype(f32)
    gm = jnp.asarray((v_head[:, None] == v_head[None, :]).astype(f32) / HEAD_DIM, BF16)
    mq = (q_head[None, :] == np.arange(nh)[:, None]).astype(f32)[:, None, :]
    mv = (v_head[None, :] == np.arange(nh)[:, None]).astype(f32)[:, None, :]
    return tuple(jnp.asarray(a) for a in (inner, qd, kd, cd, bm)) + (gm, jnp.asarray(mq), jnp.asarray(mv))


def _trig_expanders():
    r16, r32 = MLA_ROPE // 2, RET_DK // 2
    ret = np.zeros((LANES, 2 * LANES), np.float32)
    mla = np.zeros((LANES, 2 * LANES), np.float32)
    for j in range(r32):
        for h in range(N_HEADS):
            ret[r16 + j, h * r32 + j] = 1.0
            ret[N_FREQ + r16 + j, LANES + h * r32 + j] = 1.0
    for j in range(r16):
        for half, sign in ((0, -1.0), (1, 1.0)):
            lane = MLA_NOPE + half * r16 + j
            mla[j, lane] = 1.0
            mla[N_FREQ + j, LANES + lane] = sign
    mla[2 * N_FREQ, :MLA_NOPE] = 1.0
    return jnp.asarray(ret, BF16), jnp.asarray(mla, BF16)


def _mla_prep_kernel(u_ref, tab_ref, ex_ref, mu_ref, invu_ref, gu_ref, wbig_ref, sq_ref,
                     invq_ref, gq_ref, gqs_ref, gk_ref, onev_ref, q_ref, k_ref, v_ref):
    x = u_ref[0].astype(F32)
    ss = _dot((x * x).astype(BF16), mu_ref[...]) * invu_ref[...]
    xn = (x * lax.rsqrt(ss + EPS) * gu_ref[...]).astype(BF16)
    big = _dot(xn, wbig_ref[...])
    hw = N_HEADS * LANES
    q, qs, kn, v = big[:, :hw], big[:, hw:2 * hw], big[:, 2 * hw:3 * hw], big[:, 3 * hw:4 * hw]
    kr, krs = big[:, 4 * hw:4 * hw + LANES], big[:, 4 * hw + LANES:]
    cos, sin = _expand_trig(tab_ref[0], ex_ref[...])
    rq = lax.rsqrt(_dot((q * q).astype(BF16), sq_ref[...]) * invq_ref[...] + EPS)
    rk = lax.rsqrt(_dot((kn * kn).astype(BF16), sq_ref[...]) * invq_ref[...] + EPS)
    krot = kr * cos + krs * sin
    for h in range(N_HEADS):
        sl = slice(h * LANES, (h + 1) * LANES)
        qh = (q[:, sl] * gq_ref[...] * cos + qs[:, sl] * gqs_ref[...] * sin) * rq[:, sl]
        q_ref[0, h] = qh.astype(BF16)
        k_ref[0, h] = (kn[:, sl] * gk_ref[...] * rk[:, sl] + krot).astype(BF16)
        v_ref[0, h] = (v[:, sl] + onev_ref[...]).astype(BF16)


def _mla_prep(u, trig, expand, mu, invu, gu, wbig, sq, invq, gq, gqs, gk, onev, tm):
    bsz, s, _ = u.shape
    full = lambda a: pl.BlockSpec(a.shape, lambda b, i: (0,) * a.ndim)
    tab = pl.BlockSpec((1, tm, LANES), lambda b, i: (b, i, 0))
    out = jax.ShapeDtypeStruct((bsz, N_HEADS, s, LANES), BF16)
    ospec = pl.BlockSpec((1, N_HEADS, tm, LANES), lambda b, i: (b, 0, i, 0))
    return pl.pallas_call(
        _mla_prep_kernel,
        grid=(bsz, s // tm),
        in_specs=[pl.BlockSpec((1, tm, U_MLA), lambda b, i: (b, i, (U_COLS - U_MLA) // U_MLA)),
                  tab, full(expand), full(mu), full(invu), full(gu), full(wbig), full(sq), full(invq),
                  full(gq), full(gqs), full(gk), full(onev)],
        out_specs=[ospec, ospec, ospec],
        out_shape=[out, out, out],
        compiler_params=_cparams(("parallel", "parallel")),
        name="mla_prep",
    )(u, trig, expand, mu, invu, gu, wbig, sq, invq, gq, gqs, gk, onev)


def _flash_kernel(q_ref, k_ref, v_ref, o_ref, sa_ref, sb_ref, mca_ref, mcb_ref, m_ref, acc_ref, *, tq):
    qi = pl.program_id(2)
    q = q_ref[0, 0]
    bufs = ((sa_ref, mca_ref), (sb_ref, mcb_ref))
    m_ref[...] = jnp.full((tq, LANES), NEG_BIG, F32)
    acc_ref[...] = jnp.zeros((tq, LANES), F32)

    def scores(c, masked, dst):
        s_ref, mc_ref = dst
        start = pl.multiple_of(c * tq, tq)
        s = _dot_nt(q, k_ref[0, 0, pl.ds(start, tq), :])
        if masked:
            row = qi * tq + lax.broadcasted_iota(jnp.int32, (tq, tq), 0)
            col = start + lax.broadcasted_iota(jnp.int32, (tq, tq), 1)
            s = jnp.where(col <= row, s, NEG_BIG)
        s_ref[...] = s
        mc_ref[...] = jnp.broadcast_to(jnp.max(s, axis=-1, keepdims=True), (tq, LANES))

    def accumulate(c, src):
        s_ref, mc_ref = src
        start = pl.multiple_of(c * tq, tq)
        m_prev = m_ref[...]
        m_new = jnp.maximum(m_prev, mc_ref[...])
        alpha = jnp.exp2(m_prev - m_new)
        p = jnp.exp2(s_ref[...] - jnp.tile(m_new, (1, tq // LANES)))
        pv = _dot(p.astype(BF16), v_ref[0, 0, pl.ds(start, tq), :])
        acc_ref[...] = alpha * acc_ref[...] + pv
        m_ref[...] = m_new

    def by_parity(c, fn):
        for par in range(2):
            pl.when(c % 2 == par)(functools.partial(fn, par))

    def pipelined(c, masked, par):
        scores(c + 1, masked, bufs[1 - par])
        accumulate(c, bufs[par])

    scores(0, True, bufs[0])
    n_plain = jnp.maximum(qi - 1, 0)

    def two_steps(i, carry):
        pipelined(2 * i, False, 0)
        pipelined(2 * i + 1, False, 1)
        return carry

    lax.fori_loop(0, n_plain // 2, two_steps, 0)

    @pl.when(n_plain % 2 == 1)
    def _():
        pipelined(n_plain - 1, False, 0)

    @pl.when(qi >= 1)
    def _():
        by_parity(qi - 1, functools.partial(pipelined, qi - 1, True))

    by_parity(qi, lambda par: accumulate(qi, bufs[par]))

    acc = acc_ref[...]
    lane = lax.broadcasted_iota(jnp.int32, (tq, LANES), 1)
    denom = jnp.sum(jnp.where(lane == HEAD_DIM, acc, 0.0), axis=-1, keepdims=True)
    o_ref[0, 0] = jnp.where(lane < HEAD_DIM, acc / denom, 0.0).astype(BF16)


def _flash_attention(q, k, v, tq):
    bsz, nh, s, _ = q.shape
    kv_spec = pl.BlockSpec((1, 1, s, LANES), lambda b, h, i: (b, h, 0, 0))
    blk = pl.BlockSpec((1, 1, tq, LANES), lambda b, h, i: (b, h, i, 0))
    stat = pltpu.VMEM((tq, LANES), F32)
    return pl.pallas_call(
        functools.partial(_flash_kernel, tq=tq),
        grid=(bsz, nh, s // tq),
        in_specs=[blk, kv_spec, kv_spec],
        out_specs=blk,
        out_shape=jax.ShapeDtypeStruct((bsz, nh, s, LANES), BF16),
        scratch_shapes=[pltpu.VMEM((tq, tq), F32), pltpu.VMEM((tq, tq), F32), stat, stat, stat, stat],
        compiler_params=_cparams(("parallel", "parallel", "arbitrary")),
        name="flash_attention",
    )(q, k, v)


def _outproj_kernel(x_ref, yc_ref, om_ref, yr_ref, yl_ref, wc_ref, wm_ref, wr_ref, wl_ref,
                    gmla_ref, gt_ref, gf_ref, scf_ref, shf_ref, wrt_ref, brt_ref, tri_ref,
                    xo_ref, h_ref, meta_ref, cnt_ref, run_ref):
    @pl.when((pl.program_id(0) == 0) & (pl.program_id(1) == 0))
    def _():
        run_ref[...] = jnp.zeros((1, LANES), F32)

    om = [om_ref[0, h].astype(F32) for h in range(N_HEADS)]
    ssq = om[0] * om[0]
    for h in range(1, N_HEADS):
        ssq = ssq + om[h] * om[h]
    r_mla = lax.rsqrt(jnp.sum(ssq, axis=-1, keepdims=True) / GROUP_W + EPS)
    y = _dot(yc_ref[0], wc_ref[...]) + _dot(yr_ref[0], wr_ref[...]) + _dot(yl_ref[0], wl_ref[...])
    for h in range(0, N_HEADS, 2):
        pair = jnp.concatenate([(om[h] * r_mla * gmla_ref[h]).astype(BF16),
                                (om[h + 1] * r_mla * gmla_ref[h + 1]).astype(BF16)], axis=1)
        y = y + _dot(pair, wm_ref[h // 2])
    x = x_ref[0] + gt_ref[0] * y
    xo_ref[0] = x
    hf = _rms_rows(x, gf_ref[...]) * (1.0 + scf_ref[0]) + shf_ref[0]
    h_ref[0] = _pack_bf16_pairs(hf)

    h_hi = hf.astype(BF16)
    h_lo = (hf - h_hi.astype(F32)).astype(BF16)
    both = _dot(h_hi, wrt_ref[...])
    lg = both[:, :LANES] + both[:, LANES:] + _dot(h_lo, wrt_ref[:, :LANES])
    tm = lg.shape[0]
    lane = lax.broadcasted_iota(jnp.int32, (tm, LANES), 1)
    bias = brt_ref[...]
    is_g = (lane >= N_EXPERTS) & (lane < N_EXPERTS + MOE_GROUPS)
    is_e = lane < N_EXPERTS

    def first_argmax(val):
        mx = jnp.max(val, axis=-1, keepdims=True)
        return jnp.min(jnp.where(val == mx, lane, LANES), axis=-1, keepdims=True)

    gl = jnp.where(is_g, lg, NEG_BIG)
    ge = jnp.exp(gl - jnp.max(gl, axis=-1, keepdims=True))
    gp = ge / jnp.sum(ge, axis=-1, keepdims=True)
    g_idx = first_argmax(jnp.where(is_g, gp + bias, NEG_BIG))
    g_weight = jnp.sum(jnp.where(lane == g_idx, gp, 0.0), axis=-1, keepdims=True)
    in_group = is_e & ((lane // EXPERTS_PER_GROUP) == (g_idx - N_EXPERTS))
    el = jnp.where(in_group, lg, NEG_BIG)
    ee = jnp.exp(el - jnp.max(el, axis=-1, keepdims=True))
    ep = ee / jnp.sum(ee, axis=-1, keepdims=True)
    score = jnp.where(in_group, ep + bias, NEG_BIG)
    i1 = first_argmax(score)
    sel1 = lane == i1
    i2 = first_argmax(jnp.where(sel1, NEG_BIG, score))
    sel2 = lane == i2
    p1 = jnp.sum(jnp.where(sel1, ep, 0.0), axis=-1, keepdims=True)
    p2 = jnp.sum(jnp.where(sel2, ep, 0.0), axis=-1, keepdims=True)
    psum = p1 + p2
    w1 = p1 / psum * g_weight
    w2 = p2 / psum * g_weight

    onehot = jnp.where(sel1, 1.0, jnp.where(sel2, 1.0, 0.0)).astype(BF16)
    incl = _dot(tri_ref[...], onehot)
    base = run_ref[...] + incl - 1.0
    r1 = jnp.sum(jnp.where(sel1, base, 0.0), axis=-1, keepdims=True)
    r2 = jnp.sum(jnp.where(sel2, base, 0.0), axis=-1, keepdims=True)
    run_ref[...] = run_ref[...] + incl[tm - 1:tm, :]
    cnt_ref[0] = run_ref[...]
    fields = (i1.astype(F32), i2.astype(F32), r1, r2, w1, w2)
    meta = jnp.zeros((tm, LANES), F32)
    for pos, val in enumerate(fields):
        meta = jnp.where(lane == pos, val, meta)
    meta_ref[0] = meta


def _outproj(x, yc, om, yr, yl, wc, wm, wr, wl, gmla, gt, gf, scf, shf, wrt, brt, tm):
    bsz, s, d = x.shape
    nt = s // tm
    full = lambda a: pl.BlockSpec(a.shape, lambda b, i: (0,) * a.ndim)
    tok = lambda w: pl.BlockSpec((1, tm, w), lambda b, i: (b, i, 0))
    vec = pl.BlockSpec((1, 1, d), lambda b, i: (b, 0, 0))
    tri = jnp.asarray(np.tril(np.ones((tm, tm), np.float32)), BF16)
    return pl.pallas_call(
        _outproj_kernel,
        grid=(bsz, nt),
        in_specs=[tok(d), tok(GROUP_W),
                  pl.BlockSpec((1, N_HEADS, tm, LANES), lambda b, i: (b, 0, i, 0)),
                  tok(GROUP_W), tok(GROUP_W),
                  full(wc), full(wm), full(wr), full(wl), full(gmla), vec, full(gf), vec, vec,
                  full(wrt), full(brt), full(tri)],
        out_specs=[tok(d), tok(d // 2), tok(LANES),
                   pl.BlockSpec((1, 1, LANES), lambda b, i: (b * nt + i, 0, 0))],
        out_shape=[jax.ShapeDtypeStruct((bsz, s, d), F32), jax.ShapeDtypeStruct((bsz, s, d // 2), jnp.uint32),
                   jax.ShapeDtypeStruct((bsz, s, LANES), F32),
                   jax.ShapeDtypeStruct((bsz * nt, 1, LANES), F32)],
        scratch_shapes=[pltpu.VMEM((1, LANES), F32)],
        compiler_params=_cparams(("arbitrary", "arbitrary")),
        name="outproj_router",
    )(x, yc, om, yr, yl, wc, wm, wr, wl, gmla, gt, gf, scf, shf, wrt, brt, tri)


def _route_plan(meta, cnt, tmg):
    t = meta.shape[0] * meta.shape[1]
    m = meta.reshape(t, LANES)
    e = m[:, 0:2].astype(jnp.int32)
    r = m[:, 2:4].astype(jnp.int32)
    counts = cnt[-1, 0, :N_EXPERTS].astype(jnp.int32)
    padded = (counts + tmg - 1) // tmg * tmg
    ends = jnp.cumsum(padded)
    starts = ends - padded
    dest = jnp.take(starts, e) + r
    n_tiles = (2 * t) // tmg + N_EXPERTS
    tile_start = jnp.arange(n_tiles, dtype=jnp.int32) * tmg
    tile_exp = jnp.sum((ends[None, :] <= tile_start[:, None]).astype(jnp.int32), axis=1)
    tile_exp = jnp.minimum(tile_exp, N_EXPERTS - 1)
    n_used = (ends[-1:] // tmg).astype(jnp.int32)
    last_tile = jnp.maximum(ends - tmg, 0).astype(jnp.int32)
    return dest[:, 0], dest[:, 1], tile_exp, n_used, last_tile, padded.astype(jnp.int32)


def _dispatch_kernel(zs_ref, zv_ref, nu_ref, d1_ref, d2_ref, h_ref, xs_ref, zero_ref, sem, *, tmc, tmg):
    @pl.when(pl.program_id(0) == 0)
    def _():
        zero_ref[...] = jnp.zeros(zero_ref.shape, jnp.uint32)
        n_tiles = xs_ref.shape[0] // tmg

        def fill(start):
            return pltpu.make_async_copy(zero_ref, xs_ref.at[pl.ds(pl.multiple_of(start, tmg), tmg), :], sem)

        def fill_tail(j, carry, wait):
            cp = fill(j * tmg)
            cp.wait() if wait else cp.start()
            return carry

        for wait in (False, True):
            for e in range(N_EXPERTS):
                cp = fill(zs_ref[e])
                pl.when(zv_ref[e] > 0)(cp.wait if wait else cp.start)
            lax.fori_loop(nu_ref[0], n_tiles, functools.partial(fill_tail, wait=wait), 0)

    def push(r, carry):
        src = h_ref.at[pl.ds(r, 1), :]
        pltpu.make_async_copy(src, xs_ref.at[pl.ds(d1_ref[0, 0, r], 1), :], sem).start()
        pltpu.make_async_copy(src, xs_ref.at[pl.ds(d2_ref[0, 0, r], 1), :], sem).start()
        return carry

    lax.fori_loop(0, tmc, push, 0, unroll=8)
    one_row = pltpu.make_async_copy(h_ref.at[pl.ds(0, 1), :], xs_ref.at[pl.ds(0, 1), :], sem)
    for _ in range(2 * tmc):
        one_row.wait()


def _dispatch(hp, d1, d2, last_tile, padded, n_used, tmc, tmg):
    t, dw = hp.shape
    n_rows = 2 * t + N_EXPERTS * tmg
    smem_rows = pl.BlockSpec((1, 1, tmc), lambda i, zs, zv, nu: (i, 0, 0), memory_space=pltpu.SMEM)
    return pl.pallas_call(
        functools.partial(_dispatch_kernel, tmc=tmc, tmg=tmg),
        grid_spec=pltpu.PrefetchScalarGridSpec(
            num_scalar_prefetch=3, grid=(t // tmc,),
            in_specs=[smem_rows, smem_rows, pl.BlockSpec((tmc, dw), lambda i, zs, zv, nu: (i, 0))],
            out_specs=pl.BlockSpec(memory_space=pl.ANY),
            scratch_shapes=[pltpu.VMEM((tmg, dw), jnp.uint32), pltpu.SemaphoreType.DMA(())]),
        out_shape=jax.ShapeDtypeStruct((n_rows, dw), jnp.uint32),
        compiler_params=_cparams(("arbitrary",)),
        name="moe_dispatch",
    )(last_tile, padded, n_used, d1.reshape(t // tmc, 1, tmc), d2.reshape(t // tmc, 1, tmc), hp)


def _experts_kernel(te_ref, nu_ref, xs_ref, wg_ref, wu_ref, wd_ref, y_ref):
    used = pl.program_id(0) < nu_ref[0]

    @pl.when(used)
    def _():
        x = _unpack_bf16_pairs(xs_ref[...]).astype(BF16)
        gate = _dot(x, wg_ref[0, 0].astype(BF16))
        hid = gate * _sigmoid(gate) * _dot(x, wu_ref[0, 0].astype(BF16))
        y_ref[...] = _pack_bf16_pairs(_dot(hid.astype(BF16), wd_ref[0, 0].astype(BF16)))

    @pl.when(jnp.logical_not(used))
    def _():
        y_ref[...] = jnp.zeros(y_ref.shape, jnp.uint32)


def _experts(xs, tile_exp, n_used, layer, wg, wu, wd, tmg):
    n_rows, dw = xs.shape
    d = 2 * dw
    tile = lambda i, te, nu: jnp.minimum(i, nu[0] - 1)
    rows = pl.BlockSpec((tmg, dw), lambda i, te, nu: (tile(i, te, nu), 0))
    wspec = lambda shape: pl.BlockSpec((1, 1) + shape,
                                       lambda i, te, nu: (layer, te[tile(i, te, nu)], 0, 0))
    return pl.pallas_call(
        _experts_kernel,
        grid_spec=pltpu.PrefetchScalarGridSpec(
            num_scalar_prefetch=2, grid=(n_rows // tmg,),
            in_specs=[rows, wspec((d, D_EXPERT)), wspec((d, D_EXPERT)), wspec((D_EXPERT, d))],
            out_specs=pl.BlockSpec((tmg, dw), lambda i, te, nu: (i, 0))),
        out_shape=jax.ShapeDtypeStruct((n_rows, dw), jnp.uint32),
        compiler_params=_cparams(("arbitrary",)),
        name="moe_experts",
    )(tile_exp, n_used, xs, wg, wu, wd)


def _combine_kernel(d1_ref, d2_ref, x_ref, meta_ref, gt_ref, y_ref, o_ref, b1_ref, b2_ref, sem, *, tmc):
    def pull(r, carry):
        pltpu.make_async_copy(y_ref.at[pl.ds(d1_ref[0, 0, r], 1), :], b1_ref.at[pl.ds(r, 1), :], sem).start()
        pltpu.make_async_copy(y_ref.at[pl.ds(d2_ref[0, 0, r], 1), :], b2_ref.at[pl.ds(r, 1), :], sem).start()
        return carry

    lax.fori_loop(0, tmc, pull, 0, unroll=8)
    one_row = pltpu.make_async_copy(y_ref.at[pl.ds(0, 1), :], b1_ref.at[pl.ds(0, 1), :], sem)
    for _ in range(2 * tmc):
        one_row.wait()
    w1, w2 = _router_weights(meta_ref[0])
    y = w1 * _unpack_bf16_pairs(b1_ref[...]) + w2 * _unpack_bf16_pairs(b2_ref[...])
    o_ref[0] = x_ref[0] + gt_ref[0] * y


def _combine(x, meta, gt, y, d1, d2, tmc):
    bsz, s, d = x.shape
    nt = s // tmc
    t = bsz * s
    smem_rows = pl.BlockSpec((1, 1, tmc), lambda b, i: (b * nt + i, 0, 0), memory_space=pltpu.SMEM)
    tok = lambda w: pl.BlockSpec((1, tmc, w), lambda b, i: (b, i, 0))
    buf = pltpu.VMEM((tmc, d // 2), jnp.uint32)
    return pl.pallas_call(
        functools.partial(_combine_kernel, tmc=tmc),
        grid=(bsz, nt),
        in_specs=[smem_rows, smem_rows, tok(d), tok(LANES),
                  pl.BlockSpec((1, 1, d), lambda b, i: (b, 0, 0)),
                  pl.BlockSpec(memory_space=pl.ANY)],
        out_specs=tok(d),
        out_shape=jax.ShapeDtypeStruct((bsz, s, d), F32),
        scratch_shapes=[buf, buf, pltpu.SemaphoreType.DMA(())],
        compiler_params=_cparams(("arbitrary", "arbitrary")),
        name="moe_combine",
    )(d1.reshape(t // tmc, 1, tmc), d2.reshape(t // tmc, 1, tmc), x, meta, gt, y)


def _layer_weights(l, w_in, mla_q_norm_g, mla_w_uq, mla_kv_norm_g, mla_w_ukv, mla_q_qk_g,
                   mla_k_qk_g, lru_w_a, lru_w_x, mix_norm_g, w_out, router_group_w,
                   router_group_b, router_expert_w, router_expert_b):
    half = RET_DK // 2
    perm = np.concatenate([np.arange(half) + HEAD_DIM * h for h in range(N_HEADS)]
                          + [np.arange(half) + half + HEAD_DIM * h for h in range(N_HEADS)])
    w = w_in[l]
    o_mla, o_ret, o_lru = U_CONV, U_CONV + 352, U_CONV + 352 + U_RET
    w_ret = w[:, o_ret:o_ret + U_RET]
    w_ret = jnp.concatenate([w_ret[:, perm], w_ret[:, GROUP_W + perm], w_ret[:, 2 * GROUP_W:]], axis=1)
    w_all = jnp.concatenate([w_ret, w[:, o_lru:o_lru + U_LRU], w[:, :U_CONV], w[:, o_mla:o_mla + 352],
                             jnp.zeros((D_MODEL, U_MLA - 352), F32)], axis=1).astype(BF16)

    hw = N_HEADS * LANES
    r16 = MLA_ROPE // 2
    wq = mla_w_uq[l].reshape(Q_LORA, N_HEADS, MLA_QK)
    zq = jnp.zeros((Q_LORA, N_HEADS, LANES - MLA_QK), F32)
    q_cols = jnp.concatenate([wq, zq], axis=2).reshape(Q_LORA, hw)
    wq_sw = jnp.concatenate([jnp.zeros((Q_LORA, N_HEADS, MLA_NOPE), F32), wq[:, :, MLA_NOPE + r16:],
                             wq[:, :, MLA_NOPE:MLA_NOPE + r16], zq], axis=2).reshape(Q_LORA, hw)
    wkv = mla_w_ukv[l].reshape(KV_LORA, N_HEADS, MLA_NOPE + HEAD_DIM)
    zk = jnp.zeros((KV_LORA, N_HEADS, LANES - MLA_NOPE), F32)
    k_cols = jnp.concatenate([wkv[:, :, :MLA_NOPE], zk], axis=2).reshape(KV_LORA, hw)
    v_cols = jnp.concatenate([wkv[:, :, MLA_NOPE:], zk], axis=2).reshape(KV_LORA, hw)
    eye = jnp.eye(MLA_ROPE, dtype=F32)
    place = jnp.concatenate([jnp.zeros((MLA_ROPE, MLA_NOPE), F32), eye,
                             jnp.zeros((MLA_ROPE, LANES - MLA_QK), F32)], axis=1)
    eye_sw = jnp.concatenate([eye[:, r16:], eye[:, :r16]], axis=1)
    place_sw = jnp.concatenate([jnp.zeros((MLA_ROPE, MLA_NOPE), F32), eye_sw,
                                jnp.zeros((MLA_ROPE, LANES - MLA_QK), F32)], axis=1)
    n_big = 4 * hw + 2 * LANES
    wbig = jnp.zeros((U_MLA, n_big), F32)
    wbig = wbig.at[:Q_LORA, :hw].set(q_cols).at[:Q_LORA, hw:2 * hw].set(wq_sw)
    wbig = wbig.at[Q_LORA:Q_LORA + KV_LORA, 2 * hw:3 * hw].set(k_cols)
    wbig = wbig.at[Q_LORA:Q_LORA + KV_LORA, 3 * hw:4 * hw].set(v_cols)
    wbig = wbig.at[Q_LORA + KV_LORA:352, 4 * hw:4 * hw + LANES].set(place)
    wbig = wbig.at[Q_LORA + KV_LORA:352, 4 * hw + LANES:].set(place_sw)
    wbig = wbig.astype(BF16)

    gu = jnp.concatenate([mla_q_norm_g[l], mla_kv_norm_g[l], mla_k_qk_g[l][MLA_NOPE:],
                          jnp.zeros((U_MLA - 352,), F32)])[None, :]
    qscale = (MLA_QK ** -0.5) * math.log2(math.e)
    gq_full = mla_q_qk_g[l]
    pad = jnp.zeros((LANES - MLA_QK,), F32)
    gq = (jnp.concatenate([gq_full, pad]) * qscale)[None, :]
    gqs = (jnp.concatenate([jnp.zeros((MLA_NOPE,), F32), gq_full[MLA_NOPE + r16:],
                            gq_full[MLA_NOPE:MLA_NOPE + r16], pad]) * qscale)[None, :]
    gk = jnp.concatenate([mla_k_qk_g[l][:MLA_NOPE], jnp.zeros((LANES - MLA_NOPE,), F32)])[None, :]

    def blockdiag(wb):
        out = jnp.zeros((GROUP_W, GROUP_W), F32)
        for n in range(wb.shape[0]):
            out = out.at[n * HEAD_DIM:(n + 1) * HEAD_DIM, n * HEAD_DIM:(n + 1) * HEAD_DIM].set(wb[n])
        return out.astype(BF16)

    gmix = mix_norm_g[l]
    wo = w_out[l].astype(BF16)
    wm = wo[GROUP_W:2 * GROUP_W].reshape(N_HEADS, HEAD_DIM, D_MODEL)
    wm = jnp.concatenate([wm, jnp.zeros((N_HEADS, LANES - HEAD_DIM, D_MODEL), BF16)], axis=1)
    wm = wm.reshape(N_HEADS // 2, 2 * LANES, D_MODEL)
    gmla = jnp.concatenate([gmix[GROUP_W:2 * GROUP_W].reshape(N_HEADS, 1, HEAD_DIM),
                            jnp.zeros((N_HEADS, 1, LANES - HEAD_DIM), F32)], axis=2)
    wrt = jnp.concatenate([router_expert_w[l], router_group_w[l],
                           jnp.zeros((D_MODEL, LANES - N_EXPERTS - MOE_GROUPS), F32)], axis=1)
    wrt_hi = wrt.astype(BF16)
    wrt = jnp.concatenate([wrt_hi, (wrt - wrt_hi.astype(F32)).astype(BF16)], axis=1)
    brt = jnp.concatenate([router_expert_b[l], router_group_b[l],
                           jnp.zeros((LANES - N_EXPERTS - MOE_GROUPS,), F32)])[None, :]
    return dict(w_all=w_all, wbig=wbig, gu=gu, gq=gq, gqs=gqs, gk=gk,
                wa=blockdiag(lru_w_a[l]), wx=blockdiag(lru_w_x[l]),
                g_conv=gmix[None, :GROUP_W], g_ret=gmix[None, 2 * GROUP_W:3 * GROUP_W],
                g_lru=gmix[None, 3 * GROUP_W:], gmla=gmla,
                wc=wo[:GROUP_W], wm=wm, wr=wo[2 * GROUP_W:3 * GROUP_W], wl=wo[3 * GROUP_W:],
                wrt=wrt, brt=brt)


def _mla_consts():
    seg_u = np.concatenate([np.zeros(Q_LORA), np.ones(KV_LORA), 2 * np.ones(MLA_ROPE),
                            3 * np.ones(U_MLA - 352)])
    mu = jnp.asarray(seg_u[:, None] == seg_u[None, :], BF16)
    invu = jnp.asarray(np.concatenate([np.full(Q_LORA, 1.0 / Q_LORA), np.full(KV_LORA, 1.0 / KV_LORA),
                                       np.full(MLA_ROPE, 1.0 / MLA_ROPE), np.ones(U_MLA - 352)]), F32)[None, :]
    lane = np.arange(N_HEADS * LANES)
    seg_q = (lane // LANES) * 3 + np.where(lane % LANES < MLA_NOPE, 0, np.where(lane % LANES < MLA_QK, 1, 2))
    sq = jnp.asarray(seg_q[:, None] == seg_q[None, :], BF16)
    inv_head = np.concatenate([np.full(MLA_NOPE, 1.0 / MLA_NOPE), np.full(MLA_ROPE, 1.0 / MLA_ROPE),
                               np.ones(LANES - MLA_QK)])
    invq = jnp.asarray(np.tile(inv_head, N_HEADS), F32)[None, :]
    onev = jnp.asarray((np.arange(LANES) == HEAD_DIM).astype(np.float32))[None, :]
    return mu, invu, sq, invq, onev


def kernel(x, c, positions, ada_w, ada_b, norm_mix_g, w_in, conv_w, mla_q_norm_g, mla_w_uq, mla_kv_norm_g, mla_w_ukv, mla_q_qk_g, mla_k_qk_g, lru_conv_w, lru_conv_b, lru_w_a, lru_b_a, lru_w_x, lru_b_x, lru_lambda, mix_norm_g, w_out, norm_ffn_g, router_group_w, router_group_b, router_expert_w, router_expert_b, exp_w_gate, exp_w_up, exp_w_down):
    bsz, s, d = x.shape
    depth = ada_w.shape[0]
    tm = min(512, s)
    chunk = min(256, s)
    tq = min(1024, s)
    tmc = min(512, s)
    tmg = 512

    inv = jnp.concatenate([1.0 / (ROPE_BASE ** (jnp.arange(0, MLA_ROPE, 2, dtype=F32) / MLA_ROPE)),
                           1.0 / (ROPE_BASE ** (jnp.arange(0, RET_DK, 2, dtype=F32) / RET_DK))])[:, None]
    trig = _rope_tables(positions, inv)
    ex_ret, ex_mla = _trig_expanders()

    c_pad = jnp.concatenate([c, jnp.zeros((8 - bsz, d), F32)], axis=0)
    mod = _modulation(c_pad, ada_w, ada_b)[:, :bsz]
    ret_consts = _ret_consts(chunk)
    mu, invu, sq, invq, onev = _mla_consts()

    pending = None
    for l in range(depth):
        sh_m, sc_m, gt_m, sh_f, sc_f, gt_f = [m[:, None, :] for m in jnp.split(mod[l], 6, axis=-1)]
        lw = _layer_weights(l, w_in, mla_q_norm_g, mla_w_uq, mla_kv_norm_g, mla_w_ukv, mla_q_qk_g,
                            mla_k_qk_g, lru_w_a, lru_w_x, mix_norm_g, w_out, router_group_w,
                            router_group_b, router_expert_w, router_expert_b)
        if pending is None:
            u = _inproj(x, norm_mix_g[l][None, :], sc_m, sh_m, lw["w_all"], tm)
        else:
            x, u = _inproj_combine(x, *pending, norm_mix_g[l][None, :], sc_m, sh_m, lw["w_all"], tm)
        y_conv = _conv_mixer(u, conv_w[l], lw["g_conv"], tm)
        y_lru = _lru_mixer(u, lru_conv_w[l], lru_conv_b[l][None, :], lw["wa"], lru_b_a[l][None, :],
                           lw["wx"], lru_b_x[l][None, :], lru_lambda[l][None, :], lw["g_lru"], tm)
        y_ret = _ret_mixer(u, trig, ex_ret, ret_consts, lw["g_ret"], chunk)
        q, k, v = _mla_prep(u, trig, ex_mla, mu, invu, lw["gu"], lw["wbig"], sq, invq,
                            lw["gq"], lw["gqs"], lw["gk"], onev, tm)
        o_mla = _flash_attention(q, k, v, tq)
        x, hp, meta, cnt = _outproj(x, y_conv, o_mla, y_ret, y_lru, lw["wc"], lw["wm"], lw["wr"],
                                    lw["wl"], lw["gmla"], gt_m, norm_ffn_g[l][None, :], sc_f, sh_f,
                                    lw["wrt"], lw["brt"], tm)
        d1, d2, tile_exp, n_used, last_tile, padded = _route_plan(meta, cnt, tmg)
        xs = _dispatch(hp.reshape(bsz * s, d // 2), d1, d2, last_tile, padded, n_used, tmc, tmg)
        ys = _experts(xs, tile_exp, n_used, l, exp_w_gate, exp_w_up, exp_w_down, tmg)
        pending = (meta, gt_f, ys, d1, d2)
    meta, gt_f, ys, d1, d2 = pending
    return _combine(x, meta, gt_f, ys, d1, d2, tmc)
```

```python
import functools
import math

import jax
import jax.numpy as jnp
import numpy as np
from jax import lax
from jax.experimental import pallas as pl
from jax.experimental.pallas import tpu as pltpu

F32 = jnp.float32
BF16 = jnp.bfloat16
HIGHEST = lax.Precision.HIGHEST

D_MODEL = 1024
GROUP_W = 256
HEAD_DIM = 64
N_HEADS = 4
MLA_NOPE = 64
MLA_ROPE = 32
MLA_QK = 96
Q_LORA = 192
KV_LORA = 128
RET_DK = 64
LRU_C = 8.0
MOE_GROUPS = 4
EXPERTS_PER_GROUP = 8
N_EXPERTS = 32
D_EXPERT = 256
ROPE_BASE = 10000.0
EPS = 1e-6

LANES = 128
SUBLANES = 8
MXU_DIM = 256
U_RET, U_LRU, U_CONV, U_MLA = 1024, 512, 768, 384
U_COLS = U_RET + U_LRU + U_CONV + U_MLA
N_FREQ = MLA_ROPE // 2 + RET_DK // 2
NEG_BIG = -1e30
VMEM_LIMIT = 56 * 1024 * 1024


def _cparams(sem):
    return pltpu.CompilerParams(dimension_semantics=sem, vmem_limit_bytes=VMEM_LIMIT)


def _dot(a, b):
    return jnp.dot(a, b, preferred_element_type=F32)


def _dot_nt(a, b):
    return lax.dot_general(a, b, (((1,), (1,)), ((), ())), preferred_element_type=F32)


def _dot_tn(a, b):
    return lax.dot_general(a, b, (((0,), (0,)), ((), ())), preferred_element_type=F32)


def _rms_rows(y, g):
    return y * lax.rsqrt(jnp.mean(y * y, axis=-1, keepdims=True) + EPS) * g


def _sigmoid(x):
    return 0.5 * jnp.tanh(0.5 * x) + 0.5


def _pack_bf16_pairs(a):
    k = a.shape[1] // 2
    rounded = a.astype(BF16).astype(F32)
    lo = lax.bitcast_convert_type(rounded[:, :k], jnp.uint32) >> 16
    hi = lax.bitcast_convert_type(rounded[:, k:], jnp.uint32) & jnp.uint32(0xFFFF0000)
    return lo | hi


def _unpack_bf16_pairs(w):
    lo = lax.bitcast_convert_type(w << 16, F32)
    hi = lax.bitcast_convert_type(w & jnp.uint32(0xFFFF0000), F32)
    return jnp.concatenate([lo, hi], axis=1)


def _rope_kernel(pos_ref, inv_ref, tab_ref):
    ang = pos_ref[0].astype(F32) * inv_ref[...]
    row = lax.broadcasted_iota(jnp.int32, (LANES - 2 * N_FREQ, ang.shape[1]), 0)
    pad = jnp.where(row == 0, 1.0, 0.0)
    tab_ref[0] = jnp.concatenate([jnp.cos(ang), jnp.sin(ang), pad], axis=0).T


def _rope_tables(positions, inv):
    bsz, s = positions.shape
    ts = min(s, 2048)
    return pl.pallas_call(
        _rope_kernel,
        grid=(bsz, s // ts),
        in_specs=[pl.BlockSpec((1, 1, ts), lambda b, i: (b, 0, i)),
                  pl.BlockSpec((N_FREQ, 1), lambda b, i: (0, 0))],
        out_specs=pl.BlockSpec((1, ts, LANES), lambda b, i: (b, i, 0)),
        out_shape=jax.ShapeDtypeStruct((bsz, s, LANES), F32),
        compiler_params=_cparams(("parallel", "parallel")),
        name="rope_tables",
    )(positions.reshape(bsz, 1, s), inv)


def _expand_trig(tab, expand):
    hi = tab.astype(BF16)
    lo = (tab - hi.astype(F32)).astype(BF16)
    trig = _dot(hi, expand) + _dot(lo, expand)
    return trig[:, :LANES], trig[:, LANES:]


def _mod_kernel(c_ref, w_ref, b_ref, o_ref):
    c = c_ref[...]
    ca = c * _sigmoid(c)
    o_ref[0] = jnp.dot(ca, w_ref[0], precision=HIGHEST, preferred_element_type=F32) + b_ref[0]


def _modulation(c_pad, ada_w, ada_b):
    nl, d, n = ada_w.shape
    tn = 1536
    return pl.pallas_call(
        _mod_kernel,
        grid=(nl, n // tn),
        in_specs=[pl.BlockSpec((8, d), lambda l, j: (0, 0)),
                  pl.BlockSpec((1, d, tn), lambda l, j: (l, 0, j)),
                  pl.BlockSpec((1, 1, tn), lambda l, j: (l, 0, j))],
        out_specs=pl.BlockSpec((1, 8, tn), lambda l, j: (l, 0, j)),
        out_shape=jax.ShapeDtypeStruct((nl, 8, n), F32),
        compiler_params=_cparams(("parallel", "parallel")),
        name="adaln_mod",
    )(c_pad, ada_w, ada_b.reshape(nl, 1, n))


def _norm_project(x, g_ref, sc_ref, sh_ref, w_ref):
    h = _rms_rows(x, g_ref[...]) * (1.0 + sc_ref[0]) + sh_ref[0]
    return _dot(h.astype(BF16), w_ref[...]).astype(BF16)


def _inproj_kernel(x_ref, g_ref, sc_ref, sh_ref, w_ref, u_ref):
    u_ref[0] = _norm_project(x_ref[0], g_ref, sc_ref, sh_ref, w_ref)


def _router_weights(meta):
    lane = lax.broadcasted_iota(jnp.int32, meta.shape, 1)
    w1 = jnp.sum(jnp.where(lane == 4, meta, 0.0), axis=-1, keepdims=True)
    w2 = jnp.sum(jnp.where(lane == 5, meta, 0.0), axis=-1, keepdims=True)
    return w1, w2


def _inproj_combine_kernel(d1c_ref, d2c_ref, d1n_ref, d2n_ref, x_ref, meta_ref, gtf_ref, y_hbm,
                           g_ref, sc_ref, sh_ref, w_ref, xo_ref, u_ref,
                           a1_ref, a2_ref, b1_ref, b2_ref, sem_a, sem_b, *, tm):
    n = pl.program_id(0)
    bufs = ((a1_ref, a2_ref, sem_a), (b1_ref, b2_ref, sem_b))

    def pull(d_ref, buf, sem, r):
        return pltpu.make_async_copy(y_hbm.at[pl.ds(d_ref[0, 0, r], 1), :], buf.at[pl.ds(r, 1), :], sem)

    def wait_tile(buf, sem):
        one_row = pltpu.make_async_copy(y_hbm.at[pl.ds(0, 1), :], buf.at[pl.ds(0, 1), :], sem)
        for _ in range(2 * tm):
            one_row.wait()

    @pl.when(n == 0)
    def _():
        def first(r, carry):
            pull(d1c_ref, a1_ref, sem_a, r).start(priority=0)
            pull(d2c_ref, a2_ref, sem_a, r).start(priority=1)
            return carry
        lax.fori_loop(0, tm, first, 0, unroll=8)

    def step(par):
        y1_ref, y2_ref, sem = bufs[par]
        n1_ref, n2_ref, nsem = bufs[1 - par]
        wait_tile(y1_ref, sem)
        for r in range(tm):
            pull(d1n_ref, n1_ref, nsem, r).start(priority=0)
            pull(d2n_ref, n2_ref, nsem, r).start(priority=1)
        w1, w2 = _router_weights(meta_ref[0])
        y = w1 * _unpack_bf16_pairs(y1_ref[...]) + w2 * _unpack_bf16_pairs(y2_ref[...])
        x = x_ref[0] + gtf_ref[0] * y
        xo_ref[0] = x
        u_ref[0] = _norm_project(x, g_ref, sc_ref, sh_ref, w_ref)

        @pl.when(n == pl.num_programs(0) - 1)
        def _():
            wait_tile(n1_ref, nsem)

    for par in range(2):
        pl.when(n % 2 == par)(functools.partial(step, par))


def _inproj_combine(x, meta, gtf, ys, d1, d2, g, sc, sh, w, tm):
    bsz, s, d = x.shape
    nt = s // tm
    n_steps = bsz * nt
    nxt = lambda n: jnp.minimum(n + 1, n_steps - 1)
    smem = lambda f: pl.BlockSpec((1, 1, tm), lambda n: (f(n), 0, 0), memory_space=pltpu.SMEM)
    tok = lambda wdt: pl.BlockSpec((1, tm, wdt), lambda n: (n // nt, n % nt, 0))
    vec = pl.BlockSpec((1, 1, d), lambda n: (n // nt, 0, 0))
    buf = pltpu.VMEM((tm, d // 2), jnp.uint32)
    dd1, dd2 = d1.reshape(n_steps, 1, tm), d2.reshape(n_steps, 1, tm)
    return pl.pallas_call(
        functools.partial(_inproj_combine_kernel, tm=tm),
        grid=(n_steps,),
        in_specs=[smem(lambda n: n), smem(lambda n: n), smem(nxt), smem(nxt),
                  tok(d), tok(LANES), vec, pl.BlockSpec(memory_space=pl.ANY),
                  pl.BlockSpec((1, d), lambda n: (0, 0)), vec, vec,
                  pl.BlockSpec((d, U_COLS), lambda n: (0, 0))],
        out_specs=[tok(d), tok(U_COLS)],
        out_shape=[jax.ShapeDtypeStruct((bsz, s, d), F32), jax.ShapeDtypeStruct((bsz, s, U_COLS), BF16)],
        scratch_shapes=[buf, buf, buf, buf, pltpu.SemaphoreType.DMA(()), pltpu.SemaphoreType.DMA(())],
        compiler_params=_cparams(("arbitrary",)),
        name="inproj_combine",
    )(dd1, dd2, dd1, dd2, x, meta, gtf, ys, g, sc, sh, w)


def _inproj(x, g, sc, sh, w, tm):
    bsz, s, d = x.shape
    vec = pl.BlockSpec((1, 1, d), lambda b, i: (b, 0, 0))
    return pl.pallas_call(
        _inproj_kernel,
        grid=(bsz, s // tm),
        in_specs=[pl.BlockSpec((1, tm, d), lambda b, i: (b, i, 0)),
                  pl.BlockSpec((1, d), lambda b, i: (0, 0)),
                  vec, vec,
                  pl.BlockSpec((d, U_COLS), lambda b, i: (0, 0))],
        out_specs=pl.BlockSpec((1, tm, U_COLS), lambda b, i: (b, i, 0)),
        out_shape=jax.ShapeDtypeStruct((bsz, s, U_COLS), BF16),
        compiler_params=_cparams(("parallel", "parallel")),
        name="inproj",
    )(x, g, sc, sh, w)


def _conv_kernel(u_ref, w_ref, g_ref, y_ref, buf_ref, *, tm):
    @pl.when(pl.program_id(1) == 0)
    def _():
        buf_ref[0:8, :] = jnp.zeros((8, GROUP_W), F32)

    u = u_ref[0].astype(F32)
    b_gate, c_gate, xin = u[:, :GROUP_W], u[:, GROUP_W:2 * GROUP_W], u[:, 2 * GROUP_W:]
    cx = c_gate * xin
    buf_ref[8:8 + tm, :] = cx
    conv = (w_ref[2:3, :] * cx + w_ref[1:2, :] * buf_ref[7:7 + tm, :]
            + w_ref[0:1, :] * buf_ref[6:6 + tm, :])
    buf_ref[0:8, :] = cx[tm - 8:, :]
    y_ref[0] = _rms_rows(b_gate * conv, g_ref[...]).astype(BF16)


def _conv_mixer(u, w, g, tm):
    bsz, s, _ = u.shape
    return pl.pallas_call(
        functools.partial(_conv_kernel, tm=tm),
        grid=(bsz, s // tm),
        in_specs=[pl.BlockSpec((1, tm, U_CONV), lambda b, i: (b, i, (U_RET + U_LRU) // U_CONV)),
                  pl.BlockSpec((3, GROUP_W), lambda b, i: (0, 0)),
                  pl.BlockSpec((1, GROUP_W), lambda b, i: (0, 0))],
        out_specs=pl.BlockSpec((1, tm, GROUP_W), lambda b, i: (b, i, 0)),
        out_shape=jax.ShapeDtypeStruct((bsz, s, GROUP_W), BF16),
        scratch_shapes=[pltpu.VMEM((tm + 8, GROUP_W), F32)],
        compiler_params=_cparams(("parallel", "arbitrary")),
        name="conv_mixer",
    )(u, w, g)


def _lru_kernel(u_ref, cw_ref, cb_ref, wa_ref, ba_ref, wx_ref, bx_ref, lam_ref, g_ref,
                y_ref, buf_ref, h_ref, *, tm):
    @pl.when(pl.program_id(1) == 0)
    def _():
        buf_ref[0:8, :] = jnp.zeros((8, GROUP_W), F32)
        h_ref[...] = jnp.zeros((1, GROUP_W), F32)

    u = u_ref[0].astype(F32)
    xraw, gate = u[:, :GROUP_W], u[:, GROUP_W:]
    buf_ref[8:8 + tm, :] = xraw
    xb = (cw_ref[3:4, :] * xraw + cw_ref[2:3, :] * buf_ref[7:7 + tm, :]
          + cw_ref[1:2, :] * buf_ref[6:6 + tm, :] + cw_ref[0:1, :] * buf_ref[5:5 + tm, :]
          + cb_ref[...])
    buf_ref[0:8, :] = xraw[tm - 8:, :]

    xbb = xb.astype(BF16)
    r = _sigmoid(_dot(xbb, wa_ref[...]) + ba_ref[...])
    i = _sigmoid(_dot(xbb, wx_ref[...]) + bx_ref[...])
    nlam = -lam_ref[...]
    softplus = jnp.maximum(nlam, 0.0) + jnp.log(1.0 + jnp.exp(-jnp.abs(nlam)))
    log_a = (-LRU_C) * r * softplus
    a = jnp.exp(log_a)
    b = jnp.sqrt(1.0 - a * a) * (i * xb)

    n_groups = tm // SUBLANES
    a = a.reshape(n_groups, SUBLANES, GROUP_W)
    b = b.reshape(n_groups, SUBLANES, GROUP_W)
    sub = lax.broadcasted_iota(jnp.int32, a.shape, 1)
    d = 1
    while d < SUBLANES:
        keep = sub >= d
        a_sh = jnp.where(keep, pltpu.roll(a, d, 1), 1.0)
        b_sh = jnp.where(keep, pltpu.roll(b, d, 1), 0.0)
        b = a * b_sh + b
        a = a * a_sh
        d *= 2
    carry = h_ref[...]
    groups = []
    for g in range(n_groups):
        hg = a[g] * carry + b[g]
        carry = hg[SUBLANES - 1:, :]
        groups.append(hg)
    h = jnp.concatenate(groups, axis=0)
    h_ref[...] = carry

    gelu = 0.5 * gate * (1.0 + jnp.tanh(math.sqrt(2.0 / math.pi) * (gate + 0.044715 * gate * gate * gate)))
    y_ref[0] = _rms_rows(h * gelu, g_ref[...]).astype(BF16)


def _lru_mixer(u, cw, cb, wa, ba, wx, bx, lam, g, tm):
    bsz, s, _ = u.shape
    row = pl.BlockSpec((1, GROUP_W), lambda b, i: (0, 0))
    mat = pl.BlockSpec((GROUP_W, GROUP_W), lambda b, i: (0, 0))
    return pl.pallas_call(
        functools.partial(_lru_kernel, tm=tm),
        grid=(bsz, s // tm),
        in_specs=[pl.BlockSpec((1, tm, U_LRU), lambda b, i: (b, i, U_RET // U_LRU)),
                  pl.BlockSpec((4, GROUP_W), lambda b, i: (0, 0)),
                  row, mat, row, mat, row, row, row],
        out_specs=pl.BlockSpec((1, tm, GROUP_W), lambda b, i: (b, i, 0)),
        out_shape=jax.ShapeDtypeStruct((bsz, s, GROUP_W), BF16),
        scratch_shapes=[pltpu.VMEM((tm + 8, GROUP_W), F32), pltpu.VMEM((1, GROUP_W), F32)],
        compiler_params=_cparams(("parallel", "arbitrary")),
        name="lru_mixer",
    )(u, cw, cb, wa, ba, wx, bx, lam, g)


def _ret_kernel(u_ref, tab_ref, ex_ref, inner_ref, qd_ref, kd_ref, cd_ref, bm_ref, gm_ref,
                mq_ref, mv_ref, g_ref, y_ref, st_ref):
    @pl.when(pl.program_id(0) == 0)
    def _():
        st_ref[...] = jnp.zeros(st_ref.shape, F32)

    for b in range(u_ref.shape[0]):
        u = u_ref[b].astype(F32)
        q, k = u[:, :GROUP_W], u[:, GROUP_W:2 * GROUP_W]
        v, gate = u[:, 2 * GROUP_W:3 * GROUP_W], u[:, 3 * GROUP_W:]
        cos, sin = _expand_trig(tab_ref[b], ex_ref[...])

        def rope(t):
            t1, t2 = t[:, :LANES], t[:, LANES:]
            return jnp.concatenate([t1 * cos - t2 * sin, t2 * cos + t1 * sin], axis=-1)

        qr = rope(q)
        kr = rope(k) * (RET_DK ** -0.5)
        krb = kr.astype(BF16)
        vb = v.astype(BF16)
        state = st_ref[b]
        o = _dot(qr.astype(BF16), state.astype(BF16)) * qd_ref[...]
        for h in range(N_HEADS):
            qh = (qr * mq_ref[h]).astype(BF16)
            sc = _dot_nt(qh, krb) * inner_ref[h]
            o = o + _dot(sc.astype(BF16), vb) * mv_ref[h]
        st_ref[b] = state * cd_ref[...] + bm_ref[...] * _dot_tn((kr * kd_ref[...]).astype(BF16), vb)

        gm = gm_ref[...]
        o_hi = o.astype(BF16)
        o_lo = (o - o_hi.astype(F32)).astype(BF16)
        mu = _dot(o_hi, gm) + _dot(o_lo, gm)
        dlt = o - mu
        var = _dot((dlt * dlt).astype(BF16), gm)
        y = dlt * lax.rsqrt(var + EPS)
        y = gate * _sigmoid(gate) * y
        y_ref[b] = _rms_rows(y, g_ref[...]).astype(BF16)


def _ret_mixer(u, trig, expand, consts, g, chunk):
    bsz, s, _ = u.shape
    inner, qd, kd, cd, bm, gm, mq, mv = consts
    full = lambda shape: pl.BlockSpec(shape, lambda i: (0,) * len(shape))
    tok = lambda w: pl.BlockSpec((bsz, chunk, w), lambda i: (0, i, 0))
    return pl.pallas_call(
        _ret_kernel,
        grid=(s // chunk,),
        in_specs=[tok(U_RET), tok(LANES), full(expand.shape),
                  full(inner.shape), full(qd.shape), full(kd.shape), full(cd.shape),
                  full(bm.shape), full(gm.shape), full(mq.shape), full(mv.shape),
                  full((1, GROUP_W))],
        out_specs=tok(GROUP_W),
        out_shape=jax.ShapeDtypeStruct((bsz, s, GROUP_W), BF16),
        scratch_shapes=[pltpu.VMEM((bsz, GROUP_W, GROUP_W), F32)],
        compiler_params=_cparams(("arbitrary",)),
        name="ret_mixer",
    )(u, trig, expand, inner, qd, kd, cd, bm, gm, mq, mv, g)


def _ret_consts(chunk):
    nh = N_HEADS
    f32 = np.float32
    log_g = np.log(f32(1.0) - f32(2.0) ** (f32(-5.0) - np.arange(nh, dtype=f32)))
    idx = np.arange(chunk, dtype=f32)
    rel = idx[:, None] - idx[None, :]
    inner = np.where(rel >= 0, np.exp(log_g[:, None, None] * np.maximum(rel, 0.0)), 0.0).astype(f32)
    v_head = np.arange(GROUP_W) // HEAD_DIM
    q_head = (np.arange(GROUP_W) % LANES) // (RET_DK // 2)
    qd = np.exp(log_g[v_head][None, :] * (idx[:, None] + 1.0)).astype(f32)
    kd = np.exp(log_g[q_head][None, :] * (chunk - 1.0 - idx[:, None])).astype(f32)
    cd = np.exp(log_g[v_head] * chunk)[None, :].astype(f32)
    bm = (q_head[:, None] == v_head[None, :]).astype(f32)
    gm = jnp.asarray((v_head[:, None] == v_head[None, :]).astype(f32) / HEAD_DIM, BF16)
    mq = (q_head[None, :] == np.arange(nh)[:, None]).astype(f32)[:, None, :]
    mv = (v_head[None, :] == np.arange(nh)[:, None]).astype(f32)[:, None, :]
    return tuple(jnp.asarray(a) for a in (inner, qd, kd, cd, bm)) + (gm, jnp.asarray(mq), jnp.asarray(mv))


def _trig_expanders():
    r16, r32 = MLA_ROPE // 2, RET_DK // 2
    ret = np.zeros((LANES, 2 * LANES), np.float32)
    mla = np.zeros((LANES, 2 * LANES), np.float32)
    for j in range(r32):
        for h in range(N_HEADS):
            ret[r16 + j, h * r32 + j] = 1.0
            ret[N_FREQ + r16 + j, LANES + h * r32 + j] = 1.0
    for j in range(r16):
        for half, sign in ((0, -1.0), (1, 1.0)):
            lane = MLA_NOPE + half * r16 + j
            mla[j, lane] = 1.0
            mla[N_FREQ + j, LANES + lane] = sign
    mla[2 * N_FREQ, :MLA_NOPE] = 1.0
    return jnp.asarray(ret, BF16), jnp.asarray(mla, BF16)


def _mla_prep_kernel(u_ref, tab_ref, ex_ref, mu_ref, invu_ref, gu_ref, wbig_ref, sq_ref,
                     invq_ref, gq_ref, gqs_ref, gk_ref, onev_ref, q_ref, k_ref, v_ref):
    x = u_ref[0].astype(F32)
    ss = _dot((x * x).astype(BF16), mu_ref[...]) * invu_ref[...]
    xn = (x * lax.rsqrt(ss + EPS) * gu_ref[...]).astype(BF16)
    big = _dot(xn, wbig_ref[...])
    hw = N_HEADS * LANES
    q, qs, kn, v = big[:, :hw], big[:, hw:2 * hw], big[:, 2 * hw:3 * hw], big[:, 3 * hw:4 * hw]
    kr, krs = big[:, 4 * hw:4 * hw + LANES], big[:, 4 * hw + LANES:]
    cos, sin = _expand_trig(tab_ref[0], ex_ref[...])
    rq = lax.rsqrt(_dot((q * q).astype(BF16), sq_ref[...]) * invq_ref[...] + EPS)
    rk = lax.rsqrt(_dot((kn * kn).astype(BF16), sq_ref[...]) * invq_ref[...] + EPS)
    krot = kr * cos + krs * sin
    for h in range(N_HEADS):
        sl = slice(h * LANES, (h + 1) * LANES)
        qh = (q[:, sl] * gq_ref[...] * cos + qs[:, sl] * gqs_ref[...] * sin) * rq[:, sl]
        q_ref[0, h] = qh.astype(BF16)
        k_ref[0, h] = (kn[:, sl] * gk_ref[...] * rk[:, sl] + krot).astype(BF16)
        v_ref[0, h] = (v[:, sl] + onev_ref[...]).astype(BF16)


def _mla_prep(u, trig, expand, mu, invu, gu, wbig, sq, invq, gq, gqs, gk, onev, tm):
    bsz, s, _ = u.shape
    full = lambda a: pl.BlockSpec(a.shape, lambda b, i: (0,) * a.ndim)
    tab = pl.BlockSpec((1, tm, LANES), lambda b, i: (b, i, 0))
    out = jax.ShapeDtypeStruct((bsz, N_HEADS, s, LANES), BF16)
    ospec = pl.BlockSpec((1, N_HEADS, tm, LANES), lambda b, i: (b, 0, i, 0))
    return pl.pallas_call(
        _mla_prep_kernel,
        grid=(bsz, s // tm),
        in_specs=[pl.BlockSpec((1, tm, U_MLA), lambda b, i: (b, i, (U_COLS - U_MLA) // U_MLA)),
                  tab, full(expand), full(mu), full(invu), full(gu), full(wbig), full(sq), full(invq),
                  full(gq), full(gqs), full(gk), full(onev)],
        out_specs=[ospec, ospec, ospec],
        out_shape=[out, out, out],
        compiler_params=_cparams(("parallel", "parallel")),
        name="mla_prep",
    )(u, trig, expand, mu, invu, gu, wbig, sq, invq, gq, gqs, gk, onev)


def _flash_kernel(q_ref, k_ref, v_ref, o_ref, sa_ref, sb_ref, mca_ref, mcb_ref, m_ref, acc_ref, *, tq):
    qi = pl.program_id(2)
    q = q_ref[0, 0]
    bufs = ((sa_ref, mca_ref), (sb_ref, mcb_ref))
    m_ref[...] = jnp.full((tq, LANES), NEG_BIG, F32)
    acc_ref[...] = jnp.zeros((tq, LANES), F32)

    def scores(c, masked, dst):
        s_ref, mc_ref = dst
        start = pl.multiple_of(c * tq, tq)
        s = _dot_nt(q, k_ref[0, 0, pl.ds(start, tq), :])
        if masked:
            row = qi * tq + lax.broadcasted_iota(jnp.int32, (tq, tq), 0)
            col = start + lax.broadcasted_iota(jnp.int32, (tq, tq), 1)
            s = jnp.where(col <= row, s, NEG_BIG)
        s_ref[...] = s
        mc_ref[...] = jnp.broadcast_to(jnp.max(s, axis=-1, keepdims=True), (tq, LANES))

    def accumulate(c, src):
        s_ref, mc_ref = src
        start = pl.multiple_of(c * tq, tq)
        m_prev = m_ref[...]
        m_new = jnp.maximum(m_prev, mc_ref[...])
        alpha = jnp.exp2(m_prev - m_new)
        p = jnp.exp2(s_ref[...] - jnp.tile(m_new, (1, tq // LANES)))
        pv = _dot(p.astype(BF16), v_ref[0, 0, pl.ds(start, tq), :])
        acc_ref[...] = alpha * acc_ref[...] + pv
        m_ref[...] = m_new

    def by_parity(c, fn):
        for par in range(2):
            pl.when(c % 2 == par)(functools.partial(fn, par))

    def pipelined(c, masked, par):
        scores(c + 1, masked, bufs[1 - par])
        accumulate(c, bufs[par])

    scores(0, True, bufs[0])
    n_plain = jnp.maximum(qi - 1, 0)

    def two_steps(i, carry):
        pipelined(2 * i, False, 0)
        pipelined(2 * i + 1, False, 1)
        return carry

    lax.fori_loop(0, n_plain // 2, two_steps, 0)

    @pl.when(n_plain % 2 == 1)
    def _():
        pipelined(n_plain - 1, False, 0)

    @pl.when(qi >= 1)
    def _():
        by_parity(qi - 1, functools.partial(pipelined, qi - 1, True))

    by_parity(qi, lambda par: accumulate(qi, bufs[par]))

    acc = acc_ref[...]
    lane = lax.broadcasted_iota(jnp.int32, (tq, LANES), 1)
    denom = jnp.sum(jnp.where(lane == HEAD_DIM, acc, 0.0), axis=-1, keepdims=True)
    o_ref[0, 0] = jnp.where(lane < HEAD_DIM, acc / denom, 0.0).astype(BF16)


def _flash_attention(q, k, v, tq):
    bsz, nh, s, _ = q.shape
    kv_spec = pl.BlockSpec((1, 1, s, LANES), lambda b, h, i: (b, h, 0, 0))
    blk = pl.BlockSpec((1, 1, tq, LANES), lambda b, h, i: (b, h, i, 0))
    stat = pltpu.VMEM((tq, LANES), F32)
    return pl.pallas_call(
        functools.partial(_flash_kernel, tq=tq),
        grid=(bsz, nh, s // tq),
        in_specs=[blk, kv_spec, kv_spec],
        out_specs=blk,
        out_shape=jax.ShapeDtypeStruct((bsz, nh, s, LANES), BF16),
        scratch_shapes=[pltpu.VMEM((tq, tq), F32), pltpu.VMEM((tq, tq), F32), stat, stat, stat, stat],
        compiler_params=_cparams(("parallel", "parallel", "arbitrary")),
        name="flash_attention",
    )(q, k, v)


def _outproj_kernel(x_ref, yc_ref, om_ref, yr_ref, yl_ref, wc_ref, wm_ref, wr_ref, wl_ref,
                    gmla_ref, gt_ref, gf_ref, scf_ref, shf_ref, wrt_ref, brt_ref, tri_ref,
                    xo_ref, h_ref, meta_ref, cnt_ref, run_ref):
    @pl.when((pl.program_id(0) == 0) & (pl.program_id(1) == 0))
    def _():
        run_ref[...] = jnp.zeros((1, LANES), F32)

    om = [om_ref[0, h].astype(F32) for h in range(N_HEADS)]
    ssq = om[0] * om[0]
    for h in range(1, N_HEADS):
        ssq = ssq + om[h] * om[h]
    r_mla = lax.rsqrt(jnp.sum(ssq, axis=-1, keepdims=True) / GROUP_W + EPS)
    y = _dot(yc_ref[0], wc_ref[...]) + _dot(yr_ref[0], wr_ref[...]) + _dot(yl_ref[0], wl_ref[...])
    for h in range(0, N_HEADS, 2):
        pair = jnp.concatenate([(om[h] * r_mla * gmla_ref[h]).astype(BF16),
                                (om[h + 1] * r_mla * gmla_ref[h + 1]).astype(BF16)], axis=1)
        y = y + _dot(pair, wm_ref[h // 2])
    x = x_ref[0] + gt_ref[0] * y
    xo_ref[0] = x
    hf = _rms_rows(x, gf_ref[...]) * (1.0 + scf_ref[0]) + shf_ref[0]
    h_ref[0] = _pack_bf16_pairs(hf)

    h_hi = hf.astype(BF16)
    h_lo = (hf - h_hi.astype(F32)).astype(BF16)
    both = _dot(h_hi, wrt_ref[...])
    lg = both[:, :LANES] + both[:, LANES:] + _dot(h_lo, wrt_ref[:, :LANES])
    tm = lg.shape[0]
    lane = lax.broadcasted_iota(jnp.int32, (tm, LANES), 1)
    bias = brt_ref[...]
    is_g = (lane >= N_EXPERTS) & (lane < N_EXPERTS + MOE_GROUPS)
    is_e = lane < N_EXPERTS

    def first_argmax(val):
        mx = jnp.max(val, axis=-1, keepdims=True)
        return jnp.min(jnp.where(val == mx, lane, LANES), axis=-1, keepdims=True)

    gl = jnp.where(is_g, lg, NEG_BIG)
    ge = jnp.exp(gl - jnp.max(gl, axis=-1, keepdims=True))
    gp = ge / jnp.sum(ge, axis=-1, keepdims=True)
    g_idx = first_argmax(jnp.where(is_g, gp + bias, NEG_BIG))
    g_weight = jnp.sum(jnp.where(lane == g_idx, gp, 0.0), axis=-1, keepdims=True)
    in_group = is_e & ((lane // EXPERTS_PER_GROUP) == (g_idx - N_EXPERTS))
    el = jnp.where(in_group, lg, NEG_BIG)
    ee = jnp.exp(el - jnp.max(el, axis=-1, keepdims=True))
    ep = ee / jnp.sum(ee, axis=-1, keepdims=True)
    score = jnp.where(in_group, ep + bias, NEG_BIG)
    i1 = first_argmax(score)
    sel1 = lane == i1
    i2 = first_argmax(jnp.where(sel1, NEG_BIG, score))
    sel2 = lane == i2
    p1 = jnp.sum(jnp.where(sel1, ep, 0.0), axis=-1, keepdims=True)
    p2 = jnp.sum(jnp.where(sel2, ep, 0.0), axis=-1, keepdims=True)
    psum = p1 + p2
    w1 = p1 / psum * g_weight
    w2 = p2 / psum * g_weight

    onehot = jnp.where(sel1, 1.0, jnp.where(sel2, 1.0, 0.0)).astype(BF16)
    incl = _dot(tri_ref[...], onehot)
    base = run_ref[...] + incl - 1.0
    r1 = jnp.sum(jnp.where(sel1, base, 0.0), axis=-1, keepdims=True)
    r2 = jnp.sum(jnp.where(sel2, base, 0.0), axis=-1, keepdims=True)
    run_ref[...] = run_ref[...] + incl[tm - 1:tm, :]
    cnt_ref[0] = run_ref[...]
    fields = (i1.astype(F32), i2.astype(F32), r1, r2, w1, w2)
    meta = jnp.zeros((tm, LANES), F32)
    for pos, val in enumerate(fields):
        meta = jnp.where(lane == pos, val, meta)
    meta_ref[0] = meta


def _outproj(x, yc, om, yr, yl, wc, wm, wr, wl, gmla, gt, gf, scf, shf, wrt, brt, tm):
    bsz, s, d = x.shape
    nt = s // tm
    full = lambda a: pl.BlockSpec(a.shape, lambda b, i: (0,) * a.ndim)
    tok = lambda w: pl.BlockSpec((1, tm, w), lambda b, i: (b, i, 0))
    vec = pl.BlockSpec((1, 1, d), lambda b, i: (b, 0, 0))
    tri = jnp.asarray(np.tril(np.ones((tm, tm), np.float32)), BF16)
    return pl.pallas_call(
        _outproj_kernel,
        grid=(bsz, nt),
        in_specs=[tok(d), tok(GROUP_W),
                  pl.BlockSpec((1, N_HEADS, tm, LANES), lambda b, i: (b, 0, i, 0)),
                  tok(GROUP_W), tok(GROUP_W),
                  full(wc), full(wm), full(wr), full(wl), full(gmla), vec, full(gf), vec, vec,
                  full(wrt), full(brt), full(tri)],
        out_specs=[tok(d), tok(d // 2), tok(LANES),
                   pl.BlockSpec((1, 1, LANES), lambda b, i: (b * nt + i, 0, 0))],
        out_shape=[jax.ShapeDtypeStruct((bsz, s, d), F32), jax.ShapeDtypeStruct((bsz, s, d // 2), jnp.uint32),
                   jax.ShapeDtypeStruct((bsz, s, LANES), F32),
                   jax.ShapeDtypeStruct((bsz * nt, 1, LANES), F32)],
        scratch_shapes=[pltpu.VMEM((1, LANES), F32)],
        compiler_params=_cparams(("arbitrary", "arbitrary")),
        name="outproj_router",
    )(x, yc, om, yr, yl, wc, wm, wr, wl, gmla, gt, gf, scf, shf, wrt, brt, tri)


def _route_plan(meta, cnt, tmg):
    t = meta.shape[0] * meta.shape[1]
    m = meta.reshape(t, LANES)
    e = m[:, 0:2].astype(jnp.int32)
    r = m[:, 2:4].astype(jnp.int32)
    counts = cnt[-1, 0, :N_EXPERTS].astype(jnp.int32)
    padded = (counts + tmg - 1) // tmg * tmg
    ends = jnp.cumsum(padded)
    starts = ends - padded
    dest = jnp.take(starts, e) + r
    n_tiles = (2 * t) // tmg + N_EXPERTS
    tile_start = jnp.arange(n_tiles, dtype=jnp.int32) * tmg
    tile_exp = jnp.sum((ends[None, :] <= tile_start[:, None]).astype(jnp.int32), axis=1)
    tile_exp = jnp.minimum(tile_exp, N_EXPERTS - 1)
    n_used = (ends[-1:] // tmg).astype(jnp.int32)
    last_tile = jnp.maximum(ends - tmg, 0).astype(jnp.int32)
    return dest[:, 0], dest[:, 1], tile_exp, n_used, last_tile, padded.astype(jnp.int32)


def _dispatch_kernel(zs_ref, zv_ref, nu_ref, d1_ref, d2_ref, h_ref, xs_ref, zero_ref, sem, *, tmc, tmg):
    @pl.when(pl.program_id(0) == 0)
    def _():
        zero_ref[...] = jnp.zeros(zero_ref.shape, jnp.uint32)
        n_tiles = xs_ref.shape[0] // tmg

        def fill(start):
            return pltpu.make_async_copy(zero_ref, xs_ref.at[pl.ds(pl.multiple_of(start, tmg), tmg), :], sem)

        def fill_tail(j, carry, wait):
            cp = fill(j * tmg)
            cp.wait() if wait else cp.start()
            return carry

        for wait in (False, True):
            for e in range(N_EXPERTS):
                cp = fill(zs_ref[e])
                pl.when(zv_ref[e] > 0)(cp.wait if wait else cp.start)
            lax.fori_loop(nu_ref[0], n_tiles, functools.partial(fill_tail, wait=wait), 0)

    def push(r, carry):
        src = h_ref.at[pl.ds(r, 1), :]
        pltpu.make_async_copy(src, xs_ref.at[pl.ds(d1_ref[0, 0, r], 1), :], sem).start(priority=0)
        pltpu.make_async_copy(src, xs_ref.at[pl.ds(d2_ref[0, 0, r], 1), :], sem).start(priority=1)
        return carry

    lax.fori_loop(0, tmc, push, 0, unroll=8)
    one_row = pltpu.make_async_copy(h_ref.at[pl.ds(0, 1), :], xs_ref.at[pl.ds(0, 1), :], sem)
    for _ in range(2 * tmc):
        one_row.wait()


def _dispatch(hp, d1, d2, last_tile, padded, n_used, tmc, tmg):
    t, dw = hp.shape
    n_rows = 2 * t + N_EXPERTS * tmg
    smem_rows = pl.BlockSpec((1, 1, tmc), lambda i, zs, zv, nu: (i, 0, 0), memory_space=pltpu.SMEM)
    return pl.pallas_call(
        functools.partial(_dispatch_kernel, tmc=tmc, tmg=tmg),
        grid_spec=pltpu.PrefetchScalarGridSpec(
            num_scalar_prefetch=3, grid=(t // tmc,),
            in_specs=[smem_rows, smem_rows, pl.BlockSpec((tmc, dw), lambda i, zs, zv, nu: (i, 0))],
            out_specs=pl.BlockSpec(memory_space=pl.ANY),
            scratch_shapes=[pltpu.VMEM((tmg, dw), jnp.uint32), pltpu.SemaphoreType.DMA(())]),
        out_shape=jax.ShapeDtypeStruct((n_rows, dw), jnp.uint32),
        compiler_params=_cparams(("arbitrary",)),
        name="moe_dispatch",
    )(last_tile, padded, n_used, d1.reshape(t // tmc, 1, tmc), d2.reshape(t // tmc, 1, tmc), hp)


def _experts_kernel(te_ref, nu_ref, xs_ref, wg_ref, wu_ref, wd_ref, y_ref):
    used = pl.program_id(0) < nu_ref[0]

    @pl.when(used)
    def _():
        x = _unpack_bf16_pairs(xs_ref[...]).astype(BF16)
        gate = _dot(x, wg_ref[0, 0].astype(BF16))
        hid = gate * _sigmoid(gate) * _dot(x, wu_ref[0, 0].astype(BF16))
        y_ref[...] = _pack_bf16_pairs(_dot(hid.astype(BF16), wd_ref[0, 0].astype(BF16)))

    @pl.when(jnp.logical_not(used))
    def _():
        y_ref[...] = jnp.zeros(y_ref.shape, jnp.uint32)


def _experts(xs, tile_exp, n_used, layer, wg, wu, wd, tmg):
    n_rows, dw = xs.shape
    d = 2 * dw
    tile = lambda i, te, nu: jnp.minimum(i, nu[0] - 1)
    rows = pl.BlockSpec((tmg, dw), lambda i, te, nu: (tile(i, te, nu), 0))
    wspec = lambda shape: pl.BlockSpec((1, 1) + shape,
                                       lambda i, te, nu: (layer, te[tile(i, te, nu)], 0, 0))
    return pl.pallas_call(
        _experts_kernel,
        grid_spec=pltpu.PrefetchScalarGridSpec(
            num_scalar_prefetch=2, grid=(n_rows // tmg,),
            in_specs=[rows, wspec((d, D_EXPERT)), wspec((d, D_EXPERT)), wspec((D_EXPERT, d))],
            out_specs=pl.BlockSpec((tmg, dw), lambda i, te, nu: (i, 0))),
        out_shape=jax.ShapeDtypeStruct((n_rows, dw), jnp.uint32),
        compiler_params=_cparams(("arbitrary",)),
        name="moe_experts",
    )(tile_exp, n_used, xs, wg, wu, wd)


def _combine_kernel(d1_ref, d2_ref, x_ref, meta_ref, gt_ref, y_ref, o_ref, b1_ref, b2_ref, sem, *, tmc):
    def pull(r, carry):
        pltpu.make_async_copy(y_ref.at[pl.ds(d1_ref[0, 0, r], 1), :], b1_ref.at[pl.ds(r, 1), :], sem).start(priority=0)
        pltpu.make_async_copy(y_ref.at[pl.ds(d2_ref[0, 0, r], 1), :], b2_ref.at[pl.ds(r, 1), :], sem).start(priority=1)
        return carry

    lax.fori_loop(0, tmc, pull, 0, unroll=8)
    one_row = pltpu.make_async_copy(y_ref.at[pl.ds(0, 1), :], b1_ref.at[pl.ds(0, 1), :], sem)
    for _ in range(2 * tmc):
        one_row.wait()
    w1, w2 = _router_weights(meta_ref[0])
    y = w1 * _unpack_bf16_pairs(b1_ref[...]) + w2 * _unpack_bf16_pairs(b2_ref[...])
    o_ref[0] = x_ref[0] + gt_ref[0] * y


def _combine(x, meta, gt, y, d1, d2, tmc):
    bsz, s, d = x.shape
    nt = s // tmc
    t = bsz * s
    smem_rows = pl.BlockSpec((1, 1, tmc), lambda b, i: (b * nt + i, 0, 0), memory_space=pltpu.SMEM)
    tok = lambda w: pl.BlockSpec((1, tmc, w), lambda b, i: (b, i, 0))
    buf = pltpu.VMEM((tmc, d // 2), jnp.uint32)
    return pl.pallas_call(
        functools.partial(_combine_kernel, tmc=tmc),
        grid=(bsz, nt),
        in_specs=[smem_rows, smem_rows, tok(d), tok(LANES),
                  pl.BlockSpec((1, 1, d), lambda b, i: (b, 0, 0)),
                  pl.BlockSpec(memory_space=pl.ANY)],
        out_specs=tok(d),
        out_shape=jax.ShapeDtypeStruct((bsz, s, d), F32),
        scratch_shapes=[buf, buf, pltpu.SemaphoreType.DMA(())],
        compiler_params=_cparams(("arbitrary", "arbitrary")),
        name="moe_combine",
    )(d1.reshape(t // tmc, 1, tmc), d2.reshape(t // tmc, 1, tmc), x, meta, gt, y)


def _layer_weights(l, w_in, mla_q_norm_g, mla_w_uq, mla_kv_norm_g, mla_w_ukv, mla_q_qk_g,
                   mla_k_qk_g, lru_w_a, lru_w_x, mix_norm_g, w_out, router_group_w,
                   router_group_b, router_expert_w, router_expert_b):
    half = RET_DK // 2
    perm = np.concatenate([np.arange(half) + HEAD_DIM * h for h in range(N_HEADS)]
                          + [np.arange(half) + half + HEAD_DIM * h for h in range(N_HEADS)])
    w = w_in[l]
    o_mla, o_ret, o_lru = U_CONV, U_CONV + 352, U_CONV + 352 + U_RET
    w_ret = w[:, o_ret:o_ret + U_RET]
    w_ret = jnp.concatenate([w_ret[:, perm], w_ret[:, GROUP_W + perm], w_ret[:, 2 * GROUP_W:]], axis=1)
    w_all = jnp.concatenate([w_ret, w[:, o_lru:o_lru + U_LRU], w[:, :U_CONV], w[:, o_mla:o_mla + 352],
                             jnp.zeros((D_MODEL, U_MLA - 352), F32)], axis=1).astype(BF16)

    hw = N_HEADS * LANES
    r16 = MLA_ROPE // 2
    wq = mla_w_uq[l].reshape(Q_LORA, N_HEADS, MLA_QK)
    zq = jnp.zeros((Q_LORA, N_HEADS, LANES - MLA_QK), F32)
    q_cols = jnp.concatenate([wq, zq], axis=2).reshape(Q_LORA, hw)
    wq_sw = jnp.concatenate([jnp.zeros((Q_LORA, N_HEADS, MLA_NOPE), F32), wq[:, :, MLA_NOPE + r16:],
                             wq[:, :, MLA_NOPE:MLA_NOPE + r16], zq], axis=2).reshape(Q_LORA, hw)
    wkv = mla_w_ukv[l].reshape(KV_LORA, N_HEADS, MLA_NOPE + HEAD_DIM)
    zk = jnp.zeros((KV_LORA, N_HEADS, LANES - MLA_NOPE), F32)
    k_cols = jnp.concatenate([wkv[:, :, :MLA_NOPE], zk], axis=2).reshape(KV_LORA, hw)
    v_cols = jnp.concatenate([wkv[:, :, MLA_NOPE:], zk], axis=2).reshape(KV_LORA, hw)
    eye = jnp.eye(MLA_ROPE, dtype=F32)
    place = jnp.concatenate([jnp.zeros((MLA_ROPE, MLA_NOPE), F32), eye,
                             jnp.zeros((MLA_ROPE, LANES - MLA_QK), F32)], axis=1)
    eye_sw = jnp.concatenate([eye[:, r16:], eye[:, :r16]], axis=1)
    place_sw = jnp.concatenate([jnp.zeros((MLA_ROPE, MLA_NOPE), F32), eye_sw,
                                jnp.zeros((MLA_ROPE, LANES - MLA_QK), F32)], axis=1)
    n_big = 4 * hw + 2 * LANES
    wbig = jnp.zeros((U_MLA, n_big), F32)
    wbig = wbig.at[:Q_LORA, :hw].set(q_cols).at[:Q_LORA, hw:2 * hw].set(wq_sw)
    wbig = wbig.at[Q_LORA:Q_LORA + KV_LORA, 2 * hw:3 * hw].set(k_cols)
    wbig = wbig.at[Q_LORA:Q_LORA + KV_LORA, 3 * hw:4 * hw].set(v_cols)
    wbig = wbig.at[Q_LORA + KV_LORA:352, 4 * hw:4 * hw + LANES].set(place)
    wbig = wbig.at[Q_LORA + KV_LORA:352, 4 * hw + LANES:].set(place_sw)
    wbig = wbig.astype(BF16)

    gu = jnp.concatenate([mla_q_norm_g[l], mla_kv_norm_g[l], mla_k_qk_g[l][MLA_NOPE:],
                          jnp.zeros((U_MLA - 352,), F32)])[None, :]
    qscale = (MLA_QK ** -0.5) * math.log2(math.e)
    gq_full = mla_q_qk_g[l]
    pad = jnp.zeros((LANES - MLA_QK,), F32)
    gq = (jnp.concatenate([gq_full, pad]) * qscale)[None, :]
    gqs = (jnp.concatenate([jnp.zeros((MLA_NOPE,), F32), gq_full[MLA_NOPE + r16:],
                            gq_full[MLA_NOPE:MLA_NOPE + r16], pad]) * qscale)[None, :]
    gk = jnp.concatenate([mla_k_qk_g[l][:MLA_NOPE], jnp.zeros((LANES - MLA_NOPE,), F32)])[None, :]

    def blockdiag(wb):
        out = jnp.zeros((GROUP_W, GROUP_W), F32)
        for n in range(wb.shape[0]):
            out = out.at[n * HEAD_DIM:(n + 1) * HEAD_DIM, n * HEAD_DIM:(n + 1) * HEAD_DIM].set(wb[n])
        return out.astype(BF16)

    gmix = mix_norm_g[l]
    wo = w_out[l].astype(BF16)
    wm = wo[GROUP_W:2 * GROUP_W].reshape(N_HEADS, HEAD_DIM, D_MODEL)
    wm = jnp.concatenate([wm, jnp.zeros((N_HEADS, LANES - HEAD_DIM, D_MODEL), BF16)], axis=1)
    wm = wm.reshape(N_HEADS // 2, 2 * LANES, D_MODEL)
    gmla = jnp.concatenate([gmix[GROUP_W:2 * GROUP_W].reshape(N_HEADS, 1, HEAD_DIM),
                            jnp.zeros((N_HEADS, 1, LANES - HEAD_DIM), F32)], axis=2)
    wrt = jnp.concatenate([router_expert_w[l], router_group_w[l],
                           jnp.zeros((D_MODEL, LANES - N_EXPERTS - MOE_GROUPS), F32)], axis=1)
    wrt_hi = wrt.astype(BF16)
    wrt = jnp.concatenate([wrt_hi, (wrt - wrt_hi.astype(F32)).astype(BF16)], axis=1)
    brt = jnp.concatenate([router_expert_b[l], router_group_b[l],
                           jnp.zeros((LANES - N_EXPERTS - MOE_GROUPS,), F32)])[None, :]
    return dict(w_all=w_all, wbig=wbig, gu=gu, gq=gq, gqs=gqs, gk=gk,
                wa=blockdiag(lru_w_a[l]), wx=blockdiag(lru_w_x[l]),
                g_conv=gmix[None, :GROUP_W], g_ret=gmix[None, 2 * GROUP_W:3 * GROUP_W],
                g_lru=gmix[None, 3 * GROUP_W:], gmla=gmla,
                wc=wo[:GROUP_W], wm=wm, wr=wo[2 * GROUP_W:3 * GROUP_W], wl=wo[3 * GROUP_W:],
                wrt=wrt, brt=brt)


def _mla_consts():
    seg_u = np.concatenate([np.zeros(Q_LORA), np.ones(KV_LORA), 2 * np.ones(MLA_ROPE),
                            3 * np.ones(U_MLA - 352)])
    mu = jnp.asarray(seg_u[:, None] == seg_u[None, :], BF16)
    invu = jnp.asarray(np.concatenate([np.full(Q_LORA, 1.0 / Q_LORA), np.full(KV_LORA, 1.0 / KV_LORA),
                                       np.full(MLA_ROPE, 1.0 / MLA_ROPE), np.ones(U_MLA - 352)]), F32)[None, :]
    lane = np.arange(N_HEADS * LANES)
    seg_q = (lane // LANES) * 3 + np.where(lane % LANES < MLA_NOPE, 0, np.where(lane % LANES < MLA_QK, 1, 2))
    sq = jnp.asarray(seg_q[:, None] == seg_q[None, :], BF16)
    inv_head = np.concatenate([np.full(MLA_NOPE, 1.0 / MLA_NOPE), np.full(MLA_ROPE, 1.0 / MLA_ROPE),
                               np.ones(LANES - MLA_QK)])
    invq = jnp.asarray(np.tile(inv_head, N_HEADS), F32)[None, :]
    onev = jnp.asarray((np.arange(LANES) == HEAD_DIM).astype(np.float32))[None, :]
    return mu, invu, sq, invq, onev


def kernel(x, c, positions, ada_w, ada_b, norm_mix_g, w_in, conv_w, mla_q_norm_g, mla_w_uq, mla_kv_norm_g, mla_w_ukv, mla_q_qk_g, mla_k_qk_g, lru_conv_w, lru_conv_b, lru_w_a, lru_b_a, lru_w_x, lru_b_x, lru_lambda, mix_norm_g, w_out, norm_ffn_g, router_group_w, router_group_b, router_expert_w, router_expert_b, exp_w_gate, exp_w_up, exp_w_down):
    bsz, s, d = x.shape
    depth = ada_w.shape[0]
    tm = min(512, s)
    chunk = min(256, s)
    tq = min(1024, s)
    tmc = min(512, s)
    tmg = 512

    inv = jnp.concatenate([1.0 / (ROPE_BASE ** (jnp.arange(0, MLA_ROPE, 2, dtype=F32) / MLA_ROPE)),
                           1.0 / (ROPE_BASE ** (jnp.arange(0, RET_DK, 2, dtype=F32) / RET_DK))])[:, None]
    trig = _rope_tables(positions, inv)
    ex_ret, ex_mla = _trig_expanders()

    c_pad = jnp.concatenate([c, jnp.zeros((8 - bsz, d), F32)], axis=0)
    mod = _modulation(c_pad, ada_w, ada_b)[:, :bsz]
    ret_consts = _ret_consts(chunk)
    mu, invu, sq, invq, onev = _mla_consts()

    pending = None
    for l in range(depth):
        sh_m, sc_m, gt_m, sh_f, sc_f, gt_f = [m[:, None, :] for m in jnp.split(mod[l], 6, axis=-1)]
        lw = _layer_weights(l, w_in, mla_q_norm_g, mla_w_uq, mla_kv_norm_g, mla_w_ukv, mla_q_qk_g,
                            mla_k_qk_g, lru_w_a, lru_w_x, mix_norm_g, w_out, router_group_w,
                            router_group_b, router_expert_w, router_expert_b)
        if pending is None:
            u = _inproj(x, norm_mix_g[l][None, :], sc_m, sh_m, lw["w_all"], tm)
        else:
            x, u = _inproj_combine(x, *pending, norm_mix_g[l][None, :], sc_m, sh_m, lw["w_all"], tm)
        y_conv = _conv_mixer(u, conv_w[l], lw["g_conv"], tm)
        y_lru = _lru_mixer(u, lru_conv_w[l], lru_conv_b[l][None, :], lw["wa"], lru_b_a[l][None, :],
                           lw["wx"], lru_b_x[l][None, :], lru_lambda[l][None, :], lw["g_lru"], tm)
        y_ret = _ret_mixer(u, trig, ex_ret, ret_consts, lw["g_ret"], chunk)
        q, k, v = _mla_prep(u, trig, ex_mla, mu, invu, lw["gu"], lw["wbig"], sq, invq,
                            lw["gq"], lw["gqs"], lw["gk"], onev, tm)
        o_mla = _flash_attention(q, k, v, tq)
        x, hp, meta, cnt = _outproj(x, y_conv, o_mla, y_ret, y_lru, lw["wc"], lw["wm"], lw["wr"],
                                    lw["wl"], lw["gmla"], gt_m, norm_ffn_g[l][None, :], sc_f, sh_f,
                                    lw["wrt"], lw["brt"], tm)
        d1, d2, tile_exp, n_used, last_tile, padded = _route_plan(meta, cnt, tmg)
        xs = _dispatch(hp.reshape(bsz * s, d // 2), d1, d2, last_tile, padded, n_used, tmc, tmg)
        ys = _experts(xs, tile_exp, n_used, l, exp_w_gate, exp_w_up, exp_w_down, tmg)
        pending = (meta, gt_f, ys, d1, d2)
    meta, gt_f, ys, d1, d2 = pending
    return _combine(x, meta, gt_f, ys, d1, d2, tmc)
```

```python
import functools
import math

import jax
import jax.numpy as jnp
import numpy as np
from jax import lax
from jax.experimental import pallas as pl
from jax.experimental.pallas import tpu as pltpu

F32 = jnp.float32
BF16 = jnp.bfloat16
HIGHEST = lax.Precision.HIGHEST

D_MODEL = 1024
GROUP_W = 256
HEAD_DIM = 64
N_HEADS = 4
MLA_NOPE = 64
MLA_ROPE = 32
MLA_QK = 96
Q_LORA = 192
KV_LORA = 128
RET_DK = 64
LRU_C = 8.0
MOE_GROUPS = 4
EXPERTS_PER_GROUP = 8
N_EXPERTS = 32
D_EXPERT = 256
ROPE_BASE = 10000.0
EPS = 1e-6

LANES = 128
SUBLANES = 8
MXU_DIM = 256
U_RET, U_LRU, U_CONV, U_MLA = 1024, 512, 768, 384
U_COLS = U_RET + U_LRU + U_CONV + U_MLA
N_FREQ = MLA_ROPE // 2 + RET_DK // 2
NEG_BIG = -1e30
VMEM_LIMIT = 56 * 1024 * 1024


def _cparams(sem):
    return pltpu.CompilerParams(dimension_semantics=sem, vmem_limit_bytes=VMEM_LIMIT)


def _dot(a, b):
    return jnp.dot(a, b, preferred_element_type=F32)


def _dot_nt(a, b):
    return lax.dot_general(a, b, (((1,), (1,)), ((), ())), preferred_element_type=F32)


def _dot_tn(a, b):
    return lax.dot_general(a, b, (((0,), (0,)), ((), ())), preferred_element_type=F32)


def _rms_rows(y, g):
    return y * lax.rsqrt(jnp.mean(y * y, axis=-1, keepdims=True) + EPS) * g


def _sigmoid(x):
    return 0.5 * jnp.tanh(0.5 * x) + 0.5


def _pack_bf16_pairs(a):
    k = a.shape[1] // 2
    rounded = a.astype(BF16).astype(F32)
    lo = lax.bitcast_convert_type(rounded[:, :k], jnp.uint32) >> 16
    hi = lax.bitcast_convert_type(rounded[:, k:], jnp.uint32) & jnp.uint32(0xFFFF0000)
    return lo | hi


def _unpack_bf16_pairs(w):
    lo = lax.bitcast_convert_type(w << 16, F32)
    hi = lax.bitcast_convert_type(w & jnp.uint32(0xFFFF0000), F32)
    return jnp.concatenate([lo, hi], axis=1)


def _rope_kernel(pos_ref, inv_ref, tab_ref):
    ang = pos_ref[0].astype(F32) * inv_ref[...]
    row = lax.broadcasted_iota(jnp.int32, (LANES - 2 * N_FREQ, ang.shape[1]), 0)
    pad = jnp.where(row == 0, 1.0, 0.0)
    tab_ref[0] = jnp.concatenate([jnp.cos(ang), jnp.sin(ang), pad], axis=0).T


def _rope_tables(positions, inv):
    bsz, s = positions.shape
    ts = min(s, 2048)
    return pl.pallas_call(
        _rope_kernel,
        grid=(bsz, s // ts),
        in_specs=[pl.BlockSpec((1, 1, ts), lambda b, i: (b, 0, i)),
                  pl.BlockSpec((N_FREQ, 1), lambda b, i: (0, 0))],
        out_specs=pl.BlockSpec((1, ts, LANES), lambda b, i: (b, i, 0)),
        out_shape=jax.ShapeDtypeStruct((bsz, s, LANES), F32),
        compiler_params=_cparams(("parallel", "parallel")),
        name="rope_tables",
    )(positions.reshape(bsz, 1, s), inv)


def _expand_trig(tab, expand):
    hi = tab.astype(BF16)
    lo = (tab - hi.astype(F32)).astype(BF16)
    trig = _dot(hi, expand) + _dot(lo, expand)
    return trig[:, :LANES], trig[:, LANES:]


def _mod_kernel(c_ref, w_ref, b_ref, o_ref):
    c = c_ref[...]
    ca = c * _sigmoid(c)
    o_ref[0] = jnp.dot(ca, w_ref[0], precision=HIGHEST, preferred_element_type=F32) + b_ref[0]


def _modulation(c_pad, ada_w, ada_b):
    nl, d, n = ada_w.shape
    tn = 1536
    return pl.pallas_call(
        _mod_kernel,
        grid=(nl, n // tn),
        in_specs=[pl.BlockSpec((8, d), lambda l, j: (0, 0)),
                  pl.BlockSpec((1, d, tn), lambda l, j: (l, 0, j)),
                  pl.BlockSpec((1, 1, tn), lambda l, j: (l, 0, j))],
        out_specs=pl.BlockSpec((1, 8, tn), lambda l, j: (l, 0, j)),
        out_shape=jax.ShapeDtypeStruct((nl, 8, n), F32),
        compiler_params=_cparams(("parallel", "parallel")),
        name="adaln_mod",
    )(c_pad, ada_w, ada_b.reshape(nl, 1, n))


def _norm_project(x, g_ref, sc_ref, sh_ref, w_ref):
    h = _rms_rows(x, g_ref[...]) * (1.0 + sc_ref[0]) + sh_ref[0]
    return _dot(h.astype(BF16), w_ref[...]).astype(BF16)


def _inproj_kernel(x_ref, g_ref, sc_ref, sh_ref, w_ref, u_ref):
    u_ref[0] = _norm_project(x_ref[0], g_ref, sc_ref, sh_ref, w_ref)


def _router_weights(meta):
    lane = lax.broadcasted_iota(jnp.int32, meta.shape, 1)
    w1 = jnp.sum(jnp.where(lane == 4, meta, 0.0), axis=-1, keepdims=True)
    w2 = jnp.sum(jnp.where(lane == 5, meta, 0.0), axis=-1, keepdims=True)
    return w1, w2


def _combine_kernel(d1c_ref, d2c_ref, d1n_ref, d2n_ref, x_ref, meta_ref, gtf_ref, y_hbm, *rest, tm, project):
    if project:
        g_ref, sc_ref, sh_ref, w_ref, xo_ref, u_ref, *scratch = rest
    else:
        xo_ref, *scratch = rest
    a1_ref, a2_ref, b1_ref, b2_ref, sem_a, sem_b = scratch
    n = pl.program_id(0)
    bufs = ((a1_ref, a2_ref, sem_a), (b1_ref, b2_ref, sem_b))

    def pull(d_ref, buf, sem, r):
        return pltpu.make_async_copy(y_hbm.at[pl.ds(d_ref[0, 0, r], 1), :], buf.at[pl.ds(r, 1), :], sem)

    def wait_tile(buf, sem):
        one_row = pltpu.make_async_copy(y_hbm.at[pl.ds(0, 1), :], buf.at[pl.ds(0, 1), :], sem)
        for _ in range(2 * tm):
            one_row.wait()

    @pl.when(n == 0)
    def _():
        def first(r, carry):
            pull(d1c_ref, a1_ref, sem_a, r).start(priority=0)
            pull(d2c_ref, a2_ref, sem_a, r).start(priority=1)
            return carry
        lax.fori_loop(0, tm, first, 0, unroll=8)

    def step(par):
        y1_ref, y2_ref, sem = bufs[par]
        n1_ref, n2_ref, nsem = bufs[1 - par]
        wait_tile(y1_ref, sem)
        for r in range(tm):
            pull(d1n_ref, n1_ref, nsem, r).start(priority=0)
            pull(d2n_ref, n2_ref, nsem, r).start(priority=1)
        w1, w2 = _router_weights(meta_ref[0])
        y = w1 * _unpack_bf16_pairs(y1_ref[...]) + w2 * _unpack_bf16_pairs(y2_ref[...])
        x = x_ref[0] + gtf_ref[0] * y
        xo_ref[0] = x
        if project:
            u_ref[0] = _norm_project(x, g_ref, sc_ref, sh_ref, w_ref)

        @pl.when(n == pl.num_programs(0) - 1)
        def _():
            wait_tile(n1_ref, nsem)

    for par in range(2):
        pl.when(n % 2 == par)(functools.partial(step, par))


def _combine(x, meta, gtf, ys, d1, d2, tm, proj=None):
    bsz, s, d = x.shape
    nt = s // tm
    n_steps = bsz * nt
    nxt = lambda n: jnp.minimum(n + 1, n_steps - 1)
    smem = lambda f: pl.BlockSpec((1, 1, tm), lambda n: (f(n), 0, 0), memory_space=pltpu.SMEM)
    tok = lambda wdt: pl.BlockSpec((1, tm, wdt), lambda n: (n // nt, n % nt, 0))
    vec = pl.BlockSpec((1, 1, d), lambda n: (n // nt, 0, 0))
    buf = pltpu.VMEM((tm, d // 2), jnp.uint32)
    dd1, dd2 = d1.reshape(n_steps, 1, tm), d2.reshape(n_steps, 1, tm)
    in_specs = [smem(lambda n: n), smem(lambda n: n), smem(nxt), smem(nxt),
                tok(d), tok(LANES), vec, pl.BlockSpec(memory_space=pl.ANY)]
    out_specs = [tok(d)]
    out_shape = [jax.ShapeDtypeStruct((bsz, s, d), F32)]
    args = (dd1, dd2, dd1, dd2, x, meta, gtf, ys)
    if proj is not None:
        in_specs += [pl.BlockSpec((1, d), lambda n: (0, 0)), vec, vec, pl.BlockSpec((d, U_COLS), lambda n: (0, 0))]
        out_specs.append(tok(U_COLS))
        out_shape.append(jax.ShapeDtypeStruct((bsz, s, U_COLS), BF16))
        args += tuple(proj)
    return pl.pallas_call(
        functools.partial(_combine_kernel, tm=tm, project=proj is not None),
        grid=(n_steps,),
        in_specs=in_specs,
        out_specs=out_specs,
        out_shape=out_shape,
        scratch_shapes=[buf, buf, buf, buf, pltpu.SemaphoreType.DMA(()), pltpu.SemaphoreType.DMA(())],
        compiler_params=_cparams(("arbitrary",)),
        name="moe_combine_inproj" if proj is not None else "moe_combine",
    )(*args)


def _inproj(x, g, sc, sh, w, tm):
    bsz, s, d = x.shape
    vec = pl.BlockSpec((1, 1, d), lambda b, i: (b, 0, 0))
    return pl.pallas_call(
        _inproj_kernel,
        grid=(bsz, s // tm),
        in_specs=[pl.BlockSpec((1, tm, d), lambda b, i: (b, i, 0)),
                  pl.BlockSpec((1, d), lambda b, i: (0, 0)),
                  vec, vec,
                  pl.BlockSpec((d, U_COLS), lambda b, i: (0, 0))],
        out_specs=pl.BlockSpec((1, tm, U_COLS), lambda b, i: (b, i, 0)),
        out_shape=jax.ShapeDtypeStruct((bsz, s, U_COLS), BF16),
        compiler_params=_cparams(("parallel", "parallel")),
        name="inproj",
    )(x, g, sc, sh, w)


def _conv_kernel(u_ref, w_ref, g_ref, y_ref, buf_ref, *, tm):
    @pl.when(pl.program_id(1) == 0)
    def _():
        buf_ref[0:8, :] = jnp.zeros((8, GROUP_W), F32)

    u = u_ref[0].astype(F32)
    b_gate, c_gate, xin = u[:, :GROUP_W], u[:, GROUP_W:2 * GROUP_W], u[:, 2 * GROUP_W:]
    cx = c_gate * xin
    buf_ref[8:8 + tm, :] = cx
    conv = (w_ref[2:3, :] * cx + w_ref[1:2, :] * buf_ref[7:7 + tm, :]
            + w_ref[0:1, :] * buf_ref[6:6 + tm, :])
    buf_ref[0:8, :] = cx[tm - 8:, :]
    y_ref[0] = _rms_rows(b_gate * conv, g_ref[...]).astype(BF16)


def _conv_mixer(u, w, g, tm):
    bsz, s, _ = u.shape
    return pl.pallas_call(
        functools.partial(_conv_kernel, tm=tm),
        grid=(bsz, s // tm),
        in_specs=[pl.BlockSpec((1, tm, U_CONV), lambda b, i: (b, i, (U_RET + U_LRU) // U_CONV)),
                  pl.BlockSpec((3, GROUP_W), lambda b, i: (0, 0)),
                  pl.BlockSpec((1, GROUP_W), lambda b, i: (0, 0))],
        out_specs=pl.BlockSpec((1, tm, GROUP_W), lambda b, i: (b, i, 0)),
        out_shape=jax.ShapeDtypeStruct((bsz, s, GROUP_W), BF16),
        scratch_shapes=[pltpu.VMEM((tm + 8, GROUP_W), F32)],
        compiler_params=_cparams(("parallel", "arbitrary")),
        name="conv_mixer",
    )(u, w, g)


def _lru_kernel(u_ref, cw_ref, cb_ref, wa_ref, ba_ref, wx_ref, bx_ref, lam_ref, g_ref,
                y_ref, buf_ref, h_ref, *, tm):
    @pl.when(pl.program_id(1) == 0)
    def _():
        buf_ref[0:8, :] = jnp.zeros((8, GROUP_W), F32)
        h_ref[...] = jnp.zeros((1, GROUP_W), F32)

    u = u_ref[0].astype(F32)
    xraw, gate = u[:, :GROUP_W], u[:, GROUP_W:]
    buf_ref[8:8 + tm, :] = xraw
    xb = (cw_ref[3:4, :] * xraw + cw_ref[2:3, :] * buf_ref[7:7 + tm, :]
          + cw_ref[1:2, :] * buf_ref[6:6 + tm, :] + cw_ref[0:1, :] * buf_ref[5:5 + tm, :]
          + cb_ref[...])
    buf_ref[0:8, :] = xraw[tm - 8:, :]

    xbb = xb.astype(BF16)
    r = _sigmoid(_dot(xbb, wa_ref[...]) + ba_ref[...])
    i = _sigmoid(_dot(xbb, wx_ref[...]) + bx_ref[...])
    nlam = -lam_ref[...]
    softplus = jnp.maximum(nlam, 0.0) + jnp.log(1.0 + jnp.exp(-jnp.abs(nlam)))
    log_a = (-LRU_C) * r * softplus
    a = jnp.exp(log_a)
    b = jnp.sqrt(1.0 - a * a) * (i * xb)

    n_groups = tm // SUBLANES
    a = a.reshape(n_groups, SUBLANES, GROUP_W)
    b = b.reshape(n_groups, SUBLANES, GROUP_W)
    sub = lax.broadcasted_iota(jnp.int32, a.shape, 1)
    d = 1
    while d < SUBLANES:
        keep = sub >= d
        a_sh = jnp.where(keep, pltpu.roll(a, d, 1), 1.0)
        b_sh = jnp.where(keep, pltpu.roll(b, d, 1), 0.0)
        b = a * b_sh + b
        a = a * a_sh
        d *= 2
    carry = h_ref[...]
    groups = []
    for g in range(n_groups):
        hg = a[g] * carry + b[g]
        carry = hg[SUBLANES - 1:, :]
        groups.append(hg)
    h = jnp.concatenate(groups, axis=0)
    h_ref[...] = carry

    gelu = 0.5 * gate * (1.0 + jnp.tanh(math.sqrt(2.0 / math.pi) * (gate + 0.044715 * gate * gate * gate)))
    y_ref[0] = _rms_rows(h * gelu, g_ref[...]).astype(BF16)


def _lru_mixer(u, cw, cb, wa, ba, wx, bx, lam, g, tm):
    bsz, s, _ = u.shape
    row = pl.BlockSpec((1, GROUP_W), lambda b, i: (0, 0))
    mat = pl.BlockSpec((GROUP_W, GROUP_W), lambda b, i: (0, 0))
    return pl.pallas_call(
        functools.partial(_lru_kernel, tm=tm),
        grid=(bsz, s // tm),
        in_specs=[pl.BlockSpec((1, tm, U_LRU), lambda b, i: (b, i, U_RET // U_LRU)),
                  pl.BlockSpec((4, GROUP_W), lambda b, i: (0, 0)),
                  row, mat, row, mat, row, row, row],
        out_specs=pl.BlockSpec((1, tm, GROUP_W), lambda b, i: (b, i, 0)),
        out_shape=jax.ShapeDtypeStruct((bsz, s, GROUP_W), BF16),
        scratch_shapes=[pltpu.VMEM((tm + 8, GROUP_W), F32), pltpu.VMEM((1, GROUP_W), F32)],
        compiler_params=_cparams(("parallel", "arbitrary")),
        name="lru_mixer",
    )(u, cw, cb, wa, ba, wx, bx, lam, g)


def _ret_kernel(u_ref, tab_ref, ex_ref, inner_ref, qd_ref, kd_ref, cd_ref, bm_ref, gm_ref,
                mq_ref, mv_ref, g_ref, y_ref, st_ref):
    @pl.when(pl.program_id(0) == 0)
    def _():
        st_ref[...] = jnp.zeros(st_ref.shape, F32)

    for b in range(u_ref.shape[0]):
        u = u_ref[b].astype(F32)
        q, k = u[:, :GROUP_W], u[:, GROUP_W:2 * GROUP_W]
        v, gate = u[:, 2 * GROUP_W:3 * GROUP_W], u[:, 3 * GROUP_W:]
        cos, sin = _expand_trig(tab_ref[b], ex_ref[...])

        def rope(t):
            t1, t2 = t[:, :LANES], t[:, LANES:]
            return jnp.concatenate([t1 * cos - t2 * sin, t2 * cos + t1 * sin], axis=-1)

        qr = rope(q)
        kr = rope(k) * (RET_DK ** -0.5)
        krb = kr.astype(BF16)
        vb = v.astype(BF16)
        state = st_ref[b]
        o = _dot(qr.astype(BF16), state.astype(BF16)) * qd_ref[...]
        for h in range(N_HEADS):
            qh = (qr * mq_ref[h]).astype(BF16)
            sc = _dot_nt(qh, krb) * inner_ref[h]
            o = o + _dot(sc.astype(BF16), vb) * mv_ref[h]
        st_ref[b] = state * cd_ref[...] + bm_ref[...] * _dot_tn((kr * kd_ref[...]).astype(BF16), vb)

        gm = gm_ref[...]
        o_hi = o.astype(BF16)
        o_lo = (o - o_hi.astype(F32)).astype(BF16)
        mu = _dot(o_hi, gm) + _dot(o_lo, gm)
        dlt = o - mu
        var = _dot((dlt * dlt).astype(BF16), gm)
        y = dlt * lax.rsqrt(var + EPS)
        y = gate * _sigmoid(gate) * y
        y_ref[b] = _rms_rows(y, g_ref[...]).astype(BF16)


def _ret_mixer(u, trig, expand, consts, g, chunk):
    bsz, s, _ = u.shape
    inner, qd, kd, cd, bm, gm, mq, mv = consts
    full = lambda shape: pl.BlockSpec(shape, lambda i: (0,) * len(shape))
    tok = lambda w: pl.BlockSpec((bsz, chunk, w), lambda i: (0, i, 0))
    return pl.pallas_call(
        _ret_kernel,
        grid=(s // chunk,),
        in_specs=[tok(U_RET), tok(LANES), full(expand.shape),
                  full(inner.shape), full(qd.shape), full(kd.shape), full(cd.shape),
                  full(bm.shape), full(gm.shape), full(mq.shape), full(mv.shape),
                  full((1, GROUP_W))],
        out_specs=tok(GROUP_W),
        out_shape=jax.ShapeDtypeStruct((bsz, s, GROUP_W), BF16),
        scratch_shapes=[pltpu.VMEM((bsz, GROUP_W, GROUP_W), F32)],
        compiler_params=_cparams(("arbitrary",)),
        name="ret_mixer",
    )(u, trig, expand, inner, qd, kd, cd, bm, gm, mq, mv, g)


def _ret_consts(chunk):
    nh = N_HEADS
    f32 = np.float32
    log_g = np.log(f32(1.0) - f32(2.0) ** (f32(-5.0) - np.arange(nh, dtype=f32)))
    idx = np.arange(chunk, dtype=f32)
    rel = idx[:, None] - idx[None, :]
    inner = np.where(rel >= 0, np.exp(log_g[:, None, None] * np.maximum(rel, 0.0)), 0.0).astype(f32)
    v_head = np.arange(GROUP_W) // HEAD_DIM
    q_head = (np.arange(GROUP_W) % LANES) // (RET_DK // 2)
    qd = np.exp(log_g[v_head][None, :] * (idx[:, None] + 1.0)).astype(f32)
    kd = np.exp(log_g[q_head][None, :] * (chunk - 1.0 - idx[:, None])).astype(f32)
    cd = np.exp(log_g[v_head] * chunk)[None, :].astype(f32)
    bm = (q_head[:, None] == v_head[None, :]).astype(f32)
    gm = jnp.asarray((v_head[:, None] == v_head[None, :]).astype(f32) / HEAD_DIM, BF16)
    mq = (q_head[None, :] == np.arange(nh)[:, None]).astype(f32)[:, None, :]
    mv = (v_head[None, :] == np.arange(nh)[:, None]).astype(f32)[:, None, :]
    return tuple(jnp.asarray(a) for a in (inner, qd, kd, cd, bm)) + (gm, jnp.asarray(mq), jnp.asarray(mv))


def _trig_expanders():
    r16, r32 = MLA_ROPE // 2, RET_DK // 2
    ret = np.zeros((LANES, 2 * LANES), np.float32)
    mla = np.zeros((LANES, 2 * LANES), np.float32)
    for j in range(r32):
        for h in range(N_HEADS):
            ret[r16 + j, h * r32 + j] = 1.0
            ret[N_FREQ + r16 + j, LANES + h * r32 + j] = 1.0
    for j in range(r16):
        for half, sign in ((0, -1.0), (1, 1.0)):
            lane = MLA_NOPE + half * r16 + j
            mla[j, lane] = 1.0
            mla[N_FREQ + j, LANES + lane] = sign
    mla[2 * N_FREQ, :MLA_NOPE] = 1.0
    return jnp.asarray(ret, BF16), jnp.asarray(mla, BF16)


def _mla_prep_kernel(u_ref, tab_ref, ex_ref, mu_ref, invu_ref, gu_ref, wbig_ref, sq_ref,
                     invq_ref, gq_ref, gqs_ref, gk_ref, onev_ref, q_ref, k_ref, v_ref):
    x = u_ref[0].astype(F32)
    ss = _dot((x * x).astype(BF16), mu_ref[...]) * invu_ref[...]
    xn = (x * lax.rsqrt(ss + EPS) * gu_ref[...]).astype(BF16)
    big = _dot(xn, wbig_ref[...])
    hw = N_HEADS * LANES
    q, qs, kn, v = big[:, :hw], big[:, hw:2 * hw], big[:, 2 * hw:3 * hw], big[:, 3 * hw:4 * hw]
    kr, krs = big[:, 4 * hw:4 * hw + LANES], big[:, 4 * hw + LANES:]
    cos, sin = _expand_trig(tab_ref[0], ex_ref[...])
    rq = lax.rsqrt(_dot((q * q).astype(BF16), sq_ref[...]) * invq_ref[...] + EPS)
    rk = lax.rsqrt(_dot((kn * kn).astype(BF16), sq_ref[...]) * invq_ref[...] + EPS)
    krot = kr * cos + krs * sin
    for h in range(N_HEADS):
        sl = slice(h * LANES, (h + 1) * LANES)
        qh = (q[:, sl] * gq_ref[...] * cos + qs[:, sl] * gqs_ref[...] * sin) * rq[:, sl]
        q_ref[0, h] = qh.astype(BF16)
        k_ref[0, h] = (kn[:, sl] * gk_ref[...] * rk[:, sl] + krot).astype(BF16)
        v_ref[0, h] = (v[:, sl] + onev_ref[...]).astype(BF16)


def _mla_prep(u, trig, expand, mu, invu, gu, wbig, sq, invq, gq, gqs, gk, onev, tm):
    bsz, s, _ = u.shape
    full = lambda a: pl.BlockSpec(a.shape, lambda b, i: (0,) * a.ndim)
    tab = pl.BlockSpec((1, tm, LANES), lambda b, i: (b, i, 0))
    out = jax.ShapeDtypeStruct((bsz, N_HEADS, s, LANES), BF16)
    ospec = pl.BlockSpec((1, N_HEADS, tm, LANES), lambda b, i: (b, 0, i, 0))
    return pl.pallas_call(
        _mla_prep_kernel,
        grid=(bsz, s // tm),
        in_specs=[pl.BlockSpec((1, tm, U_MLA), lambda b, i: (b, i, (U_COLS - U_MLA) // U_MLA)),
                  tab, full(expand), full(mu), full(invu), full(gu), full(wbig), full(sq), full(invq),
                  full(gq), full(gqs), full(gk), full(onev)],
        out_specs=[ospec, ospec, ospec],
        out_shape=[out, out, out],
        compiler_params=_cparams(("parallel", "parallel")),
        name="mla_prep",
    )(u, trig, expand, mu, invu, gu, wbig, sq, invq, gq, gqs, gk, onev)


def _flash_kernel(q_ref, k_ref, v_ref, o_ref, sa_ref, sb_ref, mca_ref, mcb_ref, m_ref, acc_ref, *, tq):
    qi = pl.program_id(2)
    q = q_ref[0, 0]
    bufs = ((sa_ref, mca_ref), (sb_ref, mcb_ref))
    m_ref[...] = jnp.full((tq, LANES), NEG_BIG, F32)
    acc_ref[...] = jnp.zeros((tq, LANES), F32)

    def scores(c, masked, dst):
        s_ref, mc_ref = dst
        start = pl.multiple_of(c * tq, tq)
        s = _dot_nt(q, k_ref[0, 0, pl.ds(start, tq), :])
        if masked:
            row = qi * tq + lax.broadcasted_iota(jnp.int32, (tq, tq), 0)
            col = start + lax.broadcasted_iota(jnp.int32, (tq, tq), 1)
            s = jnp.where(col <= row, s, NEG_BIG)
        s_ref[...] = s
        mc_ref[...] = jnp.broadcast_to(jnp.max(s, axis=-1, keepdims=True), (tq, LANES))

    def accumulate(c, src):
        s_ref, mc_ref = src
        start = pl.multiple_of(c * tq, tq)
        m_prev = m_ref[...]
        m_new = jnp.maximum(m_prev, mc_ref[...])
        alpha = jnp.exp2(m_prev - m_new)
        p = jnp.exp2(s_ref[...] - jnp.tile(m_new, (1, tq // LANES)))
        pv = _dot(p.astype(BF16), v_ref[0, 0, pl.ds(start, tq), :])
        acc_ref[...] = alpha * acc_ref[...] + pv
        m_ref[...] = m_new

    def by_parity(c, fn):
        for par in range(2):
            pl.when(c % 2 == par)(functools.partial(fn, par))

    def pipelined(c, masked, par):
        scores(c + 1, masked, bufs[1 - par])
        accumulate(c, bufs[par])

    scores(0, True, bufs[0])
    n_plain = jnp.maximum(qi - 1, 0)

    def two_steps(i, carry):
        pipelined(2 * i, False, 0)
        pipelined(2 * i + 1, False, 1)
        return carry

    lax.fori_loop(0, n_plain // 2, two_steps, 0)

    @pl.when(n_plain % 2 == 1)
    def _():
        pipelined(n_plain - 1, False, 0)

    @pl.when(qi >= 1)
    def _():
        by_parity(qi - 1, functools.partial(pipelined, qi - 1, True))

    by_parity(qi, lambda par: accumulate(qi, bufs[par]))

    acc = acc_ref[...]
    lane = lax.broadcasted_iota(jnp.int32, (tq, LANES), 1)
    denom = jnp.sum(jnp.where(lane == HEAD_DIM, acc, 0.0), axis=-1, keepdims=True)
    o_ref[0, 0] = jnp.where(lane < HEAD_DIM, acc / denom, 0.0).astype(BF16)


def _flash_attention(q, k, v, tq):
    bsz, nh, s, _ = q.shape
    kv_spec = pl.BlockSpec((1, 1, s, LANES), lambda b, h, i: (b, h, 0, 0))
    blk = pl.BlockSpec((1, 1, tq, LANES), lambda b, h, i: (b, h, i, 0))
    stat = pltpu.VMEM((tq, LANES), F32)
    return pl.pallas_call(
        functools.partial(_flash_kernel, tq=tq),
        grid=(bsz, nh, s // tq),
        in_specs=[blk, kv_spec, kv_spec],
        out_specs=blk,
        out_shape=jax.ShapeDtypeStruct((bsz, nh, s, LANES), BF16),
        scratch_shapes=[pltpu.VMEM((tq, tq), F32), pltpu.VMEM((tq, tq), F32), stat, stat, stat, stat],
        compiler_params=_cparams(("parallel", "parallel", "arbitrary")),
        name="flash_attention",
    )(q, k, v)


def _outproj_kernel(x_ref, yc_ref, om_ref, yr_ref, yl_ref, wc_ref, wm_ref, wr_ref, wl_ref,
                    gmla_ref, gt_ref, gf_ref, scf_ref, shf_ref, wrt_ref, brt_ref, tri_ref,
                    xo_ref, h_ref, meta_ref, cnt_ref, run_ref):
    @pl.when((pl.program_id(0) == 0) & (pl.program_id(1) == 0))
    def _():
        run_ref[...] = jnp.zeros((1, LANES), F32)

    om = [om_ref[0, h].astype(F32) for h in range(N_HEADS)]
    ssq = om[0] * om[0]
    for h in range(1, N_HEADS):
        ssq = ssq + om[h] * om[h]
    r_mla = lax.rsqrt(jnp.sum(ssq, axis=-1, keepdims=True) / GROUP_W + EPS)
    y = _dot(yc_ref[0], wc_ref[...]) + _dot(yr_ref[0], wr_ref[...]) + _dot(yl_ref[0], wl_ref[...])
    for h in range(0, N_HEADS, 2):
        pair = jnp.concatenate([(om[h] * r_mla * gmla_ref[h]).astype(BF16),
                                (om[h + 1] * r_mla * gmla_ref[h + 1]).astype(BF16)], axis=1)
        y = y + _dot(pair, wm_ref[h // 2])
    x = x_ref[0] + gt_ref[0] * y
    xo_ref[0] = x
    hf = _rms_rows(x, gf_ref[...]) * (1.0 + scf_ref[0]) + shf_ref[0]
    h_ref[0] = _pack_bf16_pairs(hf)

    h_hi = hf.astype(BF16)
    h_lo = (hf - h_hi.astype(F32)).astype(BF16)
    both = _dot(h_hi, wrt_ref[...])
    lg = both[:, :LANES] + both[:, LANES:] + _dot(h_lo, wrt_ref[:, :LANES])
    tm = lg.shape[0]
    lane = lax.broadcasted_iota(jnp.int32, (tm, LANES), 1)
    bias = brt_ref[...]
    is_g = (lane >= N_EXPERTS) & (lane < N_EXPERTS + MOE_GROUPS)
    is_e = lane < N_EXPERTS

    def first_argmax(val):
        mx = jnp.max(val, axis=-1, keepdims=True)
        return jnp.min(jnp.where(val == mx, lane, LANES), axis=-1, keepdims=True)

    gl = jnp.where(is_g, lg, NEG_BIG)
    ge = jnp.exp(gl - jnp.max(gl, axis=-1, keepdims=True))
    gp = ge / jnp.sum(ge, axis=-1, keepdims=True)
    g_idx = first_argmax(jnp.where(is_g, gp + bias, NEG_BIG))
    g_weight = jnp.sum(jnp.where(lane == g_idx, gp, 0.0), axis=-1, keepdims=True)
    in_group = is_e & ((lane // EXPERTS_PER_GROUP) == (g_idx - N_EXPERTS))
    el = jnp.where(in_group, lg, NEG_BIG)
    ee = jnp.exp(el - jnp.max(el, axis=-1, keepdims=True))
    ep = ee / jnp.sum(ee, axis=-1, keepdims=True)
    score = jnp.where(in_group, ep + bias, NEG_BIG)
    i1 = first_argmax(score)
    sel1 = lane == i1
    i2 = first_argmax(jnp.where(sel1, NEG_BIG, score))
    sel2 = lane == i2
    p1 = jnp.sum(jnp.where(sel1, ep, 0.0), axis=-1, keepdims=True)
    p2 = jnp.sum(jnp.where(sel2, ep, 0.0), axis=-1, keepdims=True)
    psum = p1 + p2
    w1 = p1 / psum * g_weight
    w2 = p2 / psum * g_weight

    onehot = jnp.where(sel1, 1.0, jnp.where(sel2, 1.0, 0.0)).astype(BF16)
    incl = _dot(tri_ref[...], onehot)
    base = run_ref[...] + incl - 1.0
    r1 = jnp.sum(jnp.where(sel1, base, 0.0), axis=-1, keepdims=True)
    r2 = jnp.sum(jnp.where(sel2, base, 0.0), axis=-1, keepdims=True)
    run_ref[...] = run_ref[...] + incl[tm - 1:tm, :]
    cnt_ref[0] = run_ref[...]
    fields = (i1.astype(F32), i2.astype(F32), r1, r2, w1, w2)
    meta = jnp.zeros((tm, LANES), F32)
    for pos, val in enumerate(fields):
        meta = jnp.where(lane == pos, val, meta)
    meta_ref[0] = meta


def _outproj(x, yc, om, yr, yl, wc, wm, wr, wl, gmla, gt, gf, scf, shf, wrt, brt, tm):
    bsz, s, d = x.shape
    nt = s // tm
    full = lambda a: pl.BlockSpec(a.shape, lambda b, i: (0,) * a.ndim)
    tok = lambda w: pl.BlockSpec((1, tm, w), lambda b, i: (b, i, 0))
    vec = pl.BlockSpec((1, 1, d), lambda b, i: (b, 0, 0))
    tri = jnp.asarray(np.tril(np.ones((tm, tm), np.float32)), BF16)
    return pl.pallas_call(
        _outproj_kernel,
        grid=(bsz, nt),
        in_specs=[tok(d), tok(GROUP_W),
                  pl.BlockSpec((1, N_HEADS, tm, LANES), lambda b, i: (b, 0, i, 0)),
                  tok(GROUP_W), tok(GROUP_W),
                  full(wc), full(wm), full(wr), full(wl), full(gmla), vec, full(gf), vec, vec,
                  full(wrt), full(brt), full(tri)],
        out_specs=[tok(d), tok(d // 2), tok(LANES),
                   pl.BlockSpec((1, 1, LANES), lambda b, i: (b * nt + i, 0, 0))],
        out_shape=[jax.ShapeDtypeStruct((bsz, s, d), F32), jax.ShapeDtypeStruct((bsz, s, d // 2), jnp.uint32),
                   jax.ShapeDtypeStruct((bsz, s, LANES), F32),
                   jax.ShapeDtypeStruct((bsz * nt, 1, LANES), F32)],
        scratch_shapes=[pltpu.VMEM((1, LANES), F32)],
        compiler_params=_cparams(("arbitrary", "arbitrary")),
        name="outproj_router",
    )(x, yc, om, yr, yl, wc, wm, wr, wl, gmla, gt, gf, scf, shf, wrt, brt, tri)


def _route_plan(meta, cnt, tmg):
    t = meta.shape[0] * meta.shape[1]
    m = meta.reshape(t, LANES)
    e = m[:, 0:2].astype(jnp.int32)
    r = m[:, 2:4].astype(jnp.int32)
    counts = cnt[-1, 0, :N_EXPERTS].astype(jnp.int32)
    padded = (counts + tmg - 1) // tmg * tmg
    ends = jnp.cumsum(padded)
    starts = ends - padded
    dest = jnp.take(starts, e) + r
    n_tiles = (2 * t) // tmg + N_EXPERTS
    tile_start = jnp.arange(n_tiles, dtype=jnp.int32) * tmg
    tile_exp = jnp.sum((ends[None, :] <= tile_start[:, None]).astype(jnp.int32), axis=1)
    tile_exp = jnp.minimum(tile_exp, N_EXPERTS - 1)
    n_used = (ends[-1:] // tmg).astype(jnp.int32)
    last_tile = jnp.maximum(ends - tmg, 0).astype(jnp.int32)
    return dest[:, 0], dest[:, 1], tile_exp, n_used, last_tile, padded.astype(jnp.int32)


def _dispatch_kernel(zs_ref, zv_ref, nu_ref, d1_ref, d2_ref, h_ref, xs_ref, zero_ref, sem, *, tmc, tmg):
    @pl.when(pl.program_id(0) == 0)
    def _():
        zero_ref[...] = jnp.zeros(zero_ref.shape, jnp.uint32)
        n_tiles = xs_ref.shape[0] // tmg

        def fill(start):
            return pltpu.make_async_copy(zero_ref, xs_ref.at[pl.ds(pl.multiple_of(start, tmg), tmg), :], sem)

        def fill_tail(j, carry, wait):
            cp = fill(j * tmg)
            cp.wait() if wait else cp.start()
            return carry

        for wait in (False, True):
            for e in range(N_EXPERTS):
                cp = fill(zs_ref[e])
                pl.when(zv_ref[e] > 0)(cp.wait if wait else cp.start)
            lax.fori_loop(nu_ref[0], n_tiles, functools.partial(fill_tail, wait=wait), 0)

    def push(r, carry):
        src = h_ref.at[pl.ds(r, 1), :]
        pltpu.make_async_copy(src, xs_ref.at[pl.ds(d1_ref[0, 0, r], 1), :], sem).start(priority=0)
        pltpu.make_async_copy(src, xs_ref.at[pl.ds(d2_ref[0, 0, r], 1), :], sem).start(priority=1)
        return carry

    lax.fori_loop(0, tmc, push, 0, unroll=8)
    one_row = pltpu.make_async_copy(h_ref.at[pl.ds(0, 1), :], xs_ref.at[pl.ds(0, 1), :], sem)
    for _ in range(2 * tmc):
        one_row.wait()


def _dispatch(hp, d1, d2, last_tile, padded, n_used, tmc, tmg):
    t, dw = hp.shape
    n_rows = 2 * t + N_EXPERTS * tmg
    smem_rows = pl.BlockSpec((1, 1, tmc), lambda i, zs, zv, nu: (i, 0, 0), memory_space=pltpu.SMEM)
    return pl.pallas_call(
        functools.partial(_dispatch_kernel, tmc=tmc, tmg=tmg),
        grid_spec=pltpu.PrefetchScalarGridSpec(
            num_scalar_prefetch=3, grid=(t // tmc,),
            in_specs=[smem_rows, smem_rows, pl.BlockSpec((tmc, dw), lambda i, zs, zv, nu: (i, 0))],
            out_specs=pl.BlockSpec(memory_space=pl.ANY),
            scratch_shapes=[pltpu.VMEM((tmg, dw), jnp.uint32), pltpu.SemaphoreType.DMA(())]),
        out_shape=jax.ShapeDtypeStruct((n_rows, dw), jnp.uint32),
        compiler_params=_cparams(("arbitrary",)),
        name="moe_dispatch",
    )(last_tile, padded, n_used, d1.reshape(t // tmc, 1, tmc), d2.reshape(t // tmc, 1, tmc), hp)


def _experts_kernel(te_ref, nu_ref, xs_ref, wg_ref, wu_ref, wd_ref, y_ref, wgb_ref, wub_ref, wdb_ref):
    j = pl.program_id(0)
    used = j < nu_ref[0]
    tile = jnp.minimum(j, nu_ref[0] - 1)

    @pl.when((j == 0) | (te_ref[tile] != te_ref[jnp.maximum(tile - 1, 0)]))
    def _():
        wgb_ref[...] = wg_ref[0, 0].astype(BF16)
        wub_ref[...] = wu_ref[0, 0].astype(BF16)
        wdb_ref[...] = wd_ref[0, 0].astype(BF16)

    @pl.when(used)
    def _():
        x = _unpack_bf16_pairs(xs_ref[...]).astype(BF16)
        gate = _dot(x, wgb_ref[...])
        hid = gate * _sigmoid(gate) * _dot(x, wub_ref[...])
        y_ref[...] = _pack_bf16_pairs(_dot(hid.astype(BF16), wdb_ref[...]))

    @pl.when(jnp.logical_not(used))
    def _():
        y_ref[...] = jnp.zeros(y_ref.shape, jnp.uint32)


def _experts(xs, tile_exp, n_used, layer, wg, wu, wd, tmg):
    n_rows, dw = xs.shape
    d = 2 * dw
    tile = lambda i, te, nu: jnp.minimum(i, nu[0] - 1)
    rows = pl.BlockSpec((tmg, dw), lambda i, te, nu: (tile(i, te, nu), 0))
    wspec = lambda shape: pl.BlockSpec((1, 1) + shape,
                                       lambda i, te, nu: (layer, te[tile(i, te, nu)], 0, 0))
    return pl.pallas_call(
        _experts_kernel,
        grid_spec=pltpu.PrefetchScalarGridSpec(
            num_scalar_prefetch=2, grid=(n_rows // tmg,),
            in_specs=[rows, wspec((d, D_EXPERT)), wspec((d, D_EXPERT)), wspec((D_EXPERT, d))],
            out_specs=pl.BlockSpec((tmg, dw), lambda i, te, nu: (i, 0)),
            scratch_shapes=[pltpu.VMEM((d, D_EXPERT), BF16), pltpu.VMEM((d, D_EXPERT), BF16),
                            pltpu.VMEM((D_EXPERT, d), BF16)]),
        out_shape=jax.ShapeDtypeStruct((n_rows, dw), jnp.uint32),
        compiler_params=_cparams(("arbitrary",)),
        name="moe_experts",
    )(tile_exp, n_used, xs, wg, wu, wd)


def _layer_weights(l, w_in, mla_q_norm_g, mla_w_uq, mla_kv_norm_g, mla_w_ukv, mla_q_qk_g,
                   mla_k_qk_g, lru_w_a, lru_w_x, mix_norm_g, w_out, router_group_w,
                   router_group_b, router_expert_w, router_expert_b):
    half = RET_DK // 2
    perm = np.concatenate([np.arange(half) + HEAD_DIM * h for h in range(N_HEADS)]
                          + [np.arange(half) + half + HEAD_DIM * h for h in range(N_HEADS)])
    w = w_in[l]
    o_mla, o_ret, o_lru = U_CONV, U_CONV + 352, U_CONV + 352 + U_RET
    w_ret = w[:, o_ret:o_ret + U_RET]
    w_ret = jnp.concatenate([w_ret[:, perm], w_ret[:, GROUP_W + perm], w_ret[:, 2 * GROUP_W:]], axis=1)
    w_all = jnp.concatenate([w_ret, w[:, o_lru:o_lru + U_LRU], w[:, :U_CONV], w[:, o_mla:o_mla + 352],
                             jnp.zeros((D_MODEL, U_MLA - 352), F32)], axis=1).astype(BF16)

    hw = N_HEADS * LANES
    r16 = MLA_ROPE // 2
    wq = mla_w_uq[l].reshape(Q_LORA, N_HEADS, MLA_QK)
    zq = jnp.zeros((Q_LORA, N_HEADS, LANES - MLA_QK), F32)
    q_cols = jnp.concatenate([wq, zq], axis=2).reshape(Q_LORA, hw)
    wq_sw = jnp.concatenate([jnp.zeros((Q_LORA, N_HEADS, MLA_NOPE), F32), wq[:, :, MLA_NOPE + r16:],
                             wq[:, :, MLA_NOPE:MLA_NOPE + r16], zq], axis=2).reshape(Q_LORA, hw)
    wkv = mla_w_ukv[l].reshape(KV_LORA, N_HEADS, MLA_NOPE + HEAD_DIM)
    zk = jnp.zeros((KV_LORA, N_HEADS, LANES - MLA_NOPE), F32)
    k_cols = jnp.concatenate([wkv[:, :, :MLA_NOPE], zk], axis=2).reshape(KV_LORA, hw)
    v_cols = jnp.concatenate([wkv[:, :, MLA_NOPE:], zk], axis=2).reshape(KV_LORA, hw)
    eye = jnp.eye(MLA_ROPE, dtype=F32)
    place = jnp.concatenate([jnp.zeros((MLA_ROPE, MLA_NOPE), F32), eye,
                             jnp.zeros((MLA_ROPE, LANES - MLA_QK), F32)], axis=1)
    eye_sw = jnp.concatenate([eye[:, r16:], eye[:, :r16]], axis=1)
    place_sw = jnp.concatenate([jnp.zeros((MLA_ROPE, MLA_NOPE), F32), eye_sw,
                                jnp.zeros((MLA_ROPE, LANES - MLA_QK), F32)], axis=1)
    n_big = 4 * hw + 2 * LANES
    wbig = jnp.zeros((U_MLA, n_big), F32)
    wbig = wbig.at[:Q_LORA, :hw].set(q_cols).at[:Q_LORA, hw:2 * hw].set(wq_sw)
    wbig = wbig.at[Q_LORA:Q_LORA + KV_LORA, 2 * hw:3 * hw].set(k_cols)
    wbig = wbig.at[Q_LORA:Q_LORA + KV_LORA, 3 * hw:4 * hw].set(v_cols)
    wbig = wbig.at[Q_LORA + KV_LORA:352, 4 * hw:4 * hw + LANES].set(place)
    wbig = wbig.at[Q_LORA + KV_LORA:352, 4 * hw + LANES:].set(place_sw)
    wbig = wbig.astype(BF16)

    gu = jnp.concatenate([mla_q_norm_g[l], mla_kv_norm_g[l], mla_k_qk_g[l][MLA_NOPE:],
                          jnp.zeros((U_MLA - 352,), F32)])[None, :]
    qscale = (MLA_QK ** -0.5) * math.log2(math.e)
    gq_full = mla_q_qk_g[l]
    pad = jnp.zeros((LANES - MLA_QK,), F32)
    gq = (jnp.concatenate([gq_full, pad]) * qscale)[None, :]
    gqs = (jnp.concatenate([jnp.zeros((MLA_NOPE,), F32), gq_full[MLA_NOPE + r16:],
                            gq_full[MLA_NOPE:MLA_NOPE + r16], pad]) * qscale)[None, :]
    gk = jnp.concatenate([mla_k_qk_g[l][:MLA_NOPE], jnp.zeros((LANES - MLA_NOPE,), F32)])[None, :]

    def blockdiag(wb):
        out = jnp.zeros((GROUP_W, GROUP_W), F32)
        for n in range(wb.shape[0]):
            out = out.at[n * HEAD_DIM:(n + 1) * HEAD_DIM, n * HEAD_DIM:(n + 1) * HEAD_DIM].set(wb[n])
        return out.astype(BF16)

    gmix = mix_norm_g[l]
    wo = w_out[l].astype(BF16)
    wm = wo[GROUP_W:2 * GROUP_W].reshape(N_HEADS, HEAD_DIM, D_MODEL)
    wm = jnp.concatenate([wm, jnp.zeros((N_HEADS, LANES - HEAD_DIM, D_MODEL), BF16)], axis=1)
    wm = wm.reshape(N_HEADS // 2, 2 * LANES, D_MODEL)
    gmla = jnp.concatenate([gmix[GROUP_W:2 * GROUP_W].reshape(N_HEADS, 1, HEAD_DIM),
                            jnp.zeros((N_HEADS, 1, LANES - HEAD_DIM), F32)], axis=2)
    wrt = jnp.concatenate([router_expert_w[l], router_group_w[l],
                           jnp.zeros((D_MODEL, LANES - N_EXPERTS - MOE_GROUPS), F32)], axis=1)
    wrt_hi = wrt.astype(BF16)
    wrt = jnp.concatenate([wrt_hi, (wrt - wrt_hi.astype(F32)).astype(BF16)], axis=1)
    brt = jnp.concatenate([router_expert_b[l], router_group_b[l],
                           jnp.zeros((LANES - N_EXPERTS - MOE_GROUPS,), F32)])[None, :]
    return dict(w_all=w_all, wbig=wbig, gu=gu, gq=gq, gqs=gqs, gk=gk,
                wa=blockdiag(lru_w_a[l]), wx=blockdiag(lru_w_x[l]),
                g_conv=gmix[None, :GROUP_W], g_ret=gmix[None, 2 * GROUP_W:3 * GROUP_W],
                g_lru=gmix[None, 3 * GROUP_W:], gmla=gmla,
                wc=wo[:GROUP_W], wm=wm, wr=wo[2 * GROUP_W:3 * GROUP_W], wl=wo[3 * GROUP_W:],
                wrt=wrt, brt=brt)


def _mla_consts():
    seg_u = np.concatenate([np.zeros(Q_LORA), np.ones(KV_LORA), 2 * np.ones(MLA_ROPE),
                            3 * np.ones(U_MLA - 352)])
    mu = jnp.asarray(seg_u[:, None] == seg_u[None, :], BF16)
    invu = jnp.asarray(np.concatenate([np.full(Q_LORA, 1.0 / Q_LORA), np.full(KV_LORA, 1.0 / KV_LORA),
                                       np.full(MLA_ROPE, 1.0 / MLA_ROPE), np.ones(U_MLA - 352)]), F32)[None, :]
    lane = np.arange(N_HEADS * LANES)
    seg_q = (lane // LANES) * 3 + np.where(lane % LANES < MLA_NOPE, 0, np.where(lane % LANES < MLA_QK, 1, 2))
    sq = jnp.asarray(seg_q[:, None] == seg_q[None, :], BF16)
    inv_head = np.concatenate([np.full(MLA_NOPE, 1.0 / MLA_NOPE), np.full(MLA_ROPE, 1.0 / MLA_ROPE),
                               np.ones(LANES - MLA_QK)])
    invq = jnp.asarray(np.tile(inv_head, N_HEADS), F32)[None, :]
    onev = jnp.asarray((np.arange(LANES) == HEAD_DIM).astype(np.float32))[None, :]
    return mu, invu, sq, invq, onev


def kernel(x, c, positions, ada_w, ada_b, norm_mix_g, w_in, conv_w, mla_q_norm_g, mla_w_uq, mla_kv_norm_g, mla_w_ukv, mla_q_qk_g, mla_k_qk_g, lru_conv_w, lru_conv_b, lru_w_a, lru_b_a, lru_w_x, lru_b_x, lru_lambda, mix_norm_g, w_out, norm_ffn_g, router_group_w, router_group_b, router_expert_w, router_expert_b, exp_w_gate, exp_w_up, exp_w_down):
    bsz, s, d = x.shape
    depth = ada_w.shape[0]
    tm = min(512, s)
    chunk = min(256, s)
    tq = min(1024, s)
    tmc = min(512, s)
    tmg = 512

    inv = jnp.concatenate([1.0 / (ROPE_BASE ** (jnp.arange(0, MLA_ROPE, 2, dtype=F32) / MLA_ROPE)),
                           1.0 / (ROPE_BASE ** (jnp.arange(0, RET_DK, 2, dtype=F32) / RET_DK))])[:, None]
    trig = _rope_tables(positions, inv)
    ex_ret, ex_mla = _trig_expanders()

    c_pad = jnp.concatenate([c, jnp.zeros((8 - bsz, d), F32)], axis=0)
    mod = _modulation(c_pad, ada_w, ada_b)[:, :bsz]
    ret_consts = _ret_consts(chunk)
    mu, invu, sq, invq, onev = _mla_consts()

    pending = None
    for l in range(depth):
        sh_m, sc_m, gt_m, sh_f, sc_f, gt_f = [m[:, None, :] for m in jnp.split(mod[l], 6, axis=-1)]
        lw = _layer_weights(l, w_in, mla_q_norm_g, mla_w_uq, mla_kv_norm_g, mla_w_ukv, mla_q_qk_g,
                            mla_k_qk_g, lru_w_a, lru_w_x, mix_norm_g, w_out, router_group_w,
                            router_group_b, router_expert_w, router_expert_b)
        if pending is None:
            u = _inproj(x, norm_mix_g[l][None, :], sc_m, sh_m, lw["w_all"], tm)
        else:
            x, u = _combine(x, *pending, tm, proj=(norm_mix_g[l][None, :], sc_m, sh_m, lw["w_all"]))
        y_conv = _conv_mixer(u, conv_w[l], lw["g_conv"], tm)
        y_lru = _lru_mixer(u, lru_conv_w[l], lru_conv_b[l][None, :], lw["wa"], lru_b_a[l][None, :],
                           lw["wx"], lru_b_x[l][None, :], lru_lambda[l][None, :], lw["g_lru"], tm)
        y_ret = _ret_mixer(u, trig, ex_ret, ret_consts, lw["g_ret"], chunk)
        q, k, v = _mla_prep(u, trig, ex_mla, mu, invu, lw["gu"], lw["wbig"], sq, invq,
                            lw["gq"], lw["gqs"], lw["gk"], onev, tm)
        o_mla = _flash_attention(q, k, v, tq)
        x, hp, meta, cnt = _outproj(x, y_conv, o_mla, y_ret, y_lru, lw["wc"], lw["wm"], lw["wr"],
                                    lw["wl"], lw["gmla"], gt_m, norm_ffn_g[l][None, :], sc_f, sh_f,
                                    lw["wrt"], lw["brt"], tm)
        d1, d2, tile_exp, n_used, last_tile, padded = _route_plan(meta, cnt, tmg)
        xs = _dispatch(hp.reshape(bsz * s, d // 2), d1, d2, last_tile, padded, n_used, tmc, tmg)
        ys = _experts(xs, tile_exp, n_used, l, exp_w_gate, exp_w_up, exp_w_down, tmg)
        pending = (meta, gt_f, ys, d1, d2)
    return _combine(x, *pending, tmc)[0]
```

```python
import functools
import math

import jax
import jax.numpy as jnp
import numpy as np
from jax import lax
from jax.experimental import pallas as pl
from jax.experimental.pallas import tpu as pltpu

F32 = jnp.float32
BF16 = jnp.bfloat16
HIGHEST = lax.Precision.HIGHEST

D_MODEL = 1024
GROUP_W = 256
HEAD_DIM = 64
N_HEADS = 4
MLA_NOPE = 64
MLA_ROPE = 32
MLA_QK = 96
Q_LORA = 192
KV_LORA = 128
RET_DK = 64
LRU_C = 8.0
MOE_GROUPS = 4
EXPERTS_PER_GROUP = 8
N_EXPERTS = 32
D_EXPERT = 256
ROPE_BASE = 10000.0
EPS = 1e-6

LANES = 128
SUBLANES = 8
MXU_DIM = 256
U_RET, U_LRU, U_CONV, U_MLA = 1024, 512, 768, 384
U_COLS = U_RET + U_LRU + U_CONV + U_MLA
N_FREQ = MLA_ROPE // 2 + RET_DK // 2
NEG_BIG = -1e30
VMEM_LIMIT = 56 * 1024 * 1024


def _cparams(sem):
    return pltpu.CompilerParams(dimension_semantics=sem, vmem_limit_bytes=VMEM_LIMIT)


def _dot(a, b):
    return jnp.dot(a, b, preferred_element_type=F32)


def _dot_nt(a, b):
    return lax.dot_general(a, b, (((1,), (1,)), ((), ())), preferred_element_type=F32)


def _dot_tn(a, b):
    return lax.dot_general(a, b, (((0,), (0,)), ((), ())), preferred_element_type=F32)


def _rms_rows(y, g):
    return y * lax.rsqrt(jnp.mean(y * y, axis=-1, keepdims=True) + EPS) * g


def _sigmoid(x):
    return 0.5 * jnp.tanh(0.5 * x) + 0.5


def _pack_bf16_pairs(a):
    k = a.shape[1] // 2
    rounded = a.astype(BF16).astype(F32)
    lo = lax.bitcast_convert_type(rounded[:, :k], jnp.uint32) >> 16
    hi = lax.bitcast_convert_type(rounded[:, k:], jnp.uint32) & jnp.uint32(0xFFFF0000)
    return lo | hi


def _unpack_bf16_pairs(w):
    lo = lax.bitcast_convert_type(w << 16, F32)
    hi = lax.bitcast_convert_type(w & jnp.uint32(0xFFFF0000), F32)
    return jnp.concatenate([lo, hi], axis=1)


def _rope_kernel(pos_ref, inv_ref, tab_ref):
    ang = pos_ref[0].astype(F32) * inv_ref[...]
    row = lax.broadcasted_iota(jnp.int32, (LANES - 2 * N_FREQ, ang.shape[1]), 0)
    pad = jnp.where(row == 0, 1.0, 0.0)
    tab_ref[0] = jnp.concatenate([jnp.cos(ang), jnp.sin(ang), pad], axis=0).T


def _rope_tables(positions, inv):
    bsz, s = positions.shape
    ts = min(s, 2048)
    return pl.pallas_call(
        _rope_kernel,
        grid=(bsz, s // ts),
        in_specs=[pl.BlockSpec((1, 1, ts), lambda b, i: (b, 0, i)),
                  pl.BlockSpec((N_FREQ, 1), lambda b, i: (0, 0))],
        out_specs=pl.BlockSpec((1, ts, LANES), lambda b, i: (b, i, 0)),
        out_shape=jax.ShapeDtypeStruct((bsz, s, LANES), F32),
        compiler_params=_cparams(("parallel", "parallel")),
        name="rope_tables",
    )(positions.reshape(bsz, 1, s), inv)


def _expand_trig(tab, expand):
    hi = tab.astype(BF16)
    lo = (tab - hi.astype(F32)).astype(BF16)
    trig = _dot(hi, expand) + _dot(lo, expand)
    return trig[:, :LANES], trig[:, LANES:]


def _mod_kernel(c_ref, w_ref, b_ref, o_ref):
    c = c_ref[...]
    ca = c * _sigmoid(c)
    o_ref[0] = jnp.dot(ca, w_ref[0], precision=HIGHEST, preferred_element_type=F32) + b_ref[0]


def _modulation(c_pad, ada_w, ada_b):
    nl, d, n = ada_w.shape
    tn = 1536
    return pl.pallas_call(
        _mod_kernel,
        grid=(nl, n // tn),
        in_specs=[pl.BlockSpec((8, d), lambda l, j: (0, 0)),
                  pl.BlockSpec((1, d, tn), lambda l, j: (l, 0, j)),
                  pl.BlockSpec((1, 1, tn), lambda l, j: (l, 0, j))],
        out_specs=pl.BlockSpec((1, 8, tn), lambda l, j: (l, 0, j)),
        out_shape=jax.ShapeDtypeStruct((nl, 8, n), F32),
        compiler_params=_cparams(("parallel", "parallel")),
        name="adaln_mod",
    )(c_pad, ada_w, ada_b.reshape(nl, 1, n))


def _norm_project(x, g_ref, sc_ref, sh_ref, w_ref):
    h = _rms_rows(x, g_ref[...]) * (1.0 + sc_ref[0]) + sh_ref[0]
    return _dot(h.astype(BF16), w_ref[...]).astype(BF16)


def _inproj_kernel(x_ref, g_ref, sc_ref, sh_ref, w_ref, u_ref):
    u_ref[0] = _norm_project(x_ref[0], g_ref, sc_ref, sh_ref, w_ref)


def _router_weights(meta):
    lane = lax.broadcasted_iota(jnp.int32, meta.shape, 1)
    w1 = jnp.sum(jnp.where(lane == 4, meta, 0.0), axis=-1, keepdims=True)
    w2 = jnp.sum(jnp.where(lane == 5, meta, 0.0), axis=-1, keepdims=True)
    return w1, w2


def _combine_kernel(d1c_ref, d2c_ref, d1n_ref, d2n_ref, x_ref, meta_ref, gtf_ref, y_hbm, *rest, tm, project):
    if project:
        g_ref, sc_ref, sh_ref, w_ref, xo_ref, u_ref, *scratch = rest
    else:
        xo_ref, *scratch = rest
    a1_ref, a2_ref, b1_ref, b2_ref, sem_a, sem_b = scratch
    n = pl.program_id(0)
    bufs = ((a1_ref, a2_ref, sem_a), (b1_ref, b2_ref, sem_b))

    def pull(d_ref, buf, sem, r):
        return pltpu.make_async_copy(y_hbm.at[pl.ds(d_ref[0, 0, r], 1), :], buf.at[pl.ds(r, 1), :], sem)

    def wait_tile(buf, sem):
        one_row = pltpu.make_async_copy(y_hbm.at[pl.ds(0, 1), :], buf.at[pl.ds(0, 1), :], sem)
        for _ in range(2 * tm):
            one_row.wait()

    @pl.when(n == 0)
    def _():
        def first(r, carry):
            pull(d1c_ref, a1_ref, sem_a, r).start(priority=0)
            pull(d2c_ref, a2_ref, sem_a, r).start(priority=1)
            return carry
        lax.fori_loop(0, tm, first, 0, unroll=8)

    def step(par):
        y1_ref, y2_ref, sem = bufs[par]
        n1_ref, n2_ref, nsem = bufs[1 - par]
        wait_tile(y1_ref, sem)
        for r in range(tm):
            pull(d1n_ref, n1_ref, nsem, r).start(priority=0)
            pull(d2n_ref, n2_ref, nsem, r).start(priority=1)
        w1, w2 = _router_weights(meta_ref[0])
        y = w1 * _unpack_bf16_pairs(y1_ref[...]) + w2 * _unpack_bf16_pairs(y2_ref[...])
        x = x_ref[0] + gtf_ref[0] * y
        xo_ref[0] = x
        if project:
            u_ref[0] = _norm_project(x, g_ref, sc_ref, sh_ref, w_ref)

        @pl.when(n == pl.num_programs(0) - 1)
        def _():
            wait_tile(n1_ref, nsem)

    for par in range(2):
        pl.when(n % 2 == par)(functools.partial(step, par))


def _combine(x, meta, gtf, ys, d1, d2, tm, proj=None):
    bsz, s, d = x.shape
    nt = s // tm
    n_steps = bsz * nt
    nxt = lambda n: jnp.minimum(n + 1, n_steps - 1)
    smem = lambda f: pl.BlockSpec((1, 1, tm), lambda n: (f(n), 0, 0), memory_space=pltpu.SMEM)
    tok = lambda wdt: pl.BlockSpec((1, tm, wdt), lambda n: (n // nt, n % nt, 0))
    vec = pl.BlockSpec((1, 1, d), lambda n: (n // nt, 0, 0))
    buf = pltpu.VMEM((tm, d // 2), jnp.uint32)
    dd1, dd2 = d1.reshape(n_steps, 1, tm), d2.reshape(n_steps, 1, tm)
    in_specs = [smem(lambda n: n), smem(lambda n: n), smem(nxt), smem(nxt),
                tok(d), tok(LANES), vec, pl.BlockSpec(memory_space=pl.ANY)]
    out_specs = [tok(d)]
    out_shape = [jax.ShapeDtypeStruct((bsz, s, d), F32)]
    args = (dd1, dd2, dd1, dd2, x, meta, gtf, ys)
    if proj is not None:
        in_specs += [pl.BlockSpec((1, d), lambda n: (0, 0)), vec, vec, pl.BlockSpec((d, U_COLS), lambda n: (0, 0))]
        out_specs.append(tok(U_COLS))
        out_shape.append(jax.ShapeDtypeStruct((bsz, s, U_COLS), BF16))
        args += tuple(proj)
    return pl.pallas_call(
        functools.partial(_combine_kernel, tm=tm, project=proj is not None),
        grid=(n_steps,),
        in_specs=in_specs,
        out_specs=out_specs,
        out_shape=out_shape,
        scratch_shapes=[buf, buf, buf, buf, pltpu.SemaphoreType.DMA(()), pltpu.SemaphoreType.DMA(())],
        compiler_params=_cparams(("arbitrary",)),
        name="moe_combine_inproj" if proj is not None else "moe_combine",
    )(*args)


def _inproj(x, g, sc, sh, w, tm):
    bsz, s, d = x.shape
    vec = pl.BlockSpec((1, 1, d), lambda b, i: (b, 0, 0))
    return pl.pallas_call(
        _inproj_kernel,
        grid=(bsz, s // tm),
        in_specs=[pl.BlockSpec((1, tm, d), lambda b, i: (b, i, 0)),
                  pl.BlockSpec((1, d), lambda b, i: (0, 0)),
                  vec, vec,
                  pl.BlockSpec((d, U_COLS), lambda b, i: (0, 0))],
        out_specs=pl.BlockSpec((1, tm, U_COLS), lambda b, i: (b, i, 0)),
        out_shape=jax.ShapeDtypeStruct((bsz, s, U_COLS), BF16),
        compiler_params=_cparams(("parallel", "parallel")),
        name="inproj",
    )(x, g, sc, sh, w)


def _conv_kernel(u_ref, w_ref, g_ref, y_ref, buf_ref, *, tm):
    @pl.when(pl.program_id(1) == 0)
    def _():
        buf_ref[0:8, :] = jnp.zeros((8, GROUP_W), F32)

    u = u_ref[0].astype(F32)
    b_gate, c_gate, xin = u[:, :GROUP_W], u[:, GROUP_W:2 * GROUP_W], u[:, 2 * GROUP_W:]
    cx = c_gate * xin
    buf_ref[8:8 + tm, :] = cx
    conv = (w_ref[2:3, :] * cx + w_ref[1:2, :] * buf_ref[7:7 + tm, :]
            + w_ref[0:1, :] * buf_ref[6:6 + tm, :])
    buf_ref[0:8, :] = cx[tm - 8:, :]
    y_ref[0] = _rms_rows(b_gate * conv, g_ref[...]).astype(BF16)


def _conv_mixer(u, w, g, tm):
    bsz, s, _ = u.shape
    return pl.pallas_call(
        functools.partial(_conv_kernel, tm=tm),
        grid=(bsz, s // tm),
        in_specs=[pl.BlockSpec((1, tm, U_CONV), lambda b, i: (b, i, (U_RET + U_LRU) // U_CONV)),
                  pl.BlockSpec((3, GROUP_W), lambda b, i: (0, 0)),
                  pl.BlockSpec((1, GROUP_W), lambda b, i: (0, 0))],
        out_specs=pl.BlockSpec((1, tm, GROUP_W), lambda b, i: (b, i, 0)),
        out_shape=jax.ShapeDtypeStruct((bsz, s, GROUP_W), BF16),
        scratch_shapes=[pltpu.VMEM((tm + 8, GROUP_W), F32)],
        compiler_params=_cparams(("parallel", "arbitrary")),
        name="conv_mixer",
    )(u, w, g)


def _lru_kernel(u_ref, cw_ref, cb_ref, wa_ref, ba_ref, wx_ref, bx_ref, lam_ref, g_ref,
                y_ref, buf_ref, h_ref, *, tm):
    @pl.when(pl.program_id(1) == 0)
    def _():
        buf_ref[0:8, :] = jnp.zeros((8, GROUP_W), F32)
        h_ref[...] = jnp.zeros((1, GROUP_W), F32)

    u = u_ref[0].astype(F32)
    xraw, gate = u[:, :GROUP_W], u[:, GROUP_W:]
    buf_ref[8:8 + tm, :] = xraw
    xb = (cw_ref[3:4, :] * xraw + cw_ref[2:3, :] * buf_ref[7:7 + tm, :]
          + cw_ref[1:2, :] * buf_ref[6:6 + tm, :] + cw_ref[0:1, :] * buf_ref[5:5 + tm, :]
          + cb_ref[...])
    buf_ref[0:8, :] = xraw[tm - 8:, :]

    xbb = xb.astype(BF16)
    r = _sigmoid(_dot(xbb, wa_ref[...]) + ba_ref[...])
    i = _sigmoid(_dot(xbb, wx_ref[...]) + bx_ref[...])
    nlam = -lam_ref[...]
    softplus = jnp.maximum(nlam, 0.0) + jnp.log(1.0 + jnp.exp(-jnp.abs(nlam)))
    log_a = (-LRU_C) * r * softplus
    a = jnp.exp(log_a)
    b = jnp.sqrt(1.0 - a * a) * (i * xb)

    n_groups = tm // SUBLANES
    a = a.reshape(n_groups, SUBLANES, GROUP_W)
    b = b.reshape(n_groups, SUBLANES, GROUP_W)
    sub = lax.broadcasted_iota(jnp.int32, a.shape, 1)
    d = 1
    while d < SUBLANES:
        keep = sub >= d
        a_sh = jnp.where(keep, pltpu.roll(a, d, 1), 1.0)
        b_sh = jnp.where(keep, pltpu.roll(b, d, 1), 0.0)
        b = a * b_sh + b
        a = a * a_sh
        d *= 2
    carry = h_ref[...]
    groups = []
    for g in range(n_groups):
        hg = a[g] * carry + b[g]
        carry = hg[SUBLANES - 1:, :]
        groups.append(hg)
    h = jnp.concatenate(groups, axis=0)
    h_ref[...] = carry

    gelu = 0.5 * gate * (1.0 + jnp.tanh(math.sqrt(2.0 / math.pi) * (gate + 0.044715 * gate * gate * gate)))
    y_ref[0] = _rms_rows(h * gelu, g_ref[...]).astype(BF16)


def _lru_mixer(u, cw, cb, wa, ba, wx, bx, lam, g, tm):
    bsz, s, _ = u.shape
    row = pl.BlockSpec((1, GROUP_W), lambda b, i: (0, 0))
    mat = pl.BlockSpec((GROUP_W, GROUP_W), lambda b, i: (0, 0))
    return pl.pallas_call(
        functools.partial(_lru_kernel, tm=tm),
        grid=(bsz, s // tm),
        in_specs=[pl.BlockSpec((1, tm, U_LRU), lambda b, i: (b, i, U_RET // U_LRU)),
                  pl.BlockSpec((4, GROUP_W), lambda b, i: (0, 0)),
                  row, mat, row, mat, row, row, row],
        out_specs=pl.BlockSpec((1, tm, GROUP_W), lambda b, i: (b, i, 0)),
        out_shape=jax.ShapeDtypeStruct((bsz, s, GROUP_W), BF16),
        scratch_shapes=[pltpu.VMEM((tm + 8, GROUP_W), F32), pltpu.VMEM((1, GROUP_W), F32)],
        compiler_params=_cparams(("parallel", "arbitrary")),
        name="lru_mixer",
    )(u, cw, cb, wa, ba, wx, bx, lam, g)


def _ret_kernel(u_ref, tab_ref, ex_ref, inner_ref, qd_ref, kd_ref, cd_ref, bm_ref, gm_ref,
                mq_ref, mv_ref, g_ref, y_ref, st_ref):
    @pl.when(pl.program_id(0) == 0)
    def _():
        st_ref[...] = jnp.zeros(st_ref.shape, F32)

    for b in range(u_ref.shape[0]):
        u = u_ref[b].astype(F32)
        q, k = u[:, :GROUP_W], u[:, GROUP_W:2 * GROUP_W]
        v, gate = u[:, 2 * GROUP_W:3 * GROUP_W], u[:, 3 * GROUP_W:]
        cos, sin = _expand_trig(tab_ref[b], ex_ref[...])

        def rope(t):
            t1, t2 = t[:, :LANES], t[:, LANES:]
            return jnp.concatenate([t1 * cos - t2 * sin, t2 * cos + t1 * sin], axis=-1)

        qr = rope(q)
        kr = rope(k) * (RET_DK ** -0.5)
        krb = kr.astype(BF16)
        vb = v.astype(BF16)
        state = st_ref[b]
        o = _dot(qr.astype(BF16), state.astype(BF16)) * qd_ref[...]
        for h in range(N_HEADS):
            qh = (qr * mq_ref[h]).astype(BF16)
            sc = _dot_nt(qh, krb) * inner_ref[h]
            o = o + _dot(sc.astype(BF16), vb) * mv_ref[h]
        st_ref[b] = state * cd_ref[...] + bm_ref[...] * _dot_tn((kr * kd_ref[...]).astype(BF16), vb)

        gm = gm_ref[...]
        o_hi = o.astype(BF16)
        o_lo = (o - o_hi.astype(F32)).astype(BF16)
        mu = _dot(o_hi, gm) + _dot(o_lo, gm)
        dlt = o - mu
        var = _dot((dlt * dlt).astype(BF16), gm)
        y = dlt * lax.rsqrt(var + EPS)
        y = gate * _sigmoid(gate) * y
        y_ref[b] = _rms_rows(y, g_ref[...]).astype(BF16)


def _ret_mixer(u, trig, expand, consts, g, chunk):
    bsz, s, _ = u.shape
    inner, qd, kd, cd, bm, gm, mq, mv = consts
    full = lambda shape: pl.BlockSpec(shape, lambda i: (0,) * len(shape))
    tok = lambda w: pl.BlockSpec((bsz, chunk, w), lambda i: (0, i, 0))
    return pl.pallas_call(
        _ret_kernel,
        grid=(s // chunk,),
        in_specs=[tok(U_RET), tok(LANES), full(expand.shape),
                  full(inner.shape), full(qd.shape), full(kd.shape), full(cd.shape),
                  full(bm.shape), full(gm.shape), full(mq.shape), full(mv.shape),
                  full((1, GROUP_W))],
        out_specs=tok(GROUP_W),
        out_shape=jax.ShapeDtypeStruct((bsz, s, GROUP_W), BF16),
        scratch_shapes=[pltpu.VMEM((bsz, GROUP_W, GROUP_W), F32)],
        compiler_params=_cparams(("arbitrary",)),
        name="ret_mixer",
    )(u, trig, expand, inner, qd, kd, cd, bm, gm, mq, mv, g)


def _ret_consts(chunk):
    nh = N_HEADS
    f32 = np.float32
    log_g = np.log(f32(1.0) - f32(2.0) ** (f32(-5.0) - np.arange(nh, dtype=f32)))
    idx = np.arange(chunk, dtype=f32)
    rel = idx[:, None] - idx[None, :]
    inner = np.where(rel >= 0, np.exp(log_g[:, None, None] * np.maximum(rel, 0.0)), 0.0).astype(f32)
    v_head = np.arange(GROUP_W) // HEAD_DIM
    q_head = (np.arange(GROUP_W) % LANES) // (RET_DK // 2)
    qd = np.exp(log_g[v_head][None, :] * (idx[:, None] + 1.0)).astype(f32)
    kd = np.exp(log_g[q_head][None, :] * (chunk - 1.0 - idx[:, None])).astype(f32)
    cd = np.exp(log_g[v_head] * chunk)[None, :].astype(f32)
    bm = (q_head[:, None] == v_head[None, :]).astype(f32)
    gm = jnp.asarray((v_head[:, None] == v_head[None, :]).astype(f32) / HEAD_DIM, BF16)
    mq = (q_head[None, :] == np.arange(nh)[:, None]).astype(f32)[:, None, :]
    mv = (v_head[None, :] == np.arange(nh)[:, None]).astype(f32)[:, None, :]
    return tuple(jnp.asarray(a) for a in (inner, qd, kd, cd, bm)) + (gm, jnp.asarray(mq), jnp.asarray(mv))


def _trig_expanders():
    r16, r32 = MLA_ROPE // 2, RET_DK // 2
    ret = np.zeros((LANES, 2 * LANES), np.float32)
    mla = np.zeros((LANES, 2 * LANES), np.float32)
    for j in range(r32):
        for h in range(N_HEADS):
            ret[r16 + j, h * r32 + j] = 1.0
            ret[N_FREQ + r16 + j, LANES + h * r32 + j] = 1.0
    for j in range(r16):
        for half, sign in ((0, -1.0), (1, 1.0)):
            lane = MLA_NOPE + half * r16 + j
            mla[j, lane] = 1.0
            mla[N_FREQ + j, LANES + lane] = sign
    mla[2 * N_FREQ, :MLA_NOPE] = 1.0
    return jnp.asarray(ret, BF16), jnp.asarray(mla, BF16)


def _mla_prep_kernel(u_ref, tab_ref, ex_ref, mu_ref, invu_ref, gu_ref, wbig_ref, sq_ref,
                     invq_ref, gq_ref, gqs_ref, gk_ref, onev_ref, q_ref, k_ref, v_ref):
    x = u_ref[0].astype(F32)
    ss = _dot((x * x).astype(BF16), mu_ref[...]) * invu_ref[...]
    xn = (x * lax.rsqrt(ss + EPS) * gu_ref[...]).astype(BF16)
    big = _dot(xn, wbig_ref[...])
    hw = N_HEADS * LANES
    q, qs, kn, v = big[:, :hw], big[:, hw:2 * hw], big[:, 2 * hw:3 * hw], big[:, 3 * hw:4 * hw]
    kr, krs = big[:, 4 * hw:4 * hw + LANES], big[:, 4 * hw + LANES:]
    cos, sin = _expand_trig(tab_ref[0], ex_ref[...])
    rq = lax.rsqrt(_dot((q * q).astype(BF16), sq_ref[...]) * invq_ref[...] + EPS)
    rk = lax.rsqrt(_dot((kn * kn).astype(BF16), sq_ref[...]) * invq_ref[...] + EPS)
    krot = kr * cos + krs * sin
    for h in range(N_HEADS):
        sl = slice(h * LANES, (h + 1) * LANES)
        qh = (q[:, sl] * gq_ref[...] * cos + qs[:, sl] * gqs_ref[...] * sin) * rq[:, sl]
        q_ref[0, h] = qh.astype(BF16)
        k_ref[0, h] = (kn[:, sl] * gk_ref[...] * rk[:, sl] + krot).astype(BF16)
        v_ref[0, h] = (v[:, sl] + onev_ref[...]).astype(BF16)


def _mla_prep(u, trig, expand, mu, invu, gu, wbig, sq, invq, gq, gqs, gk, onev, tm):
    bsz, s, _ = u.shape
    full = lambda a: pl.BlockSpec(a.shape, lambda b, i: (0,) * a.ndim)
    tab = pl.BlockSpec((1, tm, LANES), lambda b, i: (b, i, 0))
    out = jax.ShapeDtypeStruct((bsz, N_HEADS, s, LANES), BF16)
    ospec = pl.BlockSpec((1, N_HEADS, tm, LANES), lambda b, i: (b, 0, i, 0))
    return pl.pallas_call(
        _mla_prep_kernel,
        grid=(bsz, s // tm),
        in_specs=[pl.BlockSpec((1, tm, U_MLA), lambda b, i: (b, i, (U_COLS - U_MLA) // U_MLA)),
                  tab, full(expand), full(mu), full(invu), full(gu), full(wbig), full(sq), full(invq),
                  full(gq), full(gqs), full(gk), full(onev)],
        out_specs=[ospec, ospec, ospec],
        out_shape=[out, out, out],
        compiler_params=_cparams(("parallel", "parallel")),
        name="mla_prep",
    )(u, trig, expand, mu, invu, gu, wbig, sq, invq, gq, gqs, gk, onev)


def _flash_kernel(q_ref, k_ref, v_ref, o_ref, sa_ref, sb_ref, mca_ref, mcb_ref, m_ref, acc_ref, *, tq):
    qi = pl.program_id(2)
    q = q_ref[0, 0]
    bufs = ((sa_ref, mca_ref), (sb_ref, mcb_ref))
    m_ref[...] = jnp.full((tq, LANES), NEG_BIG, F32)
    acc_ref[...] = jnp.zeros((tq, LANES), F32)

    def scores(c, masked, dst):
        s_ref, mc_ref = dst
        start = pl.multiple_of(c * tq, tq)
        s = _dot_nt(q, k_ref[0, 0, pl.ds(start, tq), :])
        if masked:
            row = qi * tq + lax.broadcasted_iota(jnp.int32, (tq, tq), 0)
            col = start + lax.broadcasted_iota(jnp.int32, (tq, tq), 1)
            s = jnp.where(col <= row, s, NEG_BIG)
        s_ref[...] = s
        mc_ref[...] = jnp.broadcast_to(jnp.max(s, axis=-1, keepdims=True), (tq, LANES))

    def accumulate(c, src):
        s_ref, mc_ref = src
        start = pl.multiple_of(c * tq, tq)
        m_prev = m_ref[...]
        m_new = jnp.maximum(m_prev, mc_ref[...])
        alpha = jnp.exp2(m_prev - m_new)
        p = jnp.exp2(s_ref[...] - jnp.tile(m_new, (1, tq // LANES)))
        pv = _dot(p.astype(BF16), v_ref[0, 0, pl.ds(start, tq), :])
        acc_ref[...] = alpha * acc_ref[...] + pv
        m_ref[...] = m_new

    def by_parity(c, fn):
        for par in range(2):
            pl.when(c % 2 == par)(functools.partial(fn, par))

    def pipelined(c, masked, par):
        scores(c + 1, masked, bufs[1 - par])
        accumulate(c, bufs[par])

    scores(0, True, bufs[0])
    n_plain = jnp.maximum(qi - 1, 0)

    def two_steps(i, carry):
        pipelined(2 * i, False, 0)
        pipelined(2 * i + 1, False, 1)
        return carry

    lax.fori_loop(0, n_plain // 2, two_steps, 0)

    @pl.when(n_plain % 2 == 1)
    def _():
        pipelined(n_plain - 1, False, 0)

    @pl.when(qi >= 1)
    def _():
        by_parity(qi - 1, functools.partial(pipelined, qi - 1, True))

    by_parity(qi, lambda par: accumulate(qi, bufs[par]))

    acc = acc_ref[...]
    lane = lax.broadcasted_iota(jnp.int32, (tq, LANES), 1)
    denom = jnp.sum(jnp.where(lane == HEAD_DIM, acc, 0.0), axis=-1, keepdims=True)
    o_ref[0, 0] = jnp.where(lane < HEAD_DIM, acc / denom, 0.0).astype(BF16)


def _flash_attention(q, k, v, tq):
    bsz, nh, s, _ = q.shape
    kv_spec = pl.BlockSpec((1, 1, s, LANES), lambda b, h, i: (b, h, 0, 0))
    blk = pl.BlockSpec((1, 1, tq, LANES), lambda b, h, i: (b, h, i, 0))
    stat = pltpu.VMEM((tq, LANES), F32)
    return pl.pallas_call(
        functools.partial(_flash_kernel, tq=tq),
        grid=(bsz, nh, s // tq),
        in_specs=[blk, kv_spec, kv_spec],
        out_specs=blk,
        out_shape=jax.ShapeDtypeStruct((bsz, nh, s, LANES), BF16),
        scratch_shapes=[pltpu.VMEM((tq, tq), F32), pltpu.VMEM((tq, tq), F32), stat, stat, stat, stat],
        compiler_params=_cparams(("parallel", "parallel", "arbitrary")),
        name="flash_attention",
    )(q, k, v)


def _outproj_kernel(x_ref, yc_ref, om_ref, yr_ref, yl_ref, wc_ref, wm_ref, wr_ref, wl_ref,
                    gmla_ref, gt_ref, gf_ref, scf_ref, shf_ref, wrt_ref, brt_ref, tri_ref,
                    xo_ref, h_ref, meta_ref, cnt_ref, run_ref, lg_ref):
    n = pl.program_id(0)

    @pl.when(n == 0)
    def _():
        run_ref[...] = jnp.zeros((1, LANES), F32)
        lg_ref[...] = jnp.zeros(lg_ref.shape, F32)

    lg_prev = lg_ref[...]
    routed = (n > 0).astype(F32)

    om = [om_ref[0, h].astype(F32) for h in range(N_HEADS)]
    ssq = om[0] * om[0]
    for h in range(1, N_HEADS):
        ssq = ssq + om[h] * om[h]
    r_mla = lax.rsqrt(jnp.sum(ssq, axis=-1, keepdims=True) / GROUP_W + EPS)
    y = _dot(yc_ref[0], wc_ref[...]) + _dot(yr_ref[0], wr_ref[...]) + _dot(yl_ref[0], wl_ref[...])
    for h in range(0, N_HEADS, 2):
        pair = jnp.concatenate([(om[h] * r_mla * gmla_ref[h]).astype(BF16),
                                (om[h + 1] * r_mla * gmla_ref[h + 1]).astype(BF16)], axis=1)
        y = y + _dot(pair, wm_ref[h // 2])
    x = x_ref[0] + gt_ref[0] * y
    xo_ref[0] = x
    hf = _rms_rows(x, gf_ref[...]) * (1.0 + scf_ref[0]) + shf_ref[0]
    h_ref[0] = _pack_bf16_pairs(hf)

    h_hi = hf.astype(BF16)
    h_lo = (hf - h_hi.astype(F32)).astype(BF16)
    both = _dot(h_hi, wrt_ref[...])
    lg_ref[...] = both[:, :LANES] + both[:, LANES:] + _dot(h_lo, wrt_ref[:, :LANES])

    lg = lg_prev
    tm = lg.shape[0]
    lane = lax.broadcasted_iota(jnp.int32, (tm, LANES), 1)
    bias = brt_ref[...]
    is_g = (lane >= N_EXPERTS) & (lane < N_EXPERTS + MOE_GROUPS)
    is_e = lane < N_EXPERTS

    def first_argmax(val):
        mx = jnp.max(val, axis=-1, keepdims=True)
        return jnp.min(jnp.where(val == mx, lane, LANES), axis=-1, keepdims=True)

    gl = jnp.where(is_g, lg, NEG_BIG)
    ge = jnp.exp(gl - jnp.max(gl, axis=-1, keepdims=True))
    gp = ge / jnp.sum(ge, axis=-1, keepdims=True)
    g_idx = first_argmax(jnp.where(is_g, gp + bias, NEG_BIG))
    g_weight = jnp.sum(jnp.where(lane == g_idx, gp, 0.0), axis=-1, keepdims=True)
    in_group = is_e & ((lane // EXPERTS_PER_GROUP) == (g_idx - N_EXPERTS))
    el = jnp.where(in_group, lg, NEG_BIG)
    ee = jnp.exp(el - jnp.max(el, axis=-1, keepdims=True))
    ep = ee / jnp.sum(ee, axis=-1, keepdims=True)
    score = jnp.where(in_group, ep + bias, NEG_BIG)
    i1 = first_argmax(score)
    sel1 = lane == i1
    i2 = first_argmax(jnp.where(sel1, NEG_BIG, score))
    sel2 = lane == i2
    p1 = jnp.sum(jnp.where(sel1, ep, 0.0), axis=-1, keepdims=True)
    p2 = jnp.sum(jnp.where(sel2, ep, 0.0), axis=-1, keepdims=True)
    psum = p1 + p2
    w1 = p1 / psum * g_weight
    w2 = p2 / psum * g_weight

    onehot = jnp.where(sel1, 1.0, jnp.where(sel2, 1.0, 0.0)).astype(BF16)
    incl = _dot(tri_ref[...], onehot) * routed
    base = run_ref[...] + incl - 1.0
    r1 = jnp.sum(jnp.where(sel1, base, 0.0), axis=-1, keepdims=True)
    r2 = jnp.sum(jnp.where(sel2, base, 0.0), axis=-1, keepdims=True)
    run_ref[...] = run_ref[...] + incl[tm - 1:tm, :]
    cnt_ref[0] = run_ref[...]
    fields = (i1.astype(F32), i2.astype(F32), r1, r2, w1, w2)
    meta = jnp.zeros((tm, LANES), F32)
    for pos, val in enumerate(fields):
        meta = jnp.where(lane == pos, val, meta)
    meta_ref[0] = meta


def _outproj(x, yc, om, yr, yl, wc, wm, wr, wl, gmla, gt, gf, scf, shf, wrt, brt, tm):
    bsz, s, d = x.shape
    nt = s // tm
    n_tiles = bsz * nt
    cur = lambda n: jnp.minimum(n, n_tiles - 1)
    lag = lambda n: jnp.maximum(n - 1, 0)
    full = lambda a: pl.BlockSpec(a.shape, lambda n: (0,) * a.ndim)
    tok = lambda w, f=cur: pl.BlockSpec((1, tm, w), lambda n: (f(n) // nt, f(n) % nt, 0))
    vec = pl.BlockSpec((1, 1, d), lambda n: (cur(n) // nt, 0, 0))
    tri = jnp.asarray(np.tril(np.ones((tm, tm), np.float32)), BF16)
    return pl.pallas_call(
        _outproj_kernel,
        grid=(n_tiles + 1,),
        in_specs=[tok(d), tok(GROUP_W),
                  pl.BlockSpec((1, N_HEADS, tm, LANES), lambda n: (cur(n) // nt, 0, cur(n) % nt, 0)),
                  tok(GROUP_W), tok(GROUP_W),
                  full(wc), full(wm), full(wr), full(wl), full(gmla), vec, full(gf), vec, vec,
                  full(wrt), full(brt), full(tri)],
        out_specs=[tok(d), tok(d // 2), tok(LANES, lag),
                   pl.BlockSpec((1, 1, LANES), lambda n: (lag(n), 0, 0))],
        out_shape=[jax.ShapeDtypeStruct((bsz, s, d), F32), jax.ShapeDtypeStruct((bsz, s, d // 2), jnp.uint32),
                   jax.ShapeDtypeStruct((bsz, s, LANES), F32),
                   jax.ShapeDtypeStruct((bsz * nt, 1, LANES), F32)],
        scratch_shapes=[pltpu.VMEM((1, LANES), F32), pltpu.VMEM((tm, LANES), F32)],
        compiler_params=_cparams(("arbitrary",)),
        name="outproj_router",
    )(x, yc, om, yr, yl, wc, wm, wr, wl, gmla, gt, gf, scf, shf, wrt, brt, tri)


def _route_plan(meta, cnt, tmg):
    t = meta.shape[0] * meta.shape[1]
    m = meta.reshape(t, LANES)
    e = m[:, 0:2].astype(jnp.int32)
    r = m[:, 2:4].astype(jnp.int32)
    counts = cnt[-1, 0, :N_EXPERTS].astype(jnp.int32)
    padded = (counts + tmg - 1) // tmg * tmg
    ends = jnp.cumsum(padded)
    starts = ends - padded
    dest = jnp.take(starts, e) + r
    n_tiles = (2 * t) // tmg + N_EXPERTS
    tile_start = jnp.arange(n_tiles, dtype=jnp.int32) * tmg
    tile_exp = jnp.sum((ends[None, :] <= tile_start[:, None]).astype(jnp.int32), axis=1)
    tile_exp = jnp.minimum(tile_exp, N_EXPERTS - 1)
    n_used = (ends[-1:] // tmg).astype(jnp.int32)
    last_tile = jnp.maximum(ends - tmg, 0).astype(jnp.int32)
    return dest[:, 0], dest[:, 1], tile_exp, n_used, last_tile, padded.astype(jnp.int32)


def _dispatch_kernel(zs_ref, zv_ref, nu_ref, d1_ref, d2_ref, h_ref, xs_ref, zero_ref, sem, *, tmc, tmg):
    @pl.when(pl.program_id(0) == 0)
    def _():
        zero_ref[...] = jnp.zeros(zero_ref.shape, jnp.uint32)
        n_tiles = xs_ref.shape[0] // tmg

        def fill(start):
            return pltpu.make_async_copy(zero_ref, xs_ref.at[pl.ds(pl.multiple_of(start, tmg), tmg), :], sem)

        def fill_tail(j, carry, wait):
            cp = fill(j * tmg)
            cp.wait() if wait else cp.start()
            return carry

        for wait in (False, True):
            for e in range(N_EXPERTS):
                cp = fill(zs_ref[e])
                pl.when(zv_ref[e] > 0)(cp.wait if wait else cp.start)
            lax.fori_loop(nu_ref[0], n_tiles, functools.partial(fill_tail, wait=wait), 0)

    def push(r, carry):
        src = h_ref.at[pl.ds(r, 1), :]
        pltpu.make_async_copy(src, xs_ref.at[pl.ds(d1_ref[0, 0, r], 1), :], sem).start(priority=0)
        pltpu.make_async_copy(src, xs_ref.at[pl.ds(d2_ref[0, 0, r], 1), :], sem).start(priority=1)
        return carry

    lax.fori_loop(0, tmc, push, 0, unroll=8)
    one_row = pltpu.make_async_copy(h_ref.at[pl.ds(0, 1), :], xs_ref.at[pl.ds(0, 1), :], sem)
    for _ in range(2 * tmc):
        one_row.wait()


def _dispatch(hp, d1, d2, last_tile, padded, n_used, tmc, tmg):
    t, dw = hp.shape
    n_rows = 2 * t + N_EXPERTS * tmg
    smem_rows = pl.BlockSpec((1, 1, tmc), lambda i, zs, zv, nu: (i, 0, 0), memory_space=pltpu.SMEM)
    return pl.pallas_call(
        functools.partial(_dispatch_kernel, tmc=tmc, tmg=tmg),
        grid_spec=pltpu.PrefetchScalarGridSpec(
            num_scalar_prefetch=3, grid=(t // tmc,),
            in_specs=[smem_rows, smem_rows, pl.BlockSpec((tmc, dw), lambda i, zs, zv, nu: (i, 0))],
            out_specs=pl.BlockSpec(memory_space=pl.ANY),
            scratch_shapes=[pltpu.VMEM((tmg, dw), jnp.uint32), pltpu.SemaphoreType.DMA(())]),
        out_shape=jax.ShapeDtypeStruct((n_rows, dw), jnp.uint32),
        compiler_params=_cparams(("arbitrary",)),
        name="moe_dispatch",
    )(last_tile, padded, n_used, d1.reshape(t // tmc, 1, tmc), d2.reshape(t // tmc, 1, tmc), hp)


def _experts_kernel(te_ref, nu_ref, xs_ref, wg_ref, wu_ref, wd_ref, y_ref, wgb_ref, wub_ref, wdb_ref):
    j = pl.program_id(0)
    used = j < nu_ref[0]
    tile = jnp.minimum(j, nu_ref[0] - 1)

    @pl.when((j == 0) | (te_ref[tile] != te_ref[jnp.maximum(tile - 1, 0)]))
    def _():
        wgb_ref[...] = wg_ref[0, 0].astype(BF16)
        wub_ref[...] = wu_ref[0, 0].astype(BF16)
        wdb_ref[...] = wd_ref[0, 0].astype(BF16)

    @pl.when(used)
    def _():
        x = _unpack_bf16_pairs(xs_ref[...]).astype(BF16)
        gate = _dot(x, wgb_ref[...])
        hid = gate * _sigmoid(gate) * _dot(x, wub_ref[...])
        y_ref[...] = _pack_bf16_pairs(_dot(hid.astype(BF16), wdb_ref[...]))

    @pl.when(jnp.logical_not(used))
    def _():
        y_ref[...] = jnp.zeros(y_ref.shape, jnp.uint32)


def _experts(xs, tile_exp, n_used, layer, wg, wu, wd, tmg):
    n_rows, dw = xs.shape
    d = 2 * dw
    tile = lambda i, te, nu: jnp.minimum(i, nu[0] - 1)
    rows = pl.BlockSpec((tmg, dw), lambda i, te, nu: (tile(i, te, nu), 0))
    wspec = lambda shape: pl.BlockSpec((1, 1) + shape,
                                       lambda i, te, nu: (layer, te[tile(i, te, nu)], 0, 0))
    return pl.pallas_call(
        _experts_kernel,
        grid_spec=pltpu.PrefetchScalarGridSpec(
            num_scalar_prefetch=2, grid=(n_rows // tmg,),
            in_specs=[rows, wspec((d, D_EXPERT)), wspec((d, D_EXPERT)), wspec((D_EXPERT, d))],
            out_specs=pl.BlockSpec((tmg, dw), lambda i, te, nu: (i, 0)),
            scratch_shapes=[pltpu.VMEM((d, D_EXPERT), BF16), pltpu.VMEM((d, D_EXPERT), BF16),
                            pltpu.VMEM((D_EXPERT, d), BF16)]),
        out_shape=jax.ShapeDtypeStruct((n_rows, dw), jnp.uint32),
        compiler_params=_cparams(("arbitrary",)),
        name="moe_experts",
    )(tile_exp, n_used, xs, wg, wu, wd)


def _layer_weights(l, w_in, mla_q_norm_g, mla_w_uq, mla_kv_norm_g, mla_w_ukv, mla_q_qk_g,
                   mla_k_qk_g, lru_w_a, lru_w_x, mix_norm_g, w_out, router_group_w,
                   router_group_b, router_expert_w, router_expert_b):
    half = RET_DK // 2
    perm = np.concatenate([np.arange(half) + HEAD_DIM * h for h in range(N_HEADS)]
                          + [np.arange(half) + half + HEAD_DIM * h for h in range(N_HEADS)])
    w = w_in[l]
    o_mla, o_ret, o_lru = U_CONV, U_CONV + 352, U_CONV + 352 + U_RET
    w_ret = w[:, o_ret:o_ret + U_RET]
    w_ret = jnp.concatenate([w_ret[:, perm], w_ret[:, GROUP_W + perm], w_ret[:, 2 * GROUP_W:]], axis=1)
    w_all = jnp.concatenate([w_ret, w[:, o_lru:o_lru + U_LRU], w[:, :U_CONV], w[:, o_mla:o_mla + 352],
                             jnp.zeros((D_MODEL, U_MLA - 352), F32)], axis=1).astype(BF16)

    hw = N_HEADS * LANES
    r16 = MLA_ROPE // 2
    wq = mla_w_uq[l].reshape(Q_LORA, N_HEADS, MLA_QK)
    zq = jnp.zeros((Q_LORA, N_HEADS, LANES - MLA_QK), F32)
    q_cols = jnp.concatenate([wq, zq], axis=2).reshape(Q_LORA, hw)
    wq_sw = jnp.concatenate([jnp.zeros((Q_LORA, N_HEADS, MLA_NOPE), F32), wq[:, :, MLA_NOPE + r16:],
                             wq[:, :, MLA_NOPE:MLA_NOPE + r16], zq], axis=2).reshape(Q_LORA, hw)
    wkv = mla_w_ukv[l].reshape(KV_LORA, N_HEADS, MLA_NOPE + HEAD_DIM)
    zk = jnp.zeros((KV_LORA, N_HEADS, LANES - MLA_NOPE), F32)
    k_cols = jnp.concatenate([wkv[:, :, :MLA_NOPE], zk], axis=2).reshape(KV_LORA, hw)
    v_cols = jnp.concatenate([wkv[:, :, MLA_NOPE:], zk], axis=2).reshape(KV_LORA, hw)
    eye = jnp.eye(MLA_ROPE, dtype=F32)
    place = jnp.concatenate([jnp.zeros((MLA_ROPE, MLA_NOPE), F32), eye,
                             jnp.zeros((MLA_ROPE, LANES - MLA_QK), F32)], axis=1)
    eye_sw = jnp.concatenate([eye[:, r16:], eye[:, :r16]], axis=1)
    place_sw = jnp.concatenate([jnp.zeros((MLA_ROPE, MLA_NOPE), F32), eye_sw,
                                jnp.zeros((MLA_ROPE, LANES - MLA_QK), F32)], axis=1)
    n_big = 4 * hw + 2 * LANES
    wbig = jnp.zeros((U_MLA, n_big), F32)
    wbig = wbig.at[:Q_LORA, :hw].set(q_cols).at[:Q_LORA, hw:2 * hw].set(wq_sw)
    wbig = wbig.at[Q_LORA:Q_LORA + KV_LORA, 2 * hw:3 * hw].set(k_cols)
    wbig = wbig.at[Q_LORA:Q_LORA + KV_LORA, 3 * hw:4 * hw].set(v_cols)
    wbig = wbig.at[Q_LORA + KV_LORA:352, 4 * hw:4 * hw + LANES].set(place)
    wbig = wbig.at[Q_LORA + KV_LORA:352, 4 * hw + LANES:].set(place_sw)
    wbig = wbig.astype(BF16)

    gu = jnp.concatenate([mla_q_norm_g[l], mla_kv_norm_g[l], mla_k_qk_g[l][MLA_NOPE:],
                          jnp.zeros((U_MLA - 352,), F32)])[None, :]
    qscale = (MLA_QK ** -0.5) * math.log2(math.e)
    gq_full = mla_q_qk_g[l]
    pad = jnp.zeros((LANES - MLA_QK,), F32)
    gq = (jnp.concatenate([gq_full, pad]) * qscale)[None, :]
    gqs = (jnp.concatenate([jnp.zeros((MLA_NOPE,), F32), gq_full[MLA_NOPE + r16:],
                            gq_full[MLA_NOPE:MLA_NOPE + r16], pad]) * qscale)[None, :]
    gk = jnp.concatenate([mla_k_qk_g[l][:MLA_NOPE], jnp.zeros((LANES - MLA_NOPE,), F32)])[None, :]

    def blockdiag(wb):
        out = jnp.zeros((GROUP_W, GROUP_W), F32)
        for n in range(wb.shape[0]):
            out = out.at[n * HEAD_DIM:(n + 1) * HEAD_DIM, n * HEAD_DIM:(n + 1) * HEAD_DIM].set(wb[n])
        return out.astype(BF16)

    gmix = mix_norm_g[l]
    wo = w_out[l].astype(BF16)
    wm = wo[GROUP_W:2 * GROUP_W].reshape(N_HEADS, HEAD_DIM, D_MODEL)
    wm = jnp.concatenate([wm, jnp.zeros((N_HEADS, LANES - HEAD_DIM, D_MODEL), BF16)], axis=1)
    wm = wm.reshape(N_HEADS // 2, 2 * LANES, D_MODEL)
    gmla = jnp.concatenate([gmix[GROUP_W:2 * GROUP_W].reshape(N_HEADS, 1, HEAD_DIM),
                            jnp.zeros((N_HEADS, 1, LANES - HEAD_DIM), F32)], axis=2)
    wrt = jnp.concatenate([router_expert_w[l], router_group_w[l],
                           jnp.zeros((D_MODEL, LANES - N_EXPERTS - MOE_GROUPS), F32)], axis=1)
    wrt_hi = wrt.astype(BF16)
    wrt = jnp.concatenate([wrt_hi, (wrt - wrt_hi.astype(F32)).astype(BF16)], axis=1)
    brt = jnp.concatenate([router_expert_b[l], router_group_b[l],
                           jnp.zeros((LANES - N_EXPERTS - MOE_GROUPS,), F32)])[None, :]
    return dict(w_all=w_all, wbig=wbig, gu=gu, gq=gq, gqs=gqs, gk=gk,
                wa=blockdiag(lru_w_a[l]), wx=blockdiag(lru_w_x[l]),
                g_conv=gmix[None, :GROUP_W], g_ret=gmix[None, 2 * GROUP_W:3 * GROUP_W],
                g_lru=gmix[None, 3 * GROUP_W:], gmla=gmla,
                wc=wo[:GROUP_W], wm=wm, wr=wo[2 * GROUP_W:3 * GROUP_W], wl=wo[3 * GROUP_W:],
                wrt=wrt, brt=brt)


def _mla_consts():
    seg_u = np.concatenate([np.zeros(Q_LORA), np.ones(KV_LORA), 2 * np.ones(MLA_ROPE),
                            3 * np.ones(U_MLA - 352)])
    mu = jnp.asarray(seg_u[:, None] == seg_u[None, :], BF16)
    invu = jnp.asarray(np.concatenate([np.full(Q_LORA, 1.0 / Q_LORA), np.full(KV_LORA, 1.0 / KV_LORA),
                                       np.full(MLA_ROPE, 1.0 / MLA_ROPE), np.ones(U_MLA - 352)]), F32)[None, :]
    lane = np.arange(N_HEADS * LANES)
    seg_q = (lane // LANES) * 3 + np.where(lane % LANES < MLA_NOPE, 0, np.where(lane % LANES < MLA_QK, 1, 2))
    sq = jnp.asarray(seg_q[:, None] == seg_q[None, :], BF16)
    inv_head = np.concatenate([np.full(MLA_NOPE, 1.0 / MLA_NOPE), np.full(MLA_ROPE, 1.0 / MLA_ROPE),
                               np.ones(LANES - MLA_QK)])
    invq = jnp.asarray(np.tile(inv_head, N_HEADS), F32)[None, :]
    onev = jnp.asarray((np.arange(LANES) == HEAD_DIM).astype(np.float32))[None, :]
    return mu, invu, sq, invq, onev


def kernel(x, c, positions, ada_w, ada_b, norm_mix_g, w_in, conv_w, mla_q_norm_g, mla_w_uq, mla_kv_norm_g, mla_w_ukv, mla_q_qk_g, mla_k_qk_g, lru_conv_w, lru_conv_b, lru_w_a, lru_b_a, lru_w_x, lru_b_x, lru_lambda, mix_norm_g, w_out, norm_ffn_g, router_group_w, router_group_b, router_expert_w, router_expert_b, exp_w_gate, exp_w_up, exp_w_down):
    bsz, s, d = x.shape
    depth = ada_w.shape[0]
    tm = min(512, s)
    chunk = min(256, s)
    tq = min(1024, s)
    tmc = min(512, s)
    tmd = min(1024, s)
    tmg = 512

    inv = jnp.concatenate([1.0 / (ROPE_BASE ** (jnp.arange(0, MLA_ROPE, 2, dtype=F32) / MLA_ROPE)),
                           1.0 / (ROPE_BASE ** (jnp.arange(0, RET_DK, 2, dtype=F32) / RET_DK))])[:, None]
    trig = _rope_tables(positions, inv)
    ex_ret, ex_mla = _trig_expanders()

    c_pad = jnp.concatenate([c, jnp.zeros((8 - bsz, d), F32)], axis=0)
    mod = _modulation(c_pad, ada_w, ada_b)[:, :bsz]
    ret_consts = _ret_consts(chunk)
    mu, invu, sq, invq, onev = _mla_consts()

    pending = None
    for l in range(depth):
        sh_m, sc_m, gt_m, sh_f, sc_f, gt_f = [m[:, None, :] for m in jnp.split(mod[l], 6, axis=-1)]
        lw = _layer_weights(l, w_in, mla_q_norm_g, mla_w_uq, mla_kv_norm_g, mla_w_ukv, mla_q_qk_g,
                            mla_k_qk_g, lru_w_a, lru_w_x, mix_norm_g, w_out, router_group_w,
                            router_group_b, router_expert_w, router_expert_b)
        if pending is None:
            u = _inproj(x, norm_mix_g[l][None, :], sc_m, sh_m, lw["w_all"], tm)
        else:
            x, u = _combine(x, *pending, tm, proj=(norm_mix_g[l][None, :], sc_m, sh_m, lw["w_all"]))
        y_conv = _conv_mixer(u, conv_w[l], lw["g_conv"], tm)
        y_lru = _lru_mixer(u, lru_conv_w[l], lru_conv_b[l][None, :], lw["wa"], lru_b_a[l][None, :],
                           lw["wx"], lru_b_x[l][None, :], lru_lambda[l][None, :], lw["g_lru"], tm)
        y_ret = _ret_mixer(u, trig, ex_ret, ret_consts, lw["g_ret"], chunk)
        q, k, v = _mla_prep(u, trig, ex_mla, mu, invu, lw["gu"], lw["wbig"], sq, invq,
                            lw["gq"], lw["gqs"], lw["gk"], onev, tm)
        o_mla = _flash_attention(q, k, v, tq)
        x, hp, meta, cnt = _outproj(x, y_conv, o_mla, y_ret, y_lru, lw["wc"], lw["wm"], lw["wr"],
                                    lw["wl"], lw["gmla"], gt_m, norm_ffn_g[l][None, :], sc_f, sh_f,
                                    lw["wrt"], lw["brt"], tm)
        d1, d2, tile_exp, n_used, last_tile, padded = _route_plan(meta, cnt, tmg)
        xs = _dispatch(hp.reshape(bsz * s, d // 2), d1, d2, last_tile, padded, n_used, tmd, tmg)
        ys = _experts(xs, tile_exp, n_used, l, exp_w_gate, exp_w_up, exp_w_down, tmg)
        pending = (meta, gt_f, ys, d1, d2)
    return _combine(x, *pending, tmc)[0]
```

```python
import functools
import math

import jax
import jax.numpy as jnp
import numpy as np
from jax import lax
from jax.experimental import pallas as pl
from jax.experimental.pallas import tpu as pltpu

F32 = jnp.float32
BF16 = jnp.bfloat16
HIGHEST = lax.Precision.HIGHEST

D_MODEL = 1024
GROUP_W = 256
HEAD_DIM = 64
N_HEADS = 4
MLA_NOPE = 64
MLA_ROPE = 32
MLA_QK = 96
Q_LORA = 192
KV_LORA = 128
RET_DK = 64
LRU_C = 8.0
MOE_GROUPS = 4
EXPERTS_PER_GROUP = 8
N_EXPERTS = 32
D_EXPERT = 256
ROPE_BASE = 10000.0
EPS = 1e-6

LANES = 128
SUBLANES = 8
MXU_DIM = 256
U_RET, U_LRU, U_CONV, U_MLA = 1024, 512, 768, 384
U_COLS = U_RET + U_LRU + U_CONV + U_MLA
N_FREQ = MLA_ROPE // 2 + RET_DK // 2
NEG_BIG = -1e30
VMEM_LIMIT = 56 * 1024 * 1024


def _cparams(sem):
    return pltpu.CompilerParams(dimension_semantics=sem, vmem_limit_bytes=VMEM_LIMIT)


def _dot(a, b):
    return jnp.dot(a, b, preferred_element_type=F32)


def _dot_nt(a, b):
    return lax.dot_general(a, b, (((1,), (1,)), ((), ())), preferred_element_type=F32)


def _dot_tn(a, b):
    return lax.dot_general(a, b, (((0,), (0,)), ((), ())), preferred_element_type=F32)


def _rms_rows(y, g):
    return y * lax.rsqrt(jnp.mean(y * y, axis=-1, keepdims=True) + EPS) * g


def _sigmoid(x):
    return 0.5 * jnp.tanh(0.5 * x) + 0.5


def _pack_bf16_pairs(a):
    k = a.shape[1] // 2
    rounded = a.astype(BF16).astype(F32)
    lo = lax.bitcast_convert_type(rounded[:, :k], jnp.uint32) >> 16
    hi = lax.bitcast_convert_type(rounded[:, k:], jnp.uint32) & jnp.uint32(0xFFFF0000)
    return lo | hi


def _unpack_bf16_pairs(w):
    lo = lax.bitcast_convert_type(w << 16, F32)
    hi = lax.bitcast_convert_type(w & jnp.uint32(0xFFFF0000), F32)
    return jnp.concatenate([lo, hi], axis=1)


def _rope_kernel(pos_ref, inv_ref, tab_ref):
    ang = pos_ref[0].astype(F32) * inv_ref[...]
    row = lax.broadcasted_iota(jnp.int32, (LANES - 2 * N_FREQ, ang.shape[1]), 0)
    pad = jnp.where(row == 0, 1.0, 0.0)
    tab_ref[0] = jnp.concatenate([jnp.cos(ang), jnp.sin(ang), pad], axis=0).T


def _rope_tables(positions, inv):
    bsz, s = positions.shape
    ts = min(s, 2048)
    return pl.pallas_call(
        _rope_kernel,
        grid=(bsz, s // ts),
        in_specs=[pl.BlockSpec((1, 1, ts), lambda b, i: (b, 0, i)),
                  pl.BlockSpec((N_FREQ, 1), lambda b, i: (0, 0))],
        out_specs=pl.BlockSpec((1, ts, LANES), lambda b, i: (b, i, 0)),
        out_shape=jax.ShapeDtypeStruct((bsz, s, LANES), F32),
        compiler_params=_cparams(("parallel", "parallel")),
        name="rope_tables",
    )(positions.reshape(bsz, 1, s), inv)


def _expand_trig(tab, expand):
    hi = tab.astype(BF16)
    lo = (tab - hi.astype(F32)).astype(BF16)
    trig = _dot(hi, expand) + _dot(lo, expand)
    return trig[:, :LANES], trig[:, LANES:]


def _mod_kernel(c_ref, w_ref, b_ref, o_ref):
    c = c_ref[...]
    ca = c * _sigmoid(c)
    o_ref[0] = jnp.dot(ca, w_ref[0], precision=HIGHEST, preferred_element_type=F32) + b_ref[0]


def _modulation(c_pad, ada_w, ada_b):
    nl, d, n = ada_w.shape
    tn = 1536
    return pl.pallas_call(
        _mod_kernel,
        grid=(nl, n // tn),
        in_specs=[pl.BlockSpec((8, d), lambda l, j: (0, 0)),
                  pl.BlockSpec((1, d, tn), lambda l, j: (l, 0, j)),
                  pl.BlockSpec((1, 1, tn), lambda l, j: (l, 0, j))],
        out_specs=pl.BlockSpec((1, 8, tn), lambda l, j: (l, 0, j)),
        out_shape=jax.ShapeDtypeStruct((nl, 8, n), F32),
        compiler_params=_cparams(("parallel", "parallel")),
        name="adaln_mod",
    )(c_pad, ada_w, ada_b.reshape(nl, 1, n))


def _norm_project(x, g_ref, sc_ref, sh_ref, w_ref):
    h = _rms_rows(x, g_ref[...]) * (1.0 + sc_ref[0]) + sh_ref[0]
    return _dot(h.astype(BF16), w_ref[...]).astype(BF16)


def _inproj_kernel(x_ref, g_ref, sc_ref, sh_ref, w_ref, u_ref):
    u_ref[0] = _norm_project(x_ref[0], g_ref, sc_ref, sh_ref, w_ref)


def _router_weights(meta):
    lane = lax.broadcasted_iota(jnp.int32, meta.shape, 1)
    w1 = jnp.sum(jnp.where(lane == 4, meta, 0.0), axis=-1, keepdims=True)
    w2 = jnp.sum(jnp.where(lane == 5, meta, 0.0), axis=-1, keepdims=True)
    return w1, w2


def _combine_kernel(d1c_ref, d2c_ref, d1n_ref, d2n_ref, x_ref, meta_ref, gtf_ref, y_hbm, *rest, tm, project):
    if project:
        g_ref, sc_ref, sh_ref, w_ref, xo_ref, u_ref, *scratch = rest
    else:
        xo_ref, *scratch = rest
    a1_ref, a2_ref, b1_ref, b2_ref, sem_a, sem_b = scratch
    n = pl.program_id(0)
    bufs = ((a1_ref, a2_ref, sem_a), (b1_ref, b2_ref, sem_b))

    def pull(d_ref, buf, sem, r):
        return pltpu.make_async_copy(y_hbm.at[pl.ds(d_ref[0, 0, r], 1), :], buf.at[pl.ds(r, 1), :], sem)

    def wait_tile(buf, sem):
        one_row = pltpu.make_async_copy(y_hbm.at[pl.ds(0, 1), :], buf.at[pl.ds(0, 1), :], sem)
        for _ in range(2 * tm):
            one_row.wait()

    @pl.when(n == 0)
    def _():
        def first(r, carry):
            pull(d1c_ref, a1_ref, sem_a, r).start(priority=0)
            pull(d2c_ref, a2_ref, sem_a, r).start(priority=1)
            return carry
        lax.fori_loop(0, tm, first, 0, unroll=8)

    def step(par):
        y1_ref, y2_ref, sem = bufs[par]
        n1_ref, n2_ref, nsem = bufs[1 - par]
        wait_tile(y1_ref, sem)
        for r in range(tm):
            pull(d1n_ref, n1_ref, nsem, r).start(priority=0)
            pull(d2n_ref, n2_ref, nsem, r).start(priority=1)
        w1, w2 = _router_weights(meta_ref[0])
        y = w1 * _unpack_bf16_pairs(y1_ref[...]) + w2 * _unpack_bf16_pairs(y2_ref[...])
        x = x_ref[0] + gtf_ref[0] * y
        xo_ref[0] = x
        if project:
            u_ref[0] = _norm_project(x, g_ref, sc_ref, sh_ref, w_ref)

        @pl.when(n == pl.num_programs(0) - 1)
        def _():
            wait_tile(n1_ref, nsem)

    for par in range(2):
        pl.when(n % 2 == par)(functools.partial(step, par))


def _combine(x, meta, gtf, ys, d1, d2, tm, proj=None):
    bsz, s, d = x.shape
    nt = s // tm
    n_steps = bsz * nt
    nxt = lambda n: jnp.minimum(n + 1, n_steps - 1)
    smem = lambda f: pl.BlockSpec((1, 1, tm), lambda n: (f(n), 0, 0), memory_space=pltpu.SMEM)
    tok = lambda wdt: pl.BlockSpec((1, tm, wdt), lambda n: (n // nt, n % nt, 0))
    vec = pl.BlockSpec((1, 1, d), lambda n: (n // nt, 0, 0))
    buf = pltpu.VMEM((tm, d // 2), jnp.uint32)
    dd1, dd2 = d1.reshape(n_steps, 1, tm), d2.reshape(n_steps, 1, tm)
    in_specs = [smem(lambda n: n), smem(lambda n: n), smem(nxt), smem(nxt),
                tok(d), tok(LANES), vec, pl.BlockSpec(memory_space=pl.ANY)]
    out_specs = [tok(d)]
    out_shape = [jax.ShapeDtypeStruct((bsz, s, d), F32)]
    args = (dd1, dd2, dd1, dd2, x, meta, gtf, ys)
    if proj is not None:
        in_specs += [pl.BlockSpec((1, d), lambda n: (0, 0)), vec, vec, pl.BlockSpec((d, U_COLS), lambda n: (0, 0))]
        out_specs.append(tok(U_COLS))
        out_shape.append(jax.ShapeDtypeStruct((bsz, s, U_COLS), BF16))
        args += tuple(proj)
    return pl.pallas_call(
        functools.partial(_combine_kernel, tm=tm, project=proj is not None),
        grid=(n_steps,),
        in_specs=in_specs,
        out_specs=out_specs,
        out_shape=out_shape,
        scratch_shapes=[buf, buf, buf, buf, pltpu.SemaphoreType.DMA(()), pltpu.SemaphoreType.DMA(())],
        compiler_params=_cparams(("arbitrary",)),
        name="moe_combine_inproj" if proj is not None else "moe_combine",
    )(*args)


def _inproj(x, g, sc, sh, w, tm):
    bsz, s, d = x.shape
    vec = pl.BlockSpec((1, 1, d), lambda b, i: (b, 0, 0))
    return pl.pallas_call(
        _inproj_kernel,
        grid=(bsz, s // tm),
        in_specs=[pl.BlockSpec((1, tm, d), lambda b, i: (b, i, 0)),
                  pl.BlockSpec((1, d), lambda b, i: (0, 0)),
                  vec, vec,
                  pl.BlockSpec((d, U_COLS), lambda b, i: (0, 0))],
        out_specs=pl.BlockSpec((1, tm, U_COLS), lambda b, i: (b, i, 0)),
        out_shape=jax.ShapeDtypeStruct((bsz, s, U_COLS), BF16),
        compiler_params=_cparams(("parallel", "parallel")),
        name="inproj",
    )(x, g, sc, sh, w)


def _conv_kernel(u_ref, w_ref, g_ref, y_ref, buf_ref, *, tm):
    @pl.when(pl.program_id(1) == 0)
    def _():
        buf_ref[0:8, :] = jnp.zeros((8, GROUP_W), F32)

    u = u_ref[0].astype(F32)
    b_gate, c_gate, xin = u[:, :GROUP_W], u[:, GROUP_W:2 * GROUP_W], u[:, 2 * GROUP_W:]
    cx = c_gate * xin
    buf_ref[8:8 + tm, :] = cx
    conv = (w_ref[2:3, :] * cx + w_ref[1:2, :] * buf_ref[7:7 + tm, :]
            + w_ref[0:1, :] * buf_ref[6:6 + tm, :])
    buf_ref[0:8, :] = cx[tm - 8:, :]
    y_ref[0] = _rms_rows(b_gate * conv, g_ref[...]).astype(BF16)


def _conv_mixer(u, w, g, tm):
    bsz, s, _ = u.shape
    return pl.pallas_call(
        functools.partial(_conv_kernel, tm=tm),
        grid=(bsz, s // tm),
        in_specs=[pl.BlockSpec((1, tm, U_CONV), lambda b, i: (b, i, (U_RET + U_LRU) // U_CONV)),
                  pl.BlockSpec((3, GROUP_W), lambda b, i: (0, 0)),
                  pl.BlockSpec((1, GROUP_W), lambda b, i: (0, 0))],
        out_specs=pl.BlockSpec((1, tm, GROUP_W), lambda b, i: (b, i, 0)),
        out_shape=jax.ShapeDtypeStruct((bsz, s, GROUP_W), BF16),
        scratch_shapes=[pltpu.VMEM((tm + 8, GROUP_W), F32)],
        compiler_params=_cparams(("parallel", "arbitrary")),
        name="conv_mixer",
    )(u, w, g)


def _lru_kernel(u_ref, cw_ref, cb_ref, wa_ref, ba_ref, wx_ref, bx_ref, lam_ref, g_ref,
                y_ref, buf_ref, h_ref, *, tm):
    @pl.when(pl.program_id(1) == 0)
    def _():
        buf_ref[0:8, :] = jnp.zeros((8, GROUP_W), F32)
        h_ref[...] = jnp.zeros((1, GROUP_W), F32)

    u = u_ref[0].astype(F32)
    xraw, gate = u[:, :GROUP_W], u[:, GROUP_W:]
    buf_ref[8:8 + tm, :] = xraw
    xb = (cw_ref[3:4, :] * xraw + cw_ref[2:3, :] * buf_ref[7:7 + tm, :]
          + cw_ref[1:2, :] * buf_ref[6:6 + tm, :] + cw_ref[0:1, :] * buf_ref[5:5 + tm, :]
          + cb_ref[...])
    buf_ref[0:8, :] = xraw[tm - 8:, :]

    xbb = xb.astype(BF16)
    r = _sigmoid(_dot(xbb, wa_ref[...]) + ba_ref[...])
    i = _sigmoid(_dot(xbb, wx_ref[...]) + bx_ref[...])
    nlam = -lam_ref[...]
    softplus = jnp.maximum(nlam, 0.0) + jnp.log(1.0 + jnp.exp(-jnp.abs(nlam)))
    log_a = (-LRU_C) * r * softplus
    a = jnp.exp(log_a)
    b = jnp.sqrt(1.0 - a * a) * (i * xb)

    n_groups = tm // SUBLANES
    a = a.reshape(n_groups, SUBLANES, GROUP_W)
    b = b.reshape(n_groups, SUBLANES, GROUP_W)
    sub = lax.broadcasted_iota(jnp.int32, a.shape, 1)
    d = 1
    while d < SUBLANES:
        keep = sub >= d
        a_sh = jnp.where(keep, pltpu.roll(a, d, 1), 1.0)
        b_sh = jnp.where(keep, pltpu.roll(b, d, 1), 0.0)
        b = a * b_sh + b
        a = a * a_sh
        d *= 2
    carry = h_ref[...]
    groups = []
    for g in range(n_groups):
        hg = a[g] * carry + b[g]
        carry = hg[SUBLANES - 1:, :]
        groups.append(hg)
    h = jnp.concatenate(groups, axis=0)
    h_ref[...] = carry

    gelu = 0.5 * gate * (1.0 + jnp.tanh(math.sqrt(2.0 / math.pi) * (gate + 0.044715 * gate * gate * gate)))
    y_ref[0] = _rms_rows(h * gelu, g_ref[...]).astype(BF16)


def _lru_mixer(u, cw, cb, wa, ba, wx, bx, lam, g, tm):
    bsz, s, _ = u.shape
    row = pl.BlockSpec((1, GROUP_W), lambda b, i: (0, 0))
    mat = pl.BlockSpec((GROUP_W, GROUP_W), lambda b, i: (0, 0))
    return pl.pallas_call(
        functools.partial(_lru_kernel, tm=tm),
        grid=(bsz, s // tm),
        in_specs=[pl.BlockSpec((1, tm, U_LRU), lambda b, i: (b, i, U_RET // U_LRU)),
                  pl.BlockSpec((4, GROUP_W), lambda b, i: (0, 0)),
                  row, mat, row, mat, row, row, row],
        out_specs=pl.BlockSpec((1, tm, GROUP_W), lambda b, i: (b, i, 0)),
        out_shape=jax.ShapeDtypeStruct((bsz, s, GROUP_W), BF16),
        scratch_shapes=[pltpu.VMEM((tm + 8, GROUP_W), F32), pltpu.VMEM((1, GROUP_W), F32)],
        compiler_params=_cparams(("parallel", "arbitrary")),
        name="lru_mixer",
    )(u, cw, cb, wa, ba, wx, bx, lam, g)


def _ret_kernel(u_ref, tab_ref, ex_ref, inner_ref, qd_ref, kd_ref, cd_ref, bm_ref, gm_ref,
                mq_ref, mv_ref, g_ref, y_ref, st_ref):
    @pl.when(pl.program_id(0) == 0)
    def _():
        st_ref[...] = jnp.zeros(st_ref.shape, F32)

    for b in range(u_ref.shape[0]):
        u = u_ref[b].astype(F32)
        q, k = u[:, :GROUP_W], u[:, GROUP_W:2 * GROUP_W]
        v, gate = u[:, 2 * GROUP_W:3 * GROUP_W], u[:, 3 * GROUP_W:]
        cos, sin = _expand_trig(tab_ref[b], ex_ref[...])

        def rope(t):
            t1, t2 = t[:, :LANES], t[:, LANES:]
            return jnp.concatenate([t1 * cos - t2 * sin, t2 * cos + t1 * sin], axis=-1)

        qr = rope(q)
        kr = rope(k) * (RET_DK ** -0.5)
        krb = kr.astype(BF16)
        vb = v.astype(BF16)
        state = st_ref[b]
        o = _dot(qr.astype(BF16), state.astype(BF16)) * qd_ref[...]
        for h in range(N_HEADS):
            qh = (qr * mq_ref[h]).astype(BF16)
            sc = _dot_nt(qh, krb) * inner_ref[h]
            o = o + _dot(sc.astype(BF16), vb) * mv_ref[h]
        st_ref[b] = state * cd_ref[...] + bm_ref[...] * _dot_tn((kr * kd_ref[...]).astype(BF16), vb)

        gm = gm_ref[...]
        o_hi = o.astype(BF16)
        o_lo = (o - o_hi.astype(F32)).astype(BF16)
        mu = _dot(o_hi, gm) + _dot(o_lo, gm)
        dlt = o - mu
        var = _dot((dlt * dlt).astype(BF16), gm)
        y = dlt * lax.rsqrt(var + EPS)
        y = gate * _sigmoid(gate) * y
        y_ref[b] = _rms_rows(y, g_ref[...]).astype(BF16)


def _ret_mixer(u, trig, expand, consts, g, chunk):
    bsz, s, _ = u.shape
    inner, qd, kd, cd, bm, gm, mq, mv = consts
    full = lambda shape: pl.BlockSpec(shape, lambda i: (0,) * len(shape))
    tok = lambda w: pl.BlockSpec((bsz, chunk, w), lambda i: (0, i, 0))
    return pl.pallas_call(
        _ret_kernel,
        grid=(s // chunk,),
        in_specs=[tok(U_RET), tok(LANES), full(expand.shape),
                  full(inner.shape), full(qd.shape), full(kd.shape), full(cd.shape),
                  full(bm.shape), full(gm.shape), full(mq.shape), full(mv.shape),
                  full((1, GROUP_W))],
        out_specs=tok(GROUP_W),
        out_shape=jax.ShapeDtypeStruct((bsz, s, GROUP_W), BF16),
        scratch_shapes=[pltpu.VMEM((bsz, GROUP_W, GROUP_W), F32)],
        compiler_params=_cparams(("arbitrary",)),
        name="ret_mixer",
    )(u, trig, expand, inner, qd, kd, cd, bm, gm, mq, mv, g)


def _ret_consts(chunk):
    nh = N_HEADS
    f32 = np.float32
    log_g = np.log(f32(1.0) - f32(2.0) ** (f32(-5.0) - np.arange(nh, dtype=f32)))
    idx = np.arange(chunk, dtype=f32)
    rel = idx[:, None] - idx[None, :]
    inner = np.where(rel >= 0, np.exp(log_g[:, None, None] * np.maximum(rel, 0.0)), 0.0).astype(f32)
    v_head = np.arange(GROUP_W) // HEAD_DIM
    q_head = (np.arange(GROUP_W) % LANES) // (RET_DK // 2)
    qd = np.exp(log_g[v_head][None, :] * (idx[:, None] + 1.0)).astype(f32)
    kd = np.exp(log_g[q_head][None, :] * (chunk - 1.0 - idx[:, None])).astype(f32)
    cd = np.exp(log_g[v_head] * chunk)[None, :].astype(f32)
    bm = (q_head[:, None] == v_head[None, :]).astype(f32)
    gm = jnp.asarray((v_head[:, None] == v_head[None, :]).astype(f32) / HEAD_DIM, BF16)
    mq = (q_head[None, :] == np.arange(nh)[:, None]).astype(f32)[:, None, :]
    mv = (v_head[None, :] == np.arange(nh)[:, None]).astype(f32)[:, None, :]
    return tuple(jnp.asarray(a) for a in (inner, qd, kd, cd, bm)) + (gm, jnp.asarray(mq), jnp.asarray(mv))


def _trig_expanders():
    r16, r32 = MLA_ROPE // 2, RET_DK // 2
    ret = np.zeros((LANES, 2 * LANES), np.float32)
    mla = np.zeros((LANES, 2 * LANES), np.float32)
    for j in range(r32):
        for h in range(N_HEADS):
            ret[r16 + j, h * r32 + j] = 1.0
            ret[N_FREQ + r16 + j, LANES + h * r32 + j] = 1.0
    for j in range(r16):
        for half, sign in ((0, -1.0), (1, 1.0)):
            lane = MLA_NOPE + half * r16 + j
            mla[j, lane] = 1.0
            mla[N_FREQ + j, LANES + lane] = sign
    mla[2 * N_FREQ, :MLA_NOPE] = 1.0
    return jnp.asarray(ret, BF16), jnp.asarray(mla, BF16)


def _mla_prep_kernel(u_ref, tab_ref, ex_ref, mu_ref, invu_ref, gu_ref, wbig_ref, sq_ref,
                     invq_ref, gq_ref, gqs_ref, gk_ref, onev_ref, q_ref, k_ref, v_ref):
    x = u_ref[0].astype(F32)
    ss = _dot((x * x).astype(BF16), mu_ref[...]) * invu_ref[...]
    xn = (x * lax.rsqrt(ss + EPS) * gu_ref[...]).astype(BF16)
    big = _dot(xn, wbig_ref[...])
    hw = N_HEADS * LANES
    q, qs, kn, v = big[:, :hw], big[:, hw:2 * hw], big[:, 2 * hw:3 * hw], big[:, 3 * hw:4 * hw]
    kr, krs = big[:, 4 * hw:4 * hw + LANES], big[:, 4 * hw + LANES:]
    cos, sin = _expand_trig(tab_ref[0], ex_ref[...])
    rq = lax.rsqrt(_dot((q * q).astype(BF16), sq_ref[...]) * invq_ref[...] + EPS)
    rk = lax.rsqrt(_dot((kn * kn).astype(BF16), sq_ref[...]) * invq_ref[...] + EPS)
    krot = kr * cos + krs * sin
    for h in range(N_HEADS):
        sl = slice(h * LANES, (h + 1) * LANES)
        qh = (q[:, sl] * gq_ref[...] * cos + qs[:, sl] * gqs_ref[...] * sin) * rq[:, sl]
        q_ref[0, h] = qh.astype(BF16)
        k_ref[0, h] = (kn[:, sl] * gk_ref[...] * rk[:, sl] + krot).astype(BF16)
        v_ref[0, h] = (v[:, sl] + onev_ref[...]).astype(BF16)


def _mla_prep(u, trig, expand, mu, invu, gu, wbig, sq, invq, gq, gqs, gk, onev, tm):
    bsz, s, _ = u.shape
    full = lambda a: pl.BlockSpec(a.shape, lambda b, i: (0,) * a.ndim)
    tab = pl.BlockSpec((1, tm, LANES), lambda b, i: (b, i, 0))
    out = jax.ShapeDtypeStruct((bsz, N_HEADS, s, LANES), BF16)
    ospec = pl.BlockSpec((1, N_HEADS, tm, LANES), lambda b, i: (b, 0, i, 0))
    return pl.pallas_call(
        _mla_prep_kernel,
        grid=(bsz, s // tm),
        in_specs=[pl.BlockSpec((1, tm, U_MLA), lambda b, i: (b, i, (U_COLS - U_MLA) // U_MLA)),
                  tab, full(expand), full(mu), full(invu), full(gu), full(wbig), full(sq), full(invq),
                  full(gq), full(gqs), full(gk), full(onev)],
        out_specs=[ospec, ospec, ospec],
        out_shape=[out, out, out],
        compiler_params=_cparams(("parallel", "parallel")),
        name="mla_prep",
    )(u, trig, expand, mu, invu, gu, wbig, sq, invq, gq, gqs, gk, onev)


def _flash_kernel(q_ref, k_ref, v_ref, o_ref, sa_ref, sb_ref, mca_ref, mcb_ref, m_ref, acc_ref, *, tq):
    qi = pl.program_id(2)
    q = q_ref[0, 0]
    bufs = ((sa_ref, mca_ref), (sb_ref, mcb_ref))
    m_ref[...] = jnp.full((tq, LANES), NEG_BIG, F32)
    acc_ref[...] = jnp.zeros((tq, LANES), F32)

    def scores(c, masked, dst):
        s_ref, mc_ref = dst
        start = pl.multiple_of(c * tq, tq)
        s = _dot_nt(q, k_ref[0, 0, pl.ds(start, tq), :])
        if masked:
            row = qi * tq + lax.broadcasted_iota(jnp.int32, (tq, tq), 0)
            col = start + lax.broadcasted_iota(jnp.int32, (tq, tq), 1)
            s = jnp.where(col <= row, s, NEG_BIG)
        s_ref[...] = s
        mc_ref[...] = jnp.broadcast_to(jnp.max(s, axis=-1, keepdims=True), (tq, LANES))

    def accumulate(c, src):
        s_ref, mc_ref = src
        start = pl.multiple_of(c * tq, tq)
        m_prev = m_ref[...]
        m_new = jnp.maximum(m_prev, mc_ref[...])
        alpha = jnp.exp2(m_prev - m_new)
        p = jnp.exp2(s_ref[...] - jnp.tile(m_new, (1, tq // LANES)))
        pv = _dot(p.astype(BF16), v_ref[0, 0, pl.ds(start, tq), :])
        acc_ref[...] = alpha * acc_ref[...] + pv
        m_ref[...] = m_new

    def by_parity(c, fn):
        for par in range(2):
            pl.when(c % 2 == par)(functools.partial(fn, par))

    def pipelined(c, masked, par):
        scores(c + 1, masked, bufs[1 - par])
        accumulate(c, bufs[par])

    scores(0, True, bufs[0])
    n_plain = jnp.maximum(qi - 1, 0)

    def two_steps(i, carry):
        pipelined(2 * i, False, 0)
        pipelined(2 * i + 1, False, 1)
        return carry

    lax.fori_loop(0, n_plain // 2, two_steps, 0)

    @pl.when(n_plain % 2 == 1)
    def _():
        pipelined(n_plain - 1, False, 0)

    @pl.when(qi >= 1)
    def _():
        by_parity(qi - 1, functools.partial(pipelined, qi - 1, True))

    by_parity(qi, lambda par: accumulate(qi, bufs[par]))

    acc = acc_ref[...]
    lane = lax.broadcasted_iota(jnp.int32, (tq, LANES), 1)
    denom = jnp.sum(jnp.where(lane == HEAD_DIM, acc, 0.0), axis=-1, keepdims=True)
    o_ref[0, 0] = jnp.where(lane < HEAD_DIM, acc / denom, 0.0).astype(BF16)


def _flash_attention(q, k, v, tq):
    bsz, nh, s, _ = q.shape
    kv_spec = pl.BlockSpec((1, 1, s, LANES), lambda b, h, i: (b, h, 0, 0))
    blk = pl.BlockSpec((1, 1, tq, LANES), lambda b, h, i: (b, h, i, 0))
    stat = pltpu.VMEM((tq, LANES), F32)
    return pl.pallas_call(
        functools.partial(_flash_kernel, tq=tq),
        grid=(bsz, nh, s // tq),
        in_specs=[blk, kv_spec, kv_spec],
        out_specs=blk,
        out_shape=jax.ShapeDtypeStruct((bsz, nh, s, LANES), BF16),
        scratch_shapes=[pltpu.VMEM((tq, tq), F32), pltpu.VMEM((tq, tq), F32), stat, stat, stat, stat],
        compiler_params=_cparams(("parallel", "parallel", "arbitrary")),
        name="flash_attention",
    )(q, k, v)


def _outproj_kernel(x_ref, yc_ref, om_ref, yr_ref, yl_ref, wc_ref, wm_ref, wr_ref, wl_ref,
                    gmla_ref, gt_ref, gf_ref, scf_ref, shf_ref, wrt_ref, brt_ref, tri_ref,
                    xo_ref, h_ref, meta_ref, routes_ref, cnt_ref, run_ref, lg_ref):
    n = pl.program_id(0)

    @pl.when(n == 0)
    def _():
        run_ref[...] = jnp.zeros((1, LANES), F32)
        lg_ref[...] = jnp.zeros(lg_ref.shape, F32)

    lg_prev = lg_ref[...]
    routed = (n > 0).astype(F32)

    om = [om_ref[0, h].astype(F32) for h in range(N_HEADS)]
    ssq = om[0] * om[0]
    for h in range(1, N_HEADS):
        ssq = ssq + om[h] * om[h]
    r_mla = lax.rsqrt(jnp.sum(ssq, axis=-1, keepdims=True) / GROUP_W + EPS)
    y = _dot(yc_ref[0], wc_ref[...]) + _dot(yr_ref[0], wr_ref[...]) + _dot(yl_ref[0], wl_ref[...])
    for h in range(0, N_HEADS, 2):
        pair = jnp.concatenate([(om[h] * r_mla * gmla_ref[h]).astype(BF16),
                                (om[h + 1] * r_mla * gmla_ref[h + 1]).astype(BF16)], axis=1)
        y = y + _dot(pair, wm_ref[h // 2])
    x = x_ref[0] + gt_ref[0] * y
    xo_ref[0] = x
    hf = _rms_rows(x, gf_ref[...]) * (1.0 + scf_ref[0]) + shf_ref[0]
    h_ref[0] = _pack_bf16_pairs(hf)

    h_hi = hf.astype(BF16)
    h_lo = (hf - h_hi.astype(F32)).astype(BF16)
    both = _dot(h_hi, wrt_ref[...])
    lg_ref[...] = both[:, :LANES] + both[:, LANES:] + _dot(h_lo, wrt_ref[:, :LANES])

    lg = lg_prev
    tm = lg.shape[0]
    lane = lax.broadcasted_iota(jnp.int32, (tm, LANES), 1)
    bias = brt_ref[...]
    is_g = (lane >= N_EXPERTS) & (lane < N_EXPERTS + MOE_GROUPS)
    is_e = lane < N_EXPERTS

    def first_argmax(val):
        mx = jnp.max(val, axis=-1, keepdims=True)
        return jnp.min(jnp.where(val == mx, lane, LANES), axis=-1, keepdims=True)

    gl = jnp.where(is_g, lg, NEG_BIG)
    ge = jnp.exp(gl - jnp.max(gl, axis=-1, keepdims=True))
    gp = ge / jnp.sum(ge, axis=-1, keepdims=True)
    g_idx = first_argmax(jnp.where(is_g, gp + bias, NEG_BIG))
    g_weight = jnp.sum(jnp.where(lane == g_idx, gp, 0.0), axis=-1, keepdims=True)
    in_group = is_e & ((lane // EXPERTS_PER_GROUP) == (g_idx - N_EXPERTS))
    el = jnp.where(in_group, lg, NEG_BIG)
    ee = jnp.exp(el - jnp.max(el, axis=-1, keepdims=True))
    ep = ee / jnp.sum(ee, axis=-1, keepdims=True)
    score = jnp.where(in_group, ep + bias, NEG_BIG)
    i1 = first_argmax(score)
    sel1 = lane == i1
    i2 = first_argmax(jnp.where(sel1, NEG_BIG, score))
    sel2 = lane == i2
    p1 = jnp.sum(jnp.where(sel1, ep, 0.0), axis=-1, keepdims=True)
    p2 = jnp.sum(jnp.where(sel2, ep, 0.0), axis=-1, keepdims=True)
    psum = p1 + p2
    w1 = p1 / psum * g_weight
    w2 = p2 / psum * g_weight

    onehot = jnp.where(sel1, 1.0, jnp.where(sel2, 1.0, 0.0)).astype(BF16)
    incl = _dot(tri_ref[...], onehot) * routed
    base = run_ref[...] + incl - 1.0
    r1 = jnp.sum(jnp.where(sel1, base, 0.0), axis=-1, keepdims=True)
    r2 = jnp.sum(jnp.where(sel2, base, 0.0), axis=-1, keepdims=True)
    run_ref[...] = run_ref[...] + incl[tm - 1:tm, :]
    cnt_ref[0] = run_ref[...]
    fields = (i1.astype(F32), i2.astype(F32), r1, r2, w1, w2)
    meta = jnp.zeros((tm, LANES), F32)
    for pos, val in enumerate(fields):
        meta = jnp.where(lane == pos, val, meta)
    meta_ref[0] = meta
    routes_ref[...] = meta.T[:SUBLANES, :]


def _outproj(x, yc, om, yr, yl, wc, wm, wr, wl, gmla, gt, gf, scf, shf, wrt, brt, tm):
    bsz, s, d = x.shape
    nt = s // tm
    n_tiles = bsz * nt
    cur = lambda n: jnp.minimum(n, n_tiles - 1)
    lag = lambda n: jnp.maximum(n - 1, 0)
    full = lambda a: pl.BlockSpec(a.shape, lambda n: (0,) * a.ndim)
    tok = lambda w, f=cur: pl.BlockSpec((1, tm, w), lambda n: (f(n) // nt, f(n) % nt, 0))
    vec = pl.BlockSpec((1, 1, d), lambda n: (cur(n) // nt, 0, 0))
    tri = jnp.asarray(np.tril(np.ones((tm, tm), np.float32)), BF16)
    return pl.pallas_call(
        _outproj_kernel,
        grid=(n_tiles + 1,),
        in_specs=[tok(d), tok(GROUP_W),
                  pl.BlockSpec((1, N_HEADS, tm, LANES), lambda n: (cur(n) // nt, 0, cur(n) % nt, 0)),
                  tok(GROUP_W), tok(GROUP_W),
                  full(wc), full(wm), full(wr), full(wl), full(gmla), vec, full(gf), vec, vec,
                  full(wrt), full(brt), full(tri)],
        out_specs=[tok(d), tok(d // 2), tok(LANES, lag),
                   pl.BlockSpec((SUBLANES, tm), lambda n: (0, lag(n))),
                   pl.BlockSpec((1, 1, LANES), lambda n: (lag(n), 0, 0))],
        out_shape=[jax.ShapeDtypeStruct((bsz, s, d), F32), jax.ShapeDtypeStruct((bsz, s, d // 2), jnp.uint32),
                   jax.ShapeDtypeStruct((bsz, s, LANES), F32),
                   jax.ShapeDtypeStruct((SUBLANES, bsz * s), F32),
                   jax.ShapeDtypeStruct((bsz * nt, 1, LANES), F32)],
        scratch_shapes=[pltpu.VMEM((1, LANES), F32), pltpu.VMEM((tm, LANES), F32)],
        compiler_params=_cparams(("arbitrary",)),
        name="outproj_router",
    )(x, yc, om, yr, yl, wc, wm, wr, wl, gmla, gt, gf, scf, shf, wrt, brt, tri)


def _route_plan(routes, cnt, tmg):
    t = routes.shape[1]
    e = routes[0:2].astype(jnp.int32)
    r = routes[2:4].astype(jnp.int32)
    counts = cnt[-1, 0, :N_EXPERTS].astype(jnp.int32)
    padded = (counts + tmg - 1) // tmg * tmg
    ends = jnp.cumsum(padded)
    starts = ends - padded
    dest = jnp.take(starts, e) + r
    n_tiles = (2 * t) // tmg + N_EXPERTS
    tile_start = jnp.arange(n_tiles, dtype=jnp.int32) * tmg
    tile_exp = jnp.sum((ends[None, :] <= tile_start[:, None]).astype(jnp.int32), axis=1)
    tile_exp = jnp.minimum(tile_exp, N_EXPERTS - 1)
    n_used = (ends[-1:] // tmg).astype(jnp.int32)
    last_tile = jnp.maximum(ends - tmg, 0).astype(jnp.int32)
    return dest[0], dest[1], tile_exp, n_used, last_tile, padded.astype(jnp.int32)


def _dispatch_kernel(zs_ref, zv_ref, nu_ref, d1_ref, d2_ref, h_ref, xs_ref, zero_ref, sem, *, tmc, tmg):
    @pl.when(pl.program_id(0) == 0)
    def _():
        zero_ref[...] = jnp.zeros(zero_ref.shape, jnp.uint32)
        n_tiles = xs_ref.shape[0] // tmg

        def fill(start):
            return pltpu.make_async_copy(zero_ref, xs_ref.at[pl.ds(pl.multiple_of(start, tmg), tmg), :], sem)

        def fill_tail(j, carry, wait):
            cp = fill(j * tmg)
            cp.wait() if wait else cp.start()
            return carry

        for wait in (False, True):
            for e in range(N_EXPERTS):
                cp = fill(zs_ref[e])
                pl.when(zv_ref[e] > 0)(cp.wait if wait else cp.start)
            lax.fori_loop(nu_ref[0], n_tiles, functools.partial(fill_tail, wait=wait), 0)

    def push(r, carry):
        src = h_ref.at[pl.ds(r, 1), :]
        pltpu.make_async_copy(src, xs_ref.at[pl.ds(d1_ref[0, 0, r], 1), :], sem).start(priority=0)
        pltpu.make_async_copy(src, xs_ref.at[pl.ds(d2_ref[0, 0, r], 1), :], sem).start(priority=1)
        return carry

    lax.fori_loop(0, tmc, push, 0, unroll=8)
    one_row = pltpu.make_async_copy(h_ref.at[pl.ds(0, 1), :], xs_ref.at[pl.ds(0, 1), :], sem)
    for _ in range(2 * tmc):
        one_row.wait()


def _dispatch(hp, d1, d2, last_tile, padded, n_used, tmc, tmg):
    t, dw = hp.shape
    n_rows = 2 * t + N_EXPERTS * tmg
    smem_rows = pl.BlockSpec((1, 1, tmc), lambda i, zs, zv, nu: (i, 0, 0), memory_space=pltpu.SMEM)
    return pl.pallas_call(
        functools.partial(_dispatch_kernel, tmc=tmc, tmg=tmg),
        grid_spec=pltpu.PrefetchScalarGridSpec(
            num_scalar_prefetch=3, grid=(t // tmc,),
            in_specs=[smem_rows, smem_rows, pl.BlockSpec((tmc, dw), lambda i, zs, zv, nu: (i, 0))],
            out_specs=pl.BlockSpec(memory_space=pl.ANY),
            scratch_shapes=[pltpu.VMEM((tmg, dw), jnp.uint32), pltpu.SemaphoreType.DMA(())]),
        out_shape=jax.ShapeDtypeStruct((n_rows, dw), jnp.uint32),
        compiler_params=_cparams(("arbitrary",)),
        name="moe_dispatch",
    )(last_tile, padded, n_used, d1.reshape(t // tmc, 1, tmc), d2.reshape(t // tmc, 1, tmc), hp)


def _experts_kernel(te_ref, nu_ref, xs_ref, wg_ref, wu_ref, wd_ref, y_ref, wgb_ref, wub_ref, wdb_ref):
    j = pl.program_id(0)
    used = j < nu_ref[0]
    tile = jnp.minimum(j, nu_ref[0] - 1)

    @pl.when((j == 0) | (te_ref[tile] != te_ref[jnp.maximum(tile - 1, 0)]))
    def _():
        wgb_ref[...] = wg_ref[0, 0].astype(BF16)
        wub_ref[...] = wu_ref[0, 0].astype(BF16)
        wdb_ref[...] = wd_ref[0, 0].astype(BF16)

    @pl.when(used)
    def _():
        x = _unpack_bf16_pairs(xs_ref[...]).astype(BF16)
        gate = _dot(x, wgb_ref[...])
        hid = gate * _sigmoid(gate) * _dot(x, wub_ref[...])
        y_ref[...] = _pack_bf16_pairs(_dot(hid.astype(BF16), wdb_ref[...]))

    @pl.when(jnp.logical_not(used))
    def _():
        y_ref[...] = jnp.zeros(y_ref.shape, jnp.uint32)


def _experts(xs, tile_exp, n_used, layer, wg, wu, wd, tmg):
    n_rows, dw = xs.shape
    d = 2 * dw
    tile = lambda i, te, nu: jnp.minimum(i, nu[0] - 1)
    rows = pl.BlockSpec((tmg, dw), lambda i, te, nu: (tile(i, te, nu), 0))
    wspec = lambda shape: pl.BlockSpec((1, 1) + shape,
                                       lambda i, te, nu: (layer, te[tile(i, te, nu)], 0, 0))
    return pl.pallas_call(
        _experts_kernel,
        grid_spec=pltpu.PrefetchScalarGridSpec(
            num_scalar_prefetch=2, grid=(n_rows // tmg,),
            in_specs=[rows, wspec((d, D_EXPERT)), wspec((d, D_EXPERT)), wspec((D_EXPERT, d))],
            out_specs=pl.BlockSpec((tmg, dw), lambda i, te, nu: (i, 0)),
            scratch_shapes=[pltpu.VMEM((d, D_EXPERT), BF16), pltpu.VMEM((d, D_EXPERT), BF16),
                            pltpu.VMEM((D_EXPERT, d), BF16)]),
        out_shape=jax.ShapeDtypeStruct((n_rows, dw), jnp.uint32),
        compiler_params=_cparams(("arbitrary",)),
        name="moe_experts",
    )(tile_exp, n_used, xs, wg, wu, wd)


def _layer_weights(l, w_in, mla_q_norm_g, mla_w_uq, mla_kv_norm_g, mla_w_ukv, mla_q_qk_g,
                   mla_k_qk_g, lru_w_a, lru_w_x, mix_norm_g, w_out, router_group_w,
                   router_group_b, router_expert_w, router_expert_b):
    half = RET_DK // 2
    perm = np.concatenate([np.arange(half) + HEAD_DIM * h for h in range(N_HEADS)]
                          + [np.arange(half) + half + HEAD_DIM * h for h in range(N_HEADS)])
    w = w_in[l]
    o_mla, o_ret, o_lru = U_CONV, U_CONV + 352, U_CONV + 352 + U_RET
    w_ret = w[:, o_ret:o_ret + U_RET]
    w_ret = jnp.concatenate([w_ret[:, perm], w_ret[:, GROUP_W + perm], w_ret[:, 2 * GROUP_W:]], axis=1)
    w_all = jnp.concatenate([w_ret, w[:, o_lru:o_lru + U_LRU], w[:, :U_CONV], w[:, o_mla:o_mla + 352],
                             jnp.zeros((D_MODEL, U_MLA - 352), F32)], axis=1).astype(BF16)

    hw = N_HEADS * LANES
    r16 = MLA_ROPE // 2
    wq = mla_w_uq[l].reshape(Q_LORA, N_HEADS, MLA_QK)
    zq = jnp.zeros((Q_LORA, N_HEADS, LANES - MLA_QK), F32)
    q_cols = jnp.concatenate([wq, zq], axis=2).reshape(Q_LORA, hw)
    wq_sw = jnp.concatenate([jnp.zeros((Q_LORA, N_HEADS, MLA_NOPE), F32), wq[:, :, MLA_NOPE + r16:],
                             wq[:, :, MLA_NOPE:MLA_NOPE + r16], zq], axis=2).reshape(Q_LORA, hw)
    wkv = mla_w_ukv[l].reshape(KV_LORA, N_HEADS, MLA_NOPE + HEAD_DIM)
    zk = jnp.zeros((KV_LORA, N_HEADS, LANES - MLA_NOPE), F32)
    k_cols = jnp.concatenate([wkv[:, :, :MLA_NOPE], zk], axis=2).reshape(KV_LORA, hw)
    v_cols = jnp.concatenate([wkv[:, :, MLA_NOPE:], zk], axis=2).reshape(KV_LORA, hw)
    eye = jnp.eye(MLA_ROPE, dtype=F32)
    place = jnp.concatenate([jnp.zeros((MLA_ROPE, MLA_NOPE), F32), eye,
                             jnp.zeros((MLA_ROPE, LANES - MLA_QK), F32)], axis=1)
    eye_sw = jnp.concatenate([eye[:, r16:], eye[:, :r16]], axis=1)
    place_sw = jnp.concatenate([jnp.zeros((MLA_ROPE, MLA_NOPE), F32), eye_sw,
                                jnp.zeros((MLA_ROPE, LANES - MLA_QK), F32)], axis=1)
    n_big = 4 * hw + 2 * LANES
    wbig = jnp.zeros((U_MLA, n_big), F32)
    wbig = wbig.at[:Q_LORA, :hw].set(q_cols).at[:Q_LORA, hw:2 * hw].set(wq_sw)
    wbig = wbig.at[Q_LORA:Q_LORA + KV_LORA, 2 * hw:3 * hw].set(k_cols)
    wbig = wbig.at[Q_LORA:Q_LORA + KV_LORA, 3 * hw:4 * hw].set(v_cols)
    wbig = wbig.at[Q_LORA + KV_LORA:352, 4 * hw:4 * hw + LANES].set(place)
    wbig = wbig.at[Q_LORA + KV_LORA:352, 4 * hw + LANES:].set(place_sw)
    wbig = wbig.astype(BF16)

    gu = jnp.concatenate([mla_q_norm_g[l], mla_kv_norm_g[l], mla_k_qk_g[l][MLA_NOPE:],
                          jnp.zeros((U_MLA - 352,), F32)])[None, :]
    qscale = (MLA_QK ** -0.5) * math.log2(math.e)
    gq_full = mla_q_qk_g[l]
    pad = jnp.zeros((LANES - MLA_QK,), F32)
    gq = (jnp.concatenate([gq_full, pad]) * qscale)[None, :]
    gqs = (jnp.concatenate([jnp.zeros((MLA_NOPE,), F32), gq_full[MLA_NOPE + r16:],
                            gq_full[MLA_NOPE:MLA_NOPE + r16], pad]) * qscale)[None, :]
    gk = jnp.concatenate([mla_k_qk_g[l][:MLA_NOPE], jnp.zeros((LANES - MLA_NOPE,), F32)])[None, :]

    def blockdiag(wb):
        out = jnp.zeros((GROUP_W, GROUP_W), F32)
        for n in range(wb.shape[0]):
            out = out.at[n * HEAD_DIM:(n + 1) * HEAD_DIM, n * HEAD_DIM:(n + 1) * HEAD_DIM].set(wb[n])
        return out.astype(BF16)

    gmix = mix_norm_g[l]
    wo = w_out[l].astype(BF16)
    wm = wo[GROUP_W:2 * GROUP_W].reshape(N_HEADS, HEAD_DIM, D_MODEL)
    wm = jnp.concatenate([wm, jnp.zeros((N_HEADS, LANES - HEAD_DIM, D_MODEL), BF16)], axis=1)
    wm = wm.reshape(N_HEADS // 2, 2 * LANES, D_MODEL)
    gmla = jnp.concatenate([gmix[GROUP_W:2 * GROUP_W].reshape(N_HEADS, 1, HEAD_DIM),
                            jnp.zeros((N_HEADS, 1, LANES - HEAD_DIM), F32)], axis=2)
    wrt = jnp.concatenate([router_expert_w[l], router_group_w[l],
                           jnp.zeros((D_MODEL, LANES - N_EXPERTS - MOE_GROUPS), F32)], axis=1)
    wrt_hi = wrt.astype(BF16)
    wrt = jnp.concatenate([wrt_hi, (wrt - wrt_hi.astype(F32)).astype(BF16)], axis=1)
    brt = jnp.concatenate([router_expert_b[l], router_group_b[l],
                           jnp.zeros((LANES - N_EXPERTS - MOE_GROUPS,), F32)])[None, :]
    return dict(w_all=w_all, wbig=wbig, gu=gu, gq=gq, gqs=gqs, gk=gk,
                wa=blockdiag(lru_w_a[l]), wx=blockdiag(lru_w_x[l]),
                g_conv=gmix[None, :GROUP_W], g_ret=gmix[None, 2 * GROUP_W:3 * GROUP_W],
                g_lru=gmix[None, 3 * GROUP_W:], gmla=gmla,
                wc=wo[:GROUP_W], wm=wm, wr=wo[2 * GROUP_W:3 * GROUP_W], wl=wo[3 * GROUP_W:],
                wrt=wrt, brt=brt)


def _mla_consts():
    seg_u = np.concatenate([np.zeros(Q_LORA), np.ones(KV_LORA), 2 * np.ones(MLA_ROPE),
                            3 * np.ones(U_MLA - 352)])
    mu = jnp.asarray(seg_u[:, None] == seg_u[None, :], BF16)
    invu = jnp.asarray(np.concatenate([np.full(Q_LORA, 1.0 / Q_LORA), np.full(KV_LORA, 1.0 / KV_LORA),
                                       np.full(MLA_ROPE, 1.0 / MLA_ROPE), np.ones(U_MLA - 352)]), F32)[None, :]
    lane = np.arange(N_HEADS * LANES)
    seg_q = (lane // LANES) * 3 + np.where(lane % LANES < MLA_NOPE, 0, np.where(lane % LANES < MLA_QK, 1, 2))
    sq = jnp.asarray(seg_q[:, None] == seg_q[None, :], BF16)
    inv_head = np.concatenate([np.full(MLA_NOPE, 1.0 / MLA_NOPE), np.full(MLA_ROPE, 1.0 / MLA_ROPE),
                               np.ones(LANES - MLA_QK)])
    invq = jnp.asarray(np.tile(inv_head, N_HEADS), F32)[None, :]
    onev = jnp.asarray((np.arange(LANES) == HEAD_DIM).astype(np.float32))[None, :]
    return mu, invu, sq, invq, onev


def kernel(x, c, positions, ada_w, ada_b, norm_mix_g, w_in, conv_w, mla_q_norm_g, mla_w_uq, mla_kv_norm_g, mla_w_ukv, mla_q_qk_g, mla_k_qk_g, lru_conv_w, lru_conv_b, lru_w_a, lru_b_a, lru_w_x, lru_b_x, lru_lambda, mix_norm_g, w_out, norm_ffn_g, router_group_w, router_group_b, router_expert_w, router_expert_b, exp_w_gate, exp_w_up, exp_w_down):
    bsz, s, d = x.shape
    depth = ada_w.shape[0]
    tm = min(512, s)
    chunk = min(256, s)
    tq = min(1024, s)
    tmc = min(512, s)
    tmd = min(1024, s)
    tmg = 512

    inv = jnp.concatenate([1.0 / (ROPE_BASE ** (jnp.arange(0, MLA_ROPE, 2, dtype=F32) / MLA_ROPE)),
                           1.0 / (ROPE_BASE ** (jnp.arange(0, RET_DK, 2, dtype=F32) / RET_DK))])[:, None]
    trig = _rope_tables(positions, inv)
    ex_ret, ex_mla = _trig_expanders()

    c_pad = jnp.concatenate([c, jnp.zeros((8 - bsz, d), F32)], axis=0)
    mod = _modulation(c_pad, ada_w, ada_b)[:, :bsz]
    ret_consts = _ret_consts(chunk)
    mu, invu, sq, invq, onev = _mla_consts()

    pending = None
    for l in range(depth):
        sh_m, sc_m, gt_m, sh_f, sc_f, gt_f = [m[:, None, :] for m in jnp.split(mod[l], 6, axis=-1)]
        lw = _layer_weights(l, w_in, mla_q_norm_g, mla_w_uq, mla_kv_norm_g, mla_w_ukv, mla_q_qk_g,
                            mla_k_qk_g, lru_w_a, lru_w_x, mix_norm_g, w_out, router_group_w,
                            router_group_b, router_expert_w, router_expert_b)
        if pending is None:
            u = _inproj(x, norm_mix_g[l][None, :], sc_m, sh_m, lw["w_all"], tm)
        else:
            x, u = _combine(x, *pending, tm, proj=(norm_mix_g[l][None, :], sc_m, sh_m, lw["w_all"]))
        y_conv = _conv_mixer(u, conv_w[l], lw["g_conv"], tm)
        y_lru = _lru_mixer(u, lru_conv_w[l], lru_conv_b[l][None, :], lw["wa"], lru_b_a[l][None, :],
                           lw["wx"], lru_b_x[l][None, :], lru_lambda[l][None, :], lw["g_lru"], tm)
        y_ret = _ret_mixer(u, trig, ex_ret, ret_consts, lw["g_ret"], chunk)
        q, k, v = _mla_prep(u, trig, ex_mla, mu, invu, lw["gu"], lw["wbig"], sq, invq,
                            lw["gq"], lw["gqs"], lw["gk"], onev, tm)
        o_mla = _flash_attention(q, k, v, tq)
        x, hp, meta, routes, cnt = _outproj(x, y_conv, o_mla, y_ret, y_lru, lw["wc"], lw["wm"], lw["wr"],
                                            lw["wl"], lw["gmla"], gt_m, norm_ffn_g[l][None, :], sc_f, sh_f,
                                            lw["wrt"], lw["brt"], tm)
        d1, d2, tile_exp, n_used, last_tile, padded = _route_plan(routes, cnt, tmg)
        xs = _dispatch(hp.reshape(bsz * s, d // 2), d1, d2, last_tile, padded, n_used, tmd, tmg)
        ys = _experts(xs, tile_exp, n_used, l, exp_w_gate, exp_w_up, exp_w_down, tmg)
        pending = (meta, gt_f, ys, d1, d2)
    return _combine(x, *pending, tmc)[0]
```

```python
import functools
import math

import jax
import jax.numpy as jnp
import numpy as np
from jax import lax
from jax.experimental import pallas as pl
from jax.experimental.pallas import tpu as pltpu

F32 = jnp.float32
BF16 = jnp.bfloat16
HIGHEST = lax.Precision.HIGHEST

D_MODEL = 1024
GROUP_W = 256
HEAD_DIM = 64
N_HEADS = 4
MLA_NOPE = 64
MLA_ROPE = 32
MLA_QK = 96
Q_LORA = 192
KV_LORA = 128
RET_DK = 64
LRU_C = 8.0
MOE_GROUPS = 4
EXPERTS_PER_GROUP = 8
N_EXPERTS = 32
D_EXPERT = 256
ROPE_BASE = 10000.0
EPS = 1e-6

LANES = 128
SUBLANES = 8
MXU_DIM = 256
U_RET, U_LRU, U_CONV, U_MLA = 1024, 512, 768, 384
U_COLS = U_RET + U_LRU + U_CONV + U_MLA
N_FREQ = MLA_ROPE // 2 + RET_DK // 2
NEG_BIG = -1e30
VMEM_LIMIT = 56 * 1024 * 1024


def _cparams(sem):
    return pltpu.CompilerParams(dimension_semantics=sem, vmem_limit_bytes=VMEM_LIMIT)


def _dot(a, b):
    return jnp.dot(a, b, preferred_element_type=F32)


def _dot_nt(a, b):
    return lax.dot_general(a, b, (((1,), (1,)), ((), ())), preferred_element_type=F32)


def _dot_tn(a, b):
    return lax.dot_general(a, b, (((0,), (0,)), ((), ())), preferred_element_type=F32)


def _rms_rows(y, g):
    return y * lax.rsqrt(jnp.mean(y * y, axis=-1, keepdims=True) + EPS) * g


def _sigmoid(x):
    return 0.5 * jnp.tanh(0.5 * x) + 0.5


def _pack_bf16_pairs(a):
    k = a.shape[1] // 2
    rounded = a.astype(BF16).astype(F32)
    lo = lax.bitcast_convert_type(rounded[:, :k], jnp.uint32) >> 16
    hi = lax.bitcast_convert_type(rounded[:, k:], jnp.uint32) & jnp.uint32(0xFFFF0000)
    return lo | hi


def _unpack_bf16_pairs(w):
    lo = lax.bitcast_convert_type(w << 16, F32)
    hi = lax.bitcast_convert_type(w & jnp.uint32(0xFFFF0000), F32)
    return jnp.concatenate([lo, hi], axis=1)


def _rope_kernel(pos_ref, inv_ref, tab_ref):
    ang = pos_ref[0].astype(F32) * inv_ref[...]
    row = lax.broadcasted_iota(jnp.int32, (LANES - 2 * N_FREQ, ang.shape[1]), 0)
    pad = jnp.where(row == 0, 1.0, 0.0)
    tab_ref[0] = jnp.concatenate([jnp.cos(ang), jnp.sin(ang), pad], axis=0).T


def _rope_tables(positions, inv):
    bsz, s = positions.shape
    ts = min(s, 2048)
    return pl.pallas_call(
        _rope_kernel,
        grid=(bsz, s // ts),
        in_specs=[pl.BlockSpec((1, 1, ts), lambda b, i: (b, 0, i)),
                  pl.BlockSpec((N_FREQ, 1), lambda b, i: (0, 0))],
        out_specs=pl.BlockSpec((1, ts, LANES), lambda b, i: (b, i, 0)),
        out_shape=jax.ShapeDtypeStruct((bsz, s, LANES), F32),
        compiler_params=_cparams(("parallel", "parallel")),
        name="rope_tables",
    )(positions.reshape(bsz, 1, s), inv)


def _expand_trig(tab, expand):
    hi = tab.astype(BF16)
    lo = (tab - hi.astype(F32)).astype(BF16)
    trig = _dot(hi, expand) + _dot(lo, expand)
    return trig[:, :LANES], trig[:, LANES:]


def _mod_kernel(c_ref, w_ref, b_ref, o_ref):
    c = c_ref[...]
    ca = c * _sigmoid(c)
    o_ref[0] = jnp.dot(ca, w_ref[0], precision=HIGHEST, preferred_element_type=F32) + b_ref[0]


def _modulation(c_pad, ada_w, ada_b):
    nl, d, n = ada_w.shape
    tn = 1536
    return pl.pallas_call(
        _mod_kernel,
        grid=(nl, n // tn),
        in_specs=[pl.BlockSpec((8, d), lambda l, j: (0, 0)),
                  pl.BlockSpec((1, d, tn), lambda l, j: (l, 0, j)),
                  pl.BlockSpec((1, 1, tn), lambda l, j: (l, 0, j))],
        out_specs=pl.BlockSpec((1, 8, tn), lambda l, j: (l, 0, j)),
        out_shape=jax.ShapeDtypeStruct((nl, 8, n), F32),
        compiler_params=_cparams(("parallel", "parallel")),
        name="adaln_mod",
    )(c_pad, ada_w, ada_b.reshape(nl, 1, n))


def _norm_project(x, g_ref, sc_ref, sh_ref, w_ref):
    h = _rms_rows(x, g_ref[...]) * (1.0 + sc_ref[0]) + sh_ref[0]
    return _dot(h.astype(BF16), w_ref[...]).astype(BF16)


def _inproj_kernel(x_ref, g_ref, sc_ref, sh_ref, w_ref, u_ref):
    u_ref[0] = _norm_project(x_ref[0], g_ref, sc_ref, sh_ref, w_ref)


def _router_weights(meta):
    lane = lax.broadcasted_iota(jnp.int32, meta.shape, 1)
    w1 = jnp.sum(jnp.where(lane == 4, meta, 0.0), axis=-1, keepdims=True)
    w2 = jnp.sum(jnp.where(lane == 5, meta, 0.0), axis=-1, keepdims=True)
    return w1, w2


def _combine_kernel(d1c_ref, d2c_ref, d1n_ref, d2n_ref, x_ref, meta_ref, gtf_ref, y_hbm, *rest, tm, project):
    if project:
        g_ref, sc_ref, sh_ref, w_ref, xo_ref, u_ref, *scratch = rest
    else:
        xo_ref, *scratch = rest
    a1_ref, a2_ref, b1_ref, b2_ref, sem_a, sem_b = scratch
    n = pl.program_id(0)
    bufs = ((a1_ref, a2_ref, sem_a), (b1_ref, b2_ref, sem_b))

    def pull(d_ref, buf, sem, r):
        return pltpu.make_async_copy(y_hbm.at[pl.ds(d_ref[0, 0, r], 1), :], buf.at[pl.ds(r, 1), :], sem)

    def wait_tile(buf, sem):
        one_row = pltpu.make_async_copy(y_hbm.at[pl.ds(0, 1), :], buf.at[pl.ds(0, 1), :], sem)
        for _ in range(2 * tm):
            one_row.wait()

    @pl.when(n == 0)
    def _():
        def first(r, carry):
            pull(d1c_ref, a1_ref, sem_a, r).start(priority=0)
            pull(d2c_ref, a2_ref, sem_a, r).start(priority=1)
            return carry
        lax.fori_loop(0, tm, first, 0, unroll=8)

    def step(par):
        y1_ref, y2_ref, sem = bufs[par]
        n1_ref, n2_ref, nsem = bufs[1 - par]
        wait_tile(y1_ref, sem)
        for r in range(tm):
            pull(d1n_ref, n1_ref, nsem, r).start(priority=0)
            pull(d2n_ref, n2_ref, nsem, r).start(priority=1)
        w1, w2 = _router_weights(meta_ref[0])
        y = w1 * _unpack_bf16_pairs(y1_ref[...]) + w2 * _unpack_bf16_pairs(y2_ref[...])
        x = x_ref[0] + gtf_ref[0] * y
        xo_ref[0] = x
        if project:
            u_ref[0] = _norm_project(x, g_ref, sc_ref, sh_ref, w_ref)

        @pl.when(n == pl.num_programs(0) - 1)
        def _():
            wait_tile(n1_ref, nsem)

    for par in range(2):
        pl.when(n % 2 == par)(functools.partial(step, par))


def _combine(x, meta, gtf, ys, d1, d2, tm, proj=None):
    bsz, s, d = x.shape
    nt = s // tm
    n_steps = bsz * nt
    nxt = lambda n: jnp.minimum(n + 1, n_steps - 1)
    smem = lambda f: pl.BlockSpec((1, 1, tm), lambda n: (f(n), 0, 0), memory_space=pltpu.SMEM)
    tok = lambda wdt: pl.BlockSpec((1, tm, wdt), lambda n: (n // nt, n % nt, 0))
    vec = pl.BlockSpec((1, 1, d), lambda n: (n // nt, 0, 0))
    buf = pltpu.VMEM((tm, d // 2), jnp.uint32)
    dd1, dd2 = d1.reshape(n_steps, 1, tm), d2.reshape(n_steps, 1, tm)
    in_specs = [smem(lambda n: n), smem(lambda n: n), smem(nxt), smem(nxt),
                tok(d), tok(LANES), vec, pl.BlockSpec(memory_space=pl.ANY)]
    out_specs = [tok(d)]
    out_shape = [jax.ShapeDtypeStruct((bsz, s, d), F32)]
    args = (dd1, dd2, dd1, dd2, x, meta, gtf, ys)
    if proj is not None:
        in_specs += [pl.BlockSpec((1, d), lambda n: (0, 0)), vec, vec, pl.BlockSpec((d, U_COLS), lambda n: (0, 0))]
        out_specs.append(tok(U_COLS))
        out_shape.append(jax.ShapeDtypeStruct((bsz, s, U_COLS), BF16))
        args += tuple(proj)
    return pl.pallas_call(
        functools.partial(_combine_kernel, tm=tm, project=proj is not None),
        grid=(n_steps,),
        in_specs=in_specs,
        out_specs=out_specs,
        out_shape=out_shape,
        scratch_shapes=[buf, buf, buf, buf, pltpu.SemaphoreType.DMA(()), pltpu.SemaphoreType.DMA(())],
        compiler_params=_cparams(("arbitrary",)),
        name="moe_combine_inproj" if proj is not None else "moe_combine",
    )(*args)


def _inproj(x, g, sc, sh, w, tm):
    bsz, s, d = x.shape
    vec = pl.BlockSpec((1, 1, d), lambda b, i: (b, 0, 0))
    return pl.pallas_call(
        _inproj_kernel,
        grid=(bsz, s // tm),
        in_specs=[pl.BlockSpec((1, tm, d), lambda b, i: (b, i, 0)),
                  pl.BlockSpec((1, d), lambda b, i: (0, 0)),
                  vec, vec,
                  pl.BlockSpec((d, U_COLS), lambda b, i: (0, 0))],
        out_specs=pl.BlockSpec((1, tm, U_COLS), lambda b, i: (b, i, 0)),
        out_shape=jax.ShapeDtypeStruct((bsz, s, U_COLS), BF16),
        compiler_params=_cparams(("parallel", "parallel")),
        name="inproj",
    )(x, g, sc, sh, w)


def _conv_kernel(u_ref, w_ref, g_ref, y_ref, buf_ref, *, tm):
    @pl.when(pl.program_id(1) == 0)
    def _():
        buf_ref[0:8, :] = jnp.zeros((8, GROUP_W), F32)

    u = u_ref[0].astype(F32)
    b_gate, c_gate, xin = u[:, :GROUP_W], u[:, GROUP_W:2 * GROUP_W], u[:, 2 * GROUP_W:]
    cx = c_gate * xin
    buf_ref[8:8 + tm, :] = cx
    conv = (w_ref[2:3, :] * cx + w_ref[1:2, :] * buf_ref[7:7 + tm, :]
            + w_ref[0:1, :] * buf_ref[6:6 + tm, :])
    buf_ref[0:8, :] = cx[tm - 8:, :]
    y_ref[0] = _rms_rows(b_gate * conv, g_ref[...]).astype(BF16)


def _conv_mixer(u, w, g, tm):
    bsz, s, _ = u.shape
    return pl.pallas_call(
        functools.partial(_conv_kernel, tm=tm),
        grid=(bsz, s // tm),
        in_specs=[pl.BlockSpec((1, tm, U_CONV), lambda b, i: (b, i, (U_RET + U_LRU) // U_CONV)),
                  pl.BlockSpec((3, GROUP_W), lambda b, i: (0, 0)),
                  pl.BlockSpec((1, GROUP_W), lambda b, i: (0, 0))],
        out_specs=pl.BlockSpec((1, tm, GROUP_W), lambda b, i: (b, i, 0)),
        out_shape=jax.ShapeDtypeStruct((bsz, s, GROUP_W), BF16),
        scratch_shapes=[pltpu.VMEM((tm + 8, GROUP_W), F32)],
        compiler_params=_cparams(("parallel", "arbitrary")),
        name="conv_mixer",
    )(u, w, g)


def _lru_kernel(u_ref, cw_ref, cb_ref, wa_ref, ba_ref, wx_ref, bx_ref, lam_ref, g_ref,
                y_ref, buf_ref, h_ref, *, tm):
    @pl.when(pl.program_id(1) == 0)
    def _():
        buf_ref[0:8, :] = jnp.zeros((8, GROUP_W), F32)
        h_ref[...] = jnp.zeros((1, GROUP_W), F32)

    u = u_ref[0].astype(F32)
    xraw, gate = u[:, :GROUP_W], u[:, GROUP_W:]
    buf_ref[8:8 + tm, :] = xraw
    xb = (cw_ref[3:4, :] * xraw + cw_ref[2:3, :] * buf_ref[7:7 + tm, :]
          + cw_ref[1:2, :] * buf_ref[6:6 + tm, :] + cw_ref[0:1, :] * buf_ref[5:5 + tm, :]
          + cb_ref[...])
    buf_ref[0:8, :] = xraw[tm - 8:, :]

    xbb = xb.astype(BF16)
    r = _sigmoid(_dot(xbb, wa_ref[...]) + ba_ref[...])
    i = _sigmoid(_dot(xbb, wx_ref[...]) + bx_ref[...])
    nlam = -lam_ref[...]
    softplus = jnp.maximum(nlam, 0.0) + jnp.log(1.0 + jnp.exp(-jnp.abs(nlam)))
    log_a = (-LRU_C) * r * softplus
    a = jnp.exp(log_a)
    b = jnp.sqrt(1.0 - a * a) * (i * xb)

    n_groups = tm // SUBLANES
    a = a.reshape(n_groups, SUBLANES, GROUP_W)
    b = b.reshape(n_groups, SUBLANES, GROUP_W)
    sub = lax.broadcasted_iota(jnp.int32, a.shape, 1)
    d = 1
    while d < SUBLANES:
        keep = sub >= d
        a_sh = jnp.where(keep, pltpu.roll(a, d, 1), 1.0)
        b_sh = jnp.where(keep, pltpu.roll(b, d, 1), 0.0)
        b = a * b_sh + b
        a = a * a_sh
        d *= 2
    carry = h_ref[...]
    groups = []
    for g in range(n_groups):
        hg = a[g] * carry + b[g]
        carry = hg[SUBLANES - 1:, :]
        groups.append(hg)
    h = jnp.concatenate(groups, axis=0)
    h_ref[...] = carry

    gelu = 0.5 * gate * (1.0 + jnp.tanh(math.sqrt(2.0 / math.pi) * (gate + 0.044715 * gate * gate * gate)))
    y_ref[0] = _rms_rows(h * gelu, g_ref[...]).astype(BF16)


def _lru_mixer(u, cw, cb, wa, ba, wx, bx, lam, g, tm):
    bsz, s, _ = u.shape
    row = pl.BlockSpec((1, GROUP_W), lambda b, i: (0, 0))
    mat = pl.BlockSpec((GROUP_W, GROUP_W), lambda b, i: (0, 0))
    return pl.pallas_call(
        functools.partial(_lru_kernel, tm=tm),
        grid=(bsz, s // tm),
        in_specs=[pl.BlockSpec((1, tm, U_LRU), lambda b, i: (b, i, U_RET // U_LRU)),
                  pl.BlockSpec((4, GROUP_W), lambda b, i: (0, 0)),
                  row, mat, row, mat, row, row, row],
        out_specs=pl.BlockSpec((1, tm, GROUP_W), lambda b, i: (b, i, 0)),
        out_shape=jax.ShapeDtypeStruct((bsz, s, GROUP_W), BF16),
        scratch_shapes=[pltpu.VMEM((tm + 8, GROUP_W), F32), pltpu.VMEM((1, GROUP_W), F32)],
        compiler_params=_cparams(("parallel", "arbitrary")),
        name="lru_mixer",
    )(u, cw, cb, wa, ba, wx, bx, lam, g)


def _ret_kernel(u_ref, tab_ref, ex_ref, inner_ref, qd_ref, kd_ref, cd_ref, bm_ref, gm_ref,
                mq_ref, mv_ref, g_ref, y_ref, st_ref):
    @pl.when(pl.program_id(0) == 0)
    def _():
        st_ref[...] = jnp.zeros(st_ref.shape, F32)

    for b in range(u_ref.shape[0]):
        u = u_ref[b].astype(F32)
        q, k = u[:, :GROUP_W], u[:, GROUP_W:2 * GROUP_W]
        v, gate = u[:, 2 * GROUP_W:3 * GROUP_W], u[:, 3 * GROUP_W:]
        cos, sin = _expand_trig(tab_ref[b], ex_ref[...])

        def rope(t):
            t1, t2 = t[:, :LANES], t[:, LANES:]
            return jnp.concatenate([t1 * cos - t2 * sin, t2 * cos + t1 * sin], axis=-1)

        qr = rope(q)
        kr = rope(k) * (RET_DK ** -0.5)
        krb = kr.astype(BF16)
        vb = v.astype(BF16)
        state = st_ref[b]
        o = _dot(qr.astype(BF16), state.astype(BF16)) * qd_ref[...]
        for h in range(N_HEADS):
            qh = (qr * mq_ref[h]).astype(BF16)
            sc = _dot_nt(qh, krb) * inner_ref[h]
            o = o + _dot(sc.astype(BF16), vb) * mv_ref[h]
        st_ref[b] = state * cd_ref[...] + bm_ref[...] * _dot_tn((kr * kd_ref[...]).astype(BF16), vb)

        gm = gm_ref[...]
        o_hi = o.astype(BF16)
        o_lo = (o - o_hi.astype(F32)).astype(BF16)
        mu = _dot(o_hi, gm) + _dot(o_lo, gm)
        dlt = o - mu
        var = _dot((dlt * dlt).astype(BF16), gm)
        y = dlt * lax.rsqrt(var + EPS)
        y = gate * _sigmoid(gate) * y
        y_ref[b] = _rms_rows(y, g_ref[...]).astype(BF16)


def _ret_mixer(u, trig, expand, consts, g, chunk):
    bsz, s, _ = u.shape
    inner, qd, kd, cd, bm, gm, mq, mv = consts
    full = lambda shape: pl.BlockSpec(shape, lambda i: (0,) * len(shape))
    tok = lambda w: pl.BlockSpec((bsz, chunk, w), lambda i: (0, i, 0))
    return pl.pallas_call(
        _ret_kernel,
        grid=(s // chunk,),
        in_specs=[tok(U_RET), tok(LANES), full(expand.shape),
                  full(inner.shape), full(qd.shape), full(kd.shape), full(cd.shape),
                  full(bm.shape), full(gm.shape), full(mq.shape), full(mv.shape),
                  full((1, GROUP_W))],
        out_specs=tok(GROUP_W),
        out_shape=jax.ShapeDtypeStruct((bsz, s, GROUP_W), BF16),
        scratch_shapes=[pltpu.VMEM((bsz, GROUP_W, GROUP_W), F32)],
        compiler_params=_cparams(("arbitrary",)),
        name="ret_mixer",
    )(u, trig, expand, inner, qd, kd, cd, bm, gm, mq, mv, g)


def _ret_consts(chunk):
    nh = N_HEADS
    f32 = np.float32
    log_g = np.log(f32(1.0) - f32(2.0) ** (f32(-5.0) - np.arange(nh, dtype=f32)))
    idx = np.arange(chunk, dtype=f32)
    rel = idx[:, None] - idx[None, :]
    inner = np.where(rel >= 0, np.exp(log_g[:, None, None] * np.maximum(rel, 0.0)), 0.0).astype(f32)
    v_head = np.arange(GROUP_W) // HEAD_DIM
    q_head = (np.arange(GROUP_W) % LANES) // (RET_DK // 2)
    qd = np.exp(log_g[v_head][None, :] * (idx[:, None] + 1.0)).astype(f32)
    kd = np.exp(log_g[q_head][None, :] * (chunk - 1.0 - idx[:, None])).astype(f32)
    cd = np.exp(log_g[v_head] * chunk)[None, :].astype(f32)
    bm = (q_head[:, None] == v_head[None, :]).astype(f32)
    gm = jnp.asarray((v_head[:, None] == v_head[None, :]).astype(f32) / HEAD_DIM, BF16)
    mq = (q_head[None, :] == np.arange(nh)[:, None]).astype(f32)[:, None, :]
    mv = (v_head[None, :] == np.arange(nh)[:, None]).astype(f32)[:, None, :]
    return tuple(jnp.asarray(a) for a in (inner, qd, kd, cd, bm)) + (gm, jnp.asarray(mq), jnp.asarray(mv))


def _trig_expanders():
    r16, r32 = MLA_ROPE // 2, RET_DK // 2
    ret = np.zeros((LANES, 2 * LANES), np.float32)
    mla = np.zeros((LANES, 2 * LANES), np.float32)
    for j in range(r32):
        for h in range(N_HEADS):
            ret[r16 + j, h * r32 + j] = 1.0
            ret[N_FREQ + r16 + j, LANES + h * r32 + j] = 1.0
    for j in range(r16):
        for half, sign in ((0, -1.0), (1, 1.0)):
            lane = MLA_NOPE + half * r16 + j
            mla[j, lane] = 1.0
            mla[N_FREQ + j, LANES + lane] = sign
    mla[2 * N_FREQ, :MLA_NOPE] = 1.0
    return jnp.asarray(ret, BF16), jnp.asarray(mla, BF16)


def _mla_prep_kernel(u_ref, tab_ref, ex_ref, mu_ref, invu_ref, gu_ref, wbig_ref, sq_ref,
                     invq_ref, gq_ref, gqs_ref, gk_ref, onev_ref, q_ref, k_ref, v_ref):
    x = u_ref[0].astype(F32)
    ss = _dot((x * x).astype(BF16), mu_ref[...]) * invu_ref[...]
    xn = (x * lax.rsqrt(ss + EPS) * gu_ref[...]).astype(BF16)
    big = _dot(xn, wbig_ref[...])
    hw = N_HEADS * LANES
    q, qs, kn, v = big[:, :hw], big[:, hw:2 * hw], big[:, 2 * hw:3 * hw], big[:, 3 * hw:4 * hw]
    kr, krs = big[:, 4 * hw:4 * hw + LANES], big[:, 4 * hw + LANES:]
    cos, sin = _expand_trig(tab_ref[0], ex_ref[...])
    rq = lax.rsqrt(_dot((q * q).astype(BF16), sq_ref[...]) * invq_ref[...] + EPS)
    rk = lax.rsqrt(_dot((kn * kn).astype(BF16), sq_ref[...]) * invq_ref[...] + EPS)
    krot = kr * cos + krs * sin
    for h in range(N_HEADS):
        sl = slice(h * LANES, (h + 1) * LANES)
        qh = (q[:, sl] * gq_ref[...] * cos + qs[:, sl] * gqs_ref[...] * sin) * rq[:, sl]
        q_ref[0, h] = qh.astype(BF16)
        k_ref[0, h] = (kn[:, sl] * gk_ref[...] * rk[:, sl] + krot).astype(BF16)
        v_ref[0, h] = (v[:, sl] + onev_ref[...]).astype(BF16)


def _mla_prep(u, trig, expand, mu, invu, gu, wbig, sq, invq, gq, gqs, gk, onev, tm):
    bsz, s, _ = u.shape
    full = lambda a: pl.BlockSpec(a.shape, lambda b, i: (0,) * a.ndim)
    tab = pl.BlockSpec((1, tm, LANES), lambda b, i: (b, i, 0))
    out = jax.ShapeDtypeStruct((bsz, N_HEADS, s, LANES), BF16)
    ospec = pl.BlockSpec((1, N_HEADS, tm, LANES), lambda b, i: (b, 0, i, 0))
    return pl.pallas_call(
        _mla_prep_kernel,
        grid=(bsz, s // tm),
        in_specs=[pl.BlockSpec((1, tm, U_MLA), lambda b, i: (b, i, (U_COLS - U_MLA) // U_MLA)),
                  tab, full(expand), full(mu), full(invu), full(gu), full(wbig), full(sq), full(invq),
                  full(gq), full(gqs), full(gk), full(onev)],
        out_specs=[ospec, ospec, ospec],
        out_shape=[out, out, out],
        compiler_params=_cparams(("parallel", "parallel")),
        name="mla_prep",
    )(u, trig, expand, mu, invu, gu, wbig, sq, invq, gq, gqs, gk, onev)


def _flash_kernel(q_ref, k_ref, v_ref, o_ref, sa_ref, sb_ref, mca_ref, mcb_ref, m_ref, acc_ref, *, tq):
    qi = pl.program_id(2)
    q = q_ref[0, 0]
    bufs = ((sa_ref, mca_ref), (sb_ref, mcb_ref))
    m_ref[...] = jnp.full((tq, LANES), NEG_BIG, F32)
    acc_ref[...] = jnp.zeros((tq, LANES), F32)

    def scores(c, masked, dst):
        s_ref, mc_ref = dst
        start = pl.multiple_of(c * tq, tq)
        s = _dot_nt(q, k_ref[0, 0, pl.ds(start, tq), :])
        if masked:
            row = qi * tq + lax.broadcasted_iota(jnp.int32, (tq, tq), 0)
            col = start + lax.broadcasted_iota(jnp.int32, (tq, tq), 1)
            s = jnp.where(col <= row, s, NEG_BIG)
        s_ref[...] = s
        mc_ref[...] = jnp.broadcast_to(jnp.max(s, axis=-1, keepdims=True), (tq, LANES))

    def accumulate(c, src):
        s_ref, mc_ref = src
        start = pl.multiple_of(c * tq, tq)
        m_prev = m_ref[...]
        m_new = jnp.maximum(m_prev, mc_ref[...])
        alpha = jnp.exp2(m_prev - m_new)
        p = jnp.exp2(s_ref[...] - jnp.tile(m_new, (1, tq // LANES)))
        pv = _dot(p.astype(BF16), v_ref[0, 0, pl.ds(start, tq), :])
        acc_ref[...] = alpha * acc_ref[...] + pv
        m_ref[...] = m_new

    def by_parity(c, fn):
        for par in range(2):
            pl.when(c % 2 == par)(functools.partial(fn, par))

    def pipelined(c, masked, par):
        scores(c + 1, masked, bufs[1 - par])
        accumulate(c, bufs[par])

    scores(0, True, bufs[0])
    n_plain = jnp.maximum(qi - 1, 0)

    def two_steps(i, carry):
        pipelined(2 * i, False, 0)
        pipelined(2 * i + 1, False, 1)
        return carry

    lax.fori_loop(0, n_plain // 2, two_steps, 0)

    @pl.when(n_plain % 2 == 1)
    def _():
        pipelined(n_plain - 1, False, 0)

    @pl.when(qi >= 1)
    def _():
        by_parity(qi - 1, functools.partial(pipelined, qi - 1, True))

    by_parity(qi, lambda par: accumulate(qi, bufs[par]))

    acc = acc_ref[...]
    lane = lax.broadcasted_iota(jnp.int32, (tq, LANES), 1)
    denom = jnp.sum(jnp.where(lane == HEAD_DIM, acc, 0.0), axis=-1, keepdims=True)
    o_ref[0, 0] = jnp.where(lane < HEAD_DIM, acc / denom, 0.0).astype(BF16)


def _flash_attention(q, k, v, tq):
    bsz, nh, s, _ = q.shape
    kv_spec = pl.BlockSpec((1, 1, s, LANES), lambda b, h, i: (b, h, 0, 0))
    blk = pl.BlockSpec((1, 1, tq, LANES), lambda b, h, i: (b, h, i, 0))
    stat = pltpu.VMEM((tq, LANES), F32)
    return pl.pallas_call(
        functools.partial(_flash_kernel, tq=tq),
        grid=(bsz, nh, s // tq),
        in_specs=[blk, kv_spec, kv_spec],
        out_specs=blk,
        out_shape=jax.ShapeDtypeStruct((bsz, nh, s, LANES), BF16),
        scratch_shapes=[pltpu.VMEM((tq, tq), F32), pltpu.VMEM((tq, tq), F32), stat, stat, stat, stat],
        compiler_params=_cparams(("parallel", "parallel", "arbitrary")),
        name="flash_attention",
    )(q, k, v)


def _outproj_kernel(x_ref, yc_ref, om_ref, yr_ref, yl_ref, wc_ref, wm_ref, wr_ref, wl_ref,
                    gmla_ref, gt_ref, gf_ref, scf_ref, shf_ref, wrt_ref, brt_ref, tri_ref,
                    xo_ref, h_ref, meta_ref, routes_ref, cnt_ref, run_ref, lg_ref):
    n = pl.program_id(0)

    @pl.when(n == 0)
    def _():
        run_ref[...] = jnp.zeros((1, LANES), F32)
        lg_ref[...] = jnp.zeros(lg_ref.shape, F32)

    lg_prev = lg_ref[...]
    routed = (n > 0).astype(F32)

    om = [om_ref[0, h].astype(F32) for h in range(N_HEADS)]
    ssq = om[0] * om[0]
    for h in range(1, N_HEADS):
        ssq = ssq + om[h] * om[h]
    r_mla = lax.rsqrt(jnp.sum(ssq, axis=-1, keepdims=True) / GROUP_W + EPS)
    y = _dot(yc_ref[0], wc_ref[...]) + _dot(yr_ref[0], wr_ref[...]) + _dot(yl_ref[0], wl_ref[...])
    for h in range(0, N_HEADS, 2):
        pair = jnp.concatenate([(om[h] * r_mla * gmla_ref[h]).astype(BF16),
                                (om[h + 1] * r_mla * gmla_ref[h + 1]).astype(BF16)], axis=1)
        y = y + _dot(pair, wm_ref[h // 2])
    x = x_ref[0] + gt_ref[0] * y
    xo_ref[0] = x
    hf = _rms_rows(x, gf_ref[...]) * (1.0 + scf_ref[0]) + shf_ref[0]
    h_ref[0] = _pack_bf16_pairs(hf)

    h_hi = hf.astype(BF16)
    h_lo = (hf - h_hi.astype(F32)).astype(BF16)
    both = _dot(h_hi, wrt_ref[...])
    lg_ref[...] = both[:, :LANES] + both[:, LANES:] + _dot(h_lo, wrt_ref[:, :LANES])

    lg = lg_prev
    tm = lg.shape[0]
    lane = lax.broadcasted_iota(jnp.int32, (tm, LANES), 1)
    bias = brt_ref[...]
    is_g = (lane >= N_EXPERTS) & (lane < N_EXPERTS + MOE_GROUPS)
    is_e = lane < N_EXPERTS

    def first_argmax(val):
        mx = jnp.max(val, axis=-1, keepdims=True)
        return jnp.min(jnp.where(val == mx, lane, LANES), axis=-1, keepdims=True)

    gl = jnp.where(is_g, lg, NEG_BIG)
    ge = jnp.exp(gl - jnp.max(gl, axis=-1, keepdims=True))
    gp = ge / jnp.sum(ge, axis=-1, keepdims=True)
    g_idx = first_argmax(jnp.where(is_g, gp + bias, NEG_BIG))
    g_weight = jnp.sum(jnp.where(lane == g_idx, gp, 0.0), axis=-1, keepdims=True)
    in_group = is_e & ((lane // EXPERTS_PER_GROUP) == (g_idx - N_EXPERTS))
    el = jnp.where(in_group, lg, NEG_BIG)
    ee = jnp.exp(el - jnp.max(el, axis=-1, keepdims=True))
    ep = ee / jnp.sum(ee, axis=-1, keepdims=True)
    score = jnp.where(in_group, ep + bias, NEG_BIG)
    i1 = first_argmax(score)
    sel1 = lane == i1
    i2 = first_argmax(jnp.where(sel1, NEG_BIG, score))
    sel2 = lane == i2
    p1 = jnp.sum(jnp.where(sel1, ep, 0.0), axis=-1, keepdims=True)
    p2 = jnp.sum(jnp.where(sel2, ep, 0.0), axis=-1, keepdims=True)
    psum = p1 + p2
    w1 = p1 / psum * g_weight
    w2 = p2 / psum * g_weight

    onehot = jnp.where(sel1, 1.0, jnp.where(sel2, 1.0, 0.0)).astype(BF16)
    incl = _dot(tri_ref[...], onehot) * routed
    base = run_ref[...] + incl - 1.0
    r1 = jnp.sum(jnp.where(sel1, base, 0.0), axis=-1, keepdims=True)
    r2 = jnp.sum(jnp.where(sel2, base, 0.0), axis=-1, keepdims=True)
    run_ref[...] = run_ref[...] + incl[tm - 1:tm, :]
    cnt_ref[0] = run_ref[...]
    fields = (i1.astype(F32), i2.astype(F32), r1, r2, w1, w2)
    meta = jnp.zeros((tm, LANES), F32)
    for pos, val in enumerate(fields):
        meta = jnp.where(lane == pos, val, meta)
    meta_ref[0] = meta
    routes_ref[...] = meta.T[:SUBLANES, :]


def _outproj(x, yc, om, yr, yl, wc, wm, wr, wl, gmla, gt, gf, scf, shf, wrt, brt, tm):
    bsz, s, d = x.shape
    nt = s // tm
    n_tiles = bsz * nt
    cur = lambda n: jnp.minimum(n, n_tiles - 1)
    lag = lambda n: jnp.maximum(n - 1, 0)
    full = lambda a: pl.BlockSpec(a.shape, lambda n: (0,) * a.ndim)
    tok = lambda w, f=cur: pl.BlockSpec((1, tm, w), lambda n: (f(n) // nt, f(n) % nt, 0))
    vec = pl.BlockSpec((1, 1, d), lambda n: (cur(n) // nt, 0, 0))
    tri = jnp.asarray(np.tril(np.ones((tm, tm), np.float32)), BF16)
    return pl.pallas_call(
        _outproj_kernel,
        grid=(n_tiles + 1,),
        in_specs=[tok(d), tok(GROUP_W),
                  pl.BlockSpec((1, N_HEADS, tm, LANES), lambda n: (cur(n) // nt, 0, cur(n) % nt, 0)),
                  tok(GROUP_W), tok(GROUP_W),
                  full(wc), full(wm), full(wr), full(wl), full(gmla), vec, full(gf), vec, vec,
                  full(wrt), full(brt), full(tri)],
        out_specs=[tok(d), tok(d // 2), tok(LANES, lag),
                   pl.BlockSpec((SUBLANES, tm), lambda n: (0, lag(n))),
                   pl.BlockSpec((1, 1, LANES), lambda n: (lag(n), 0, 0))],
        out_shape=[jax.ShapeDtypeStruct((bsz, s, d), F32), jax.ShapeDtypeStruct((bsz, s, d // 2), jnp.uint32),
                   jax.ShapeDtypeStruct((bsz, s, LANES), F32),
                   jax.ShapeDtypeStruct((SUBLANES, bsz * s), F32),
                   jax.ShapeDtypeStruct((bsz * nt, 1, LANES), F32)],
        scratch_shapes=[pltpu.VMEM((1, LANES), F32), pltpu.VMEM((tm, LANES), F32)],
        compiler_params=_cparams(("arbitrary",)),
        name="outproj_router",
    )(x, yc, om, yr, yl, wc, wm, wr, wl, gmla, gt, gf, scf, shf, wrt, brt, tri)


def _route_plan(routes, cnt, tmg):
    t = routes.shape[1]
    e = routes[0:2].astype(jnp.int32)
    r = routes[2:4].astype(jnp.int32)
    counts = cnt[-1, 0, :N_EXPERTS].astype(jnp.int32)
    padded = (counts + tmg - 1) // tmg * tmg
    ends = jnp.cumsum(padded)
    starts = ends - padded
    expert = jnp.arange(N_EXPERTS, dtype=jnp.int32)[:, None, None]
    dest = jnp.sum(jnp.where(e[None] == expert, starts[:, None, None], 0), axis=0) + r
    n_tiles = (2 * t) // tmg + N_EXPERTS
    tile_start = jnp.arange(n_tiles, dtype=jnp.int32) * tmg
    tile_exp = jnp.sum((ends[None, :] <= tile_start[:, None]).astype(jnp.int32), axis=1)
    tile_exp = jnp.minimum(tile_exp, N_EXPERTS - 1)
    n_used = (ends[-1:] // tmg).astype(jnp.int32)
    last_tile = jnp.maximum(ends - tmg, 0).astype(jnp.int32)
    return dest[0], dest[1], tile_exp, n_used, last_tile, padded.astype(jnp.int32)


def _dispatch_kernel(zs_ref, zv_ref, nu_ref, d1_ref, d2_ref, h_ref, xs_ref, zero_ref, sem, *, tmc, tmg):
    @pl.when(pl.program_id(0) == 0)
    def _():
        zero_ref[...] = jnp.zeros(zero_ref.shape, jnp.uint32)
        n_tiles = xs_ref.shape[0] // tmg

        def fill(start):
            return pltpu.make_async_copy(zero_ref, xs_ref.at[pl.ds(pl.multiple_of(start, tmg), tmg), :], sem)

        def fill_tail(j, carry, wait):
            cp = fill(j * tmg)
            cp.wait() if wait else cp.start()
            return carry

        for wait in (False, True):
            for e in range(N_EXPERTS):
                cp = fill(zs_ref[e])
                pl.when(zv_ref[e] > 0)(cp.wait if wait else cp.start)
            lax.fori_loop(nu_ref[0], n_tiles, functools.partial(fill_tail, wait=wait), 0)

    def push(r, carry):
        src = h_ref.at[pl.ds(r, 1), :]
        pltpu.make_async_copy(src, xs_ref.at[pl.ds(d1_ref[0, 0, r], 1), :], sem).start(priority=0)
        pltpu.make_async_copy(src, xs_ref.at[pl.ds(d2_ref[0, 0, r], 1), :], sem).start(priority=1)
        return carry

    lax.fori_loop(0, tmc, push, 0, unroll=8)
    one_row = pltpu.make_async_copy(h_ref.at[pl.ds(0, 1), :], xs_ref.at[pl.ds(0, 1), :], sem)
    for _ in range(2 * tmc):
        one_row.wait()


def _dispatch(hp, d1, d2, last_tile, padded, n_used, tmc, tmg):
    t, dw = hp.shape
    n_rows = 2 * t + N_EXPERTS * tmg
    smem_rows = pl.BlockSpec((1, 1, tmc), lambda i, zs, zv, nu: (i, 0, 0), memory_space=pltpu.SMEM)
    return pl.pallas_call(
        functools.partial(_dispatch_kernel, tmc=tmc, tmg=tmg),
        grid_spec=pltpu.PrefetchScalarGridSpec(
            num_scalar_prefetch=3, grid=(t // tmc,),
            in_specs=[smem_rows, smem_rows, pl.BlockSpec((tmc, dw), lambda i, zs, zv, nu: (i, 0))],
            out_specs=pl.BlockSpec(memory_space=pl.ANY),
            scratch_shapes=[pltpu.VMEM((tmg, dw), jnp.uint32), pltpu.SemaphoreType.DMA(())]),
        out_shape=jax.ShapeDtypeStruct((n_rows, dw), jnp.uint32),
        compiler_params=_cparams(("arbitrary",)),
        name="moe_dispatch",
    )(last_tile, padded, n_used, d1.reshape(t // tmc, 1, tmc), d2.reshape(t // tmc, 1, tmc), hp)


def _experts_kernel(te_ref, nu_ref, xs_ref, wg_ref, wu_ref, wd_ref, y_ref, wgb_ref, wub_ref, wdb_ref):
    j = pl.program_id(0)
    used = j < nu_ref[0]
    tile = jnp.minimum(j, nu_ref[0] - 1)

    @pl.when((j == 0) | (te_ref[tile] != te_ref[jnp.maximum(tile - 1, 0)]))
    def _():
        wgb_ref[...] = wg_ref[0, 0].astype(BF16)
        wub_ref[...] = wu_ref[0, 0].astype(BF16)
        wdb_ref[...] = wd_ref[0, 0].astype(BF16)

    @pl.when(used)
    def _():
        x = _unpack_bf16_pairs(xs_ref[...]).astype(BF16)
        gate = _dot(x, wgb_ref[...])
        hid = gate * _sigmoid(gate) * _dot(x, wub_ref[...])
        y_ref[...] = _pack_bf16_pairs(_dot(hid.astype(BF16), wdb_ref[...]))

    @pl.when(jnp.logical_not(used))
    def _():
        y_ref[...] = jnp.zeros(y_ref.shape, jnp.uint32)


def _experts(xs, tile_exp, n_used, layer, wg, wu, wd, tmg):
    n_rows, dw = xs.shape
    d = 2 * dw
    tile = lambda i, te, nu: jnp.minimum(i, nu[0] - 1)
    rows = pl.BlockSpec((tmg, dw), lambda i, te, nu: (tile(i, te, nu), 0))
    wspec = lambda shape: pl.BlockSpec((1, 1) + shape,
                                       lambda i, te, nu: (layer, te[tile(i, te, nu)], 0, 0))
    return pl.pallas_call(
        _experts_kernel,
        grid_spec=pltpu.PrefetchScalarGridSpec(
            num_scalar_prefetch=2, grid=(n_rows // tmg,),
            in_specs=[rows, wspec((d, D_EXPERT)), wspec((d, D_EXPERT)), wspec((D_EXPERT, d))],
            out_specs=pl.BlockSpec((tmg, dw), lambda i, te, nu: (i, 0)),
            scratch_shapes=[pltpu.VMEM((d, D_EXPERT), BF16), pltpu.VMEM((d, D_EXPERT), BF16),
                            pltpu.VMEM((D_EXPERT, d), BF16)]),
        out_shape=jax.ShapeDtypeStruct((n_rows, dw), jnp.uint32),
        compiler_params=_cparams(("arbitrary",)),
        name="moe_experts",
    )(tile_exp, n_used, xs, wg, wu, wd)


def _layer_weights(l, w_in, mla_q_norm_g, mla_w_uq, mla_kv_norm_g, mla_w_ukv, mla_q_qk_g,
                   mla_k_qk_g, lru_w_a, lru_w_x, mix_norm_g, w_out, router_group_w,
                   router_group_b, router_expert_w, router_expert_b):
    half = RET_DK // 2
    perm = np.concatenate([np.arange(half) + HEAD_DIM * h for h in range(N_HEADS)]
                          + [np.arange(half) + half + HEAD_DIM * h for h in range(N_HEADS)])
    w = w_in[l]
    o_mla, o_ret, o_lru = U_CONV, U_CONV + 352, U_CONV + 352 + U_RET
    w_ret = w[:, o_ret:o_ret + U_RET]
    w_ret = jnp.concatenate([w_ret[:, perm], w_ret[:, GROUP_W + perm], w_ret[:, 2 * GROUP_W:]], axis=1)
    w_all = jnp.concatenate([w_ret, w[:, o_lru:o_lru + U_LRU], w[:, :U_CONV], w[:, o_mla:o_mla + 352],
                             jnp.zeros((D_MODEL, U_MLA - 352), F32)], axis=1).astype(BF16)

    hw = N_HEADS * LANES
    r16 = MLA_ROPE // 2
    wq = mla_w_uq[l].reshape(Q_LORA, N_HEADS, MLA_QK)
    zq = jnp.zeros((Q_LORA, N_HEADS, LANES - MLA_QK), F32)
    q_cols = jnp.concatenate([wq, zq], axis=2).reshape(Q_LORA, hw)
    wq_sw = jnp.concatenate([jnp.zeros((Q_LORA, N_HEADS, MLA_NOPE), F32), wq[:, :, MLA_NOPE + r16:],
                             wq[:, :, MLA_NOPE:MLA_NOPE + r16], zq], axis=2).reshape(Q_LORA, hw)
    wkv = mla_w_ukv[l].reshape(KV_LORA, N_HEADS, MLA_NOPE + HEAD_DIM)
    zk = jnp.zeros((KV_LORA, N_HEADS, LANES - MLA_NOPE), F32)
    k_cols = jnp.concatenate([wkv[:, :, :MLA_NOPE], zk], axis=2).reshape(KV_LORA, hw)
    v_cols = jnp.concatenate([wkv[:, :, MLA_NOPE:], zk], axis=2).reshape(KV_LORA, hw)
    eye = jnp.eye(MLA_ROPE, dtype=F32)
    place = jnp.concatenate([jnp.zeros((MLA_ROPE, MLA_NOPE), F32), eye,
                             jnp.zeros((MLA_ROPE, LANES - MLA_QK), F32)], axis=1)
    eye_sw = jnp.concatenate([eye[:, r16:], eye[:, :r16]], axis=1)
    place_sw = jnp.concatenate([jnp.zeros((MLA_ROPE, MLA_NOPE), F32), eye_sw,
                                jnp.zeros((MLA_ROPE, LANES - MLA_QK), F32)], axis=1)
    n_big = 4 * hw + 2 * LANES
    wbig = jnp.zeros((U_MLA, n_big), F32)
    wbig = wbig.at[:Q_LORA, :hw].set(q_cols).at[:Q_LORA, hw:2 * hw].set(wq_sw)
    wbig = wbig.at[Q_LORA:Q_LORA + KV_LORA, 2 * hw:3 * hw].set(k_cols)
    wbig = wbig.at[Q_LORA:Q_LORA + KV_LORA, 3 * hw:4 * hw].set(v_cols)
    wbig = wbig.at[Q_LORA + KV_LORA:352, 4 * hw:4 * hw + LANES].set(place)
    wbig = wbig.at[Q_LORA + KV_LORA:352, 4 * hw + LANES:].set(place_sw)
    wbig = wbig.astype(BF16)

    gu = jnp.concatenate([mla_q_norm_g[l], mla_kv_norm_g[l], mla_k_qk_g[l][MLA_NOPE:],
                          jnp.zeros((U_MLA - 352,), F32)])[None, :]
    qscale = (MLA_QK ** -0.5) * math.log2(math.e)
    gq_full = mla_q_qk_g[l]
    pad = jnp.zeros((LANES - MLA_QK,), F32)
    gq = (jnp.concatenate([gq_full, pad]) * qscale)[None, :]
    gqs = (jnp.concatenate([jnp.zeros((MLA_NOPE,), F32), gq_full[MLA_NOPE + r16:],
                            gq_full[MLA_NOPE:MLA_NOPE + r16], pad]) * qscale)[None, :]
    gk = jnp.concatenate([mla_k_qk_g[l][:MLA_NOPE], jnp.zeros((LANES - MLA_NOPE,), F32)])[None, :]

    def blockdiag(wb):
        out = jnp.zeros((GROUP_W, GROUP_W), F32)
        for n in range(wb.shape[0]):
            out = out.at[n * HEAD_DIM:(n + 1) * HEAD_DIM, n * HEAD_DIM:(n + 1) * HEAD_DIM].set(wb[n])
        return out.astype(BF16)

    gmix = mix_norm_g[l]
    wo = w_out[l].astype(BF16)
    wm = wo[GROUP_W:2 * GROUP_W].reshape(N_HEADS, HEAD_DIM, D_MODEL)
    wm = jnp.concatenate([wm, jnp.zeros((N_HEADS, LANES - HEAD_DIM, D_MODEL), BF16)], axis=1)
    wm = wm.reshape(N_HEADS // 2, 2 * LANES, D_MODEL)
    gmla = jnp.concatenate([gmix[GROUP_W:2 * GROUP_W].reshape(N_HEADS, 1, HEAD_DIM),
                            jnp.zeros((N_HEADS, 1, LANES - HEAD_DIM), F32)], axis=2)
    wrt = jnp.concatenate([router_expert_w[l], router_group_w[l],
                           jnp.zeros((D_MODEL, LANES - N_EXPERTS - MOE_GROUPS), F32)], axis=1)
    wrt_hi = wrt.astype(BF16)
    wrt = jnp.concatenate([wrt_hi, (wrt - wrt_hi.astype(F32)).astype(BF16)], axis=1)
    brt = jnp.concatenate([router_expert_b[l], router_group_b[l],
                           jnp.zeros((LANES - N_EXPERTS - MOE_GROUPS,), F32)])[None, :]
    return dict(w_all=w_all, wbig=wbig, gu=gu, gq=gq, gqs=gqs, gk=gk,
                wa=blockdiag(lru_w_a[l]), wx=blockdiag(lru_w_x[l]),
                g_conv=gmix[None, :GROUP_W], g_ret=gmix[None, 2 * GROUP_W:3 * GROUP_W],
                g_lru=gmix[None, 3 * GROUP_W:], gmla=gmla,
                wc=wo[:GROUP_W], wm=wm, wr=wo[2 * GROUP_W:3 * GROUP_W], wl=wo[3 * GROUP_W:],
                wrt=wrt, brt=brt)


def _mla_consts():
    seg_u = np.concatenate([np.zeros(Q_LORA), np.ones(KV_LORA), 2 * np.ones(MLA_ROPE),
                            3 * np.ones(U_MLA - 352)])
    mu = jnp.asarray(seg_u[:, None] == seg_u[None, :], BF16)
    invu = jnp.asarray(np.concatenate([np.full(Q_LORA, 1.0 / Q_LORA), np.full(KV_LORA, 1.0 / KV_LORA),
                                       np.full(MLA_ROPE, 1.0 / MLA_ROPE), np.ones(U_MLA - 352)]), F32)[None, :]
    lane = np.arange(N_HEADS * LANES)
    seg_q = (lane // LANES) * 3 + np.where(lane % LANES < MLA_NOPE, 0, np.where(lane % LANES < MLA_QK, 1, 2))
    sq = jnp.asarray(seg_q[:, None] == seg_q[None, :], BF16)
    inv_head = np.concatenate([np.full(MLA_NOPE, 1.0 / MLA_NOPE), np.full(MLA_ROPE, 1.0 / MLA_ROPE),
                               np.ones(LANES - MLA_QK)])
    invq = jnp.asarray(np.tile(inv_head, N_HEADS), F32)[None, :]
    onev = jnp.asarray((np.arange(LANES) == HEAD_DIM).astype(np.float32))[None, :]
    return mu, invu, sq, invq, onev


def kernel(x, c, positions, ada_w, ada_b, norm_mix_g, w_in, conv_w, mla_q_norm_g, mla_w_uq, mla_kv_norm_g, mla_w_ukv, mla_q_qk_g, mla_k_qk_g, lru_conv_w, lru_conv_b, lru_w_a, lru_b_a, lru_w_x, lru_b_x, lru_lambda, mix_norm_g, w_out, norm_ffn_g, router_group_w, router_group_b, router_expert_w, router_expert_b, exp_w_gate, exp_w_up, exp_w_down):
    bsz, s, d = x.shape
    depth = ada_w.shape[0]
    tm = min(512, s)
    chunk = min(256, s)
    tq = min(1024, s)
    tmc = min(512, s)
    tmd = min(1024, s)
    tmg = 512

    inv = jnp.concatenate([1.0 / (ROPE_BASE ** (jnp.arange(0, MLA_ROPE, 2, dtype=F32) / MLA_ROPE)),
                           1.0 / (ROPE_BASE ** (jnp.arange(0, RET_DK, 2, dtype=F32) / RET_DK))])[:, None]
    trig = _rope_tables(positions, inv)
    ex_ret, ex_mla = _trig_expanders()

    c_pad = jnp.concatenate([c, jnp.zeros((8 - bsz, d), F32)], axis=0)
    mod = _modulation(c_pad, ada_w, ada_b)[:, :bsz]
    ret_consts = _ret_consts(chunk)
    mu, invu, sq, invq, onev = _mla_consts()

    pending = None
    for l in range(depth):
        sh_m, sc_m, gt_m, sh_f, sc_f, gt_f = [m[:, None, :] for m in jnp.split(mod[l], 6, axis=-1)]
        lw = _layer_weights(l, w_in, mla_q_norm_g, mla_w_uq, mla_kv_norm_g, mla_w_ukv, mla_q_qk_g,
                            mla_k_qk_g, lru_w_a, lru_w_x, mix_norm_g, w_out, router_group_w,
                            router_group_b, router_expert_w, router_expert_b)
        if pending is None:
            u = _inproj(x, norm_mix_g[l][None, :], sc_m, sh_m, lw["w_all"], tm)
        else:
            x, u = _combine(x, *pending, tm, proj=(norm_mix_g[l][None, :], sc_m, sh_m, lw["w_all"]))
        y_conv = _conv_mixer(u, conv_w[l], lw["g_conv"], tm)
        y_lru = _lru_mixer(u, lru_conv_w[l], lru_conv_b[l][None, :], lw["wa"], lru_b_a[l][None, :],
                           lw["wx"], lru_b_x[l][None, :], lru_lambda[l][None, :], lw["g_lru"], tm)
        y_ret = _ret_mixer(u, trig, ex_ret, ret_consts, lw["g_ret"], chunk)
        q, k, v = _mla_prep(u, trig, ex_mla, mu, invu, lw["gu"], lw["wbig"], sq, invq,
                            lw["gq"], lw["gqs"], lw["gk"], onev, tm)
        o_mla = _flash_attention(q, k, v, tq)
        x, hp, meta, routes, cnt = _outproj(x, y_conv, o_mla, y_ret, y_lru, lw["wc"], lw["wm"], lw["wr"],
                                            lw["wl"], lw["gmla"], gt_m, norm_ffn_g[l][None, :], sc_f, sh_f,
                                            lw["wrt"], lw["brt"], tm)
        d1, d2, tile_exp, n_used, last_tile, padded = _route_plan(routes, cnt, tmg)
        xs = _dispatch(hp.reshape(bsz * s, d // 2), d1, d2, last_tile, padded, n_used, tmd, tmg)
        ys = _experts(xs, tile_exp, n_used, l, exp_w_gate, exp_w_up, exp_w_down, tmg)
        pending = (meta, gt_f, ys, d1, d2)
    return _combine(x, *pending, tmc)[0]
```

```python
import functools
import math

import jax
import jax.numpy as jnp
import numpy as np
from jax import lax
from jax.experimental import pallas as pl
from jax.experimental.pallas import tpu as pltpu

F32 = jnp.float32
BF16 = jnp.bfloat16
HIGHEST = lax.Precision.HIGHEST

D_MODEL = 1024
GROUP_W = 256
HEAD_DIM = 64
N_HEADS = 4
MLA_NOPE = 64
MLA_ROPE = 32
MLA_QK = 96
Q_LORA = 192
KV_LORA = 128
RET_DK = 64
LRU_C = 8.0
MOE_GROUPS = 4
EXPERTS_PER_GROUP = 8
N_EXPERTS = 32
D_EXPERT = 256
ROPE_BASE = 10000.0
EPS = 1e-6

LANES = 128
SUBLANES = 8
MXU_DIM = 256
U_RET, U_LRU, U_CONV, U_MLA = 1024, 512, 768, 384
U_COLS = U_RET + U_LRU + U_CONV + U_MLA
N_FREQ = MLA_ROPE // 2 + RET_DK // 2
NEG_BIG = -1e30
VMEM_LIMIT = 56 * 1024 * 1024


def _cparams(sem):
    return pltpu.CompilerParams(dimension_semantics=sem, vmem_limit_bytes=VMEM_LIMIT)


def _dot(a, b):
    return jnp.dot(a, b, preferred_element_type=F32)


def _dot_nt(a, b):
    return lax.dot_general(a, b, (((1,), (1,)), ((), ())), preferred_element_type=F32)


def _dot_tn(a, b):
    return lax.dot_general(a, b, (((0,), (0,)), ((), ())), preferred_element_type=F32)


def _rms_rows(y, g):
    return y * lax.rsqrt(jnp.mean(y * y, axis=-1, keepdims=True) + EPS) * g


def _sigmoid(x):
    return 0.5 * jnp.tanh(0.5 * x) + 0.5


def _pack_bf16_pairs(a):
    k = a.shape[1] // 2
    rounded = a.astype(BF16).astype(F32)
    lo = lax.bitcast_convert_type(rounded[:, :k], jnp.uint32) >> 16
    hi = lax.bitcast_convert_type(rounded[:, k:], jnp.uint32) & jnp.uint32(0xFFFF0000)
    return lo | hi


def _unpack_bf16_pairs(w):
    lo = lax.bitcast_convert_type(w << 16, F32)
    hi = lax.bitcast_convert_type(w & jnp.uint32(0xFFFF0000), F32)
    return jnp.concatenate([lo, hi], axis=1)


def _rope_kernel(pos_ref, inv_ref, tab_ref):
    ang = pos_ref[0].astype(F32) * inv_ref[...]
    row = lax.broadcasted_iota(jnp.int32, (LANES - 2 * N_FREQ, ang.shape[1]), 0)
    pad = jnp.where(row == 0, 1.0, 0.0)
    tab_ref[0] = jnp.concatenate([jnp.cos(ang), jnp.sin(ang), pad], axis=0).T


def _rope_tables(positions, inv):
    bsz, s = positions.shape
    ts = min(s, 2048)
    return pl.pallas_call(
        _rope_kernel,
        grid=(bsz, s // ts),
        in_specs=[pl.BlockSpec((1, 1, ts), lambda b, i: (b, 0, i)),
                  pl.BlockSpec((N_FREQ, 1), lambda b, i: (0, 0))],
        out_specs=pl.BlockSpec((1, ts, LANES), lambda b, i: (b, i, 0)),
        out_shape=jax.ShapeDtypeStruct((bsz, s, LANES), F32),
        compiler_params=_cparams(("parallel", "parallel")),
        name="rope_tables",
    )(positions.reshape(bsz, 1, s), inv)


def _expand_trig(tab, expand):
    hi = tab.astype(BF16)
    lo = (tab - hi.astype(F32)).astype(BF16)
    trig = _dot(hi, expand) + _dot(lo, expand)
    return trig[:, :LANES], trig[:, LANES:]


def _mod_kernel(c_ref, w_ref, b_ref, o_ref):
    c = c_ref[...]
    ca = c * _sigmoid(c)
    o_ref[0] = jnp.dot(ca, w_ref[0], precision=HIGHEST, preferred_element_type=F32) + b_ref[0]


def _modulation(c_pad, ada_w, ada_b):
    nl, d, n = ada_w.shape
    tn = 1536
    return pl.pallas_call(
        _mod_kernel,
        grid=(nl, n // tn),
        in_specs=[pl.BlockSpec((8, d), lambda l, j: (0, 0)),
                  pl.BlockSpec((1, d, tn), lambda l, j: (l, 0, j)),
                  pl.BlockSpec((1, 1, tn), lambda l, j: (l, 0, j))],
        out_specs=pl.BlockSpec((1, 8, tn), lambda l, j: (l, 0, j)),
        out_shape=jax.ShapeDtypeStruct((nl, 8, n), F32),
        compiler_params=_cparams(("parallel", "parallel")),
        name="adaln_mod",
    )(c_pad, ada_w, ada_b.reshape(nl, 1, n))


def _norm_project(x, g_ref, sc_ref, sh_ref, w_ref):
    h = _rms_rows(x, g_ref[...]) * (1.0 + sc_ref[0]) + sh_ref[0]
    return _dot(h.astype(BF16), w_ref[...]).astype(BF16)


def _inproj_kernel(x_ref, g_ref, sc_ref, sh_ref, w_ref, u_ref):
    u_ref[0] = _norm_project(x_ref[0], g_ref, sc_ref, sh_ref, w_ref)


def _router_weights(meta):
    lane = lax.broadcasted_iota(jnp.int32, meta.shape, 1)
    w1 = jnp.sum(jnp.where(lane == 4, meta, 0.0), axis=-1, keepdims=True)
    w2 = jnp.sum(jnp.where(lane == 5, meta, 0.0), axis=-1, keepdims=True)
    return w1, w2


def _combine_kernel(d1c_ref, d2c_ref, d1n_ref, d2n_ref, x_ref, meta_ref, gtf_ref, y_hbm, *rest, tm, project):
    if project:
        g_ref, sc_ref, sh_ref, w_ref, xo_ref, u_ref, *scratch = rest
    else:
        xo_ref, *scratch = rest
    a1_ref, a2_ref, b1_ref, b2_ref, sem_a, sem_b = scratch
    n = pl.program_id(0)
    bufs = ((a1_ref, a2_ref, sem_a), (b1_ref, b2_ref, sem_b))

    def pull(d_ref, buf, sem, r):
        return pltpu.make_async_copy(y_hbm.at[pl.ds(d_ref[0, 0, r], 1), :], buf.at[pl.ds(r, 1), :], sem)

    def wait_tile(buf, sem):
        one_row = pltpu.make_async_copy(y_hbm.at[pl.ds(0, 1), :], buf.at[pl.ds(0, 1), :], sem)
        for _ in range(2 * tm):
            one_row.wait()

    @pl.when(n == 0)
    def _():
        def first(r, carry):
            pull(d1c_ref, a1_ref, sem_a, r).start(priority=0)
            pull(d2c_ref, a2_ref, sem_a, r).start(priority=1)
            return carry
        lax.fori_loop(0, tm, first, 0, unroll=8)

    def step(par):
        y1_ref, y2_ref, sem = bufs[par]
        n1_ref, n2_ref, nsem = bufs[1 - par]
        wait_tile(y1_ref, sem)
        for r in range(tm):
            pull(d1n_ref, n1_ref, nsem, r).start(priority=0)
            pull(d2n_ref, n2_ref, nsem, r).start(priority=1)
        w1, w2 = _router_weights(meta_ref[0])
        y = w1 * _unpack_bf16_pairs(y1_ref[...]) + w2 * _unpack_bf16_pairs(y2_ref[...])
        x = x_ref[0] + gtf_ref[0] * y
        xo_ref[0] = x
        if project:
            u_ref[0] = _norm_project(x, g_ref, sc_ref, sh_ref, w_ref)

        @pl.when(n == pl.num_programs(0) - 1)
        def _():
            wait_tile(n1_ref, nsem)

    for par in range(2):
        pl.when(n % 2 == par)(functools.partial(step, par))


def _combine(x, meta, gtf, ys, d1, d2, tm, proj=None):
    bsz, s, d = x.shape
    nt = s // tm
    n_steps = bsz * nt
    nxt = lambda n: jnp.minimum(n + 1, n_steps - 1)
    smem = lambda f: pl.BlockSpec((1, 1, tm), lambda n: (f(n), 0, 0), memory_space=pltpu.SMEM)
    tok = lambda wdt: pl.BlockSpec((1, tm, wdt), lambda n: (n // nt, n % nt, 0))
    vec = pl.BlockSpec((1, 1, d), lambda n: (n // nt, 0, 0))
    buf = pltpu.VMEM((tm, d // 2), jnp.uint32)
    dd1, dd2 = d1.reshape(n_steps, 1, tm), d2.reshape(n_steps, 1, tm)
    in_specs = [smem(lambda n: n), smem(lambda n: n), smem(nxt), smem(nxt),
                tok(d), tok(LANES), vec, pl.BlockSpec(memory_space=pl.ANY)]
    out_specs = [tok(d)]
    out_shape = [jax.ShapeDtypeStruct((bsz, s, d), F32)]
    args = (dd1, dd2, dd1, dd2, x, meta, gtf, ys)
    if proj is not None:
        in_specs += [pl.BlockSpec((1, d), lambda n: (0, 0)), vec, vec, pl.BlockSpec((d, U_COLS), lambda n: (0, 0))]
        out_specs.append(tok(U_COLS))
        out_shape.append(jax.ShapeDtypeStruct((bsz, s, U_COLS), BF16))
        args += tuple(proj)
    return pl.pallas_call(
        functools.partial(_combine_kernel, tm=tm, project=proj is not None),
        grid=(n_steps,),
        in_specs=in_specs,
        out_specs=out_specs,
        out_shape=out_shape,
        scratch_shapes=[buf, buf, buf, buf, pltpu.SemaphoreType.DMA(()), pltpu.SemaphoreType.DMA(())],
        compiler_params=_cparams(("arbitrary",)),
        name="moe_combine_inproj" if proj is not None else "moe_combine",
    )(*args)


def _inproj(x, g, sc, sh, w, tm):
    bsz, s, d = x.shape
    vec = pl.BlockSpec((1, 1, d), lambda b, i: (b, 0, 0))
    return pl.pallas_call(
        _inproj_kernel,
        grid=(bsz, s // tm),
        in_specs=[pl.BlockSpec((1, tm, d), lambda b, i: (b, i, 0)),
                  pl.BlockSpec((1, d), lambda b, i: (0, 0)),
                  vec, vec,
                  pl.BlockSpec((d, U_COLS), lambda b, i: (0, 0))],
        out_specs=pl.BlockSpec((1, tm, U_COLS), lambda b, i: (b, i, 0)),
        out_shape=jax.ShapeDtypeStruct((bsz, s, U_COLS), BF16),
        compiler_params=_cparams(("parallel", "parallel")),
        name="inproj",
    )(x, g, sc, sh, w)


def _conv_kernel(u_ref, w_ref, g_ref, y_ref, buf_ref, *, tm):
    @pl.when(pl.program_id(1) == 0)
    def _():
        buf_ref[0:8, :] = jnp.zeros((8, GROUP_W), F32)

    u = u_ref[0].astype(F32)
    b_gate, c_gate, xin = u[:, :GROUP_W], u[:, GROUP_W:2 * GROUP_W], u[:, 2 * GROUP_W:]
    cx = c_gate * xin
    buf_ref[8:8 + tm, :] = cx
    conv = (w_ref[2:3, :] * cx + w_ref[1:2, :] * buf_ref[7:7 + tm, :]
            + w_ref[0:1, :] * buf_ref[6:6 + tm, :])
    buf_ref[0:8, :] = cx[tm - 8:, :]
    y_ref[0] = _rms_rows(b_gate * conv, g_ref[...]).astype(BF16)


def _conv_mixer(u, w, g, tm):
    bsz, s, _ = u.shape
    return pl.pallas_call(
        functools.partial(_conv_kernel, tm=tm),
        grid=(bsz, s // tm),
        in_specs=[pl.BlockSpec((1, tm, U_CONV), lambda b, i: (b, i, (U_RET + U_LRU) // U_CONV)),
                  pl.BlockSpec((3, GROUP_W), lambda b, i: (0, 0)),
                  pl.BlockSpec((1, GROUP_W), lambda b, i: (0, 0))],
        out_specs=pl.BlockSpec((1, tm, GROUP_W), lambda b, i: (b, i, 0)),
        out_shape=jax.ShapeDtypeStruct((bsz, s, GROUP_W), BF16),
        scratch_shapes=[pltpu.VMEM((tm + 8, GROUP_W), F32)],
        compiler_params=_cparams(("parallel", "arbitrary")),
        name="conv_mixer",
    )(u, w, g)


def _lru_kernel(u_ref, cw_ref, cb_ref, wa_ref, ba_ref, wx_ref, bx_ref, lam_ref, g_ref,
                y_ref, buf_ref, h_ref, *, tm):
    @pl.when(pl.program_id(1) == 0)
    def _():
        buf_ref[0:8, :] = jnp.zeros((8, GROUP_W), F32)
        h_ref[...] = jnp.zeros((1, GROUP_W), F32)

    u = u_ref[0].astype(F32)
    xraw, gate = u[:, :GROUP_W], u[:, GROUP_W:]
    buf_ref[8:8 + tm, :] = xraw
    xb = (cw_ref[3:4, :] * xraw + cw_ref[2:3, :] * buf_ref[7:7 + tm, :]
          + cw_ref[1:2, :] * buf_ref[6:6 + tm, :] + cw_ref[0:1, :] * buf_ref[5:5 + tm, :]
          + cb_ref[...])
    buf_ref[0:8, :] = xraw[tm - 8:, :]

    xbb = xb.astype(BF16)
    r = _sigmoid(_dot(xbb, wa_ref[...]) + ba_ref[...])
    i = _sigmoid(_dot(xbb, wx_ref[...]) + bx_ref[...])
    nlam = -lam_ref[...]
    softplus = jnp.maximum(nlam, 0.0) + jnp.log(1.0 + jnp.exp(-jnp.abs(nlam)))
    log_a = (-LRU_C) * r * softplus
    a = jnp.exp(log_a)
    b = jnp.sqrt(1.0 - a * a) * (i * xb)

    n_groups = tm // SUBLANES
    a = a.reshape(n_groups, SUBLANES, GROUP_W)
    b = b.reshape(n_groups, SUBLANES, GROUP_W)
    sub = lax.broadcasted_iota(jnp.int32, a.shape, 1)
    d = 1
    while d < SUBLANES:
        keep = sub >= d
        a_sh = jnp.where(keep, pltpu.roll(a, d, 1), 1.0)
        b_sh = jnp.where(keep, pltpu.roll(b, d, 1), 0.0)
        b = a * b_sh + b
        a = a * a_sh
        d *= 2
    carry = h_ref[...]
    groups = []
    for g in range(n_groups):
        hg = a[g] * carry + b[g]
        carry = hg[SUBLANES - 1:, :]
        groups.append(hg)
    h = jnp.concatenate(groups, axis=0)
    h_ref[...] = carry

    gelu = 0.5 * gate * (1.0 + jnp.tanh(math.sqrt(2.0 / math.pi) * (gate + 0.044715 * gate * gate * gate)))
    y_ref[0] = _rms_rows(h * gelu, g_ref[...]).astype(BF16)


def _lru_mixer(u, cw, cb, wa, ba, wx, bx, lam, g, tm):
    bsz, s, _ = u.shape
    row = pl.BlockSpec((1, GROUP_W), lambda b, i: (0, 0))
    mat = pl.BlockSpec((GROUP_W, GROUP_W), lambda b, i: (0, 0))
    return pl.pallas_call(
        functools.partial(_lru_kernel, tm=tm),
        grid=(bsz, s // tm),
        in_specs=[pl.BlockSpec((1, tm, U_LRU), lambda b, i: (b, i, U_RET // U_LRU)),
                  pl.BlockSpec((4, GROUP_W), lambda b, i: (0, 0)),
                  row, mat, row, mat, row, row, row],
        out_specs=pl.BlockSpec((1, tm, GROUP_W), lambda b, i: (b, i, 0)),
        out_shape=jax.ShapeDtypeStruct((bsz, s, GROUP_W), BF16),
        scratch_shapes=[pltpu.VMEM((tm + 8, GROUP_W), F32), pltpu.VMEM((1, GROUP_W), F32)],
        compiler_params=_cparams(("parallel", "arbitrary")),
        name="lru_mixer",
    )(u, cw, cb, wa, ba, wx, bx, lam, g)


def _ret_kernel(u_ref, tab_ref, ex_ref, inner_ref, qd_ref, kd_ref, cd_ref, bm_ref, gm_ref,
                mq_ref, mv_ref, g_ref, y_ref, st_ref):
    @pl.when(pl.program_id(0) == 0)
    def _():
        st_ref[...] = jnp.zeros(st_ref.shape, F32)

    for b in range(u_ref.shape[0]):
        u = u_ref[b].astype(F32)
        q, k = u[:, :GROUP_W], u[:, GROUP_W:2 * GROUP_W]
        v, gate = u[:, 2 * GROUP_W:3 * GROUP_W], u[:, 3 * GROUP_W:]
        cos, sin = _expand_trig(tab_ref[b], ex_ref[...])

        def rope(t):
            t1, t2 = t[:, :LANES], t[:, LANES:]
            return jnp.concatenate([t1 * cos - t2 * sin, t2 * cos + t1 * sin], axis=-1)

        qr = rope(q)
        kr = rope(k) * (RET_DK ** -0.5)
        krb = kr.astype(BF16)
        vb = v.astype(BF16)
        state = st_ref[b]
        o = _dot(qr.astype(BF16), state.astype(BF16)) * qd_ref[...]
        for h in range(N_HEADS):
            qh = (qr * mq_ref[h]).astype(BF16)
            sc = _dot_nt(qh, krb) * inner_ref[h]
            o = o + _dot(sc.astype(BF16), vb) * mv_ref[h]
        st_ref[b] = state * cd_ref[...] + bm_ref[...] * _dot_tn((kr * kd_ref[...]).astype(BF16), vb)

        gm = gm_ref[...]
        o_hi = o.astype(BF16)
        o_lo = (o - o_hi.astype(F32)).astype(BF16)
        mu = _dot(o_hi, gm) + _dot(o_lo, gm)
        dlt = o - mu
        var = _dot((dlt * dlt).astype(BF16), gm)
        y = dlt * lax.rsqrt(var + EPS)
        y = gate * _sigmoid(gate) * y
        y_ref[b] = _rms_rows(y, g_ref[...]).astype(BF16)


def _ret_mixer(u, trig, expand, consts, g, chunk):
    bsz, s, _ = u.shape
    inner, qd, kd, cd, bm, gm, mq, mv = consts
    full = lambda shape: pl.BlockSpec(shape, lambda i: (0,) * len(shape))
    tok = lambda w: pl.BlockSpec((bsz, chunk, w), lambda i: (0, i, 0))
    return pl.pallas_call(
        _ret_kernel,
        grid=(s // chunk,),
        in_specs=[tok(U_RET), tok(LANES), full(expand.shape),
                  full(inner.shape), full(qd.shape), full(kd.shape), full(cd.shape),
                  full(bm.shape), full(gm.shape), full(mq.shape), full(mv.shape),
                  full((1, GROUP_W))],
        out_specs=tok(GROUP_W),
        out_shape=jax.ShapeDtypeStruct((bsz, s, GROUP_W), BF16),
        scratch_shapes=[pltpu.VMEM((bsz, GROUP_W, GROUP_W), F32)],
        compiler_params=_cparams(("arbitrary",)),
        name="ret_mixer",
    )(u, trig, expand, inner, qd, kd, cd, bm, gm, mq, mv, g)


def _ret_consts(chunk):
    nh = N_HEADS
    f32 = np.float32
    log_g = np.log(f32(1.0) - f32(2.0) ** (f32(-5.0) - np.arange(nh, dtype=f32)))
    idx = np.arange(chunk, dtype=f32)
    rel = idx[:, None] - idx[None, :]
    inner = np.where(rel >= 0, np.exp(log_g[:, None, None] * np.maximum(rel, 0.0)), 0.0).astype(f32)
    v_head = np.arange(GROUP_W) // HEAD_DIM
    q_head = (np.arange(GROUP_W) % LANES) // (RET_DK // 2)
    qd = np.exp(log_g[v_head][None, :] * (idx[:, None] + 1.0)).astype(f32)
    kd = np.exp(log_g[q_head][None, :] * (chunk - 1.0 - idx[:, None])).astype(f32)
    cd = np.exp(log_g[v_head] * chunk)[None, :].astype(f32)
    bm = (q_head[:, None] == v_head[None, :]).astype(f32)
    gm = jnp.asarray((v_head[:, None] == v_head[None, :]).astype(f32) / HEAD_DIM, BF16)
    mq = (q_head[None, :] == np.arange(nh)[:, None]).astype(f32)[:, None, :]
    mv = (v_head[None, :] == np.arange(nh)[:, None]).astype(f32)[:, None, :]
    return tuple(jnp.asarray(a) for a in (inner, qd, kd, cd, bm)) + (gm, jnp.asarray(mq), jnp.asarray(mv))


def _trig_expanders():
    r16, r32 = MLA_ROPE // 2, RET_DK // 2
    ret = np.zeros((LANES, 2 * LANES), np.float32)
    mla = np.zeros((LANES, 2 * LANES), np.float32)
    for j in range(r32):
        for h in range(N_HEADS):
            ret[r16 + j, h * r32 + j] = 1.0
            ret[N_FREQ + r16 + j, LANES + h * r32 + j] = 1.0
    for j in range(r16):
        for half, sign in ((0, -1.0), (1, 1.0)):
            lane = MLA_NOPE + half * r16 + j
            mla[j, lane] = 1.0
            mla[N_FREQ + j, LANES + lane] = sign
    mla[2 * N_FREQ, :MLA_NOPE] = 1.0
    return jnp.asarray(ret, BF16), jnp.asarray(mla, BF16)


def _mla_prep_kernel(u_ref, tab_ref, ex_ref, mu_ref, invu_ref, gu_ref, wbig_ref, sq_ref,
                     invq_ref, gq_ref, gqs_ref, gk_ref, onev_ref, q_ref, k_ref, v_ref):
    x = u_ref[0].astype(F32)
    ss = _dot((x * x).astype(BF16), mu_ref[...]) * invu_ref[...]
    xn = (x * lax.rsqrt(ss + EPS) * gu_ref[...]).astype(BF16)
    big = _dot(xn, wbig_ref[...])
    hw = N_HEADS * LANES
    q, qs, kn, v = big[:, :hw], big[:, hw:2 * hw], big[:, 2 * hw:3 * hw], big[:, 3 * hw:4 * hw]
    kr, krs = big[:, 4 * hw:4 * hw + LANES], big[:, 4 * hw + LANES:]
    cos, sin = _expand_trig(tab_ref[0], ex_ref[...])
    rq = lax.rsqrt(_dot((q * q).astype(BF16), sq_ref[...]) * invq_ref[...] + EPS)
    rk = lax.rsqrt(_dot((kn * kn).astype(BF16), sq_ref[...]) * invq_ref[...] + EPS)
    krot = kr * cos + krs * sin
    for h in range(N_HEADS):
        sl = slice(h * LANES, (h + 1) * LANES)
        qh = (q[:, sl] * gq_ref[...] * cos + qs[:, sl] * gqs_ref[...] * sin) * rq[:, sl]
        q_ref[0, h] = qh.astype(BF16)
        k_ref[0, h] = (kn[:, sl] * gk_ref[...] * rk[:, sl] + krot).astype(BF16)
        v_ref[0, h] = (v[:, sl] + onev_ref[...]).astype(BF16)


def _mla_prep(u, trig, expand, mu, invu, gu, wbig, sq, invq, gq, gqs, gk, onev, tm):
    bsz, s, _ = u.shape
    full = lambda a: pl.BlockSpec(a.shape, lambda b, i: (0,) * a.ndim)
    tab = pl.BlockSpec((1, tm, LANES), lambda b, i: (b, i, 0))
    out = jax.ShapeDtypeStruct((bsz, N_HEADS, s, LANES), BF16)
    ospec = pl.BlockSpec((1, N_HEADS, tm, LANES), lambda b, i: (b, 0, i, 0))
    return pl.pallas_call(
        _mla_prep_kernel,
        grid=(bsz, s // tm),
        in_specs=[pl.BlockSpec((1, tm, U_MLA), lambda b, i: (b, i, (U_COLS - U_MLA) // U_MLA)),
                  tab, full(expand), full(mu), full(invu), full(gu), full(wbig), full(sq), full(invq),
                  full(gq), full(gqs), full(gk), full(onev)],
        out_specs=[ospec, ospec, ospec],
        out_shape=[out, out, out],
        compiler_params=_cparams(("parallel", "parallel")),
        name="mla_prep",
    )(u, trig, expand, mu, invu, gu, wbig, sq, invq, gq, gqs, gk, onev)


def _flash_kernel(q_ref, k_ref, v_ref, o_ref, sa_ref, sb_ref, mca_ref, mcb_ref, m_ref, acc_ref, *, tq):
    qi = pl.program_id(2)
    q = q_ref[0, 0]
    bufs = ((sa_ref, mca_ref), (sb_ref, mcb_ref))
    m_ref[...] = jnp.full((tq, LANES), NEG_BIG, F32)
    acc_ref[...] = jnp.zeros((tq, LANES), F32)

    def scores(c, masked, dst):
        s_ref, mc_ref = dst
        start = pl.multiple_of(c * tq, tq)
        s = _dot_nt(q, k_ref[0, 0, pl.ds(start, tq), :])
        if masked:
            row = qi * tq + lax.broadcasted_iota(jnp.int32, (tq, tq), 0)
            col = start + lax.broadcasted_iota(jnp.int32, (tq, tq), 1)
            s = jnp.where(col <= row, s, NEG_BIG)
        s_ref[...] = s
        mc_ref[...] = jnp.broadcast_to(jnp.max(s, axis=-1, keepdims=True), (tq, LANES))

    def accumulate(c, src):
        s_ref, mc_ref = src
        start = pl.multiple_of(c * tq, tq)
        m_prev = m_ref[...]
        m_new = jnp.maximum(m_prev, mc_ref[...])
        alpha = jnp.exp2(m_prev - m_new)
        p = jnp.exp2(s_ref[...] - jnp.tile(m_new, (1, tq // LANES)))
        pv = _dot(p.astype(BF16), v_ref[0, 0, pl.ds(start, tq), :])
        acc_ref[...] = alpha * acc_ref[...] + pv
        m_ref[...] = m_new

    def by_parity(c, fn):
        for par in range(2):
            pl.when(c % 2 == par)(functools.partial(fn, par))

    def pipelined(c, masked, par):
        scores(c + 1, masked, bufs[1 - par])
        accumulate(c, bufs[par])

    scores(0, True, bufs[0])
    n_plain = jnp.maximum(qi - 1, 0)

    def two_steps(i, carry):
        pipelined(2 * i, False, 0)
        pipelined(2 * i + 1, False, 1)
        return carry

    lax.fori_loop(0, n_plain // 2, two_steps, 0)

    @pl.when(n_plain % 2 == 1)
    def _():
        pipelined(n_plain - 1, False, 0)

    @pl.when(qi >= 1)
    def _():
        by_parity(qi - 1, functools.partial(pipelined, qi - 1, True))

    by_parity(qi, lambda par: accumulate(qi, bufs[par]))

    acc = acc_ref[...]
    lane = lax.broadcasted_iota(jnp.int32, (tq, LANES), 1)
    denom = jnp.sum(jnp.where(lane == HEAD_DIM, acc, 0.0), axis=-1, keepdims=True)
    o_ref[0, 0] = jnp.where(lane < HEAD_DIM, acc / denom, 0.0).astype(BF16)


def _flash_attention(q, k, v, tq):
    bsz, nh, s, _ = q.shape
    kv_spec = pl.BlockSpec((1, 1, s, LANES), lambda b, h, i: (b, h, 0, 0))
    blk = pl.BlockSpec((1, 1, tq, LANES), lambda b, h, i: (b, h, i, 0))
    stat = pltpu.VMEM((tq, LANES), F32)
    return pl.pallas_call(
        functools.partial(_flash_kernel, tq=tq),
        grid=(bsz, nh, s // tq),
        in_specs=[blk, kv_spec, kv_spec],
        out_specs=blk,
        out_shape=jax.ShapeDtypeStruct((bsz, nh, s, LANES), BF16),
        scratch_shapes=[pltpu.VMEM((tq, tq), F32), pltpu.VMEM((tq, tq), F32), stat, stat, stat, stat],
        compiler_params=_cparams(("parallel", "parallel", "arbitrary")),
        name="flash_attention",
    )(q, k, v)


def _outproj_kernel(x_ref, yc_ref, om_ref, yr_ref, yl_ref, wc_ref, wm_ref, wr_ref, wl_ref,
                    gmla_ref, gt_ref, gf_ref, scf_ref, shf_ref, wrt_ref, brt_ref, tri_ref,
                    xo_ref, h_ref, meta_ref, routes_ref, cnt_ref, run_ref, lg_ref):
    n = pl.program_id(0)

    @pl.when(n == 0)
    def _():
        run_ref[...] = jnp.zeros((1, LANES), F32)
        lg_ref[...] = jnp.zeros(lg_ref.shape, F32)

    lg_prev = lg_ref[...]
    routed = (n > 0).astype(F32)

    om = [om_ref[0, h].astype(F32) for h in range(N_HEADS)]
    ssq = om[0] * om[0]
    for h in range(1, N_HEADS):
        ssq = ssq + om[h] * om[h]
    r_mla = lax.rsqrt(jnp.sum(ssq, axis=-1, keepdims=True) / GROUP_W + EPS)
    y = _dot(yc_ref[0], wc_ref[...]) + _dot(yr_ref[0], wr_ref[...]) + _dot(yl_ref[0], wl_ref[...])
    for h in range(0, N_HEADS, 2):
        pair = jnp.concatenate([(om[h] * r_mla * gmla_ref[h]).astype(BF16),
                                (om[h + 1] * r_mla * gmla_ref[h + 1]).astype(BF16)], axis=1)
        y = y + _dot(pair, wm_ref[h // 2])
    x = x_ref[0] + gt_ref[0] * y
    xo_ref[0] = x
    hf = _rms_rows(x, gf_ref[...]) * (1.0 + scf_ref[0]) + shf_ref[0]
    h_ref[0] = _pack_bf16_pairs(hf)

    h_hi = hf.astype(BF16)
    h_lo = (hf - h_hi.astype(F32)).astype(BF16)
    both = _dot(h_hi, wrt_ref[...])
    lg_ref[...] = both[:, :LANES] + both[:, LANES:] + _dot(h_lo, wrt_ref[:, :LANES])

    lg = lg_prev
    tm = lg.shape[0]
    lane = lax.broadcasted_iota(jnp.int32, (tm, LANES), 1)
    bias = brt_ref[...]
    is_g = (lane >= N_EXPERTS) & (lane < N_EXPERTS + MOE_GROUPS)
    is_e = lane < N_EXPERTS

    def first_argmax(val):
        mx = jnp.max(val, axis=-1, keepdims=True)
        return jnp.min(jnp.where(val == mx, lane, LANES), axis=-1, keepdims=True)

    gl = jnp.where(is_g, lg, NEG_BIG)
    ge = jnp.exp(gl - jnp.max(gl, axis=-1, keepdims=True))
    gp = ge / jnp.sum(ge, axis=-1, keepdims=True)
    g_idx = first_argmax(jnp.where(is_g, gp + bias, NEG_BIG))
    g_weight = jnp.sum(jnp.where(lane == g_idx, gp, 0.0), axis=-1, keepdims=True)
    in_group = is_e & ((lane // EXPERTS_PER_GROUP) == (g_idx - N_EXPERTS))
    el = jnp.where(in_group, lg, NEG_BIG)
    ee = jnp.exp(el - jnp.max(el, axis=-1, keepdims=True))
    ep = ee / jnp.sum(ee, axis=-1, keepdims=True)
    score = jnp.where(in_group, ep + bias, NEG_BIG)
    i1 = first_argmax(score)
    sel1 = lane == i1
    i2 = first_argmax(jnp.where(sel1, NEG_BIG, score))
    sel2 = lane == i2
    p1 = jnp.sum(jnp.where(sel1, ep, 0.0), axis=-1, keepdims=True)
    p2 = jnp.sum(jnp.where(sel2, ep, 0.0), axis=-1, keepdims=True)
    psum = p1 + p2
    w1 = p1 / psum * g_weight
    w2 = p2 / psum * g_weight

    onehot = jnp.where(sel1, 1.0, jnp.where(sel2, 1.0, 0.0)).astype(BF16)
    incl = _dot(tri_ref[...], onehot) * routed
    base = run_ref[...] + incl - 1.0
    r1 = jnp.sum(jnp.where(sel1, base, 0.0), axis=-1, keepdims=True)
    r2 = jnp.sum(jnp.where(sel2, base, 0.0), axis=-1, keepdims=True)
    run_ref[...] = run_ref[...] + incl[tm - 1:tm, :]
    cnt_ref[0] = run_ref[...]
    fields = (i1.astype(F32), i2.astype(F32), r1, r2, w1, w2)
    meta = jnp.zeros((tm, LANES), F32)
    for pos, val in enumerate(fields):
        meta = jnp.where(lane == pos, val, meta)
    meta_ref[0] = meta
    routes_ref[...] = meta.T[:SUBLANES, :]


def _outproj(x, yc, om, yr, yl, wc, wm, wr, wl, gmla, gt, gf, scf, shf, wrt, brt, tm):
    bsz, s, d = x.shape
    nt = s // tm
    n_tiles = bsz * nt
    cur = lambda n: jnp.minimum(n, n_tiles - 1)
    lag = lambda n: jnp.maximum(n - 1, 0)
    full = lambda a: pl.BlockSpec(a.shape, lambda n: (0,) * a.ndim)
    tok = lambda w, f=cur: pl.BlockSpec((1, tm, w), lambda n: (f(n) // nt, f(n) % nt, 0))
    vec = pl.BlockSpec((1, 1, d), lambda n: (cur(n) // nt, 0, 0))
    tri = jnp.asarray(np.tril(np.ones((tm, tm), np.float32)), BF16)
    return pl.pallas_call(
        _outproj_kernel,
        grid=(n_tiles + 1,),
        in_specs=[tok(d), tok(GROUP_W),
                  pl.BlockSpec((1, N_HEADS, tm, LANES), lambda n: (cur(n) // nt, 0, cur(n) % nt, 0)),
                  tok(GROUP_W), tok(GROUP_W),
                  full(wc), full(wm), full(wr), full(wl), full(gmla), vec, full(gf), vec, vec,
                  full(wrt), full(brt), full(tri)],
        out_specs=[tok(d), tok(d // 2), tok(LANES, lag),
                   pl.BlockSpec((SUBLANES, tm), lambda n: (0, lag(n))),
                   pl.BlockSpec((1, 1, LANES), lambda n: (lag(n), 0, 0))],
        out_shape=[jax.ShapeDtypeStruct((bsz, s, d), F32), jax.ShapeDtypeStruct((bsz, s, d // 2), jnp.uint32),
                   jax.ShapeDtypeStruct((bsz, s, LANES), F32),
                   jax.ShapeDtypeStruct((SUBLANES, bsz * s), F32),
                   jax.ShapeDtypeStruct((bsz * nt, 1, LANES), F32)],
        scratch_shapes=[pltpu.VMEM((1, LANES), F32), pltpu.VMEM((tm, LANES), F32)],
        compiler_params=_cparams(("arbitrary",)),
        name="outproj_router",
    )(x, yc, om, yr, yl, wc, wm, wr, wl, gmla, gt, gf, scf, shf, wrt, brt, tri)


def _route_plan(routes, cnt, tmg):
    t = routes.shape[1]
    e = routes[0:2].astype(jnp.int32)
    r = routes[2:4].astype(jnp.int32)
    counts = cnt[-1, 0, :N_EXPERTS].astype(jnp.int32)
    padded = (counts + tmg - 1) // tmg * tmg
    ends = jnp.cumsum(padded)
    starts = ends - padded
    expert = jnp.arange(N_EXPERTS, dtype=jnp.int32)[:, None, None]
    dest = jnp.sum(jnp.where(e[None] == expert, starts[:, None, None], 0), axis=0) + r
    n_tiles = (2 * t) // tmg + N_EXPERTS
    tile_start = jnp.arange(n_tiles, dtype=jnp.int32) * tmg
    tile_exp = jnp.sum((ends[None, :] <= tile_start[:, None]).astype(jnp.int32), axis=1)
    tile_exp = jnp.minimum(tile_exp, N_EXPERTS - 1)
    n_used = (ends[-1:] // tmg).astype(jnp.int32)
    last_tile = jnp.maximum(ends - tmg, 0).astype(jnp.int32)
    return dest[0], dest[1], tile_exp, n_used, last_tile, padded.astype(jnp.int32)


def _dispatch_kernel(zs_ref, zv_ref, nu_ref, d1_ref, d2_ref, h_ref, xs_ref, zero_ref, sem, *, tmc, tmg):
    @pl.when(pl.program_id(0) == 0)
    def _():
        zero_ref[...] = jnp.zeros(zero_ref.shape, jnp.uint32)
        n_tiles = xs_ref.shape[0] // tmg

        def fill(start):
            return pltpu.make_async_copy(zero_ref, xs_ref.at[pl.ds(pl.multiple_of(start, tmg), tmg), :], sem)

        def fill_tail(j, carry, wait):
            cp = fill(j * tmg)
            cp.wait() if wait else cp.start()
            return carry

        for wait in (False, True):
            for e in range(N_EXPERTS):
                cp = fill(zs_ref[e])
                pl.when(zv_ref[e] > 0)(cp.wait if wait else cp.start)
            lax.fori_loop(nu_ref[0], n_tiles, functools.partial(fill_tail, wait=wait), 0)

    def push(r, carry):
        src = h_ref.at[pl.ds(r, 1), :]
        pltpu.make_async_copy(src, xs_ref.at[pl.ds(d1_ref[0, 0, r], 1), :], sem).start(priority=0)
        pltpu.make_async_copy(src, xs_ref.at[pl.ds(d2_ref[0, 0, r], 1), :], sem).start(priority=1)
        return carry

    lax.fori_loop(0, tmc, push, 0, unroll=8)
    one_row = pltpu.make_async_copy(h_ref.at[pl.ds(0, 1), :], xs_ref.at[pl.ds(0, 1), :], sem)
    for _ in range(2 * tmc):
        one_row.wait()


def _dispatch(hp, d1, d2, last_tile, padded, n_used, tmc, tmg):
    t, dw = hp.shape
    n_rows = 2 * t + N_EXPERTS * tmg
    smem_rows = pl.BlockSpec((1, 1, tmc), lambda i, zs, zv, nu: (i, 0, 0), memory_space=pltpu.SMEM)
    return pl.pallas_call(
        functools.partial(_dispatch_kernel, tmc=tmc, tmg=tmg),
        grid_spec=pltpu.PrefetchScalarGridSpec(
            num_scalar_prefetch=3, grid=(t // tmc,),
            in_specs=[smem_rows, smem_rows, pl.BlockSpec((tmc, dw), lambda i, zs, zv, nu: (i, 0))],
            out_specs=pl.BlockSpec(memory_space=pl.ANY),
            scratch_shapes=[pltpu.VMEM((tmg, dw), jnp.uint32), pltpu.SemaphoreType.DMA(())]),
        out_shape=jax.ShapeDtypeStruct((n_rows, dw), jnp.uint32),
        compiler_params=_cparams(("arbitrary",)),
        name="moe_dispatch",
    )(last_tile, padded, n_used, d1.reshape(t // tmc, 1, tmc), d2.reshape(t // tmc, 1, tmc), hp)


def _experts_kernel(te_ref, nu_ref, xs_ref, wg_ref, wu_ref, wd_ref, y_ref, wgb_ref, wub_ref, wdb_ref):
    j = pl.program_id(0)
    used = j < nu_ref[0]
    tile = jnp.minimum(j, nu_ref[0] - 1)

    @pl.when((j == 0) | (te_ref[tile] != te_ref[jnp.maximum(tile - 1, 0)]))
    def _():
        wgb_ref[...] = wg_ref[0, 0].astype(BF16)
        wub_ref[...] = wu_ref[0, 0].astype(BF16)
        wdb_ref[...] = wd_ref[0, 0].astype(BF16)

    @pl.when(used)
    def _():
        x = _unpack_bf16_pairs(xs_ref[...]).astype(BF16)
        gate = _dot(x, wgb_ref[...])
        hid = gate * _sigmoid(gate) * _dot(x, wub_ref[...])
        y_ref[...] = _pack_bf16_pairs(_dot(hid.astype(BF16), wdb_ref[...]))

    @pl.when(jnp.logical_not(used))
    def _():
        y_ref[...] = jnp.zeros(y_ref.shape, jnp.uint32)


def _experts(xs, tile_exp, n_used, layer, wg, wu, wd, tmg):
    n_rows, dw = xs.shape
    d = 2 * dw
    tile = lambda i, te, nu: jnp.minimum(i, nu[0] - 1)
    rows = pl.BlockSpec((tmg, dw), lambda i, te, nu: (tile(i, te, nu), 0))
    wspec = lambda shape: pl.BlockSpec((1, 1) + shape,
                                       lambda i, te, nu: (layer, te[tile(i, te, nu)], 0, 0))
    return pl.pallas_call(
        _experts_kernel,
        grid_spec=pltpu.PrefetchScalarGridSpec(
            num_scalar_prefetch=2, grid=(n_rows // tmg,),
            in_specs=[rows, wspec((d, D_EXPERT)), wspec((d, D_EXPERT)), wspec((D_EXPERT, d))],
            out_specs=pl.BlockSpec((tmg, dw), lambda i, te, nu: (i, 0)),
            scratch_shapes=[pltpu.VMEM((d, D_EXPERT), BF16), pltpu.VMEM((d, D_EXPERT), BF16),
                            pltpu.VMEM((D_EXPERT, d), BF16)]),
        out_shape=jax.ShapeDtypeStruct((n_rows, dw), jnp.uint32),
        compiler_params=_cparams(("arbitrary",)),
        name="moe_experts",
    )(tile_exp, n_used, xs, wg, wu, wd)


def _layer_weights(l, w_in, mla_q_norm_g, mla_w_uq, mla_kv_norm_g, mla_w_ukv, mla_q_qk_g,
                   mla_k_qk_g, lru_w_a, lru_w_x, mix_norm_g, w_out, router_group_w,
                   router_group_b, router_expert_w, router_expert_b):
    half = RET_DK // 2
    perm = np.concatenate([np.arange(half) + HEAD_DIM * h for h in range(N_HEADS)]
                          + [np.arange(half) + half + HEAD_DIM * h for h in range(N_HEADS)])
    w = w_in[l]
    o_mla, o_ret, o_lru = U_CONV, U_CONV + 352, U_CONV + 352 + U_RET
    w_ret = w[:, o_ret:o_ret + U_RET]
    w_ret = jnp.concatenate([w_ret[:, perm], w_ret[:, GROUP_W + perm], w_ret[:, 2 * GROUP_W:]], axis=1)
    w_all = jnp.concatenate([w_ret, w[:, o_lru:o_lru + U_LRU], w[:, :U_CONV], w[:, o_mla:o_mla + 352],
                             jnp.zeros((D_MODEL, U_MLA - 352), F32)], axis=1).astype(BF16)

    hw = N_HEADS * LANES
    r16 = MLA_ROPE // 2
    wq = mla_w_uq[l].reshape(Q_LORA, N_HEADS, MLA_QK)
    zq = jnp.zeros((Q_LORA, N_HEADS, LANES - MLA_QK), F32)
    q_cols = jnp.concatenate([wq, zq], axis=2).reshape(Q_LORA, hw)
    wq_sw = jnp.concatenate([jnp.zeros((Q_LORA, N_HEADS, MLA_NOPE), F32), wq[:, :, MLA_NOPE + r16:],
                             wq[:, :, MLA_NOPE:MLA_NOPE + r16], zq], axis=2).reshape(Q_LORA, hw)
    wkv = mla_w_ukv[l].reshape(KV_LORA, N_HEADS, MLA_NOPE + HEAD_DIM)
    zk = jnp.zeros((KV_LORA, N_HEADS, LANES - MLA_NOPE), F32)
    k_cols = jnp.concatenate([wkv[:, :, :MLA_NOPE], zk], axis=2).reshape(KV_LORA, hw)
    v_cols = jnp.concatenate([wkv[:, :, MLA_NOPE:], zk], axis=2).reshape(KV_LORA, hw)
    eye = jnp.eye(MLA_ROPE, dtype=F32)
    place = jnp.concatenate([jnp.zeros((MLA_ROPE, MLA_NOPE), F32), eye,
                             jnp.zeros((MLA_ROPE, LANES - MLA_QK), F32)], axis=1)
    eye_sw = jnp.concatenate([eye[:, r16:], eye[:, :r16]], axis=1)
    place_sw = jnp.concatenate([jnp.zeros((MLA_ROPE, MLA_NOPE), F32), eye_sw,
                                jnp.zeros((MLA_ROPE, LANES - MLA_QK), F32)], axis=1)
    n_big = 4 * hw + 2 * LANES
    wbig = jnp.zeros((U_MLA, n_big), F32)
    wbig = wbig.at[:Q_LORA, :hw].set(q_cols).at[:Q_LORA, hw:2 * hw].set(wq_sw)
    wbig = wbig.at[Q_LORA:Q_LORA + KV_LORA, 2 * hw:3 * hw].set(k_cols)
    wbig = wbig.at[Q_LORA:Q_LORA + KV_LORA, 3 * hw:4 * hw].set(v_cols)
    wbig = wbig.at[Q_LORA + KV_LORA:352, 4 * hw:4 * hw + LANES].set(place)
    wbig = wbig.at[Q_LORA + KV_LORA:352, 4 * hw + LANES:].set(place_sw)
    wbig = wbig.astype(BF16)

    gu = jnp.concatenate([mla_q_norm_g[l], mla_kv_norm_g[l], mla_k_qk_g[l][MLA_NOPE:],
                          jnp.zeros((U_MLA - 352,), F32)])[None, :]
    qscale = (MLA_QK ** -0.5) * math.log2(math.e)
    gq_full = mla_q_qk_g[l]
    pad = jnp.zeros((LANES - MLA_QK,), F32)
    gq = (jnp.concatenate([gq_full, pad]) * qscale)[None, :]
    gqs = (jnp.concatenate([jnp.zeros((MLA_NOPE,), F32), gq_full[MLA_NOPE + r16:],
                            gq_full[MLA_NOPE:MLA_NOPE + r16], pad]) * qscale)[None, :]
    gk = jnp.concatenate([mla_k_qk_g[l][:MLA_NOPE], jnp.zeros((LANES - MLA_NOPE,), F32)])[None, :]

    def blockdiag(wb):
        out = jnp.zeros((GROUP_W, GROUP_W), F32)
        for n in range(wb.shape[0]):
            out = out.at[n * HEAD_DIM:(n + 1) * HEAD_DIM, n * HEAD_DIM:(n + 1) * HEAD_DIM].set(wb[n])
        return out.astype(BF16)

    gmix = mix_norm_g[l]
    wo = w_out[l].astype(BF16)
    wm = wo[GROUP_W:2 * GROUP_W].reshape(N_HEADS, HEAD_DIM, D_MODEL)
    wm = jnp.concatenate([wm, jnp.zeros((N_HEADS, LANES - HEAD_DIM, D_MODEL), BF16)], axis=1)
    wm = wm.reshape(N_HEADS // 2, 2 * LANES, D_MODEL)
    gmla = jnp.concatenate([gmix[GROUP_W:2 * GROUP_W].reshape(N_HEADS, 1, HEAD_DIM),
                            jnp.zeros((N_HEADS, 1, LANES - HEAD_DIM), F32)], axis=2)
    wrt = jnp.concatenate([router_expert_w[l], router_group_w[l],
                           jnp.zeros((D_MODEL, LANES - N_EXPERTS - MOE_GROUPS), F32)], axis=1)
    wrt_hi = wrt.astype(BF16)
    wrt = jnp.concatenate([wrt_hi, (wrt - wrt_hi.astype(F32)).astype(BF16)], axis=1)
    brt = jnp.concatenate([router_expert_b[l], router_group_b[l],
                           jnp.zeros((LANES - N_EXPERTS - MOE_GROUPS,), F32)])[None, :]
    return dict(w_all=w_all, wbig=wbig, gu=gu, gq=gq, gqs=gqs, gk=gk,
                wa=blockdiag(lru_w_a[l]), wx=blockdiag(lru_w_x[l]),
                g_conv=gmix[None, :GROUP_W], g_ret=gmix[None, 2 * GROUP_W:3 * GROUP_W],
                g_lru=gmix[None, 3 * GROUP_W:], gmla=gmla,
                wc=wo[:GROUP_W], wm=wm, wr=wo[2 * GROUP_W:3 * GROUP_W], wl=wo[3 * GROUP_W:],
                wrt=wrt, brt=brt)


def _mla_consts():
    seg_u = np.concatenate([np.zeros(Q_LORA), np.ones(KV_LORA), 2 * np.ones(MLA_ROPE),
                            3 * np.ones(U_MLA - 352)])
    mu = jnp.asarray(seg_u[:, None] == seg_u[None, :], BF16)
    invu = jnp.asarray(np.concatenate([np.full(Q_LORA, 1.0 / Q_LORA), np.full(KV_LORA, 1.0 / KV_LORA),
                                       np.full(MLA_ROPE, 1.0 / MLA_ROPE), np.ones(U_MLA - 352)]), F32)[None, :]
    lane = np.arange(N_HEADS * LANES)
    seg_q = (lane // LANES) * 3 + np.where(lane % LANES < MLA_NOPE, 0, np.where(lane % LANES < MLA_QK, 1, 2))
    sq = jnp.asarray(seg_q[:, None] == seg_q[None, :], BF16)
    inv_head = np.concatenate([np.full(MLA_NOPE, 1.0 / MLA_NOPE), np.full(MLA_ROPE, 1.0 / MLA_ROPE),
                               np.ones(LANES - MLA_QK)])
    invq = jnp.asarray(np.tile(inv_head, N_HEADS), F32)[None, :]
    onev = jnp.asarray((np.arange(LANES) == HEAD_DIM).astype(np.float32))[None, :]
    return mu, invu, sq, invq, onev


def kernel(x, c, positions, ada_w, ada_b, norm_mix_g, w_in, conv_w, mla_q_norm_g, mla_w_uq, mla_kv_norm_g, mla_w_ukv, mla_q_qk_g, mla_k_qk_g, lru_conv_w, lru_conv_b, lru_w_a, lru_b_a, lru_w_x, lru_b_x, lru_lambda, mix_norm_g, w_out, norm_ffn_g, router_group_w, router_group_b, router_expert_w, router_expert_b, exp_w_gate, exp_w_up, exp_w_down):
    bsz, s, d = x.shape
    depth = ada_w.shape[0]
    tm = min(512, s)
    chunk = min(256, s)
    tq = min(1024, s)
    tmc = min(512, s)
    tmd = min(1024, s)
    tmg = 1024

    inv = jnp.concatenate([1.0 / (ROPE_BASE ** (jnp.arange(0, MLA_ROPE, 2, dtype=F32) / MLA_ROPE)),
                           1.0 / (ROPE_BASE ** (jnp.arange(0, RET_DK, 2, dtype=F32) / RET_DK))])[:, None]
    trig = _rope_tables(positions, inv)
    ex_ret, ex_mla = _trig_expanders()

    c_pad = jnp.concatenate([c, jnp.zeros((8 - bsz, d), F32)], axis=0)
    mod = _modulation(c_pad, ada_w, ada_b)[:, :bsz]
    ret_consts = _ret_consts(chunk)
    mu, invu, sq, invq, onev = _mla_consts()

    pending = None
    for l in range(depth):
        sh_m, sc_m, gt_m, sh_f, sc_f, gt_f = [m[:, None, :] for m in jnp.split(mod[l], 6, axis=-1)]
        lw = _layer_weights(l, w_in, mla_q_norm_g, mla_w_uq, mla_kv_norm_g, mla_w_ukv, mla_q_qk_g,
                            mla_k_qk_g, lru_w_a, lru_w_x, mix_norm_g, w_out, router_group_w,
                            router_group_b, router_expert_w, router_expert_b)
        if pending is None:
            u = _inproj(x, norm_mix_g[l][None, :], sc_m, sh_m, lw["w_all"], tm)
        else:
            x, u = _combine(x, *pending, tm, proj=(norm_mix_g[l][None, :], sc_m, sh_m, lw["w_all"]))
        y_conv = _conv_mixer(u, conv_w[l], lw["g_conv"], tm)
        y_lru = _lru_mixer(u, lru_conv_w[l], lru_conv_b[l][None, :], lw["wa"], lru_b_a[l][None, :],
                           lw["wx"], lru_b_x[l][None, :], lru_lambda[l][None, :], lw["g_lru"], tm)
        y_ret = _ret_mixer(u, trig, ex_ret, ret_consts, lw["g_ret"], chunk)
        q, k, v = _mla_prep(u, trig, ex_mla, mu, invu, lw["gu"], lw["wbig"], sq, invq,
                            lw["gq"], lw["gqs"], lw["gk"], onev, tm)
        o_mla = _flash_attention(q, k, v, tq)
        x, hp, meta, routes, cnt = _outproj(x, y_conv, o_mla, y_ret, y_lru, lw["wc"], lw["wm"], lw["wr"],
                                            lw["wl"], lw["gmla"], gt_m, norm_ffn_g[l][None, :], sc_f, sh_f,
                                            lw["wrt"], lw["brt"], tm)
        d1, d2, tile_exp, n_used, last_tile, padded = _route_plan(routes, cnt, tmg)
        xs = _dispatch(hp.reshape(bsz * s, d // 2), d1, d2, last_tile, padded, n_used, tmd, tmg)
        ys = _experts(xs, tile_exp, n_used, l, exp_w_gate, exp_w_up, exp_w_down, tmg)
        pending = (meta, gt_f, ys, d1, d2)
    return _combine(x, *pending, tmc)[0]
```

```python
import functools
import math

import jax
import jax.numpy as jnp
import numpy as np
from jax import lax
from jax.experimental import pallas as pl
from jax.experimental.pallas import tpu as pltpu

F32 = jnp.float32
BF16 = jnp.bfloat16
HIGHEST = lax.Precision.HIGHEST

D_MODEL = 1024
GROUP_W = 256
HEAD_DIM = 64
N_HEADS = 4
MLA_NOPE = 64
MLA_ROPE = 32
MLA_QK = 96
Q_LORA = 192
KV_LORA = 128
RET_DK = 64
LRU_C = 8.0
MOE_GROUPS = 4
EXPERTS_PER_GROUP = 8
N_EXPERTS = 32
D_EXPERT = 256
ROPE_BASE = 10000.0
EPS = 1e-6

LANES = 128
SUBLANES = 8
MXU_DIM = 256
U_RET, U_LRU, U_CONV, U_MLA = 1024, 512, 768, 384
U_COLS = U_RET + U_LRU + U_CONV + U_MLA
N_FREQ = MLA_ROPE // 2 + RET_DK // 2
NEG_BIG = -1e30
VMEM_LIMIT = 56 * 1024 * 1024


def _cparams(sem):
    return pltpu.CompilerParams(dimension_semantics=sem, vmem_limit_bytes=VMEM_LIMIT)


def _dot(a, b):
    return jnp.dot(a, b, preferred_element_type=F32)


def _dot_nt(a, b):
    return lax.dot_general(a, b, (((1,), (1,)), ((), ())), preferred_element_type=F32)


def _dot_tn(a, b):
    return lax.dot_general(a, b, (((0,), (0,)), ((), ())), preferred_element_type=F32)


def _rms_rows(y, g):
    return y * lax.rsqrt(jnp.mean(y * y, axis=-1, keepdims=True) + EPS) * g


def _sigmoid(x):
    return 0.5 * jnp.tanh(0.5 * x) + 0.5


def _pack_bf16_pairs(a):
    k = a.shape[1] // 2
    rounded = a.astype(BF16).astype(F32)
    lo = lax.bitcast_convert_type(rounded[:, :k], jnp.uint32) >> 16
    hi = lax.bitcast_convert_type(rounded[:, k:], jnp.uint32) & jnp.uint32(0xFFFF0000)
    return lo | hi


def _unpack_bf16_pairs(w):
    lo = lax.bitcast_convert_type(w << 16, F32)
    hi = lax.bitcast_convert_type(w & jnp.uint32(0xFFFF0000), F32)
    return jnp.concatenate([lo, hi], axis=1)


def _rope_kernel(pos_ref, inv_ref, tab_ref):
    ang = pos_ref[0].astype(F32) * inv_ref[...]
    row = lax.broadcasted_iota(jnp.int32, (LANES - 2 * N_FREQ, ang.shape[1]), 0)
    pad = jnp.where(row == 0, 1.0, 0.0)
    tab_ref[0] = jnp.concatenate([jnp.cos(ang), jnp.sin(ang), pad], axis=0).T


def _rope_tables(positions, inv):
    bsz, s = positions.shape
    ts = min(s, 2048)
    return pl.pallas_call(
        _rope_kernel,
        grid=(bsz, s // ts),
        in_specs=[pl.BlockSpec((1, 1, ts), lambda b, i: (b, 0, i)),
                  pl.BlockSpec((N_FREQ, 1), lambda b, i: (0, 0))],
        out_specs=pl.BlockSpec((1, ts, LANES), lambda b, i: (b, i, 0)),
        out_shape=jax.ShapeDtypeStruct((bsz, s, LANES), F32),
        compiler_params=_cparams(("parallel", "parallel")),
        name="rope_tables",
    )(positions.reshape(bsz, 1, s), inv)


def _expand_trig(tab, expand):
    hi = tab.astype(BF16)
    lo = (tab - hi.astype(F32)).astype(BF16)
    trig = _dot(hi, expand) + _dot(lo, expand)
    return trig[:, :LANES], trig[:, LANES:]


def _mod_kernel(c_ref, w_ref, b_ref, o_ref):
    c = c_ref[...]
    ca = c * _sigmoid(c)
    o_ref[0] = jnp.dot(ca, w_ref[0], precision=HIGHEST, preferred_element_type=F32) + b_ref[0]


def _modulation(c_pad, ada_w, ada_b):
    nl, d, n = ada_w.shape
    tn = 1536
    return pl.pallas_call(
        _mod_kernel,
        grid=(nl, n // tn),
        in_specs=[pl.BlockSpec((8, d), lambda l, j: (0, 0)),
                  pl.BlockSpec((1, d, tn), lambda l, j: (l, 0, j)),
                  pl.BlockSpec((1, 1, tn), lambda l, j: (l, 0, j))],
        out_specs=pl.BlockSpec((1, 8, tn), lambda l, j: (l, 0, j)),
        out_shape=jax.ShapeDtypeStruct((nl, 8, n), F32),
        compiler_params=_cparams(("parallel", "parallel")),
        name="adaln_mod",
    )(c_pad, ada_w, ada_b.reshape(nl, 1, n))


def _norm_project(x, g_ref, sc_ref, sh_ref, w_ref):
    h = _rms_rows(x, g_ref[...]) * (1.0 + sc_ref[0]) + sh_ref[0]
    return _dot(h.astype(BF16), w_ref[...]).astype(BF16)


def _inproj_kernel(x_ref, g_ref, sc_ref, sh_ref, w_ref, u_ref):
    u_ref[0] = _norm_project(x_ref[0], g_ref, sc_ref, sh_ref, w_ref)


def _router_weights(meta):
    lane = lax.broadcasted_iota(jnp.int32, meta.shape, 1)
    w1 = jnp.sum(jnp.where(lane == 4, meta, 0.0), axis=-1, keepdims=True)
    w2 = jnp.sum(jnp.where(lane == 5, meta, 0.0), axis=-1, keepdims=True)
    return w1, w2


def _combine_kernel(d1c_ref, d2c_ref, d1n_ref, d2n_ref, x_ref, meta_ref, gtf_ref, y_hbm, *rest, tm, project):
    if project:
        g_ref, sc_ref, sh_ref, w_ref, xo_ref, u_ref, *scratch = rest
    else:
        xo_ref, *scratch = rest
    a1_ref, a2_ref, b1_ref, b2_ref, sem_a, sem_b = scratch
    n = pl.program_id(0)
    bufs = ((a1_ref, a2_ref, sem_a), (b1_ref, b2_ref, sem_b))

    def pull(d_ref, buf, sem, r):
        return pltpu.make_async_copy(y_hbm.at[pl.ds(d_ref[0, 0, r], 1), :], buf.at[pl.ds(r, 1), :], sem)

    def wait_tile(buf, sem):
        one_row = pltpu.make_async_copy(y_hbm.at[pl.ds(0, 1), :], buf.at[pl.ds(0, 1), :], sem)
        for _ in range(2 * tm):
            one_row.wait()

    @pl.when(n == 0)
    def _():
        def first(r, carry):
            pull(d1c_ref, a1_ref, sem_a, r).start(priority=0)
            pull(d2c_ref, a2_ref, sem_a, r).start(priority=1)
            return carry
        lax.fori_loop(0, tm, first, 0, unroll=8)

    def step(par):
        y1_ref, y2_ref, sem = bufs[par]
        n1_ref, n2_ref, nsem = bufs[1 - par]
        wait_tile(y1_ref, sem)
        for r in range(tm):
            pull(d1n_ref, n1_ref, nsem, r).start(priority=0)
            pull(d2n_ref, n2_ref, nsem, r).start(priority=1)
        w1, w2 = _router_weights(meta_ref[0])
        y = w1 * _unpack_bf16_pairs(y1_ref[...]) + w2 * _unpack_bf16_pairs(y2_ref[...])
        x = x_ref[0] + gtf_ref[0] * y
        xo_ref[0] = x
        if project:
            u_ref[0] = _norm_project(x, g_ref, sc_ref, sh_ref, w_ref)

        @pl.when(n == pl.num_programs(0) - 1)
        def _():
            wait_tile(n1_ref, nsem)

    for par in range(2):
        pl.when(n % 2 == par)(functools.partial(step, par))


def _combine(x, meta, gtf, ys, d1, d2, tm, proj=None):
    bsz, s, d = x.shape
    nt = s // tm
    n_steps = bsz * nt
    nxt = lambda n: jnp.minimum(n + 1, n_steps - 1)
    smem = lambda f: pl.BlockSpec((1, 1, tm), lambda n: (f(n), 0, 0), memory_space=pltpu.SMEM)
    tok = lambda wdt: pl.BlockSpec((1, tm, wdt), lambda n: (n // nt, n % nt, 0))
    vec = pl.BlockSpec((1, 1, d), lambda n: (n // nt, 0, 0))
    buf = pltpu.VMEM((tm, d // 2), jnp.uint32)
    dd1, dd2 = d1.reshape(n_steps, 1, tm), d2.reshape(n_steps, 1, tm)
    in_specs = [smem(lambda n: n), smem(lambda n: n), smem(nxt), smem(nxt),
                tok(d), tok(LANES), vec, pl.BlockSpec(memory_space=pl.ANY)]
    out_specs = [tok(d)]
    out_shape = [jax.ShapeDtypeStruct((bsz, s, d), F32)]
    args = (dd1, dd2, dd1, dd2, x, meta, gtf, ys)
    if proj is not None:
        in_specs += [pl.BlockSpec((1, d), lambda n: (0, 0)), vec, vec, pl.BlockSpec((d, U_COLS), lambda n: (0, 0))]
        out_specs.append(tok(U_COLS))
        out_shape.append(jax.ShapeDtypeStruct((bsz, s, U_COLS), BF16))
        args += tuple(proj)
    return pl.pallas_call(
        functools.partial(_combine_kernel, tm=tm, project=proj is not None),
        grid=(n_steps,),
        in_specs=in_specs,
        out_specs=out_specs,
        out_shape=out_shape,
        scratch_shapes=[buf, buf, buf, buf, pltpu.SemaphoreType.DMA(()), pltpu.SemaphoreType.DMA(())],
        compiler_params=_cparams(("arbitrary",)),
        name="moe_combine_inproj" if proj is not None else "moe_combine",
    )(*args)


def _inproj(x, g, sc, sh, w, tm):
    bsz, s, d = x.shape
    vec = pl.BlockSpec((1, 1, d), lambda b, i: (b, 0, 0))
    return pl.pallas_call(
        _inproj_kernel,
        grid=(bsz, s // tm),
        in_specs=[pl.BlockSpec((1, tm, d), lambda b, i: (b, i, 0)),
                  pl.BlockSpec((1, d), lambda b, i: (0, 0)),
                  vec, vec,
                  pl.BlockSpec((d, U_COLS), lambda b, i: (0, 0))],
        out_specs=pl.BlockSpec((1, tm, U_COLS), lambda b, i: (b, i, 0)),
        out_shape=jax.ShapeDtypeStruct((bsz, s, U_COLS), BF16),
        compiler_params=_cparams(("parallel", "parallel")),
        name="inproj",
    )(x, g, sc, sh, w)


def _conv_kernel(u_ref, w_ref, g_ref, y_ref, buf_ref, *, tm):
    @pl.when(pl.program_id(1) == 0)
    def _():
        buf_ref[0:8, :] = jnp.zeros((8, GROUP_W), F32)

    u = u_ref[0].astype(F32)
    b_gate, c_gate, xin = u[:, :GROUP_W], u[:, GROUP_W:2 * GROUP_W], u[:, 2 * GROUP_W:]
    cx = c_gate * xin
    buf_ref[8:8 + tm, :] = cx
    conv = (w_ref[2:3, :] * cx + w_ref[1:2, :] * buf_ref[7:7 + tm, :]
            + w_ref[0:1, :] * buf_ref[6:6 + tm, :])
    buf_ref[0:8, :] = cx[tm - 8:, :]
    y_ref[0] = _rms_rows(b_gate * conv, g_ref[...]).astype(BF16)


def _conv_mixer(u, w, g, tm):
    bsz, s, _ = u.shape
    return pl.pallas_call(
        functools.partial(_conv_kernel, tm=tm),
        grid=(bsz, s // tm),
        in_specs=[pl.BlockSpec((1, tm, U_CONV), lambda b, i: (b, i, (U_RET + U_LRU) // U_CONV)),
                  pl.BlockSpec((3, GROUP_W), lambda b, i: (0, 0)),
                  pl.BlockSpec((1, GROUP_W), lambda b, i: (0, 0))],
        out_specs=pl.BlockSpec((1, tm, GROUP_W), lambda b, i: (b, i, 0)),
        out_shape=jax.ShapeDtypeStruct((bsz, s, GROUP_W), BF16),
        scratch_shapes=[pltpu.VMEM((tm + 8, GROUP_W), F32)],
        compiler_params=_cparams(("parallel", "arbitrary")),
        name="conv_mixer",
    )(u, w, g)


def _lru_kernel(u_ref, cw_ref, cb_ref, wa_ref, ba_ref, wx_ref, bx_ref, lam_ref, g_ref,
                y_ref, buf_ref, h_ref, *, tm):
    @pl.when(pl.program_id(1) == 0)
    def _():
        buf_ref[0:8, :] = jnp.zeros((8, GROUP_W), F32)
        h_ref[...] = jnp.zeros((1, GROUP_W), F32)

    u = u_ref[0].astype(F32)
    xraw, gate = u[:, :GROUP_W], u[:, GROUP_W:]
    buf_ref[8:8 + tm, :] = xraw
    xb = (cw_ref[3:4, :] * xraw + cw_ref[2:3, :] * buf_ref[7:7 + tm, :]
          + cw_ref[1:2, :] * buf_ref[6:6 + tm, :] + cw_ref[0:1, :] * buf_ref[5:5 + tm, :]
          + cb_ref[...])
    buf_ref[0:8, :] = xraw[tm - 8:, :]

    xbb = xb.astype(BF16)
    r = _sigmoid(_dot(xbb, wa_ref[...]) + ba_ref[...])
    i = _sigmoid(_dot(xbb, wx_ref[...]) + bx_ref[...])
    nlam = -lam_ref[...]
    softplus = jnp.maximum(nlam, 0.0) + jnp.log(1.0 + jnp.exp(-jnp.abs(nlam)))
    log_a = (-LRU_C) * r * softplus
    a = jnp.exp(log_a)
    b = jnp.sqrt(1.0 - a * a) * (i * xb)

    n_groups = tm // SUBLANES
    a = a.reshape(n_groups, SUBLANES, GROUP_W)
    b = b.reshape(n_groups, SUBLANES, GROUP_W)
    sub = lax.broadcasted_iota(jnp.int32, a.shape, 1)
    d = 1
    while d < SUBLANES:
        keep = sub >= d
        a_sh = jnp.where(keep, pltpu.roll(a, d, 1), 1.0)
        b_sh = jnp.where(keep, pltpu.roll(b, d, 1), 0.0)
        b = a * b_sh + b
        a = a * a_sh
        d *= 2
    carry = h_ref[...]
    groups = []
    for g in range(n_groups):
        hg = a[g] * carry + b[g]
        carry = hg[SUBLANES - 1:, :]
        groups.append(hg)
    h = jnp.concatenate(groups, axis=0)
    h_ref[...] = carry

    gelu = 0.5 * gate * (1.0 + jnp.tanh(math.sqrt(2.0 / math.pi) * (gate + 0.044715 * gate * gate * gate)))
    y_ref[0] = _rms_rows(h * gelu, g_ref[...]).astype(BF16)


def _lru_mixer(u, cw, cb, wa, ba, wx, bx, lam, g, tm):
    bsz, s, _ = u.shape
    row = pl.BlockSpec((1, GROUP_W), lambda b, i: (0, 0))
    mat = pl.BlockSpec((GROUP_W, GROUP_W), lambda b, i: (0, 0))
    return pl.pallas_call(
        functools.partial(_lru_kernel, tm=tm),
        grid=(bsz, s // tm),
        in_specs=[pl.BlockSpec((1, tm, U_LRU), lambda b, i: (b, i, U_RET // U_LRU)),
                  pl.BlockSpec((4, GROUP_W), lambda b, i: (0, 0)),
                  row, mat, row, mat, row, row, row],
        out_specs=pl.BlockSpec((1, tm, GROUP_W), lambda b, i: (b, i, 0)),
        out_shape=jax.ShapeDtypeStruct((bsz, s, GROUP_W), BF16),
        scratch_shapes=[pltpu.VMEM((tm + 8, GROUP_W), F32), pltpu.VMEM((1, GROUP_W), F32)],
        compiler_params=_cparams(("parallel", "arbitrary")),
        name="lru_mixer",
    )(u, cw, cb, wa, ba, wx, bx, lam, g)


def _ret_kernel(u_ref, tab_ref, ex_ref, inner_ref, qd_ref, kd_ref, cd_ref, bm_ref, gm_ref,
                mq_ref, mv_ref, g_ref, y_ref, st_ref):
    @pl.when(pl.program_id(0) == 0)
    def _():
        st_ref[...] = jnp.zeros(st_ref.shape, F32)

    for b in range(u_ref.shape[0]):
        u = u_ref[b].astype(F32)
        q, k = u[:, :GROUP_W], u[:, GROUP_W:2 * GROUP_W]
        v, gate = u[:, 2 * GROUP_W:3 * GROUP_W], u[:, 3 * GROUP_W:]
        cos, sin = _expand_trig(tab_ref[b], ex_ref[...])

        def rope(t):
            t1, t2 = t[:, :LANES], t[:, LANES:]
            return jnp.concatenate([t1 * cos - t2 * sin, t2 * cos + t1 * sin], axis=-1)

        qr = rope(q)
        kr = rope(k) * (RET_DK ** -0.5)
        krb = kr.astype(BF16)
        vb = v.astype(BF16)
        state = st_ref[b]
        o = _dot(qr.astype(BF16), state.astype(BF16)) * qd_ref[...]
        for h in range(N_HEADS):
            qh = (qr * mq_ref[h]).astype(BF16)
            sc = _dot_nt(qh, krb) * inner_ref[h]
            o = o + _dot(sc.astype(BF16), vb) * mv_ref[h]
        st_ref[b] = state * cd_ref[...] + bm_ref[...] * _dot_tn((kr * kd_ref[...]).astype(BF16), vb)

        gm = gm_ref[...]
        o_hi = o.astype(BF16)
        o_lo = (o - o_hi.astype(F32)).astype(BF16)
        mu = _dot(o_hi, gm) + _dot(o_lo, gm)
        dlt = o - mu
        var = _dot((dlt * dlt).astype(BF16), gm)
        y = dlt * lax.rsqrt(var + EPS)
        y = gate * _sigmoid(gate) * y
        y_ref[b] = _rms_rows(y, g_ref[...]).astype(BF16)


def _ret_mixer(u, trig, expand, consts, g, chunk):
    bsz, s, _ = u.shape
    inner, qd, kd, cd, bm, gm, mq, mv = consts
    full = lambda shape: pl.BlockSpec(shape, lambda i: (0,) * len(shape))
    tok = lambda w: pl.BlockSpec((bsz, chunk, w), lambda i: (0, i, 0))
    return pl.pallas_call(
        _ret_kernel,
        grid=(s // chunk,),
        in_specs=[tok(U_RET), tok(LANES), full(expand.shape),
                  full(inner.shape), full(qd.shape), full(kd.shape), full(cd.shape),
                  full(bm.shape), full(gm.shape), full(mq.shape), full(mv.shape),
                  full((1, GROUP_W))],
        out_specs=tok(GROUP_W),
        out_shape=jax.ShapeDtypeStruct((bsz, s, GROUP_W), BF16),
        scratch_shapes=[pltpu.VMEM((bsz, GROUP_W, GROUP_W), F32)],
        compiler_params=_cparams(("arbitrary",)),
        name="ret_mixer",
    )(u, trig, expand, inner, qd, kd, cd, bm, gm, mq, mv, g)


def _ret_consts(chunk):
    nh = N_HEADS
    f32 = np.float32
    log_g = np.log(f32(1.0) - f32(2.0) ** (f32(-5.0) - np.arange(nh, dtype=f32)))
    idx = np.arange(chunk, dtype=f32)
    rel = idx[:, None] - idx[None, :]
    inner = np.where(rel >= 0, np.exp(log_g[:, None, None] * np.maximum(rel, 0.0)), 0.0).astype(f32)
    v_head = np.arange(GROUP_W) // HEAD_DIM
    q_head = (np.arange(GROUP_W) % LANES) // (RET_DK // 2)
    qd = np.exp(log_g[v_head][None, :] * (idx[:, None] + 1.0)).astype(f32)
    kd = np.exp(log_g[q_head][None, :] * (chunk - 1.0 - idx[:, None])).astype(f32)
    cd = np.exp(log_g[v_head] * chunk)[None, :].astype(f32)
    bm = (q_head[:, None] == v_head[None, :]).astype(f32)
    gm = jnp.asarray((v_head[:, None] == v_head[None, :]).astype(f32) / HEAD_DIM, BF16)
    mq = (q_head[None, :] == np.arange(nh)[:, None]).astype(f32)[:, None, :]
    mv = (v_head[None, :] == np.arange(nh)[:, None]).astype(f32)[:, None, :]
    return tuple(jnp.asarray(a) for a in (inner, qd, kd, cd, bm)) + (gm, jnp.asarray(mq), jnp.asarray(mv))


def _trig_expanders():
    r16, r32 = MLA_ROPE // 2, RET_DK // 2
    ret = np.zeros((LANES, 2 * LANES), np.float32)
    mla = np.zeros((LANES, 2 * LANES), np.float32)
    for j in range(r32):
        for h in range(N_HEADS):
            ret[r16 + j, h * r32 + j] = 1.0
            ret[N_FREQ + r16 + j, LANES + h * r32 + j] = 1.0
    for j in range(r16):
        for half, sign in ((0, -1.0), (1, 1.0)):
            lane = MLA_NOPE + half * r16 + j
            mla[j, lane] = 1.0
            mla[N_FREQ + j, LANES + lane] = sign
    mla[2 * N_FREQ, :MLA_NOPE] = 1.0
    return jnp.asarray(ret, BF16), jnp.asarray(mla, BF16)


def _mla_prep_kernel(u_ref, tab_ref, ex_ref, mu_ref, invu_ref, gu_ref, wbig_ref, sq_ref,
                     invq_ref, gq_ref, gqs_ref, gk_ref, onev_ref, q_ref, k_ref, v_ref):
    x = u_ref[0].astype(F32)
    ss = _dot((x * x).astype(BF16), mu_ref[...]) * invu_ref[...]
    xn = (x * lax.rsqrt(ss + EPS) * gu_ref[...]).astype(BF16)
    big = _dot(xn, wbig_ref[...])
    hw = N_HEADS * LANES
    q, qs, kn, v = big[:, :hw], big[:, hw:2 * hw], big[:, 2 * hw:3 * hw], big[:, 3 * hw:4 * hw]
    kr, krs = big[:, 4 * hw:4 * hw + LANES], big[:, 4 * hw + LANES:]
    cos, sin = _expand_trig(tab_ref[0], ex_ref[...])
    rq = lax.rsqrt(_dot((q * q).astype(BF16), sq_ref[...]) * invq_ref[...] + EPS)
    rk = lax.rsqrt(_dot((kn * kn).astype(BF16), sq_ref[...]) * invq_ref[...] + EPS)
    krot = kr * cos + krs * sin
    for h in range(N_HEADS):
        sl = slice(h * LANES, (h + 1) * LANES)
        qh = (q[:, sl] * gq_ref[...] * cos + qs[:, sl] * gqs_ref[...] * sin) * rq[:, sl]
        q_ref[0, h] = qh.astype(BF16)
        k_ref[0, h] = (kn[:, sl] * gk_ref[...] * rk[:, sl] + krot).astype(BF16)
        v_ref[0, h] = (v[:, sl] + onev_ref[...]).astype(BF16)


def _mla_prep(u, trig, expand, mu, invu, gu, wbig, sq, invq, gq, gqs, gk, onev, tm):
    bsz, s, _ = u.shape
    full = lambda a: pl.BlockSpec(a.shape, lambda b, i: (0,) * a.ndim)
    tab = pl.BlockSpec((1, tm, LANES), lambda b, i: (b, i, 0))
    out = jax.ShapeDtypeStruct((bsz, N_HEADS, s, LANES), BF16)
    ospec = pl.BlockSpec((1, N_HEADS, tm, LANES), lambda b, i: (b, 0, i, 0))
    return pl.pallas_call(
        _mla_prep_kernel,
        grid=(bsz, s // tm),
        in_specs=[pl.BlockSpec((1, tm, U_MLA), lambda b, i: (b, i, (U_COLS - U_MLA) // U_MLA)),
                  tab, full(expand), full(mu), full(invu), full(gu), full(wbig), full(sq), full(invq),
                  full(gq), full(gqs), full(gk), full(onev)],
        out_specs=[ospec, ospec, ospec],
        out_shape=[out, out, out],
        compiler_params=_cparams(("parallel", "parallel")),
        name="mla_prep",
    )(u, trig, expand, mu, invu, gu, wbig, sq, invq, gq, gqs, gk, onev)


def _flash_kernel(q_ref, k_ref, v_ref, o_ref, sa_ref, sb_ref, mca_ref, mcb_ref, m_ref, acc_ref, *, tq):
    qi = pl.program_id(2)
    q = q_ref[0, 0]
    bufs = ((sa_ref, mca_ref), (sb_ref, mcb_ref))
    m_ref[...] = jnp.full((tq, LANES), NEG_BIG, F32)
    acc_ref[...] = jnp.zeros((tq, LANES), F32)

    def scores(c, masked, dst):
        s_ref, mc_ref = dst
        start = pl.multiple_of(c * tq, tq)
        s = _dot_nt(q, k_ref[0, 0, pl.ds(start, tq), :])
        if masked:
            row = qi * tq + lax.broadcasted_iota(jnp.int32, (tq, tq), 0)
            col = start + lax.broadcasted_iota(jnp.int32, (tq, tq), 1)
            s = jnp.where(col <= row, s, NEG_BIG)
        s_ref[...] = s
        mc_ref[...] = jnp.broadcast_to(jnp.max(s, axis=-1, keepdims=True), (tq, LANES))

    def accumulate(c, src):
        s_ref, mc_ref = src
        start = pl.multiple_of(c * tq, tq)
        m_prev = m_ref[...]
        m_new = jnp.maximum(m_prev, mc_ref[...])
        alpha = jnp.exp2(m_prev - m_new)
        p = jnp.exp2(s_ref[...] - jnp.tile(m_new, (1, tq // LANES)))
        pv = _dot(p.astype(BF16), v_ref[0, 0, pl.ds(start, tq), :])
        acc_ref[...] = alpha * acc_ref[...] + pv
        m_ref[...] = m_new

    def by_parity(c, fn):
        for par in range(2):
            pl.when(c % 2 == par)(functools.partial(fn, par))

    def pipelined(c, masked, par):
        scores(c + 1, masked, bufs[1 - par])
        accumulate(c, bufs[par])

    scores(0, True, bufs[0])
    n_plain = jnp.maximum(qi - 1, 0)

    def two_steps(i, carry):
        pipelined(2 * i, False, 0)
        pipelined(2 * i + 1, False, 1)
        return carry

    lax.fori_loop(0, n_plain // 2, two_steps, 0)

    @pl.when(n_plain % 2 == 1)
    def _():
        pipelined(n_plain - 1, False, 0)

    @pl.when(qi >= 1)
    def _():
        by_parity(qi - 1, functools.partial(pipelined, qi - 1, True))

    by_parity(qi, lambda par: accumulate(qi, bufs[par]))

    acc = acc_ref[...]
    lane = lax.broadcasted_iota(jnp.int32, (tq, LANES), 1)
    denom = jnp.sum(jnp.where(lane == HEAD_DIM, acc, 0.0), axis=-1, keepdims=True)
    o_ref[0, 0] = jnp.where(lane < HEAD_DIM, acc / denom, 0.0).astype(BF16)


def _flash_attention(q, k, v, tq):
    bsz, nh, s, _ = q.shape
    kv_spec = pl.BlockSpec((1, 1, s, LANES), lambda b, h, i: (b, h, 0, 0))
    blk = pl.BlockSpec((1, 1, tq, LANES), lambda b, h, i: (b, h, i, 0))
    stat = pltpu.VMEM((tq, LANES), F32)
    return pl.pallas_call(
        functools.partial(_flash_kernel, tq=tq),
        grid=(bsz, nh, s // tq),
        in_specs=[blk, kv_spec, kv_spec],
        out_specs=blk,
        out_shape=jax.ShapeDtypeStruct((bsz, nh, s, LANES), BF16),
        scratch_shapes=[pltpu.VMEM((tq, tq), F32), pltpu.VMEM((tq, tq), F32), stat, stat, stat, stat],
        compiler_params=_cparams(("parallel", "parallel", "arbitrary")),
        name="flash_attention",
    )(q, k, v)


def _outproj_kernel(x_ref, yc_ref, om_ref, yr_ref, yl_ref, wc_ref, wm_ref, wr_ref, wl_ref,
                    gmla_ref, gt_ref, gf_ref, scf_ref, shf_ref, wrt_ref, brt_ref, tri_ref,
                    xo_ref, h_ref, meta_ref, routes_ref, cnt_ref, run_ref, lg_ref):
    n = pl.program_id(0)

    @pl.when(n == 0)
    def _():
        run_ref[...] = jnp.zeros((1, LANES), F32)
        lg_ref[...] = jnp.zeros(lg_ref.shape, F32)

    lg_prev = lg_ref[...]
    routed = (n > 0).astype(F32)

    om = [om_ref[0, h].astype(F32) for h in range(N_HEADS)]
    ssq = om[0] * om[0]
    for h in range(1, N_HEADS):
        ssq = ssq + om[h] * om[h]
    r_mla = lax.rsqrt(jnp.sum(ssq, axis=-1, keepdims=True) / GROUP_W + EPS)
    y = _dot(yc_ref[0], wc_ref[...]) + _dot(yr_ref[0], wr_ref[...]) + _dot(yl_ref[0], wl_ref[...])
    for h in range(0, N_HEADS, 2):
        pair = jnp.concatenate([(om[h] * r_mla * gmla_ref[h]).astype(BF16),
                                (om[h + 1] * r_mla * gmla_ref[h + 1]).astype(BF16)], axis=1)
        y = y + _dot(pair, wm_ref[h // 2])
    x = x_ref[0] + gt_ref[0] * y
    xo_ref[0] = x
    hf = _rms_rows(x, gf_ref[...]) * (1.0 + scf_ref[0]) + shf_ref[0]
    h_ref[0] = _pack_bf16_pairs(hf)

    h_hi = hf.astype(BF16)
    h_lo = (hf - h_hi.astype(F32)).astype(BF16)
    both = _dot(h_hi, wrt_ref[...])
    lg_ref[...] = both[:, :LANES] + both[:, LANES:] + _dot(h_lo, wrt_ref[:, :LANES])

    lg = lg_prev
    tm = lg.shape[0]
    lane = lax.broadcasted_iota(jnp.int32, (tm, LANES), 1)
    bias = brt_ref[...]
    is_g = (lane >= N_EXPERTS) & (lane < N_EXPERTS + MOE_GROUPS)
    is_e = lane < N_EXPERTS

    def first_argmax(val):
        mx = jnp.max(val, axis=-1, keepdims=True)
        return jnp.min(jnp.where(val == mx, lane, LANES), axis=-1, keepdims=True)

    gl = jnp.where(is_g, lg, NEG_BIG)
    ge = jnp.exp(gl - jnp.max(gl, axis=-1, keepdims=True))
    gp = ge / jnp.sum(ge, axis=-1, keepdims=True)
    g_idx = first_argmax(jnp.where(is_g, gp + bias, NEG_BIG))
    g_weight = jnp.sum(jnp.where(lane == g_idx, gp, 0.0), axis=-1, keepdims=True)
    in_group = is_e & ((lane // EXPERTS_PER_GROUP) == (g_idx - N_EXPERTS))
    el = jnp.where(in_group, lg, NEG_BIG)
    ee = jnp.exp(el - jnp.max(el, axis=-1, keepdims=True))
    ep = ee / jnp.sum(ee, axis=-1, keepdims=True)
    score = jnp.where(in_group, ep + bias, NEG_BIG)
    i1 = first_argmax(score)
    sel1 = lane == i1
    i2 = first_argmax(jnp.where(sel1, NEG_BIG, score))
    sel2 = lane == i2
    p1 = jnp.sum(jnp.where(sel1, ep, 0.0), axis=-1, keepdims=True)
    p2 = jnp.sum(jnp.where(sel2, ep, 0.0), axis=-1, keepdims=True)
    psum = p1 + p2
    w1 = p1 / psum * g_weight
    w2 = p2 / psum * g_weight

    onehot = jnp.where(sel1, 1.0, jnp.where(sel2, 1.0, 0.0)).astype(BF16)
    incl = _dot(tri_ref[...], onehot) * routed
    base = run_ref[...] + incl - 1.0
    r1 = jnp.sum(jnp.where(sel1, base, 0.0), axis=-1, keepdims=True)
    r2 = jnp.sum(jnp.where(sel2, base, 0.0), axis=-1, keepdims=True)
    run_ref[...] = run_ref[...] + incl[tm - 1:tm, :]
    cnt_ref[0] = run_ref[...]
    fields = (i1.astype(F32), i2.astype(F32), r1, r2, w1, w2)
    meta = jnp.zeros((tm, LANES), F32)
    for pos, val in enumerate(fields):
        meta = jnp.where(lane == pos, val, meta)
    meta_ref[0] = meta
    routes_ref[...] = meta.T[:SUBLANES, :]


def _outproj(x, yc, om, yr, yl, wc, wm, wr, wl, gmla, gt, gf, scf, shf, wrt, brt, tm):
    bsz, s, d = x.shape
    nt = s // tm
    n_tiles = bsz * nt
    cur = lambda n: jnp.minimum(n, n_tiles - 1)
    lag = lambda n: jnp.maximum(n - 1, 0)
    full = lambda a: pl.BlockSpec(a.shape, lambda n: (0,) * a.ndim)
    tok = lambda w, f=cur: pl.BlockSpec((1, tm, w), lambda n: (f(n) // nt, f(n) % nt, 0))
    vec = pl.BlockSpec((1, 1, d), lambda n: (cur(n) // nt, 0, 0))
    tri = jnp.asarray(np.tril(np.ones((tm, tm), np.float32)), BF16)
    return pl.pallas_call(
        _outproj_kernel,
        grid=(n_tiles + 1,),
        in_specs=[tok(d), tok(GROUP_W),
                  pl.BlockSpec((1, N_HEADS, tm, LANES), lambda n: (cur(n) // nt, 0, cur(n) % nt, 0)),
                  tok(GROUP_W), tok(GROUP_W),
                  full(wc), full(wm), full(wr), full(wl), full(gmla), vec, full(gf), vec, vec,
                  full(wrt), full(brt), full(tri)],
        out_specs=[tok(d), tok(d // 2), tok(LANES, lag),
                   pl.BlockSpec((SUBLANES, tm), lambda n: (0, lag(n))),
                   pl.BlockSpec((1, 1, LANES), lambda n: (lag(n), 0, 0))],
        out_shape=[jax.ShapeDtypeStruct((bsz, s, d), F32), jax.ShapeDtypeStruct((bsz, s, d // 2), jnp.uint32),
                   jax.ShapeDtypeStruct((bsz, s, LANES), F32),
                   jax.ShapeDtypeStruct((SUBLANES, bsz * s), F32),
                   jax.ShapeDtypeStruct((bsz * nt, 1, LANES), F32)],
        scratch_shapes=[pltpu.VMEM((1, LANES), F32), pltpu.VMEM((tm, LANES), F32)],
        compiler_params=_cparams(("arbitrary",)),
        name="outproj_router",
    )(x, yc, om, yr, yl, wc, wm, wr, wl, gmla, gt, gf, scf, shf, wrt, brt, tri)


def _route_plan(routes, cnt, tmg):
    t = routes.shape[1]
    e = routes[0:2].astype(jnp.int32)
    r = routes[2:4].astype(jnp.int32)
    counts = cnt[-1, 0, :N_EXPERTS].astype(jnp.int32)
    padded = (counts + tmg - 1) // tmg * tmg
    ends = jnp.cumsum(padded)
    starts = ends - padded
    expert = jnp.arange(N_EXPERTS, dtype=jnp.int32)[:, None, None]
    dest = jnp.sum(jnp.where(e[None] == expert, starts[:, None, None], 0), axis=0) + r
    n_tiles = (2 * t) // tmg + N_EXPERTS
    tile_start = jnp.arange(n_tiles, dtype=jnp.int32) * tmg
    tile_exp = jnp.sum((ends[None, :] <= tile_start[:, None]).astype(jnp.int32), axis=1)
    tile_exp = jnp.minimum(tile_exp, N_EXPERTS - 1)
    n_used = (ends[-1:] // tmg).astype(jnp.int32)
    last_tile = jnp.maximum(ends - tmg, 0).astype(jnp.int32)
    return dest[0], dest[1], tile_exp, n_used, last_tile, padded.astype(jnp.int32)


def _dispatch_kernel(zs_ref, zv_ref, nu_ref, d1_ref, d2_ref, h_hbm, xs_ref, zero_ref, hbuf_ref, lsem, psem,
                     *, tmc, tmg):
    i = pl.program_id(0)
    n_steps = pl.num_programs(0)
    sem = psem.at[0]

    def load(step):
        slot = step % 3
        return pltpu.make_async_copy(h_hbm.at[pl.ds(pl.multiple_of(step * tmc, tmc), tmc), :],
                                     hbuf_ref.at[slot], lsem.at[slot])

    def wait_pushes(step):
        slot = step % 3
        one_row = pltpu.make_async_copy(hbuf_ref.at[slot, pl.ds(0, 1), :], xs_ref.at[pl.ds(0, 1), :], psem.at[slot])
        for _ in range(2 * tmc):
            one_row.wait()

    @pl.when(i == 0)
    def _():
        load(0).start()
        pl.when(n_steps > 1)(load(1).start)
        zero_ref[...] = jnp.zeros(zero_ref.shape, jnp.uint32)
        n_tiles = xs_ref.shape[0] // tmg

        def fill(start):
            return pltpu.make_async_copy(zero_ref, xs_ref.at[pl.ds(pl.multiple_of(start, tmg), tmg), :], sem)

        def fill_tail(j, carry, wait):
            cp = fill(j * tmg)
            cp.wait() if wait else cp.start()
            return carry

        for wait in (False, True):
            for e in range(N_EXPERTS):
                cp = fill(zs_ref[e])
                pl.when(zv_ref[e] > 0)(cp.wait if wait else cp.start)
            lax.fori_loop(nu_ref[0], n_tiles, functools.partial(fill_tail, wait=wait), 0)

    slot = i % 3
    load(i).wait()
    push_sem = psem.at[slot]

    def push(r, carry):
        src = hbuf_ref.at[slot, pl.ds(r, 1), :]
        pltpu.make_async_copy(src, xs_ref.at[pl.ds(d1_ref[0, 0, r], 1), :], push_sem).start(priority=0)
        pltpu.make_async_copy(src, xs_ref.at[pl.ds(d2_ref[0, 0, r], 1), :], push_sem).start(priority=1)
        return carry

    lax.fori_loop(0, tmc, push, 0, unroll=8)

    pl.when(i >= 1)(functools.partial(wait_pushes, i - 1))
    pl.when(i + 2 < n_steps)(load(i + 2).start)
    pl.when(i == n_steps - 1)(functools.partial(wait_pushes, i))


def _dispatch(hp, d1, d2, last_tile, padded, n_used, tmc, tmg):
    t, dw = hp.shape
    n_rows = 2 * t + N_EXPERTS * tmg
    smem_rows = pl.BlockSpec((1, 1, tmc), lambda i, zs, zv, nu: (i, 0, 0), memory_space=pltpu.SMEM)
    return pl.pallas_call(
        functools.partial(_dispatch_kernel, tmc=tmc, tmg=tmg),
        grid_spec=pltpu.PrefetchScalarGridSpec(
            num_scalar_prefetch=3, grid=(t // tmc,),
            in_specs=[smem_rows, smem_rows, pl.BlockSpec(memory_space=pl.ANY)],
            out_specs=pl.BlockSpec(memory_space=pl.ANY),
            scratch_shapes=[pltpu.VMEM((tmg, dw), jnp.uint32), pltpu.VMEM((3, tmc, dw), jnp.uint32),
                            pltpu.SemaphoreType.DMA((3,)), pltpu.SemaphoreType.DMA((3,))]),
        out_shape=jax.ShapeDtypeStruct((n_rows, dw), jnp.uint32),
        compiler_params=_cparams(("arbitrary",)),
        name="moe_dispatch",
    )(last_tile, padded, n_used, d1.reshape(t // tmc, 1, tmc), d2.reshape(t // tmc, 1, tmc), hp)


def _experts_kernel(te_ref, nu_ref, xs_ref, wg_ref, wu_ref, wd_ref, y_ref, wgb_ref, wub_ref, wdb_ref):
    j = pl.program_id(0)
    used = j < nu_ref[0]
    tile = jnp.minimum(j, nu_ref[0] - 1)

    @pl.when((j == 0) | (te_ref[tile] != te_ref[jnp.maximum(tile - 1, 0)]))
    def _():
        wgb_ref[...] = wg_ref[0, 0].astype(BF16)
        wub_ref[...] = wu_ref[0, 0].astype(BF16)
        wdb_ref[...] = wd_ref[0, 0].astype(BF16)

    @pl.when(used)
    def _():
        x = _unpack_bf16_pairs(xs_ref[...]).astype(BF16)
        gate = _dot(x, wgb_ref[...])
        hid = gate * _sigmoid(gate) * _dot(x, wub_ref[...])
        y_ref[...] = _pack_bf16_pairs(_dot(hid.astype(BF16), wdb_ref[...]))

    @pl.when(jnp.logical_not(used))
    def _():
        y_ref[...] = jnp.zeros(y_ref.shape, jnp.uint32)


def _experts(xs, tile_exp, n_used, layer, wg, wu, wd, tmg):
    n_rows, dw = xs.shape
    d = 2 * dw
    tile = lambda i, te, nu: jnp.minimum(i, nu[0] - 1)
    rows = pl.BlockSpec((tmg, dw), lambda i, te, nu: (tile(i, te, nu), 0))
    wspec = lambda shape: pl.BlockSpec((1, 1) + shape,
                                       lambda i, te, nu: (layer, te[tile(i, te, nu)], 0, 0))
    return pl.pallas_call(
        _experts_kernel,
        grid_spec=pltpu.PrefetchScalarGridSpec(
            num_scalar_prefetch=2, grid=(n_rows // tmg,),
            in_specs=[rows, wspec((d, D_EXPERT)), wspec((d, D_EXPERT)), wspec((D_EXPERT, d))],
            out_specs=pl.BlockSpec((tmg, dw), lambda i, te, nu: (i, 0)),
            scratch_shapes=[pltpu.VMEM((d, D_EXPERT), BF16), pltpu.VMEM((d, D_EXPERT), BF16),
                            pltpu.VMEM((D_EXPERT, d), BF16)]),
        out_shape=jax.ShapeDtypeStruct((n_rows, dw), jnp.uint32),
        compiler_params=_cparams(("arbitrary",)),
        name="moe_experts",
    )(tile_exp, n_used, xs, wg, wu, wd)


def _layer_weights(l, w_in, mla_q_norm_g, mla_w_uq, mla_kv_norm_g, mla_w_ukv, mla_q_qk_g,
                   mla_k_qk_g, lru_w_a, lru_w_x, mix_norm_g, w_out, router_group_w,
                   router_group_b, router_expert_w, router_expert_b):
    half = RET_DK // 2
    perm = np.concatenate([np.arange(half) + HEAD_DIM * h for h in range(N_HEADS)]
                          + [np.arange(half) + half + HEAD_DIM * h for h in range(N_HEADS)])
    w = w_in[l]
    o_mla, o_ret, o_lru = U_CONV, U_CONV + 352, U_CONV + 352 + U_RET
    w_ret = w[:, o_ret:o_ret + U_RET]
    w_ret = jnp.concatenate([w_ret[:, perm], w_ret[:, GROUP_W + perm], w_ret[:, 2 * GROUP_W:]], axis=1)
    w_all = jnp.concatenate([w_ret, w[:, o_lru:o_lru + U_LRU], w[:, :U_CONV], w[:, o_mla:o_mla + 352],
                             jnp.zeros((D_MODEL, U_MLA - 352), F32)], axis=1).astype(BF16)

    hw = N_HEADS * LANES
    r16 = MLA_ROPE // 2
    wq = mla_w_uq[l].reshape(Q_LORA, N_HEADS, MLA_QK)
    zq = jnp.zeros((Q_LORA, N_HEADS, LANES - MLA_QK), F32)
    q_cols = jnp.concatenate([wq, zq], axis=2).reshape(Q_LORA, hw)
    wq_sw = jnp.concatenate([jnp.zeros((Q_LORA, N_HEADS, MLA_NOPE), F32), wq[:, :, MLA_NOPE + r16:],
                             wq[:, :, MLA_NOPE:MLA_NOPE + r16], zq], axis=2).reshape(Q_LORA, hw)
    wkv = mla_w_ukv[l].reshape(KV_LORA, N_HEADS, MLA_NOPE + HEAD_DIM)
    zk = jnp.zeros((KV_LORA, N_HEADS, LANES - MLA_NOPE), F32)
    k_cols = jnp.concatenate([wkv[:, :, :MLA_NOPE], zk], axis=2).reshape(KV_LORA, hw)
    v_cols = jnp.concatenate([wkv[:, :, MLA_NOPE:], zk], axis=2).reshape(KV_LORA, hw)
    eye = jnp.eye(MLA_ROPE, dtype=F32)
    place = jnp.concatenate([jnp.zeros((MLA_ROPE, MLA_NOPE), F32), eye,
                             jnp.zeros((MLA_ROPE, LANES - MLA_QK), F32)], axis=1)
    eye_sw = jnp.concatenate([eye[:, r16:], eye[:, :r16]], axis=1)
    place_sw = jnp.concatenate([jnp.zeros((MLA_ROPE, MLA_NOPE), F32), eye_sw,
                                jnp.zeros((MLA_ROPE, LANES - MLA_QK), F32)], axis=1)
    n_big = 4 * hw + 2 * LANES
    wbig = jnp.zeros((U_MLA, n_big), F32)
    wbig = wbig.at[:Q_LORA, :hw].set(q_cols).at[:Q_LORA, hw:2 * hw].set(wq_sw)
    wbig = wbig.at[Q_LORA:Q_LORA + KV_LORA, 2 * hw:3 * hw].set(k_cols)
    wbig = wbig.at[Q_LORA:Q_LORA + KV_LORA, 3 * hw:4 * hw].set(v_cols)
    wbig = wbig.at[Q_LORA + KV_LORA:352, 4 * hw:4 * hw + LANES].set(place)
    wbig = wbig.at[Q_LORA + KV_LORA:352, 4 * hw + LANES:].set(place_sw)
    wbig = wbig.astype(BF16)

    gu = jnp.concatenate([mla_q_norm_g[l], mla_kv_norm_g[l], mla_k_qk_g[l][MLA_NOPE:],
                          jnp.zeros((U_MLA - 352,), F32)])[None, :]
    qscale = (MLA_QK ** -0.5) * math.log2(math.e)
    gq_full = mla_q_qk_g[l]
    pad = jnp.zeros((LANES - MLA_QK,), F32)
    gq = (jnp.concatenate([gq_full, pad]) * qscale)[None, :]
    gqs = (jnp.concatenate([jnp.zeros((MLA_NOPE,), F32), gq_full[MLA_NOPE + r16:],
                            gq_full[MLA_NOPE:MLA_NOPE + r16], pad]) * qscale)[None, :]
    gk = jnp.concatenate([mla_k_qk_g[l][:MLA_NOPE], jnp.zeros((LANES - MLA_NOPE,), F32)])[None, :]

    def blockdiag(wb):
        out = jnp.zeros((GROUP_W, GROUP_W), F32)
        for n in range(wb.shape[0]):
            out = out.at[n * HEAD_DIM:(n + 1) * HEAD_DIM, n * HEAD_DIM:(n + 1) * HEAD_DIM].set(wb[n])
        return out.astype(BF16)

    gmix = mix_norm_g[l]
    wo = w_out[l].astype(BF16)
    wm = wo[GROUP_W:2 * GROUP_W].reshape(N_HEADS, HEAD_DIM, D_MODEL)
    wm = jnp.concatenate([wm, jnp.zeros((N_HEADS, LANES - HEAD_DIM, D_MODEL), BF16)], axis=1)
    wm = wm.reshape(N_HEADS // 2, 2 * LANES, D_MODEL)
    gmla = jnp.concatenate([gmix[GROUP_W:2 * GROUP_W].reshape(N_HEADS, 1, HEAD_DIM),
                            jnp.zeros((N_HEADS, 1, LANES - HEAD_DIM), F32)], axis=2)
    wrt = jnp.concatenate([router_expert_w[l], router_group_w[l],
                           jnp.zeros((D_MODEL, LANES - N_EXPERTS - MOE_GROUPS), F32)], axis=1)
    wrt_hi = wrt.astype(BF16)
    wrt = jnp.concatenate([wrt_hi, (wrt - wrt_hi.astype(F32)).astype(BF16)], axis=1)
    brt = jnp.concatenate([router_expert_b[l], router_group_b[l],
                           jnp.zeros((LANES - N_EXPERTS - MOE_GROUPS,), F32)])[None, :]
    return dict(w_all=w_all, wbig=wbig, gu=gu, gq=gq, gqs=gqs, gk=gk,
                wa=blockdiag(lru_w_a[l]), wx=blockdiag(lru_w_x[l]),
                g_conv=gmix[None, :GROUP_W], g_ret=gmix[None, 2 * GROUP_W:3 * GROUP_W],
                g_lru=gmix[None, 3 * GROUP_W:], gmla=gmla,
                wc=wo[:GROUP_W], wm=wm, wr=wo[2 * GROUP_W:3 * GROUP_W], wl=wo[3 * GROUP_W:],
                wrt=wrt, brt=brt)


def _mla_consts():
    seg_u = np.concatenate([np.zeros(Q_LORA), np.ones(KV_LORA), 2 * np.ones(MLA_ROPE),
                            3 * np.ones(U_MLA - 352)])
    mu = jnp.asarray(seg_u[:, None] == seg_u[None, :], BF16)
    invu = jnp.asarray(np.concatenate([np.full(Q_LORA, 1.0 / Q_LORA), np.full(KV_LORA, 1.0 / KV_LORA),
                                       np.full(MLA_ROPE, 1.0 / MLA_ROPE), np.ones(U_MLA - 352)]), F32)[None, :]
    lane = np.arange(N_HEADS * LANES)
    seg_q = (lane // LANES) * 3 + np.where(lane % LANES < MLA_NOPE, 0, np.where(lane % LANES < MLA_QK, 1, 2))
    sq = jnp.asarray(seg_q[:, None] == seg_q[None, :], BF16)
    inv_head = np.concatenate([np.full(MLA_NOPE, 1.0 / MLA_NOPE), np.full(MLA_ROPE, 1.0 / MLA_ROPE),
                               np.ones(LANES - MLA_QK)])
    invq = jnp.asarray(np.tile(inv_head, N_HEADS), F32)[None, :]
    onev = jnp.asarray((np.arange(LANES) == HEAD_DIM).astype(np.float32))[None, :]
    return mu, invu, sq, invq, onev


def kernel(x, c, positions, ada_w, ada_b, norm_mix_g, w_in, conv_w, mla_q_norm_g, mla_w_uq, mla_kv_norm_g, mla_w_ukv, mla_q_qk_g, mla_k_qk_g, lru_conv_w, lru_conv_b, lru_w_a, lru_b_a, lru_w_x, lru_b_x, lru_lambda, mix_norm_g, w_out, norm_ffn_g, router_group_w, router_group_b, router_expert_w, router_expert_b, exp_w_gate, exp_w_up, exp_w_down):
    bsz, s, d = x.shape
    depth = ada_w.shape[0]
    tm = min(512, s)
    chunk = min(256, s)
    tq = min(1024, s)
    tmc = min(512, s)
    tmd = min(1024, s)
    tmg = 1024

    inv = jnp.concatenate([1.0 / (ROPE_BASE ** (jnp.arange(0, MLA_ROPE, 2, dtype=F32) / MLA_ROPE)),
                           1.0 / (ROPE_BASE ** (jnp.arange(0, RET_DK, 2, dtype=F32) / RET_DK))])[:, None]
    trig = _rope_tables(positions, inv)
    ex_ret, ex_mla = _trig_expanders()

    c_pad = jnp.concatenate([c, jnp.zeros((8 - bsz, d), F32)], axis=0)
    mod = _modulation(c_pad, ada_w, ada_b)[:, :bsz]
    ret_consts = _ret_consts(chunk)
    mu, invu, sq, invq, onev = _mla_consts()

    pending = None
    for l in range(depth):
        sh_m, sc_m, gt_m, sh_f, sc_f, gt_f = [m[:, None, :] for m in jnp.split(mod[l], 6, axis=-1)]
        lw = _layer_weights(l, w_in, mla_q_norm_g, mla_w_uq, mla_kv_norm_g, mla_w_ukv, mla_q_qk_g,
                            mla_k_qk_g, lru_w_a, lru_w_x, mix_norm_g, w_out, router_group_w,
                            router_group_b, router_expert_w, router_expert_b)
        if pending is None:
            u = _inproj(x, norm_mix_g[l][None, :], sc_m, sh_m, lw["w_all"], tm)
        else:
            x, u = _combine(x, *pending, tm, proj=(norm_mix_g[l][None, :], sc_m, sh_m, lw["w_all"]))
        y_conv = _conv_mixer(u, conv_w[l], lw["g_conv"], tm)
        y_lru = _lru_mixer(u, lru_conv_w[l], lru_conv_b[l][None, :], lw["wa"], lru_b_a[l][None, :],
                           lw["wx"], lru_b_x[l][None, :], lru_lambda[l][None, :], lw["g_lru"], tm)
        y_ret = _ret_mixer(u, trig, ex_ret, ret_consts, lw["g_ret"], chunk)
        q, k, v = _mla_prep(u, trig, ex_mla, mu, invu, lw["gu"], lw["wbig"], sq, invq,
                            lw["gq"], lw["gqs"], lw["gk"], onev, tm)
        o_mla = _flash_attention(q, k, v, tq)
        x, hp, meta, routes, cnt = _outproj(x, y_conv, o_mla, y_ret, y_lru, lw["wc"], lw["wm"], lw["wr"],
                                            lw["wl"], lw["gmla"], gt_m, norm_ffn_g[l][None, :], sc_f, sh_f,
                                            lw["wrt"], lw["brt"], tm)
        d1, d2, tile_exp, n_used, last_tile, padded = _route_plan(routes, cnt, tmg)
        xs = _dispatch(hp.reshape(bsz * s, d // 2), d1, d2, last_tile, padded, n_used, tmd, tmg)
        ys = _experts(xs, tile_exp, n_used, l, exp_w_gate, exp_w_up, exp_w_down, tmg)
        pending = (meta, gt_f, ys, d1, d2)
    return _combine(x, *pending, tmc)[0]
```
